```python
import math
import jax
import jax.numpy as jnp
from jax import lax
import numpy as np

D_MODEL = 1024
BATCH = 4
SEQ = 4096
DEPTH = 1

N_MEM = 256
ATTN_GROUPS = ((128, 1), (512, 4), (2048, 16))
N_GROUPS = len(ATTN_GROUPS)
ATTN_HEADS = 8
HEAD_DIM = 64
ATTN_WIDTH = ATTN_HEADS * HEAD_DIM
Q_BLOCK = 128
CONV_WIDTH = D_MODEL
CONV_K = 3
ATTN_COLS = N_GROUPS * 3 * ATTN_WIDTH
IN_WIDTH = ATTN_COLS + 3 * CONV_WIDTH
N_BRANCHES = 2
MEM_HEADS = 4
MEM_HEAD_DIM = 128
MEM_WIDTH = MEM_HEADS * MEM_HEAD_DIM
N_EXPERTS = 32
TOP_K = 4
D_FF = D_MODEL
SWIGLU_LIMIT = 7.0
SWIGLU_ALPHA = 1.702
MOE_BLOCK = 128
EPS = 1e-6
NEG_INF = -1e30

kernel_name = "hybrid_dilated_attn_shortconv_moe"


def rms_norm(x, g):
    xf = x.astype(jnp.float32)
    y = xf * lax.rsqrt(jnp.mean(xf * xf, axis=-1, keepdims=True) + EPS)
    return (y * g.astype(jnp.float32)).astype(x.dtype)


def alibi_slopes(n_heads):
    return jnp.power(2.0, -8.0 * jnp.arange(1, n_heads + 1, dtype=jnp.float32) / n_heads)


def dilated_window_attention(q, k, v, window, dilation, slopes):
    B, S, H, Dh = q.shape
    L = S // dilation
    w_sub = window // dilation
    nb = -(-L // Q_BLOCK)
    Lp = nb * Q_BLOCK

    def to_blocks(a):
        a = a.reshape(B, L, dilation, H, Dh).transpose(0, 2, 1, 3, 4)
        a = jnp.pad(a, ((0, 0), (0, 0), (0, Lp - L), (0, 0), (0, 0)))
        return a.reshape(B, dilation, nb, Q_BLOCK, H, Dh)

    def with_prev(a):
        prev = jnp.pad(a[:, :, :-1], ((0, 0), (0, 0), (1, 0), (0, 0), (0, 0), (0, 0)))
        return jnp.concatenate([prev, a], axis=3)

    qb, kb, vb = to_blocks(q), to_blocks(k), to_blocks(v)
    kk, vv = with_prev(kb), with_prev(vb)
    s = jnp.einsum('brnqhd,brnkhd->brnhqk', qb, kk).astype(jnp.float32) * (Dh ** -0.5)
    iq = jnp.arange(Q_BLOCK)
    ik = jnp.arange(2 * Q_BLOCK)
    dist = iq[:, None] + Q_BLOCK - ik[None, :]
    key_ok = (jnp.arange(nb)[:, None] * Q_BLOCK + ik[None, :] - Q_BLOCK) >= 0
    mask = ((dist >= 0) & (dist <= w_sub))[None] & key_ok[:, None, :]
    bias = -slopes[:, None, None] * (dist * dilation).astype(jnp.float32)[None]
    s = jnp.where(mask[None, None, :, None], s + bias[None, None, None], NEG_INF)
    lse = jax.nn.logsumexp(s, axis=-1)
    p = jnp.exp(s - lse[..., None])
    o = jnp.einsum('brnhqk,brnkhd->brnqhd', p.astype(v.dtype), vv)
    o = o.reshape(B, dilation, Lp, H, Dh)[:, :, :L].transpose(0, 2, 1, 3, 4).reshape(B, S, H, Dh)
    lse = lse.transpose(0, 1, 2, 4, 3).reshape(B, dilation, Lp, H)[:, :, :L]
    lse = lse.transpose(0, 2, 1, 3).reshape(B, S, H)
    return o, lse


def short_conv_causal(z, conv_w):
    S = z.shape[1]
    zp = jnp.pad(z, ((0, 0), (CONV_K - 1, 0), (0, 0)))
    return sum(zp[:, k:k + S] * conv_w[k] for k in range(CONV_K))


def hybrid_mixer(u, w_in, conv_w, w_branch_attn, w_branch_conv, w_gate, b_gate, w_out):
    B, S, _ = u.shape
    proj = u @ w_in
    attn_cols, conv_cols = proj[..., :ATTN_COLS], proj[..., ATTN_COLS:]
    slopes = alibi_slopes(ATTN_HEADS)
    outs, lses = [], []
    for g, (window, dil) in enumerate(ATTN_GROUPS):
        cols = attn_cols[..., g * 3 * ATTN_WIDTH:(g + 1) * 3 * ATTN_WIDTH]
        cols = cols.reshape(B, S, 3, ATTN_HEADS, HEAD_DIM)
        o, lse = dilated_window_attention(cols[:, :, 0], cols[:, :, 1], cols[:, :, 2], window, dil, slopes)
        outs.append(o)
        lses.append(lse)
    wts = jax.nn.softmax(jnp.stack(lses), axis=0)
    y_attn = jnp.einsum('gbsh,gbshd->bshd', wts, jnp.stack(outs).astype(jnp.float32))
    y_attn = y_attn.reshape(B, S, ATTN_WIDTH).astype(u.dtype)
    gb, gc, xc = jnp.split(conv_cols, 3, axis=-1)
    y_conv = gb * short_conv_causal(gc * xc, conv_w)
    gates = jax.nn.sigmoid(u @ w_gate + b_gate)
    g_attn, g_conv = jnp.split(gates, N_BRANCHES, axis=-1)
    mixed = g_attn * (y_attn @ w_branch_attn) + g_conv * (y_conv @ w_branch_conv)
    return mixed @ w_out


def memory_cross_attention(u, m, w_cq, w_ckv, w_co):
    B, S, _ = u.shape
    q = (u @ w_cq).reshape(B, S, MEM_HEADS, MEM_HEAD_DIM)
    k, v = jnp.split((m @ w_ckv).reshape(B, m.shape[1], 2, MEM_HEADS, MEM_HEAD_DIM), 2, axis=2)
    k, v = k[:, :, 0], v[:, :, 0]
    s = jnp.einsum('bshd,bmhd->bhsm', q, k).astype(jnp.float32) * (MEM_HEAD_DIM ** -0.5)
    p = jax.nn.softmax(s, axis=-1)
    o = jnp.einsum('bhsm,bmhd->bshd', p.astype(v.dtype), v).reshape(B, S, MEM_WIDTH)
    return o @ w_co


def moe_ffn(u, router_w, router_b, w_eg, b_eg, w_eu, b_eu, w_ed, b_ed):
    T, D = u.shape
    logits = (u @ router_w + router_b).astype(jnp.float32)
    top_logit, top_idx = lax.top_k(logits, TOP_K)
    top_w = jax.nn.softmax(top_logit, axis=-1)
    n_assign = T * TOP_K
    flat_e = top_idx.reshape(-1)
    flat_tok = jnp.arange(n_assign, dtype=jnp.int32) // TOP_K
    order = jnp.argsort(flat_e)
    e_sorted = flat_e[order]
    tok_sorted = flat_tok[order]
    w_sorted = top_w.reshape(-1)[order]
    counts = jnp.bincount(flat_e, length=N_EXPERTS)
    padded = (counts + MOE_BLOCK - 1) // MOE_BLOCK * MOE_BLOCK
    start = jnp.cumsum(counts) - counts
    pend = jnp.cumsum(padded)
    pstart = pend - padded
    row = pstart[e_sorted] + jnp.arange(n_assign) - start[e_sorted]
    n_rows = (-(-n_assign // MOE_BLOCK)) * MOE_BLOCK + N_EXPERTS * MOE_BLOCK
    n_blocks = n_rows // MOE_BLOCK
    row_tok = jnp.full((n_rows,), T, jnp.int32).at[row].set(tok_sorted)
    block_expert = jnp.clip(jnp.searchsorted(pend, jnp.arange(n_blocks) * MOE_BLOCK, side='right'),
                            0, N_EXPERTS - 1)
    u_pad = jnp.concatenate([u, jnp.zeros((1, D), u.dtype)], axis=0)
    xb = u_pad[row_tok].reshape(n_blocks, MOE_BLOCK, D)

    def expert_block(args):
        xblk, e = args
        gate = jnp.minimum(xblk @ w_eg[e] + b_eg[e], SWIGLU_LIMIT)
        lin = jnp.clip(xblk @ w_eu[e] + b_eu[e], -SWIGLU_LIMIT, SWIGLU_LIMIT)
        hdn = gate * jax.nn.sigmoid(SWIGLU_ALPHA * gate) * (lin + 1.0)
        return hdn @ w_ed[e] + b_ed[e]

    y_rows = lax.map(expert_block, (xb, block_expert)).reshape(n_rows, D)
    y_assign = y_rows[row].astype(jnp.float32) * w_sorted[:, None]
    return jax.ops.segment_sum(y_assign, tok_sorted, num_segments=T).astype(u.dtype)


def setup_inputs(seed: int = 0) -> dict:
    key = jax.random.key(seed)
    ks = jax.random.split(key, 26)
    nrm = lambda k, shape, fan_in: jax.random.normal(k, shape, jnp.float32) * (fan_in ** -0.5)
    gain = lambda k, shape: 1.0 + 0.02 * jax.random.normal(k, shape, jnp.float32)
    small = lambda k, shape: 0.01 * jax.random.normal(k, shape, jnp.float32)
    L, D, E = DEPTH, D_MODEL, N_EXPERTS
    return {
        "x": jax.random.normal(ks[0], (BATCH, SEQ, D), jnp.float32),
        "mem": jax.random.normal(ks[1], (BATCH, N_MEM, D), jnp.float32),
        "norm_mix": gain(ks[2], (L, D)),
        "w_in": nrm(ks[3], (L, D, IN_WIDTH), D),
        "conv_w": nrm(ks[4], (L, CONV_K, CONV_WIDTH), CONV_K),
        "w_branch_attn": nrm(ks[5], (L, ATTN_WIDTH, D), ATTN_WIDTH),
        "w_branch_conv": nrm(ks[6], (L, CONV_WIDTH, D), CONV_WIDTH),
        "w_gate": nrm(ks[7], (L, D, N_BRANCHES * D), D),
        "b_gate": small(ks[8], (L, N_BRANCHES * D)),
        "w_out": nrm(ks[9], (L, D, D), D),
        "norm_cross": gain(ks[10], (L, D)),
        "norm_mem": gain(ks[11], (L, D)),
        "w_cq": nrm(ks[12], (L, D, MEM_WIDTH), D),
        "w_ckv": nrm(ks[13], (L, D, 2 * MEM_WIDTH), D),
        "w_co": nrm(ks[14], (L, MEM_WIDTH, D), MEM_WIDTH),
        "norm_moe": gain(ks[15], (L, D)),
        "router_w": nrm(ks[16], (L, D, E), D),
        "router_b": small(ks[17], (L, E)),
        "w_exp_gate": nrm(ks[18], (L, E, D, D_FF), D),
        "b_exp_gate": small(ks[19], (L, E, D_FF)),
        "w_exp_up": nrm(ks[20], (L, E, D, D_FF), D),
        "b_exp_up": small(ks[21], (L, E, D_FF)),
        "w_exp_down": nrm(ks[22], (L, E, D_FF, D), D_FF),
        "b_exp_down": small(ks[23], (L, E, D)),
        "norm_final": gain(ks[24], (D,)),
    }


def reference(x, mem, norm_mix, w_in, conv_w, w_branch_attn, w_branch_conv, w_gate, b_gate, w_out,
              norm_cross, norm_mem, w_cq, w_ckv, w_co, norm_moe, router_w, router_b,
              w_exp_gate, b_exp_gate, w_exp_up, b_exp_up, w_exp_down, b_exp_down, norm_final):
    B, S, D = x.shape
    h = x
    for l in range(DEPTH):
        u = rms_norm(h, norm_mix[l])
        h = h + hybrid_mixer(u, w_in[l], conv_w[l], w_branch_attn[l], w_branch_conv[l],
                             w_gate[l], b_gate[l], w_out[l])
        u = rms_norm(h, norm_cross[l])
        m = rms_norm(mem, norm_mem[l])
        h = h + memory_cross_attention(u, m, w_cq[l], w_ckv[l], w_co[l])
        u = rms_norm(h, norm_moe[l]).reshape(B * S, D)
        h = h + moe_ffn(u, router_w[l], router_b[l], w_exp_gate[l], b_exp_gate[l],
                        w_exp_up[l], b_exp_up[l], w_exp_down[l], b_exp_down[l]).reshape(B, S, D)
    return rms_norm(h, norm_final)
```

```python
import functools

import jax
import jax.numpy as jnp
from jax import lax
from jax.experimental import pallas as pl
from jax.experimental.pallas import tpu as pltpu

F32 = jnp.float32
BF16 = jnp.bfloat16
I32 = jnp.int32

D_MODEL = 1024
ATTN_GROUPS = ((128, 1), (512, 4), (2048, 16))
ATTN_HEADS = 8
HEAD_DIM = 64
ATTN_WIDTH = ATTN_HEADS * HEAD_DIM
Q_BLOCK = 128
CONV_K = 3
MEM_HEADS = 4
MEM_HEAD_DIM = 128
MEM_WIDTH = MEM_HEADS * MEM_HEAD_DIM
N_EXPERTS = 32
TOP_K = 4
SWIGLU_LIMIT = 7.0
SWIGLU_ALPHA = 1.702
EPS = 1e-6
NEG_INF = -1e30

LANES = 128
VMEM_LIMIT = 56 * 1024 * 1024

MOE_ROWS = 256
COMBINE_ROWS = 256
CONV_HALO = 16


def _rms(x, g):
    ms = jnp.mean(x * x, axis=-1, keepdims=True)
    return x * lax.rsqrt(ms + EPS) * g


def _inproj_kernel(x_ref, g_ref, w_ref, b_ref, o_ref, u_scr, *, act):
    @pl.when(pl.program_id(3) == 0)
    def _():
        u_scr[...] = _rms(x_ref[...], g_ref[...]).astype(BF16)

    acc = jnp.dot(u_scr[...], w_ref[...], preferred_element_type=F32)
    if act == "sigmoid":
        acc = jax.nn.sigmoid(acc + b_ref[...])
    o_ref[...] = acc.astype(o_ref.dtype)


def _inproj(x, gain, w, bias, dil, tm, tn, act=None, name="inproj"):
    B, S, D = x.shape
    L = S // dil
    N = w.shape[1]
    xv = x.reshape(B, L, dil * D)
    return pl.pallas_call(
        functools.partial(_inproj_kernel, act=act),
        grid=(B, dil, L // tm, N // tn),
        in_specs=[
            pl.BlockSpec((None, tm, D), lambda b, r, i, j: (b, i, r)),
            pl.BlockSpec((1, D), lambda b, r, i, j: (0, 0)),
            pl.BlockSpec((D, tn), lambda b, r, i, j: (0, j)),
            pl.BlockSpec((1, tn), lambda b, r, i, j: (0, j)),
        ],
        out_specs=pl.BlockSpec((None, None, tm, tn), lambda b, r, i, j: (b, r, i, j)),
        out_shape=jax.ShapeDtypeStruct((B, dil, L, N), BF16),
        scratch_shapes=[pltpu.VMEM((tm, D), BF16)],
        compiler_params=pltpu.CompilerParams(
            dimension_semantics=("parallel", "parallel", "parallel", "arbitrary"),
            vmem_limit_bytes=VMEM_LIMIT),
        name=name,
    )(xv, gain.reshape(1, D), w, bias.reshape(1, N))


def _attn_kernel(q_ref, kp_ref, kc_ref, vp_ref, vc_ref, tab_ref, o_ref, lse_ref):
    first = jnp.minimum(pl.program_id(2), 1)
    lane = lax.broadcasted_iota(I32, (Q_BLOCK, LANES), 1)
    low = lane < HEAD_DIM
    lse_tile = jnp.zeros((Q_BLOCK, LANES), F32)
    for pair in range(ATTN_HEADS // 2):
        cols = slice(pair * LANES, (pair + 1) * LANES)
        q2 = q_ref[:, cols] * (HEAD_DIM ** -0.5)
        k2 = jnp.concatenate([kp_ref[:, cols], kc_ref[:, cols]], axis=0)
        v2 = jnp.concatenate([vp_ref[:, cols], vc_ref[:, cols]], axis=0)
        outs = []
        for half in range(2):
            h = 2 * pair + half
            keep = low if half == 0 else jnp.logical_not(low)
            qm = jnp.where(keep, q2, jnp.zeros_like(q2))
            s = lax.dot_general(qm, k2, (((1,), (1,)), ((), ())), preferred_element_type=F32)
            s = s + tab_ref[first, h]
            m = jnp.max(s, axis=1, keepdims=True)
            p = jnp.exp(s - m)
            l = jnp.sum(p, axis=1, keepdims=True)
            o = jnp.dot(p.astype(BF16), v2, preferred_element_type=F32) * (1.0 / l)
            outs.append(o)
            lse_tile = jnp.where(lane == h, m + jnp.log(l), lse_tile)
        o_ref[:, cols] = jnp.where(low, outs[0], outs[1]).astype(o_ref.dtype)
    lse_ref[...] = lse_tile


def _attn_bias_table(dil):
    slopes = jnp.power(2.0, -8.0 * jnp.arange(1, ATTN_HEADS + 1, dtype=F32) / ATTN_HEADS)
    iq = jnp.arange(Q_BLOCK)
    ik = jnp.arange(2 * Q_BLOCK)
    dist = iq[:, None] + Q_BLOCK - ik[None, :]
    band = (dist >= 0) & (dist <= Q_BLOCK)
    has_prev = jnp.stack([ik >= Q_BLOCK, jnp.ones_like(ik, dtype=bool)])
    mask = band[None] & has_prev[:, None, :]
    bias = -slopes[:, None, None] * (dist * dil).astype(F32)[None]
    return jnp.where(mask[:, None], bias[None], NEG_INF).astype(F32)


def _attention(qkv, dil, col0, name):
    B, _, L, _ = qkv.shape
    W = ATTN_WIDTH
    nb = L // Q_BLOCK
    blk = lambda part, prev: pl.BlockSpec(
        (None, None, Q_BLOCK, W),
        (lambda b, r, n: (b, r, jnp.maximum(n - 1, 0), col0 + part)) if prev
        else (lambda b, r, n: (b, r, n, col0 + part)))
    tab = _attn_bias_table(dil)
    o, lse = pl.pallas_call(
        _attn_kernel,
        grid=(B, dil, nb),
        in_specs=[blk(0, False), blk(1, True), blk(1, False), blk(2, True), blk(2, False),
                  pl.BlockSpec(tab.shape, lambda b, r, n: (0, 0, 0, 0))],
        out_specs=[pl.BlockSpec((None, Q_BLOCK, W), lambda b, r, n: (b, n, r)),
                   pl.BlockSpec((None, Q_BLOCK, LANES), lambda b, r, n: (b, n, r))],
        out_shape=[jax.ShapeDtypeStruct((B, L, dil * W), BF16),
                   jax.ShapeDtypeStruct((B, L, dil * LANES), F32)],
        compiler_params=pltpu.CompilerParams(
            dimension_semantics=("parallel", "parallel", "parallel"),
            vmem_limit_bytes=VMEM_LIMIT),
        name=name,
    )(qkv, qkv, qkv, qkv, qkv, tab)
    return o.reshape(B * L * dil, W), lse.reshape(B * L * dil, LANES)


def _mix_kernel(o1_ref, o2_ref, o3_ref, l1_ref, l2_ref, l3_ref, gb_ref, gc_ref, xc_ref, gch_ref, xch_ref,
                ga_ref, gv_ref, x_ref, e_ref, wa_ref, wc_ref, wo_ref, cw_ref, h_ref, z_scr, *, tm, seq):
    a1, a2, a3 = l1_ref[...], l2_ref[...], l3_ref[...]
    m = jnp.maximum(jnp.maximum(a1, a2), a3)
    e1, e2, e3 = jnp.exp(a1 - m), jnp.exp(a2 - m), jnp.exp(a3 - m)
    inv = 1.0 / (e1 + e2 + e3)
    y = jnp.zeros((tm, ATTN_WIDTH), F32)
    for e, o_ref in ((e1, o1_ref), (e2, o2_ref), (e3, o3_ref)):
        w = e * inv
        w_hi = w.astype(BF16)
        w_lo = (w - w_hi.astype(F32)).astype(BF16)
        w_heads = (jnp.dot(w_hi, e_ref[...], preferred_element_type=F32)
                   + jnp.dot(w_lo, e_ref[...], preferred_element_type=F32))
        y = y + w_heads * o_ref[...].astype(F32)
    branch_a = jnp.dot(y.astype(BF16), wa_ref[...], preferred_element_type=F32)

    at_start = (pl.program_id(0) * tm) % seq == 0
    halo = gch_ref[...].astype(F32) * xch_ref[...].astype(F32)
    z_scr[0:CONV_HALO, :] = jnp.where(at_start, jnp.zeros_like(halo), halo)
    z_scr[CONV_HALO:, :] = gc_ref[...].astype(F32) * xc_ref[...].astype(F32)
    conv = (cw_ref[0:1, :] * z_scr[pl.ds(CONV_HALO - 2, tm), :]
            + cw_ref[1:2, :] * z_scr[pl.ds(CONV_HALO - 1, tm), :]
            + cw_ref[2:3, :] * z_scr[pl.ds(CONV_HALO, tm), :])
    y_conv = gb_ref[...].astype(F32) * conv
    branch_c = jnp.dot(y_conv.astype(BF16), wc_ref[...], preferred_element_type=F32)

    mixed = ga_ref[...].astype(F32) * branch_a + gv_ref[...].astype(F32) * branch_c
    h_ref[...] = x_ref[...] + jnp.dot(mixed.astype(BF16), wo_ref[...], preferred_element_type=F32)


def _mix(o_list, lse_list, proj, gates, x2d, head_expand, wa, wc, wo, conv_w, seq, tm=512):
    T, D = x2d.shape
    W = ATTN_WIDTH
    row = lambda width, col: pl.BlockSpec((tm, width), lambda i: (i, col))
    halo = lambda col: pl.BlockSpec(
        (CONV_HALO, D), lambda i: (jnp.maximum(i * (tm // CONV_HALO) - 1, 0), col))
    full = lambda a: pl.BlockSpec(a.shape, lambda i: (0,) * a.ndim)
    return pl.pallas_call(
        functools.partial(_mix_kernel, tm=tm, seq=seq),
        grid=(T // tm,),
        in_specs=[row(W, 0)] * 3 + [row(LANES, 0)] * 3
                 + [row(D, 0), row(D, 1), row(D, 2), halo(1), halo(2)]
                 + [row(D, 0), row(D, 1), row(D, 0)]
                 + [full(head_expand), full(wa), full(wc), full(wo), full(conv_w)],
        out_specs=pl.BlockSpec((tm, D), lambda i: (i, 0)),
        out_shape=jax.ShapeDtypeStruct((T, D), F32),
        scratch_shapes=[pltpu.VMEM((tm + CONV_HALO, D), F32)],
        compiler_params=pltpu.CompilerParams(
            dimension_semantics=("parallel",), vmem_limit_bytes=VMEM_LIMIT),
        name="branch_mix",
    )(*o_list, *lse_list, proj, proj, proj, proj, proj, gates, gates, x2d,
      head_expand, wa, wc, wo, conv_w)


def _cross_kernel(h_ref, k_ref, v_ref, g2_ref, wq_ref, wo_ref, g3_ref, rw_ref, rb_ref, tri_ref,
                  h2_ref, u3_ref, eidx_ref, rank_ref, wts_ref, cnt_ref, run_scr, *, tm):
    @pl.when(pl.program_id(0) == 0)
    def _():
        run_scr[...] = jnp.zeros_like(run_scr)

    h = h_ref[...]
    u = _rms(h, g2_ref[...]).astype(BF16)
    q = jnp.dot(u, wq_ref[...], preferred_element_type=F32) * (MEM_HEAD_DIM ** -0.5)
    heads = []
    for hd in range(MEM_HEADS):
        cols = slice(hd * MEM_HEAD_DIM, (hd + 1) * MEM_HEAD_DIM)
        s = lax.dot_general(q[:, cols].astype(BF16), k_ref[:, cols], (((1,), (1,)), ((), ())),
                            preferred_element_type=F32)
        m = jnp.max(s, axis=1, keepdims=True)
        p = jnp.exp(s - m)
        l = jnp.sum(p, axis=1, keepdims=True)
        heads.append(jnp.dot(p.astype(BF16), v_ref[:, cols], preferred_element_type=F32) * (1.0 / l))
    o = jnp.concatenate(heads, axis=1).astype(BF16)
    h2 = h + jnp.dot(o, wo_ref[...], preferred_element_type=F32)
    h2_ref[...] = h2
    u3 = _rms(h2, g3_ref[...])
    u3_ref[...] = u3

    logits = jnp.dot(u3.astype(BF16), rw_ref[...], preferred_element_type=F32) + rb_ref[...]
    lane = lax.broadcasted_iota(I32, (tm, LANES), 1).astype(F32)
    work = logits
    picked = jnp.zeros((tm, LANES), F32)
    tops, idxs = [], []
    for _ in range(TOP_K):
        mk = jnp.max(work, axis=1, keepdims=True)
        ik = jnp.min(jnp.where(work == mk, lane, float(LANES)), axis=1, keepdims=True)
        sel = lane == ik
        work = jnp.where(sel, -jnp.inf, work)
        picked = picked + sel.astype(F32)
        tops.append(mk)
        idxs.append(ik)
    exps = [jnp.exp(t - tops[0]) for t in tops]
    inv = 1.0 / (exps[0] + exps[1] + exps[2] + exps[3])

    before = jnp.dot(tri_ref[...], picked.astype(BF16), preferred_element_type=F32) + run_scr[...]
    eidx = jnp.zeros((tm, LANES), I32)
    rank = jnp.zeros((tm, LANES), I32)
    wts = jnp.zeros((tm, LANES), F32)
    for k in range(TOP_K):
        rk = jnp.sum(jnp.where(lane == idxs[k], before, 0.0), axis=1, keepdims=True)
        eidx = jnp.where(lane == k, idxs[k].astype(I32), eidx)
        rank = jnp.where(lane == k, rk.astype(I32), rank)
        wts = jnp.where(lane == k, exps[k] * inv, wts)
    eidx_ref[...] = eidx
    rank_ref[...] = rank
    wts_ref[...] = wts
    run_scr[...] = run_scr[...] + jnp.sum(picked, axis=0, keepdims=True)
    cnt_ref[...] = run_scr[...]


def _cross_router(h1, kv, g2, wq, wo, g3, rw, rb, seq, tm=512):
    T, D = h1.shape
    n_mem = kv.shape[1]
    tri = (jnp.arange(tm)[:, None] > jnp.arange(tm)[None, :]).astype(BF16)
    full = lambda a: pl.BlockSpec(a.shape, lambda i: (0,) * a.ndim)
    kvspec = lambda col: pl.BlockSpec((None, n_mem, MEM_WIDTH), lambda i: ((i * tm) // seq, 0, col))
    tile = lambda width: pl.BlockSpec((tm, width), lambda i: (i, 0))
    return pl.pallas_call(
        functools.partial(_cross_kernel, tm=tm),
        grid=(T // tm,),
        in_specs=[tile(D), kvspec(0), kvspec(1), full(g2), full(wq), full(wo), full(g3),
                  full(rw), full(rb), full(tri)],
        out_specs=[tile(D), tile(D), tile(LANES), tile(LANES), tile(LANES),
                   pl.BlockSpec((1, LANES), lambda i: (0, 0))],
        out_shape=[jax.ShapeDtypeStruct((T, D), F32), jax.ShapeDtypeStruct((T, D), F32),
                   jax.ShapeDtypeStruct((T, LANES), I32), jax.ShapeDtypeStruct((T, LANES), I32),
                   jax.ShapeDtypeStruct((T, LANES), F32), jax.ShapeDtypeStruct((1, LANES), F32)],
        scratch_shapes=[pltpu.VMEM((1, LANES), F32)],
        compiler_params=pltpu.CompilerParams(
            dimension_semantics=("arbitrary",), vmem_limit_bytes=VMEM_LIMIT),
        name="cross_router",
    )(h1, kv, kv, g2, wq, wo, g3, rw, rb, tri)


def _gather_rows(idx_ref, src_hbm, dst, sem, n):
    def body(r, c):
        pltpu.make_async_copy(src_hbm.at[pl.ds(idx_ref[0, r], 1), :], dst.at[pl.ds(r, 1), :], sem).start()
        return c
    lax.fori_loop(0, n, body, 0, unroll=8)


def _wait_rows(src_hbm, dst, sem, n):
    pltpu.make_async_copy(src_hbm.at[pl.ds(0, n), :], dst, sem).wait()


def _expert_kernel(bexp_ref, nblk_ref, idx0_ref, idx1_ref, u_hbm, wg_ref, bg_ref, wu_ref, bu_ref,
                   wd_ref, bd_ref, y_ref, xbuf, wg_s, wu_s, wd_s, sem):
    i = pl.program_id(0)
    nblk = nblk_ref[0]
    slot = i % 2

    @pl.when(jnp.logical_and(i == 0, nblk > 0))
    def _():
        _gather_rows(idx0_ref, u_hbm, xbuf.at[0], sem.at[0], MOE_ROWS)

    @pl.when(i + 1 < nblk)
    def _():
        _gather_rows(idx1_ref, u_hbm, xbuf.at[1 - slot], sem.at[1 - slot], MOE_ROWS)

    changed = jnp.logical_or(i == 0, bexp_ref[i] != bexp_ref[jnp.maximum(i - 1, 0)])

    @pl.when(jnp.logical_and(changed, i < nblk))
    def _():
        wg_s[...] = wg_ref[...].astype(BF16)
        wu_s[...] = wu_ref[...].astype(BF16)
        wd_s[...] = wd_ref[...].astype(BF16)

    @pl.when(i < nblk)
    def _():
        _wait_rows(u_hbm, xbuf.at[slot], sem.at[slot], MOE_ROWS)
        x = xbuf[slot].astype(BF16)
        gate = jnp.minimum(jnp.dot(x, wg_s[...], preferred_element_type=F32) + bg_ref[...], SWIGLU_LIMIT)
        lin = jnp.clip(jnp.dot(x, wu_s[...], preferred_element_type=F32) + bu_ref[...],
                       -SWIGLU_LIMIT, SWIGLU_LIMIT)
        hdn = gate * jax.nn.sigmoid(SWIGLU_ALPHA * gate) * (lin + 1.0)
        y_ref[...] = jnp.dot(hdn.astype(BF16), wd_s[...], preferred_element_type=F32) + bd_ref[...]

    @pl.when(i >= nblk)
    def _():
        y_ref[...] = jnp.zeros_like(y_ref)


def _experts(u3, row_tok, block_expert, n_blocks_used, w_eg, b_eg, w_eu, b_eu, w_ed, b_ed):
    T, D = u3.shape
    nb = row_tok.shape[0] // MOE_ROWS
    idx = row_tok.reshape(nb, 1, MOE_ROWS)
    E, _, F = w_eg.shape
    wspec = lambda shape: pl.BlockSpec((None,) + shape, lambda i, be, nbu: (be[i], 0, 0))
    idxspec = lambda off: pl.BlockSpec(
        (None, 1, MOE_ROWS), lambda i, be, nbu: (jnp.minimum(i + off, nb - 1), 0, 0),
        memory_space=pltpu.SMEM)
    grid_spec = pltpu.PrefetchScalarGridSpec(
        num_scalar_prefetch=2,
        grid=(nb,),
        in_specs=[idxspec(0), idxspec(1), pl.BlockSpec(memory_space=pl.ANY),
                  wspec((D, F)), wspec((1, F)), wspec((D, F)), wspec((1, F)), wspec((F, D)), wspec((1, D))],
        out_specs=pl.BlockSpec((MOE_ROWS, D), lambda i, be, nbu: (i, 0)),
        scratch_shapes=[pltpu.VMEM((2, MOE_ROWS, D), F32),
                        pltpu.VMEM((D, F), BF16), pltpu.VMEM((D, F), BF16), pltpu.VMEM((F, D), BF16),
                        pltpu.SemaphoreType.DMA((2,))],
    )
    return pl.pallas_call(
        _expert_kernel,
        grid_spec=grid_spec,
        out_shape=jax.ShapeDtypeStruct((nb * MOE_ROWS, D), F32),
        compiler_params=pltpu.CompilerParams(
            dimension_semantics=("arbitrary",), vmem_limit_bytes=VMEM_LIMIT),
        name="expert_ffn",
    )(block_expert, n_blocks_used, idx, idx, u3, w_eg, b_eg.reshape(E, 1, F), w_eu, b_eu.reshape(E, 1, F),
      w_ed, b_ed.reshape(E, 1, D))


def _combine_kernel(pos0_ref, pos1_ref, y_hbm, h_ref, w_ref, g_ref, o_ref, ybuf, sem):
    i = pl.program_id(0)
    n = pl.num_programs(0)
    slot = i % 2
    rows = TOP_K * COMBINE_ROWS

    @pl.when(i == 0)
    def _():
        _gather_rows(pos0_ref, y_hbm, ybuf.at[0], sem.at[0], rows)

    @pl.when(i + 1 < n)
    def _():
        _gather_rows(pos1_ref, y_hbm, ybuf.at[1 - slot], sem.at[1 - slot], rows)

    _wait_rows(y_hbm, ybuf.at[slot], sem.at[slot], rows)
    acc = h_ref[...]
    w = w_ref[...]
    for k in range(TOP_K):
        acc = acc + w[:, k:k + 1] * ybuf[slot, pl.ds(k * COMBINE_ROWS, COMBINE_ROWS), :]
    o_ref[...] = _rms(acc, g_ref[...])


def _combine(y_rows, pos_kmajor, h2, wts, g_final):
    T, D = h2.shape
    tm = COMBINE_ROWS
    nt = T // tm
    posspec = lambda off: pl.BlockSpec(
        (None, 1, TOP_K * tm), lambda i: (jnp.minimum(i + off, nt - 1), 0, 0), memory_space=pltpu.SMEM)
    return pl.pallas_call(
        _combine_kernel,
        grid=(nt,),
        in_specs=[posspec(0), posspec(1), pl.BlockSpec(memory_space=pl.ANY),
                  pl.BlockSpec((tm, D), lambda i: (i, 0)), pl.BlockSpec((tm, LANES), lambda i: (i, 0)),
                  pl.BlockSpec((1, D), lambda i: (0, 0))],
        out_specs=pl.BlockSpec((tm, D), lambda i: (i, 0)),
        out_shape=jax.ShapeDtypeStruct((T, D), F32),
        scratch_shapes=[pltpu.VMEM((2, TOP_K * tm, D), F32), pltpu.SemaphoreType.DMA((2,))],
        compiler_params=pltpu.CompilerParams(
            dimension_semantics=("arbitrary",), vmem_limit_bytes=VMEM_LIMIT),
        name="combine_norm",
    )(pos_kmajor, pos_kmajor, y_rows, h2, wts, g_final.reshape(1, D))


def _layer(h, mem, norm_mix, w_in, conv_w, w_branch_attn, w_branch_conv, w_gate, b_gate, w_out,
           norm_cross, norm_mem, w_cq, w_ckv, w_co, norm_moe, router_w, router_b,
           w_eg, b_eg, w_eu, b_eu, w_ed, b_ed, norm_final):
    B, S, D = h.shape
    T = B * S
    W3 = 3 * ATTN_WIDTH
    n_attn = len(ATTN_GROUPS) * W3
    zero_bias = lambda n: jnp.zeros((n,), F32)

    w_in_b = w_in.astype(BF16)
    w_a = jnp.concatenate([w_in_b[:, n_attn:], w_in_b[:, :W3]], axis=1)
    proj = _inproj(h, norm_mix, w_a, zero_bias(w_a.shape[1]), 1, 1024, 768, name="inproj_conv_g1")
    gates = _inproj(h, norm_mix, w_gate.astype(BF16), b_gate, 1, 1024, 1024, act="sigmoid", name="inproj_gates")
    proj2d = proj.reshape(T, -1)
    gates2d = gates.reshape(T, -1)

    o_list, lse_list = [], []
    for g, (window, dil) in enumerate(ATTN_GROUPS):
        assert window // dil == Q_BLOCK
        if g == 0:
            qkv, col0 = proj, 3 * D // ATTN_WIDTH
        else:
            L = S // dil
            qkv = _inproj(h, norm_mix, w_in_b[:, g * W3:(g + 1) * W3], zero_bias(W3), dil,
                          min(L, 1024), W3 if L < 1024 else 768, name=f"inproj_g{g + 1}")
            col0 = 0
        o, lse = _attention(qkv, dil, col0, name=f"dilated_attn_g{g + 1}")
        o_list.append(o)
        lse_list.append(lse)

    head_expand = (jnp.arange(LANES)[:, None] == jnp.arange(ATTN_WIDTH)[None, :] // HEAD_DIM).astype(BF16)
    h1 = _mix(o_list, lse_list, proj2d, gates2d, h.reshape(T, D), head_expand,
              w_branch_attn.astype(BF16), w_branch_conv.astype(BF16), w_out.astype(BF16), conv_w, S)

    kv = _inproj(mem, norm_mem, w_ckv.astype(BF16), zero_bias(2 * MEM_WIDTH), 1, mem.shape[1], 2 * MEM_WIDTH,
                 name="mem_kv")[:, 0]
    rw = jnp.zeros((D, LANES), BF16).at[:, :N_EXPERTS].set(router_w.astype(BF16))
    rb = jnp.full((1, LANES), -jnp.inf, F32).at[0, :N_EXPERTS].set(router_b)
    h2, u3, eidx, rank, wts, counts = _cross_router(
        h1, kv, norm_cross.reshape(1, D), w_cq.astype(BF16), w_co.astype(BF16), norm_moe.reshape(1, D),
        rw, rb, S)

    n_assign = T * TOP_K
    nb = -(-n_assign // MOE_ROWS) + N_EXPERTS
    cnt = counts[0, :N_EXPERTS].astype(I32)
    padded = (cnt + MOE_ROWS - 1) // MOE_ROWS * MOE_ROWS
    pend = jnp.cumsum(padded)
    pstart = pend - padded
    e_tk = eidx[:, :TOP_K]
    pos = pstart[e_tk] + rank[:, :TOP_K]
    row_tok = jnp.zeros((nb * MOE_ROWS,), I32).at[pos.reshape(-1)].set(
        jnp.arange(n_assign, dtype=I32) // TOP_K)
    block_expert = jnp.clip(jnp.searchsorted(pend, jnp.arange(nb, dtype=I32) * MOE_ROWS, side="right"),
                            0, N_EXPERTS - 1).astype(I32)
    n_blocks_used = (pend[-1:] // MOE_ROWS).astype(I32)

    y_rows = _experts(u3, row_tok, block_expert, n_blocks_used, w_eg, b_eg, w_eu, b_eu, w_ed, b_ed)

    pos_kmajor = pos.reshape(T // COMBINE_ROWS, COMBINE_ROWS, TOP_K).transpose(0, 2, 1).reshape(
        T // COMBINE_ROWS, 1, TOP_K * COMBINE_ROWS)
    out = _combine(y_rows, pos_kmajor, h2, wts, norm_final)
    return out.reshape(B, S, D)


def kernel(x, mem, norm_mix, w_in, conv_w, w_branch_attn, w_branch_conv, w_gate, b_gate, w_out, norm_cross, norm_mem, w_cq, w_ckv, w_co, norm_moe, router_w, router_b, w_exp_gate, b_exp_gate, w_exp_up, b_exp_up, w_exp_down, b_exp_down, norm_final):
    depth = norm_mix.shape[0]
    assert depth == 1, "the final norm is fused into the last layer's combine step"
    return _layer(x, mem, norm_mix[0], w_in[0], conv_w[0], w_branch_attn[0], w_branch_conv[0], w_gate[0],
                  b_gate[0], w_out[0], norm_cross[0], norm_mem[0], w_cq[0], w_ckv[0], w_co[0], norm_moe[0],
                  router_w[0], router_b[0], w_exp_gate[0], b_exp_gate[0], w_exp_up[0], b_exp_up[0],
                  w_exp_down[0], b_exp_down[0], norm_final)
```

```python
import functools

import jax
import jax.numpy as jnp
from jax import lax
from jax.experimental import pallas as pl
from jax.experimental.pallas import tpu as pltpu

F32 = jnp.float32
BF16 = jnp.bfloat16
I32 = jnp.int32

D_MODEL = 1024
ATTN_GROUPS = ((128, 1), (512, 4), (2048, 16))
ATTN_HEADS = 8
HEAD_DIM = 64
ATTN_WIDTH = ATTN_HEADS * HEAD_DIM
Q_BLOCK = 128
CONV_K = 3
MEM_HEADS = 4
MEM_HEAD_DIM = 128
MEM_WIDTH = MEM_HEADS * MEM_HEAD_DIM
N_EXPERTS = 32
TOP_K = 4
SWIGLU_LIMIT = 7.0
SWIGLU_ALPHA = 1.702
EPS = 1e-6
NEG_INF = -1e30

LANES = 128
VMEM_LIMIT = 56 * 1024 * 1024

MOE_ROWS = 256
COMBINE_ROWS = 256
CONV_HALO = 16


def _rms(x, g):
    ms = jnp.mean(x * x, axis=-1, keepdims=True)
    return x * lax.rsqrt(ms + EPS) * g


def _norm_permute_kernel(x_ref, g_ref, *refs, dils, tm):
    out_refs, slab = refs[:-1], refs[-1]
    u = _rms(x_ref[...], g_ref[...])
    out_refs[0][...] = u.astype(BF16)
    n_slab = u.shape[1] // LANES
    for c in range(n_slab):
        slab[c] = u[:, c * LANES:(c + 1) * LANES]
    for o_ref, dil in zip(out_refs[1:], dils):
        for r in range(dil):
            for c in range(n_slab):
                o_ref[r, :, c * LANES:(c + 1) * LANES] = (
                    slab[c, pl.ds(r, tm // dil, stride=dil), :].astype(BF16))


def _norm_permute(x, gain, dils, tm=1024):
    B, S, D = x.shape
    outs = [jax.ShapeDtypeStruct((B, 1, S, D), BF16)]
    specs = [pl.BlockSpec((None, None, tm, D), lambda b, i: (b, 0, i, 0))]
    for dil in dils:
        outs.append(jax.ShapeDtypeStruct((B, dil, S // dil, D), BF16))
        specs.append(pl.BlockSpec((None, dil, tm // dil, D), lambda b, i: (b, 0, i, 0)))
    return pl.pallas_call(
        functools.partial(_norm_permute_kernel, dils=dils, tm=tm),
        grid=(B, S // tm),
        in_specs=[pl.BlockSpec((None, tm, D), lambda b, i: (b, i, 0)),
                  pl.BlockSpec((1, D), lambda b, i: (0, 0))],
        out_specs=specs,
        out_shape=outs,
        scratch_shapes=[pltpu.VMEM((D // LANES, tm, LANES), F32)],
        compiler_params=pltpu.CompilerParams(
            dimension_semantics=("parallel", "parallel"), vmem_limit_bytes=VMEM_LIMIT),
        name="norm_permute",
    )(x, gain.reshape(1, D))


def _matmul_kernel(u_ref, w_ref, b_ref, o_ref, *, act):
    acc = jnp.dot(u_ref[...], w_ref[...], preferred_element_type=F32)
    if act == "sigmoid":
        acc = jax.nn.sigmoid(acc + b_ref[...])
    o_ref[...] = acc.astype(o_ref.dtype)


def _inproj(u, w, bias, tm, tn, act=None, name="inproj"):
    B, R, L, D = u.shape
    N = w.shape[1]
    return pl.pallas_call(
        functools.partial(_matmul_kernel, act=act),
        grid=(B, R, L // tm, N // tn),
        in_specs=[
            pl.BlockSpec((None, None, tm, D), lambda b, r, i, j: (b, r, i, 0)),
            pl.BlockSpec((D, tn), lambda b, r, i, j: (0, j)),
            pl.BlockSpec((1, tn), lambda b, r, i, j: (0, j)),
        ],
        out_specs=pl.BlockSpec((None, None, tm, tn), lambda b, r, i, j: (b, r, i, j)),
        out_shape=jax.ShapeDtypeStruct((B, R, L, N), BF16),
        compiler_params=pltpu.CompilerParams(
            dimension_semantics=("parallel", "parallel", "parallel", "parallel"),
            vmem_limit_bytes=VMEM_LIMIT),
        name=name,
    )(u, w, bias.reshape(1, N))


def _mem_kv_kernel(x_ref, g_ref, w_ref, o_ref):
    u = _rms(x_ref[...], g_ref[...]).astype(BF16)
    o_ref[...] = jnp.dot(u, w_ref[...], preferred_element_type=F32).astype(o_ref.dtype)


def _mem_kv(mem, gain, w):
    B, M, D = mem.shape
    N = w.shape[1]
    return pl.pallas_call(
        _mem_kv_kernel,
        grid=(B,),
        in_specs=[pl.BlockSpec((None, M, D), lambda b: (b, 0, 0)),
                  pl.BlockSpec((1, D), lambda b: (0, 0)),
                  pl.BlockSpec((D, N), lambda b: (0, 0))],
        out_specs=pl.BlockSpec((None, M, N), lambda b: (b, 0, 0)),
        out_shape=jax.ShapeDtypeStruct((B, M, N), BF16),
        compiler_params=pltpu.CompilerParams(
            dimension_semantics=("parallel",), vmem_limit_bytes=VMEM_LIMIT),
        name="mem_kv",
    )(mem, gain.reshape(1, D), w)


def _attn_block(q_ref, kp_ref, kc_ref, vp_ref, vc_ref, tab_ref, first, r):
    lane = lax.broadcasted_iota(I32, (Q_BLOCK, LANES), 1)
    low = lane < HEAD_DIM
    lse_tile = jnp.zeros((Q_BLOCK, LANES), F32)
    pairs = []
    for pair in range(ATTN_HEADS // 2):
        cols = slice(pair * LANES, (pair + 1) * LANES)
        q2 = q_ref[r, :, cols] * (HEAD_DIM ** -0.5)
        k2 = jnp.concatenate([kp_ref[r, :, cols], kc_ref[r, :, cols]], axis=0)
        v2 = jnp.concatenate([vp_ref[r, :, cols], vc_ref[r, :, cols]], axis=0)
        outs = []
        for half in range(2):
            h = 2 * pair + half
            keep = low if half == 0 else jnp.logical_not(low)
            qm = jnp.where(keep, q2, jnp.zeros_like(q2))
            s = lax.dot_general(qm, k2, (((1,), (1,)), ((), ())), preferred_element_type=F32)
            s = s + tab_ref[first, h]
            m = jnp.max(s, axis=1, keepdims=True)
            p = jnp.exp(s - m)
            l = jnp.sum(p, axis=1, keepdims=True)
            o = jnp.dot(p.astype(BF16), v2, preferred_element_type=F32) * (1.0 / l)
            outs.append(o)
            lse_tile = jnp.where(lane == h, m + jnp.log(l), lse_tile)
        pairs.append(jnp.where(low, outs[0], outs[1]))
    return pairs, lse_tile


def _attn_kernel(q_ref, kp_ref, kc_ref, vp_ref, vc_ref, tab_ref, o_ref, lse_ref, *scratch, dil):
    first = jnp.minimum(pl.program_id(1), 1)
    blocks = (q_ref, kp_ref, kc_ref, vp_ref, vc_ref, tab_ref)
    n_pair = ATTN_HEADS // 2
    if dil == 1:
        pairs, lse_tile = _attn_block(*blocks, first, 0)
        for p in range(n_pair):
            o_ref[:, p * LANES:(p + 1) * LANES] = pairs[p].astype(o_ref.dtype)
        lse_ref[...] = lse_tile
        return

    o_scr, lse_scr = scratch

    def body(r, carry):
        pairs, lse_tile = _attn_block(*blocks, first, r)
        rows = pl.ds(r, Q_BLOCK, stride=dil)
        for p in range(n_pair):
            o_scr[p, rows, :] = pairs[p]
        lse_scr[rows, :] = lse_tile
        return carry

    lax.fori_loop(0, dil, body, 0)
    for p in range(n_pair):
        o_ref[:, p * LANES:(p + 1) * LANES] = o_scr[p].astype(o_ref.dtype)
    lse_ref[...] = lse_scr[...]


def _attn_bias_table(dil):
    slopes = jnp.power(2.0, -8.0 * jnp.arange(1, ATTN_HEADS + 1, dtype=F32) / ATTN_HEADS)
    iq = jnp.arange(Q_BLOCK)
    ik = jnp.arange(2 * Q_BLOCK)
    dist = iq[:, None] + Q_BLOCK - ik[None, :]
    band = (dist >= 0) & (dist <= Q_BLOCK)
    has_prev = jnp.stack([ik >= Q_BLOCK, jnp.ones_like(ik, dtype=bool)])
    mask = band[None] & has_prev[:, None, :]
    bias = -slopes[:, None, None] * (dist * dil).astype(F32)[None]
    return jnp.where(mask[:, None], bias[None], NEG_INF).astype(F32)


def _attention(qkv, dil, col0, name):
    B, _, L, _ = qkv.shape
    W = ATTN_WIDTH
    nb = L // Q_BLOCK
    span = Q_BLOCK * dil
    blk = lambda part, prev: pl.BlockSpec(
        (None, dil, Q_BLOCK, W),
        (lambda b, n: (b, 0, jnp.maximum(n - 1, 0), col0 + part)) if prev
        else (lambda b, n: (b, 0, n, col0 + part)))
    tab = _attn_bias_table(dil)
    scratch = [] if dil == 1 else [pltpu.VMEM((ATTN_HEADS // 2, span, LANES), F32),
                                   pltpu.VMEM((span, LANES), F32)]
    o, lse = pl.pallas_call(
        functools.partial(_attn_kernel, dil=dil),
        grid=(B, nb),
        in_specs=[blk(0, False), blk(1, True), blk(1, False), blk(2, True), blk(2, False),
                  pl.BlockSpec(tab.shape, lambda b, n: (0, 0, 0, 0))],
        out_specs=[pl.BlockSpec((None, span, W), lambda b, n: (b, n, 0)),
                   pl.BlockSpec((None, span, LANES), lambda b, n: (b, n, 0))],
        out_shape=[jax.ShapeDtypeStruct((B, L * dil, W), BF16),
                   jax.ShapeDtypeStruct((B, L * dil, LANES), F32)],
        scratch_shapes=scratch,
        compiler_params=pltpu.CompilerParams(
            dimension_semantics=("parallel", "parallel"), vmem_limit_bytes=VMEM_LIMIT),
        name=name,
    )(qkv, qkv, qkv, qkv, qkv, tab)
    return o.reshape(B * L * dil, W), lse.reshape(B * L * dil, LANES)


def _mix_kernel(o1_ref, o2_ref, o3_ref, l1_ref, l2_ref, l3_ref, gb_ref, gc_ref, xc_ref, gch_ref, xch_ref,
                ga_ref, gv_ref, x_ref, e_ref, wa_ref, wc_ref, wo_ref, cw_ref, h_ref, z_scr, *, tm, seq):
    a1, a2, a3 = l1_ref[...], l2_ref[...], l3_ref[...]
    m = jnp.maximum(jnp.maximum(a1, a2), a3)
    e1, e2, e3 = jnp.exp(a1 - m), jnp.exp(a2 - m), jnp.exp(a3 - m)
    inv = 1.0 / (e1 + e2 + e3)
    y = jnp.zeros((tm, ATTN_WIDTH), F32)
    for e, o_ref in ((e1, o1_ref), (e2, o2_ref), (e3, o3_ref)):
        w = e * inv
        w_hi = w.astype(BF16)
        w_lo = (w - w_hi.astype(F32)).astype(BF16)
        w_heads = (jnp.dot(w_hi, e_ref[...], preferred_element_type=F32)
                   + jnp.dot(w_lo, e_ref[...], preferred_element_type=F32))
        y = y + w_heads * o_ref[...].astype(F32)
    branch_a = jnp.dot(y.astype(BF16), wa_ref[...], preferred_element_type=F32)

    at_start = (pl.program_id(0) * tm) % seq == 0
    halo = gch_ref[...].astype(F32) * xch_ref[...].astype(F32)
    z_scr[0:CONV_HALO, :] = jnp.where(at_start, jnp.zeros_like(halo), halo)
    z_scr[CONV_HALO:, :] = gc_ref[...].astype(F32) * xc_ref[...].astype(F32)
    conv = (cw_ref[0:1, :] * z_scr[pl.ds(CONV_HALO - 2, tm), :]
            + cw_ref[1:2, :] * z_scr[pl.ds(CONV_HALO - 1, tm), :]
            + cw_ref[2:3, :] * z_scr[pl.ds(CONV_HALO, tm), :])
    y_conv = gb_ref[...].astype(F32) * conv
    branch_c = jnp.dot(y_conv.astype(BF16), wc_ref[...], preferred_element_type=F32)

    mixed = ga_ref[...].astype(F32) * branch_a + gv_ref[...].astype(F32) * branch_c
    h_ref[...] = x_ref[...] + jnp.dot(mixed.astype(BF16), wo_ref[...], preferred_element_type=F32)


def _mix(o_list, lse_list, proj, gates, x2d, head_expand, wa, wc, wo, conv_w, seq, tm=512):
    T, D = x2d.shape
    W = ATTN_WIDTH
    row = lambda width, col: pl.BlockSpec((tm, width), lambda i: (i, col))
    halo = lambda col: pl.BlockSpec(
        (CONV_HALO, D), lambda i: (jnp.maximum(i * (tm // CONV_HALO) - 1, 0), col))
    full = lambda a: pl.BlockSpec(a.shape, lambda i: (0,) * a.ndim)
    return pl.pallas_call(
        functools.partial(_mix_kernel, tm=tm, seq=seq),
        grid=(T // tm,),
        in_specs=[row(W, 0)] * 3 + [row(LANES, 0)] * 3
                 + [row(D, 0), row(D, 1), row(D, 2), halo(1), halo(2)]
                 + [row(D, 0), row(D, 1), row(D, 0)]
                 + [full(head_expand), full(wa), full(wc), full(wo), full(conv_w)],
        out_specs=pl.BlockSpec((tm, D), lambda i: (i, 0)),
        out_shape=jax.ShapeDtypeStruct((T, D), F32),
        scratch_shapes=[pltpu.VMEM((tm + CONV_HALO, D), F32)],
        compiler_params=pltpu.CompilerParams(
            dimension_semantics=("parallel",), vmem_limit_bytes=VMEM_LIMIT),
        name="branch_mix",
    )(*o_list, *lse_list, proj, proj, proj, proj, proj, gates, gates, x2d,
      head_expand, wa, wc, wo, conv_w)


def _cross_kernel(h_ref, k_ref, v_ref, g2_ref, wq_ref, wo_ref, g3_ref, rw_ref, rb_ref, tri_ref,
                  h2_ref, u3_ref, eidx_ref, rank_ref, wts_ref, cnt_ref, run_scr, *, tm):
    @pl.when(pl.program_id(0) == 0)
    def _():
        run_scr[...] = jnp.zeros_like(run_scr)

    h = h_ref[...]
    u = _rms(h, g2_ref[...]).astype(BF16)
    q = jnp.dot(u, wq_ref[...], preferred_element_type=F32) * (MEM_HEAD_DIM ** -0.5)
    heads = []
    for hd in range(MEM_HEADS):
        cols = slice(hd * MEM_HEAD_DIM, (hd + 1) * MEM_HEAD_DIM)
        s = lax.dot_general(q[:, cols].astype(BF16), k_ref[:, cols], (((1,), (1,)), ((), ())),
                            preferred_element_type=F32)
        m = jnp.max(s, axis=1, keepdims=True)
        p = jnp.exp(s - m)
        l = jnp.sum(p, axis=1, keepdims=True)
        heads.append(jnp.dot(p.astype(BF16), v_ref[:, cols], preferred_element_type=F32) * (1.0 / l))
    o = jnp.concatenate(heads, axis=1).astype(BF16)
    h2 = h + jnp.dot(o, wo_ref[...], preferred_element_type=F32)
    h2_ref[...] = h2
    u3 = _rms(h2, g3_ref[...])
    u3_ref[...] = u3

    logits = jnp.dot(u3.astype(BF16), rw_ref[...], preferred_element_type=F32) + rb_ref[...]
    lane = lax.broadcasted_iota(I32, (tm, LANES), 1).astype(F32)
    work = logits
    picked = jnp.zeros((tm, LANES), F32)
    tops, idxs = [], []
    for _ in range(TOP_K):
        mk = jnp.max(work, axis=1, keepdims=True)
        ik = jnp.min(jnp.where(work == mk, lane, float(LANES)), axis=1, keepdims=True)
        sel = lane == ik
        work = jnp.where(sel, -jnp.inf, work)
        picked = picked + sel.astype(F32)
        tops.append(mk)
        idxs.append(ik)
    exps = [jnp.exp(t - tops[0]) for t in tops]
    inv = 1.0 / (exps[0] + exps[1] + exps[2] + exps[3])

    before = jnp.dot(tri_ref[...], picked.astype(BF16), preferred_element_type=F32) + run_scr[...]
    eidx = jnp.zeros((tm, LANES), I32)
    rank = jnp.zeros((tm, LANES), I32)
    wts = jnp.zeros((tm, LANES), F32)
    for k in range(TOP_K):
        rk = jnp.sum(jnp.where(lane == idxs[k], before, 0.0), axis=1, keepdims=True)
        eidx = jnp.where(lane == k, idxs[k].astype(I32), eidx)
        rank = jnp.where(lane == k, rk.astype(I32), rank)
        wts = jnp.where(lane == k, exps[k] * inv, wts)
    eidx_ref[...] = eidx
    rank_ref[...] = rank
    wts_ref[...] = wts
    run_scr[...] = run_scr[...] + jnp.sum(picked, axis=0, keepdims=True)
    cnt_ref[...] = run_scr[...]


def _cross_router(h1, kv, g2, wq, wo, g3, rw, rb, seq, tm=512):
    T, D = h1.shape
    n_mem = kv.shape[1]
    tri = (jnp.arange(tm)[:, None] > jnp.arange(tm)[None, :]).astype(BF16)
    full = lambda a: pl.BlockSpec(a.shape, lambda i: (0,) * a.ndim)
    kvspec = lambda col: pl.BlockSpec((None, n_mem, MEM_WIDTH), lambda i: ((i * tm) // seq, 0, col))
    tile = lambda width: pl.BlockSpec((tm, width), lambda i: (i, 0))
    return pl.pallas_call(
        functools.partial(_cross_kernel, tm=tm),
        grid=(T // tm,),
        in_specs=[tile(D), kvspec(0), kvspec(1), full(g2), full(wq), full(wo), full(g3),
                  full(rw), full(rb), full(tri)],
        out_specs=[tile(D), tile(D), tile(LANES), tile(LANES), tile(LANES),
                   pl.BlockSpec((1, LANES), lambda i: (0, 0))],
        out_shape=[jax.ShapeDtypeStruct((T, D), F32), jax.ShapeDtypeStruct((T, D), F32),
                   jax.ShapeDtypeStruct((T, LANES), I32), jax.ShapeDtypeStruct((T, LANES), I32),
                   jax.ShapeDtypeStruct((T, LANES), F32), jax.ShapeDtypeStruct((1, LANES), F32)],
        scratch_shapes=[pltpu.VMEM((1, LANES), F32)],
        compiler_params=pltpu.CompilerParams(
            dimension_semantics=("arbitrary",), vmem_limit_bytes=VMEM_LIMIT),
        name="cross_router",
    )(h1, kv, kv, g2, wq, wo, g3, rw, rb, tri)


def _gather_rows(idx_ref, src_hbm, dst, sem, n):
    def body(r, c):
        pltpu.make_async_copy(src_hbm.at[pl.ds(idx_ref[0, r], 1), :], dst.at[pl.ds(r, 1), :], sem).start()
        return c
    lax.fori_loop(0, n, body, 0, unroll=8)


def _wait_rows(src_hbm, dst, sem, n):
    pltpu.make_async_copy(src_hbm.at[pl.ds(0, n), :], dst, sem).wait()


def _expert_kernel(bexp_ref, nblk_ref, idx0_ref, idx1_ref, u_hbm, wg_ref, bg_ref, wu_ref, bu_ref,
                   wd_ref, bd_ref, y_ref, xbuf, wg_s, wu_s, wd_s, sem):
    i = pl.program_id(0)
    nblk = nblk_ref[0]
    slot = i % 2

    @pl.when(jnp.logical_and(i == 0, nblk > 0))
    def _():
        _gather_rows(idx0_ref, u_hbm, xbuf.at[0], sem.at[0], MOE_ROWS)

    @pl.when(i + 1 < nblk)
    def _():
        _gather_rows(idx1_ref, u_hbm, xbuf.at[1 - slot], sem.at[1 - slot], MOE_ROWS)

    changed = jnp.logical_or(i == 0, bexp_ref[i] != bexp_ref[jnp.maximum(i - 1, 0)])

    @pl.when(jnp.logical_and(changed, i < nblk))
    def _():
        wg_s[...] = wg_ref[...].astype(BF16)
        wu_s[...] = wu_ref[...].astype(BF16)
        wd_s[...] = wd_ref[...].astype(BF16)

    @pl.when(i < nblk)
    def _():
        _wait_rows(u_hbm, xbuf.at[slot], sem.at[slot], MOE_ROWS)
        x = xbuf[slot].astype(BF16)
        gate = jnp.minimum(jnp.dot(x, wg_s[...], preferred_element_type=F32) + bg_ref[...], SWIGLU_LIMIT)
        lin = jnp.clip(jnp.dot(x, wu_s[...], preferred_element_type=F32) + bu_ref[...],
                       -SWIGLU_LIMIT, SWIGLU_LIMIT)
        hdn = gate * jax.nn.sigmoid(SWIGLU_ALPHA * gate) * (lin + 1.0)
        y_ref[...] = jnp.dot(hdn.astype(BF16), wd_s[...], preferred_element_type=F32) + bd_ref[...]

    @pl.when(i >= nblk)
    def _():
        y_ref[...] = jnp.zeros_like(y_ref)


def _experts(u3, row_tok, block_expert, n_blocks_used, w_eg, b_eg, w_eu, b_eu, w_ed, b_ed):
    T, D = u3.shape
    nb = row_tok.shape[0] // MOE_ROWS
    idx = row_tok.reshape(nb, 1, MOE_ROWS)
    E, _, F = w_eg.shape
    wspec = lambda shape: pl.BlockSpec((None,) + shape, lambda i, be, nbu: (be[i], 0, 0))
    idxspec = lambda off: pl.BlockSpec(
        (None, 1, MOE_ROWS), lambda i, be, nbu: (jnp.minimum(i + off, nb - 1), 0, 0),
        memory_space=pltpu.SMEM)
    grid_spec = pltpu.PrefetchScalarGridSpec(
        num_scalar_prefetch=2,
        grid=(nb,),
        in_specs=[idxspec(0), idxspec(1), pl.BlockSpec(memory_space=pl.ANY),
                  wspec((D, F)), wspec((1, F)), wspec((D, F)), wspec((1, F)), wspec((F, D)), wspec((1, D))],
        out_specs=pl.BlockSpec((MOE_ROWS, D), lambda i, be, nbu: (i, 0)),
        scratch_shapes=[pltpu.VMEM((2, MOE_ROWS, D), F32),
                        pltpu.VMEM((D, F), BF16), pltpu.VMEM((D, F), BF16), pltpu.VMEM((F, D), BF16),
                        pltpu.SemaphoreType.DMA((2,))],
    )
    return pl.pallas_call(
        _expert_kernel,
        grid_spec=grid_spec,
        out_shape=jax.ShapeDtypeStruct((nb * MOE_ROWS, D), F32),
        compiler_params=pltpu.CompilerParams(
            dimension_semantics=("arbitrary",), vmem_limit_bytes=VMEM_LIMIT),
        name="expert_ffn",
    )(block_expert, n_blocks_used, idx, idx, u3, w_eg, b_eg.reshape(E, 1, F), w_eu, b_eu.reshape(E, 1, F),
      w_ed, b_ed.reshape(E, 1, D))


def _combine_kernel(pos0_ref, pos1_ref, y_hbm, h_ref, w_ref, g_ref, o_ref, ybuf, sem):
    i = pl.program_id(0)
    n = pl.num_programs(0)
    slot = i % 2
    rows = TOP_K * COMBINE_ROWS

    @pl.when(i == 0)
    def _():
        _gather_rows(pos0_ref, y_hbm, ybuf.at[0], sem.at[0], rows)

    @pl.when(i + 1 < n)
    def _():
        _gather_rows(pos1_ref, y_hbm, ybuf.at[1 - slot], sem.at[1 - slot], rows)

    _wait_rows(y_hbm, ybuf.at[slot], sem.at[slot], rows)
    acc = h_ref[...]
    w = w_ref[...]
    for k in range(TOP_K):
        acc = acc + w[:, k:k + 1] * ybuf[slot, pl.ds(k * COMBINE_ROWS, COMBINE_ROWS), :]
    o_ref[...] = _rms(acc, g_ref[...])


def _combine(y_rows, pos_kmajor, h2, wts, g_final):
    T, D = h2.shape
    tm = COMBINE_ROWS
    nt = T // tm
    posspec = lambda off: pl.BlockSpec(
        (None, 1, TOP_K * tm), lambda i: (jnp.minimum(i + off, nt - 1), 0, 0), memory_space=pltpu.SMEM)
    return pl.pallas_call(
        _combine_kernel,
        grid=(nt,),
        in_specs=[posspec(0), posspec(1), pl.BlockSpec(memory_space=pl.ANY),
                  pl.BlockSpec((tm, D), lambda i: (i, 0)), pl.BlockSpec((tm, LANES), lambda i: (i, 0)),
                  pl.BlockSpec((1, D), lambda i: (0, 0))],
        out_specs=pl.BlockSpec((tm, D), lambda i: (i, 0)),
        out_shape=jax.ShapeDtypeStruct((T, D), F32),
        scratch_shapes=[pltpu.VMEM((2, TOP_K * tm, D), F32), pltpu.SemaphoreType.DMA((2,))],
        compiler_params=pltpu.CompilerParams(
            dimension_semantics=("arbitrary",), vmem_limit_bytes=VMEM_LIMIT),
        name="combine_norm",
    )(pos_kmajor, pos_kmajor, y_rows, h2, wts, g_final.reshape(1, D))


def _layer(h, mem, norm_mix, w_in, conv_w, w_branch_attn, w_branch_conv, w_gate, b_gate, w_out,
           norm_cross, norm_mem, w_cq, w_ckv, w_co, norm_moe, router_w, router_b,
           w_eg, b_eg, w_eu, b_eu, w_ed, b_ed, norm_final):
    B, S, D = h.shape
    T = B * S
    W3 = 3 * ATTN_WIDTH
    n_attn = len(ATTN_GROUPS) * W3
    zero_bias = lambda n: jnp.zeros((n,), F32)

    dils = tuple(dil for _, dil in ATTN_GROUPS)
    assert dils[0] == 1
    u_all = _norm_permute(h, norm_mix, dils[1:])
    w_in_b = w_in.astype(BF16)
    w_a = jnp.concatenate([w_in_b[:, n_attn:], w_in_b[:, :W3]], axis=1)
    proj = _inproj(u_all[0], w_a, zero_bias(w_a.shape[1]), 1024, 768, name="inproj_conv_g1")
    gates = _inproj(u_all[0], w_gate.astype(BF16), b_gate, 1024, 1024, act="sigmoid", name="inproj_gates")
    proj2d = proj.reshape(T, -1)
    gates2d = gates.reshape(T, -1)

    o_list, lse_list = [], []
    for g, (window, dil) in enumerate(ATTN_GROUPS):
        assert window // dil == Q_BLOCK
        if g == 0:
            qkv, col0 = proj, 3 * D // ATTN_WIDTH
        else:
            L = S // dil
            qkv = _inproj(u_all[g], w_in_b[:, g * W3:(g + 1) * W3], zero_bias(W3),
                          min(L, 1024), W3 if L < 1024 else 768, name=f"inproj_g{g + 1}")
            col0 = 0
        o, lse = _attention(qkv, dil, col0, name=f"dilated_attn_g{g + 1}")
        o_list.append(o)
        lse_list.append(lse)

    head_expand = (jnp.arange(LANES)[:, None] == jnp.arange(ATTN_WIDTH)[None, :] // HEAD_DIM).astype(BF16)
    h1 = _mix(o_list, lse_list, proj2d, gates2d, h.reshape(T, D), head_expand,
              w_branch_attn.astype(BF16), w_branch_conv.astype(BF16), w_out.astype(BF16), conv_w, S)

    kv = _mem_kv(mem, norm_mem, w_ckv.astype(BF16))
    rw = jnp.zeros((D, LANES), BF16).at[:, :N_EXPERTS].set(router_w.astype(BF16))
    rb = jnp.full((1, LANES), -jnp.inf, F32).at[0, :N_EXPERTS].set(router_b)
    h2, u3, eidx, rank, wts, counts = _cross_router(
        h1, kv, norm_cross.reshape(1, D), w_cq.astype(BF16), w_co.astype(BF16), norm_moe.reshape(1, D),
        rw, rb, S)

    n_assign = T * TOP_K
    nb = -(-n_assign // MOE_ROWS) + N_EXPERTS
    cnt = counts[0, :N_EXPERTS].astype(I32)
    padded = (cnt + MOE_ROWS - 1) // MOE_ROWS * MOE_ROWS
    pend = jnp.cumsum(padded)
    pstart = pend - padded
    e_tk = eidx[:, :TOP_K]
    pos = pstart[e_tk] + rank[:, :TOP_K]
    row_tok = jnp.zeros((nb * MOE_ROWS,), I32).at[pos.reshape(-1)].set(
        jnp.arange(n_assign, dtype=I32) // TOP_K)
    block_start = jnp.arange(nb, dtype=I32) * MOE_ROWS
    block_expert = jnp.minimum(jnp.sum(block_start[:, None] >= pend[None, :], axis=1), N_EXPERTS - 1).astype(I32)
    n_blocks_used = (pend[-1:] // MOE_ROWS).astype(I32)

    y_rows = _experts(u3, row_tok, block_expert, n_blocks_used, w_eg, b_eg, w_eu, b_eu, w_ed, b_ed)

    pos_kmajor = pos.reshape(T // COMBINE_ROWS, COMBINE_ROWS, TOP_K).transpose(0, 2, 1).reshape(
        T // COMBINE_ROWS, 1, TOP_K * COMBINE_ROWS)
    out = _combine(y_rows, pos_kmajor, h2, wts, norm_final)
    return out.reshape(B, S, D)


def kernel(x, mem, norm_mix, w_in, conv_w, w_branch_attn, w_branch_conv, w_gate, b_gate, w_out, norm_cross, norm_mem, w_cq, w_ckv, w_co, norm_moe, router_w, router_b, w_exp_gate, b_exp_gate, w_exp_up, b_exp_up, w_exp_down, b_exp_down, norm_final):
    depth = norm_mix.shape[0]
    assert depth == 1, "the final norm is fused into the last layer's combine step"
    return _layer(x, mem, norm_mix[0], w_in[0], conv_w[0], w_branch_attn[0], w_branch_conv[0], w_gate[0],
                  b_gate[0], w_out[0], norm_cross[0], norm_mem[0], w_cq[0], w_ckv[0], w_co[0], norm_moe[0],
                  router_w[0], router_b[0], w_exp_gate[0], b_exp_gate[0], w_exp_up[0], b_exp_up[0],
                  w_exp_down[0], b_exp_down[0], norm_final)
```

```python
import functools

import jax
import jax.numpy as jnp
from jax import lax
from jax.experimental import pallas as pl
from jax.experimental.pallas import tpu as pltpu

F32 = jnp.float32
BF16 = jnp.bfloat16
I32 = jnp.int32

D_MODEL = 1024
ATTN_GROUPS = ((128, 1), (512, 4), (2048, 16))
ATTN_HEADS = 8
HEAD_DIM = 64
ATTN_WIDTH = ATTN_HEADS * HEAD_DIM
Q_BLOCK = 128
CONV_K = 3
MEM_HEADS = 4
MEM_HEAD_DIM = 128
MEM_WIDTH = MEM_HEADS * MEM_HEAD_DIM
N_EXPERTS = 32
TOP_K = 4
SWIGLU_LIMIT = 7.0
SWIGLU_ALPHA = 1.702
EPS = 1e-6
NEG_INF = -1e30

LANES = 128
VMEM_LIMIT = 56 * 1024 * 1024

MOE_ROWS = 256
COMBINE_ROWS = 256
CONV_HALO = 16


def _rms(x, g):
    ms = jnp.mean(x * x, axis=-1, keepdims=True)
    return x * lax.rsqrt(ms + EPS) * g


PACK_ROWS = D_MODEL // LANES


def _pack_rows(y, out_ref, n):
    for c in range(PACK_ROWS):
        out_ref[pl.ds(c, n, stride=PACK_ROWS), :] = y[:, c * LANES:(c + 1) * LANES]


def _unpack_rows(ref, start, n):
    return jnp.concatenate(
        [ref[pl.ds(start + c, n, stride=PACK_ROWS), :] for c in range(PACK_ROWS)], axis=1)


def _norm_permute_kernel(x_ref, g_ref, *refs, dils, tm):
    out_refs, slab = refs[:-1], refs[-1]
    u = _rms(x_ref[...], g_ref[...])
    out_refs[0][...] = u.astype(BF16)
    n_slab = u.shape[1] // LANES
    for c in range(n_slab):
        slab[c] = u[:, c * LANES:(c + 1) * LANES]
    for o_ref, dil in zip(out_refs[1:], dils):
        for r in range(dil):
            for c in range(n_slab):
                o_ref[r, :, c * LANES:(c + 1) * LANES] = (
                    slab[c, pl.ds(r, tm // dil, stride=dil), :].astype(BF16))


def _norm_permute(x, gain, dils, tm=1024):
    B, S, D = x.shape
    outs = [jax.ShapeDtypeStruct((B, 1, S, D), BF16)]
    specs = [pl.BlockSpec((None, None, tm, D), lambda b, i: (b, 0, i, 0))]
    for dil in dils:
        outs.append(jax.ShapeDtypeStruct((B, dil, S // dil, D), BF16))
        specs.append(pl.BlockSpec((None, dil, tm // dil, D), lambda b, i: (b, 0, i, 0)))
    return pl.pallas_call(
        functools.partial(_norm_permute_kernel, dils=dils, tm=tm),
        grid=(B, S // tm),
        in_specs=[pl.BlockSpec((None, tm, D), lambda b, i: (b, i, 0)),
                  pl.BlockSpec((1, D), lambda b, i: (0, 0))],
        out_specs=specs,
        out_shape=outs,
        scratch_shapes=[pltpu.VMEM((D // LANES, tm, LANES), F32)],
        compiler_params=pltpu.CompilerParams(
            dimension_semantics=("parallel", "parallel"), vmem_limit_bytes=VMEM_LIMIT),
        name="norm_permute",
    )(x, gain.reshape(1, D))


def _matmul_kernel(u_ref, w_ref, b_ref, o_ref, *, act):
    acc = jnp.dot(u_ref[...], w_ref[...], preferred_element_type=F32)
    if act == "sigmoid":
        acc = jax.nn.sigmoid(acc + b_ref[...])
    o_ref[...] = acc.astype(o_ref.dtype)


def _inproj(u, w, bias, tm, tn, act=None, name="inproj"):
    B, R, L, D = u.shape
    N = w.shape[1]
    return pl.pallas_call(
        functools.partial(_matmul_kernel, act=act),
        grid=(B, R, L // tm, N // tn),
        in_specs=[
            pl.BlockSpec((None, None, tm, D), lambda b, r, i, j: (b, r, i, 0)),
            pl.BlockSpec((D, tn), lambda b, r, i, j: (0, j)),
            pl.BlockSpec((1, tn), lambda b, r, i, j: (0, j)),
        ],
        out_specs=pl.BlockSpec((None, None, tm, tn), lambda b, r, i, j: (b, r, i, j)),
        out_shape=jax.ShapeDtypeStruct((B, R, L, N), BF16),
        compiler_params=pltpu.CompilerParams(
            dimension_semantics=("parallel", "parallel", "parallel", "parallel"),
            vmem_limit_bytes=VMEM_LIMIT),
        name=name,
    )(u, w, bias.reshape(1, N))


def _mem_kv_kernel(x_ref, g_ref, w_ref, o_ref):
    u = _rms(x_ref[...], g_ref[...]).astype(BF16)
    o_ref[...] = jnp.dot(u, w_ref[...], preferred_element_type=F32).astype(o_ref.dtype)


def _mem_kv(mem, gain, w):
    B, M, D = mem.shape
    N = w.shape[1]
    return pl.pallas_call(
        _mem_kv_kernel,
        grid=(B,),
        in_specs=[pl.BlockSpec((None, M, D), lambda b: (b, 0, 0)),
                  pl.BlockSpec((1, D), lambda b: (0, 0)),
                  pl.BlockSpec((D, N), lambda b: (0, 0))],
        out_specs=pl.BlockSpec((None, M, N), lambda b: (b, 0, 0)),
        out_shape=jax.ShapeDtypeStruct((B, M, N), BF16),
        compiler_params=pltpu.CompilerParams(
            dimension_semantics=("parallel",), vmem_limit_bytes=VMEM_LIMIT),
        name="mem_kv",
    )(mem, gain.reshape(1, D), w)


def _attn_block(q_ref, kp_ref, kc_ref, vp_ref, vc_ref, tab_ref, first, r):
    lane = lax.broadcasted_iota(I32, (Q_BLOCK, LANES), 1)
    low = lane < HEAD_DIM
    lse_tile = jnp.zeros((Q_BLOCK, LANES), F32)
    pairs = []
    for pair in range(ATTN_HEADS // 2):
        cols = slice(pair * LANES, (pair + 1) * LANES)
        q2 = q_ref[r, :, cols] * (HEAD_DIM ** -0.5)
        k2 = jnp.concatenate([kp_ref[r, :, cols], kc_ref[r, :, cols]], axis=0)
        v2 = jnp.concatenate([vp_ref[r, :, cols], vc_ref[r, :, cols]], axis=0)
        outs = []
        for half in range(2):
            h = 2 * pair + half
            keep = low if half == 0 else jnp.logical_not(low)
            qm = jnp.where(keep, q2, jnp.zeros_like(q2))
            s = lax.dot_general(qm, k2, (((1,), (1,)), ((), ())), preferred_element_type=F32)
            s = s + tab_ref[first, h]
            m = jnp.max(s, axis=1, keepdims=True)
            p = jnp.exp(s - m)
            l = jnp.sum(p, axis=1, keepdims=True)
            o = jnp.dot(p.astype(BF16), v2, preferred_element_type=F32) * (1.0 / l)
            outs.append(o)
            lse_tile = jnp.where(lane == h, m + jnp.log(l), lse_tile)
        pairs.append(jnp.where(low, outs[0], outs[1]))
    return pairs, lse_tile


def _attn_kernel(q_ref, kp_ref, kc_ref, vp_ref, vc_ref, tab_ref, o_ref, lse_ref, *scratch, dil):
    first = jnp.minimum(pl.program_id(1), 1)
    blocks = (q_ref, kp_ref, kc_ref, vp_ref, vc_ref, tab_ref)
    n_pair = ATTN_HEADS // 2
    if dil == 1:
        pairs, lse_tile = _attn_block(*blocks, first, 0)
        for p in range(n_pair):
            o_ref[:, p * LANES:(p + 1) * LANES] = pairs[p].astype(o_ref.dtype)
        lse_ref[...] = lse_tile
        return

    o_scr, lse_scr = scratch

    def body(r, carry):
        pairs, lse_tile = _attn_block(*blocks, first, r)
        rows = pl.ds(r, Q_BLOCK, stride=dil)
        for p in range(n_pair):
            o_scr[p, rows, :] = pairs[p]
        lse_scr[rows, :] = lse_tile
        return carry

    lax.fori_loop(0, dil, body, 0)
    for p in range(n_pair):
        o_ref[:, p * LANES:(p + 1) * LANES] = o_scr[p].astype(o_ref.dtype)
    lse_ref[...] = lse_scr[...]


def _attn_bias_table(dil):
    slopes = jnp.power(2.0, -8.0 * jnp.arange(1, ATTN_HEADS + 1, dtype=F32) / ATTN_HEADS)
    iq = jnp.arange(Q_BLOCK)
    ik = jnp.arange(2 * Q_BLOCK)
    dist = iq[:, None] + Q_BLOCK - ik[None, :]
    band = (dist >= 0) & (dist <= Q_BLOCK)
    has_prev = jnp.stack([ik >= Q_BLOCK, jnp.ones_like(ik, dtype=bool)])
    mask = band[None] & has_prev[:, None, :]
    bias = -slopes[:, None, None] * (dist * dil).astype(F32)[None]
    return jnp.where(mask[:, None], bias[None], NEG_INF).astype(F32)


def _attention(qkv, dil, col0, name):
    B, _, L, _ = qkv.shape
    W = ATTN_WIDTH
    nb = L // Q_BLOCK
    span = Q_BLOCK * dil
    blk = lambda part, prev: pl.BlockSpec(
        (None, dil, Q_BLOCK, W),
        (lambda b, n: (b, 0, jnp.maximum(n - 1, 0), col0 + part)) if prev
        else (lambda b, n: (b, 0, n, col0 + part)))
    tab = _attn_bias_table(dil)
    scratch = [] if dil == 1 else [pltpu.VMEM((ATTN_HEADS // 2, span, LANES), F32),
                                   pltpu.VMEM((span, LANES), F32)]
    o, lse = pl.pallas_call(
        functools.partial(_attn_kernel, dil=dil),
        grid=(B, nb),
        in_specs=[blk(0, False), blk(1, True), blk(1, False), blk(2, True), blk(2, False),
                  pl.BlockSpec(tab.shape, lambda b, n: (0, 0, 0, 0))],
        out_specs=[pl.BlockSpec((None, span, W), lambda b, n: (b, n, 0)),
                   pl.BlockSpec((None, span, LANES), lambda b, n: (b, n, 0))],
        out_shape=[jax.ShapeDtypeStruct((B, L * dil, W), BF16),
                   jax.ShapeDtypeStruct((B, L * dil, LANES), F32)],
        scratch_shapes=scratch,
        compiler_params=pltpu.CompilerParams(
            dimension_semantics=("parallel", "parallel"), vmem_limit_bytes=VMEM_LIMIT),
        name=name,
    )(qkv, qkv, qkv, qkv, qkv, tab)
    return o.reshape(B * L * dil, W), lse.reshape(B * L * dil, LANES)


def _mix_kernel(o1_ref, o2_ref, o3_ref, l1_ref, l2_ref, l3_ref, gb_ref, gc_ref, xc_ref, gch_ref, xch_ref,
                ga_ref, gv_ref, x_ref, e_ref, wa_ref, wc_ref, wo_ref, cw_ref, h_ref, z_scr, *, tm, seq):
    a1, a2, a3 = l1_ref[...], l2_ref[...], l3_ref[...]
    m = jnp.maximum(jnp.maximum(a1, a2), a3)
    e1, e2, e3 = jnp.exp(a1 - m), jnp.exp(a2 - m), jnp.exp(a3 - m)
    inv = 1.0 / (e1 + e2 + e3)
    y = jnp.zeros((tm, ATTN_WIDTH), F32)
    for e, o_ref in ((e1, o1_ref), (e2, o2_ref), (e3, o3_ref)):
        w = e * inv
        w_hi = w.astype(BF16)
        w_lo = (w - w_hi.astype(F32)).astype(BF16)
        w_heads = (jnp.dot(w_hi, e_ref[...], preferred_element_type=F32)
                   + jnp.dot(w_lo, e_ref[...], preferred_element_type=F32))
        y = y + w_heads * o_ref[...].astype(F32)
    branch_a = jnp.dot(y.astype(BF16), wa_ref[...], preferred_element_type=F32)

    at_start = (pl.program_id(0) * tm) % seq == 0
    halo = gch_ref[...].astype(F32) * xch_ref[...].astype(F32)
    z_scr[0:CONV_HALO, :] = jnp.where(at_start, jnp.zeros_like(halo), halo)
    z_scr[CONV_HALO:, :] = gc_ref[...].astype(F32) * xc_ref[...].astype(F32)
    conv = (cw_ref[0:1, :] * z_scr[pl.ds(CONV_HALO - 2, tm), :]
            + cw_ref[1:2, :] * z_scr[pl.ds(CONV_HALO - 1, tm), :]
            + cw_ref[2:3, :] * z_scr[pl.ds(CONV_HALO, tm), :])
    y_conv = gb_ref[...].astype(F32) * conv
    branch_c = jnp.dot(y_conv.astype(BF16), wc_ref[...], preferred_element_type=F32)

    mixed = ga_ref[...].astype(F32) * branch_a + gv_ref[...].astype(F32) * branch_c
    h_ref[...] = x_ref[...] + jnp.dot(mixed.astype(BF16), wo_ref[...], preferred_element_type=F32)


def _mix(o_list, lse_list, proj, gates, x2d, head_expand, wa, wc, wo, conv_w, seq, tm=512):
    T, D = x2d.shape
    W = ATTN_WIDTH
    row = lambda width, col: pl.BlockSpec((tm, width), lambda i: (i, col))
    halo = lambda col: pl.BlockSpec(
        (CONV_HALO, D), lambda i: (jnp.maximum(i * (tm // CONV_HALO) - 1, 0), col))
    full = lambda a: pl.BlockSpec(a.shape, lambda i: (0,) * a.ndim)
    return pl.pallas_call(
        functools.partial(_mix_kernel, tm=tm, seq=seq),
        grid=(T // tm,),
        in_specs=[row(W, 0)] * 3 + [row(LANES, 0)] * 3
                 + [row(D, 0), row(D, 1), row(D, 2), halo(1), halo(2)]
                 + [row(D, 0), row(D, 1), row(D, 0)]
                 + [full(head_expand), full(wa), full(wc), full(wo), full(conv_w)],
        out_specs=pl.BlockSpec((tm, D), lambda i: (i, 0)),
        out_shape=jax.ShapeDtypeStruct((T, D), F32),
        scratch_shapes=[pltpu.VMEM((tm + CONV_HALO, D), F32)],
        compiler_params=pltpu.CompilerParams(
            dimension_semantics=("parallel",), vmem_limit_bytes=VMEM_LIMIT),
        name="branch_mix",
    )(*o_list, *lse_list, proj, proj, proj, proj, proj, gates, gates, x2d,
      head_expand, wa, wc, wo, conv_w)


def _cross_kernel(h_ref, k_ref, v_ref, g2_ref, wq_ref, wo_ref, g3_ref, rw_ref, rb_ref, tri_ref,
                  h2_ref, u3_ref, eidx_ref, rank_ref, wts_ref, cnt_ref, run_scr, *, tm):
    @pl.when(pl.program_id(0) == 0)
    def _():
        run_scr[...] = jnp.zeros_like(run_scr)

    h = h_ref[...]
    u = _rms(h, g2_ref[...]).astype(BF16)
    q = jnp.dot(u, wq_ref[...], preferred_element_type=F32) * (MEM_HEAD_DIM ** -0.5)
    heads = []
    for hd in range(MEM_HEADS):
        cols = slice(hd * MEM_HEAD_DIM, (hd + 1) * MEM_HEAD_DIM)
        s = lax.dot_general(q[:, cols].astype(BF16), k_ref[:, cols], (((1,), (1,)), ((), ())),
                            preferred_element_type=F32)
        m = jnp.max(s, axis=1, keepdims=True)
        p = jnp.exp(s - m)
        l = jnp.sum(p, axis=1, keepdims=True)
        heads.append(jnp.dot(p.astype(BF16), v_ref[:, cols], preferred_element_type=F32) * (1.0 / l))
    o = jnp.concatenate(heads, axis=1).astype(BF16)
    h2 = h + jnp.dot(o, wo_ref[...], preferred_element_type=F32)
    h2_ref[...] = h2
    u3 = _rms(h2, g3_ref[...])
    _pack_rows(u3, u3_ref, tm)

    logits = jnp.dot(u3.astype(BF16), rw_ref[...], preferred_element_type=F32) + rb_ref[...]
    lane = lax.broadcasted_iota(I32, (tm, LANES), 1).astype(F32)
    work = logits
    picked = jnp.zeros((tm, LANES), F32)
    tops, idxs = [], []
    for _ in range(TOP_K):
        mk = jnp.max(work, axis=1, keepdims=True)
        ik = jnp.min(jnp.where(work == mk, lane, float(LANES)), axis=1, keepdims=True)
        sel = lane == ik
        work = jnp.where(sel, -jnp.inf, work)
        picked = picked + sel.astype(F32)
        tops.append(mk)
        idxs.append(ik)
    exps = [jnp.exp(t - tops[0]) for t in tops]
    inv = 1.0 / (exps[0] + exps[1] + exps[2] + exps[3])

    before = jnp.dot(tri_ref[...], picked.astype(BF16), preferred_element_type=F32) + run_scr[...]
    eidx = jnp.zeros((tm, LANES), I32)
    rank = jnp.zeros((tm, LANES), I32)
    wts = jnp.zeros((tm, LANES), F32)
    for k in range(TOP_K):
        rk = jnp.sum(jnp.where(lane == idxs[k], before, 0.0), axis=1, keepdims=True)
        eidx = jnp.where(lane == k, idxs[k].astype(I32), eidx)
        rank = jnp.where(lane == k, rk.astype(I32), rank)
        wts = jnp.where(lane == k, exps[k] * inv, wts)
    eidx_ref[...] = eidx
    rank_ref[...] = rank
    wts_ref[...] = wts
    run_scr[...] = run_scr[...] + jnp.sum(picked, axis=0, keepdims=True)
    cnt_ref[...] = run_scr[...]


def _cross_router(h1, kv, g2, wq, wo, g3, rw, rb, seq, tm=512):
    T, D = h1.shape
    n_mem = kv.shape[1]
    tri = (jnp.arange(tm)[:, None] > jnp.arange(tm)[None, :]).astype(BF16)
    full = lambda a: pl.BlockSpec(a.shape, lambda i: (0,) * a.ndim)
    kvspec = lambda col: pl.BlockSpec((None, n_mem, MEM_WIDTH), lambda i: ((i * tm) // seq, 0, col))
    tile = lambda width: pl.BlockSpec((tm, width), lambda i: (i, 0))
    return pl.pallas_call(
        functools.partial(_cross_kernel, tm=tm),
        grid=(T // tm,),
        in_specs=[tile(D), kvspec(0), kvspec(1), full(g2), full(wq), full(wo), full(g3),
                  full(rw), full(rb), full(tri)],
        out_specs=[tile(D), pl.BlockSpec((tm * PACK_ROWS, LANES), lambda i: (i, 0)),
                   tile(LANES), tile(LANES), tile(LANES),
                   pl.BlockSpec((1, LANES), lambda i: (0, 0))],
        out_shape=[jax.ShapeDtypeStruct((T, D), F32), jax.ShapeDtypeStruct((T * PACK_ROWS, LANES), F32),
                   jax.ShapeDtypeStruct((T, LANES), I32), jax.ShapeDtypeStruct((T, LANES), I32),
                   jax.ShapeDtypeStruct((T, LANES), F32), jax.ShapeDtypeStruct((1, LANES), F32)],
        scratch_shapes=[pltpu.VMEM((1, LANES), F32)],
        compiler_params=pltpu.CompilerParams(
            dimension_semantics=("arbitrary",), vmem_limit_bytes=VMEM_LIMIT),
        name="cross_router",
    )(h1, kv, kv, g2, wq, wo, g3, rw, rb, tri)


def _wait_copies(src_hbm, dst, sem, rows):
    pltpu.make_async_copy(src_hbm.at[pl.ds(0, rows), :], dst.at[pl.ds(0, rows), :], sem).wait()


def _dispatch_kernel(pos_ref, u_hbm, x_hbm, sem, *, tm, n_token_steps):
    i = pl.program_id(0)
    slot = i % 2
    batch = tm * TOP_K * PACK_ROWS
    is_token_step = i < n_token_steps

    group = 4

    def body(g, carry):
        rows = [pos_ref[0, g * (group * TOP_K) + j] for j in range(group * TOP_K)]
        for j, p in enumerate(rows):
            t = jnp.where(is_token_step, i * tm + g * group + j // TOP_K, 0)
            src = u_hbm.at[pl.ds(pl.multiple_of(t * PACK_ROWS, PACK_ROWS), PACK_ROWS), :]
            dst = x_hbm.at[pl.ds(pl.multiple_of(p * PACK_ROWS, PACK_ROWS), PACK_ROWS), :]
            pltpu.make_async_copy(src, dst, sem.at[slot]).start(priority=j % 2)
        return carry

    lax.fori_loop(0, tm // group, body, 0)

    @pl.when(i > 0)
    def _():
        _wait_copies(u_hbm, x_hbm, sem.at[1 - slot], batch)

    @pl.when(i == pl.num_programs(0) - 1)
    def _():
        _wait_copies(u_hbm, x_hbm, sem.at[slot], batch)


def _dispatch(u_packed, pos, fill_rows, tm=256):
    T = pos.shape[0]
    per_step = TOP_K * tm
    dst_rows = jnp.concatenate([pos.reshape(-1), fill_rows])
    n_rows = dst_rows.shape[0]
    assert n_rows % per_step == 0
    steps = n_rows // per_step
    return pl.pallas_call(
        functools.partial(_dispatch_kernel, tm=tm, n_token_steps=T // tm),
        grid=(steps,),
        in_specs=[pl.BlockSpec((None, 1, per_step), lambda i: (i, 0, 0), memory_space=pltpu.SMEM),
                  pl.BlockSpec(memory_space=pl.ANY)],
        out_specs=pl.BlockSpec(memory_space=pl.ANY),
        out_shape=jax.ShapeDtypeStruct((n_rows * PACK_ROWS, LANES), F32),
        scratch_shapes=[pltpu.SemaphoreType.DMA((2,))],
        compiler_params=pltpu.CompilerParams(dimension_semantics=("arbitrary",)),
        name="dispatch_rows",
    )(dst_rows.reshape(steps, 1, per_step), u_packed)


def _expert_kernel(bexp_ref, nused_ref, x_ref, wg_ref, bg_ref, wu_ref, bu_ref,
                   wd_ref, bd_ref, y_ref, wg_s, wu_s, wd_s):
    i = pl.program_id(0)
    used = i < nused_ref[0]
    changed = jnp.logical_or(i == 0, bexp_ref[i] != bexp_ref[jnp.maximum(i - 1, 0)])

    @pl.when(jnp.logical_and(changed, used))
    def _():
        wg_s[...] = wg_ref[...].astype(BF16)
        wu_s[...] = wu_ref[...].astype(BF16)
        wd_s[...] = wd_ref[...].astype(BF16)

    @pl.when(used)
    def _():
        x = _unpack_rows(x_ref, 0, MOE_ROWS).astype(BF16)
        gate = jnp.minimum(jnp.dot(x, wg_s[...], preferred_element_type=F32) + bg_ref[...], SWIGLU_LIMIT)
        lin = jnp.clip(jnp.dot(x, wu_s[...], preferred_element_type=F32) + bu_ref[...],
                       -SWIGLU_LIMIT, SWIGLU_LIMIT)
        hdn = gate * jax.nn.sigmoid(SWIGLU_ALPHA * gate) * (lin + 1.0)
        y = jnp.dot(hdn.astype(BF16), wd_s[...], preferred_element_type=F32) + bd_ref[...]
        _pack_rows(y, y_ref, MOE_ROWS)

    @pl.when(jnp.logical_not(used))
    def _():
        y_ref[...] = jnp.zeros_like(y_ref)


def _experts(x_packed, block_expert, n_used, w_eg, b_eg, w_eu, b_eu, w_ed, b_ed):
    nb = block_expert.shape[0]
    E, D, F = w_eg.shape
    rows = lambda i, be, nu: (i, 0)
    wspec = lambda shape: pl.BlockSpec((None,) + shape, lambda i, be, nu: (be[i], 0, 0))
    grid_spec = pltpu.PrefetchScalarGridSpec(
        num_scalar_prefetch=2,
        grid=(nb,),
        in_specs=[pl.BlockSpec((MOE_ROWS * PACK_ROWS, LANES), rows),
                  wspec((D, F)), wspec((1, F)), wspec((D, F)), wspec((1, F)), wspec((F, D)), wspec((1, D))],
        out_specs=pl.BlockSpec((MOE_ROWS * PACK_ROWS, LANES), rows),
        scratch_shapes=[pltpu.VMEM((D, F), BF16), pltpu.VMEM((D, F), BF16), pltpu.VMEM((F, D), BF16)],
    )
    return pl.pallas_call(
        _expert_kernel,
        grid_spec=grid_spec,
        out_shape=jax.ShapeDtypeStruct(x_packed.shape, F32),
        compiler_params=pltpu.CompilerParams(
            dimension_semantics=("arbitrary",), vmem_limit_bytes=VMEM_LIMIT),
        name="expert_ffn",
    )(block_expert, n_used, x_packed, w_eg, b_eg.reshape(E, 1, F), w_eu,
      b_eu.reshape(E, 1, F), w_ed, b_ed.reshape(E, 1, D))


def _gather_packed(pos_ref, src_hbm, dst, sem, n):
    group = 16

    def body(g, carry):
        rows = [pos_ref[0, g * group + j] for j in range(group)]
        for j, p in enumerate(rows):
            r = g * group + j
            pltpu.make_async_copy(
                src_hbm.at[pl.ds(pl.multiple_of(p * PACK_ROWS, PACK_ROWS), PACK_ROWS), :],
                dst.at[pl.ds(pl.multiple_of(r * PACK_ROWS, PACK_ROWS), PACK_ROWS), :],
                sem).start(priority=j % 2)
        return carry
    lax.fori_loop(0, n // group, body, 0)


def _combine_kernel(pos0_ref, pos1_ref, y_hbm, h_ref, w_ref, g_ref, o_ref, ybuf, sem):
    i = pl.program_id(0)
    n = pl.num_programs(0)
    slot = i % 2
    rows = TOP_K * COMBINE_ROWS

    @pl.when(i == 0)
    def _():
        _gather_packed(pos0_ref, y_hbm, ybuf.at[0], sem.at[0], rows)

    @pl.when(i + 1 < n)
    def _():
        _gather_packed(pos1_ref, y_hbm, ybuf.at[1 - slot], sem.at[1 - slot], rows)

    _wait_copies(y_hbm, ybuf.at[slot], sem.at[slot], rows * PACK_ROWS)
    acc = h_ref[...]
    w = w_ref[...]
    for k in range(TOP_K):
        acc = acc + w[:, k:k + 1] * _unpack_rows(ybuf.at[slot], k * COMBINE_ROWS * PACK_ROWS, COMBINE_ROWS)
    o_ref[...] = _rms(acc, g_ref[...])


def _combine(y_rows, pos_kmajor, h2, wts, g_final):
    T, D = h2.shape
    tm = COMBINE_ROWS
    nt = T // tm
    posspec = lambda off: pl.BlockSpec(
        (None, 1, TOP_K * tm), lambda i: (jnp.minimum(i + off, nt - 1), 0, 0), memory_space=pltpu.SMEM)
    return pl.pallas_call(
        _combine_kernel,
        grid=(nt,),
        in_specs=[posspec(0), posspec(1), pl.BlockSpec(memory_space=pl.ANY),
                  pl.BlockSpec((tm, D), lambda i: (i, 0)), pl.BlockSpec((tm, LANES), lambda i: (i, 0)),
                  pl.BlockSpec((1, D), lambda i: (0, 0))],
        out_specs=pl.BlockSpec((tm, D), lambda i: (i, 0)),
        out_shape=jax.ShapeDtypeStruct((T, D), F32),
        scratch_shapes=[pltpu.VMEM((2, TOP_K * tm * PACK_ROWS, LANES), F32), pltpu.SemaphoreType.DMA((2,))],
        compiler_params=pltpu.CompilerParams(
            dimension_semantics=("arbitrary",), vmem_limit_bytes=VMEM_LIMIT),
        name="combine_norm",
    )(pos_kmajor, pos_kmajor, y_rows, h2, wts, g_final.reshape(1, D))


def _layer(h, mem, norm_mix, w_in, conv_w, w_branch_attn, w_branch_conv, w_gate, b_gate, w_out,
           norm_cross, norm_mem, w_cq, w_ckv, w_co, norm_moe, router_w, router_b,
           w_eg, b_eg, w_eu, b_eu, w_ed, b_ed, norm_final):
    B, S, D = h.shape
    T = B * S
    W3 = 3 * ATTN_WIDTH
    n_attn = len(ATTN_GROUPS) * W3
    zero_bias = lambda n: jnp.zeros((n,), F32)

    dils = tuple(dil for _, dil in ATTN_GROUPS)
    assert dils[0] == 1
    u_all = _norm_permute(h, norm_mix, dils[1:])
    w_in_b = w_in.astype(BF16)
    w_a = jnp.concatenate([w_in_b[:, n_attn:], w_in_b[:, :W3]], axis=1)
    proj = _inproj(u_all[0], w_a, zero_bias(w_a.shape[1]), 1024, 768, name="inproj_conv_g1")
    gates = _inproj(u_all[0], w_gate.astype(BF16), b_gate, 1024, 1024, act="sigmoid", name="inproj_gates")
    proj2d = proj.reshape(T, -1)
    gates2d = gates.reshape(T, -1)

    o_list, lse_list = [], []
    for g, (window, dil) in enumerate(ATTN_GROUPS):
        assert window // dil == Q_BLOCK
        if g == 0:
            qkv, col0 = proj, 3 * D // ATTN_WIDTH
        else:
            L = S // dil
            qkv = _inproj(u_all[g], w_in_b[:, g * W3:(g + 1) * W3], zero_bias(W3),
                          min(L, 1024), W3 if L < 1024 else 768, name=f"inproj_g{g + 1}")
            col0 = 0
        o, lse = _attention(qkv, dil, col0, name=f"dilated_attn_g{g + 1}")
        o_list.append(o)
        lse_list.append(lse)

    head_expand = (jnp.arange(LANES)[:, None] == jnp.arange(ATTN_WIDTH)[None, :] // HEAD_DIM).astype(BF16)
    h1 = _mix(o_list, lse_list, proj2d, gates2d, h.reshape(T, D), head_expand,
              w_branch_attn.astype(BF16), w_branch_conv.astype(BF16), w_out.astype(BF16), conv_w, S)

    kv = _mem_kv(mem, norm_mem, w_ckv.astype(BF16))
    rw = jnp.zeros((D, LANES), BF16).at[:, :N_EXPERTS].set(router_w.astype(BF16))
    rb = jnp.full((1, LANES), -jnp.inf, F32).at[0, :N_EXPERTS].set(router_b)
    h2, u3, eidx, rank, wts, counts = _cross_router(
        h1, kv, norm_cross.reshape(1, D), w_cq.astype(BF16), w_co.astype(BF16), norm_moe.reshape(1, D),
        rw, rb, S)

    nb = -(-(T * TOP_K) // MOE_ROWS) + N_EXPERTS
    cnt = counts[0, :N_EXPERTS].astype(I32)
    padded = (cnt + MOE_ROWS - 1) // MOE_ROWS * MOE_ROWS
    pend = jnp.cumsum(padded)
    pstart = pend - padded
    pos = pstart[eidx[:, :TOP_K]] + rank[:, :TOP_K]
    block_start = jnp.arange(nb, dtype=I32) * MOE_ROWS
    block_expert = jnp.minimum(jnp.sum(block_start[:, None] >= pend[None, :], axis=1), N_EXPERTS - 1).astype(I32)
    n_pad = padded - cnt
    pad_end = jnp.cumsum(n_pad)
    fill = jnp.arange(nb * MOE_ROWS - T * TOP_K, dtype=I32)
    fill_seg = jnp.sum(fill[:, None] >= pad_end[None, :], axis=1)
    seg_first_row = jnp.concatenate([pstart + cnt, pend[-1:]])
    seg_first_fill = jnp.concatenate([pad_end - n_pad, pad_end[-1:]])
    fill_rows = (seg_first_row[fill_seg] + fill - seg_first_fill[fill_seg]).astype(I32)

    x_rows = _dispatch(u3, pos, fill_rows)
    y_rows = _experts(x_rows, block_expert, (pend[-1:] // MOE_ROWS).astype(I32),
                      w_eg, b_eg, w_eu, b_eu, w_ed, b_ed)

    pos_kmajor = pos.reshape(T // COMBINE_ROWS, COMBINE_ROWS, TOP_K).transpose(0, 2, 1).reshape(
        T // COMBINE_ROWS, 1, TOP_K * COMBINE_ROWS)
    out = _combine(y_rows, pos_kmajor, h2, wts, norm_final)
    return out.reshape(B, S, D)


def kernel(x, mem, norm_mix, w_in, conv_w, w_branch_attn, w_branch_conv, w_gate, b_gate, w_out, norm_cross, norm_mem, w_cq, w_ckv, w_co, norm_moe, router_w, router_b, w_exp_gate, b_exp_gate, w_exp_up, b_exp_up, w_exp_down, b_exp_down, norm_final):
    depth = norm_mix.shape[0]
    assert depth == 1, "the final norm is fused into the last layer's combine step"
    return _layer(x, mem, norm_mix[0], w_in[0], conv_w[0], w_branch_attn[0], w_branch_conv[0], w_gate[0],
                  b_gate[0], w_out[0], norm_cross[0], norm_mem[0], w_cq[0], w_ckv[0], w_co[0], norm_moe[0],
                  router_w[0], router_b[0], w_exp_gate[0], b_exp_gate[0], w_exp_up[0], b_exp_up[0],
                  w_exp_down[0], b_exp_down[0], norm_final)
```

```python
import functools

import jax
import jax.numpy as jnp
from jax import lax
from jax.experimental import pallas as pl
from jax.experimental.pallas import tpu as pltpu

F32 = jnp.float32
BF16 = jnp.bfloat16
I32 = jnp.int32

D_MODEL = 1024
ATTN_GROUPS = ((128, 1), (512, 4), (2048, 16))
ATTN_HEADS = 8
HEAD_DIM = 64
ATTN_WIDTH = ATTN_HEADS * HEAD_DIM
Q_BLOCK = 128
CONV_K = 3
MEM_HEADS = 4
MEM_HEAD_DIM = 128
MEM_WIDTH = MEM_HEADS * MEM_HEAD_DIM
N_EXPERTS = 32
TOP_K = 4
SWIGLU_LIMIT = 7.0
SWIGLU_ALPHA = 1.702
EPS = 1e-6
NEG_INF = -1e30

LANES = 128
VMEM_LIMIT = 56 * 1024 * 1024

MOE_ROWS = 256
COMBINE_ROWS = 256
CONV_HALO = 16


def _rms(x, g):
    ms = jnp.mean(x * x, axis=-1, keepdims=True)
    return x * lax.rsqrt(ms + EPS) * g


PACK_ROWS = D_MODEL // LANES


def _pack_rows(y, out_ref, n):
    for c in range(PACK_ROWS):
        out_ref[pl.ds(c, n, stride=PACK_ROWS), :] = y[:, c * LANES:(c + 1) * LANES]


def _unpack_rows(ref, start, n):
    return jnp.concatenate(
        [ref[pl.ds(start + c, n, stride=PACK_ROWS), :] for c in range(PACK_ROWS)], axis=1)


def _norm_permute_kernel(x_ref, g_ref, *refs, dils, tm):
    out_refs, slab = refs[:-1], refs[-1]
    u = _rms(x_ref[...], g_ref[...])
    out_refs[0][...] = u.astype(BF16)
    n_slab = u.shape[1] // LANES
    for c in range(n_slab):
        slab[c] = u[:, c * LANES:(c + 1) * LANES]
    for o_ref, dil in zip(out_refs[1:], dils):
        for r in range(dil):
            for c in range(n_slab):
                o_ref[r, :, c * LANES:(c + 1) * LANES] = (
                    slab[c, pl.ds(r, tm // dil, stride=dil), :].astype(BF16))


def _norm_permute(x, gain, dils, tm=1024):
    B, S, D = x.shape
    outs = [jax.ShapeDtypeStruct((B, 1, S, D), BF16)]
    specs = [pl.BlockSpec((None, None, tm, D), lambda b, i: (b, 0, i, 0))]
    for dil in dils:
        outs.append(jax.ShapeDtypeStruct((B, dil, S // dil, D), BF16))
        specs.append(pl.BlockSpec((None, dil, tm // dil, D), lambda b, i: (b, 0, i, 0)))
    return pl.pallas_call(
        functools.partial(_norm_permute_kernel, dils=dils, tm=tm),
        grid=(B, S // tm),
        in_specs=[pl.BlockSpec((None, tm, D), lambda b, i: (b, i, 0)),
                  pl.BlockSpec((1, D), lambda b, i: (0, 0))],
        out_specs=specs,
        out_shape=outs,
        scratch_shapes=[pltpu.VMEM((D // LANES, tm, LANES), F32)],
        compiler_params=pltpu.CompilerParams(
            dimension_semantics=("parallel", "parallel"), vmem_limit_bytes=VMEM_LIMIT),
        name="norm_permute",
    )(x, gain.reshape(1, D))


def _matmul_kernel(u_ref, w_ref, b_ref, o_ref, *, act):
    acc = jnp.dot(u_ref[...], w_ref[...], preferred_element_type=F32)
    if act == "sigmoid":
        acc = jax.nn.sigmoid(acc + b_ref[...])
    o_ref[...] = acc.astype(o_ref.dtype)


def _inproj(u, w, bias, tm, tn, act=None, name="inproj"):
    B, R, L, D = u.shape
    N = w.shape[1]
    return pl.pallas_call(
        functools.partial(_matmul_kernel, act=act),
        grid=(B, R, L // tm, N // tn),
        in_specs=[
            pl.BlockSpec((None, None, tm, D), lambda b, r, i, j: (b, r, i, 0)),
            pl.BlockSpec((D, tn), lambda b, r, i, j: (0, j)),
            pl.BlockSpec((1, tn), lambda b, r, i, j: (0, j)),
        ],
        out_specs=pl.BlockSpec((None, None, tm, tn), lambda b, r, i, j: (b, r, i, j)),
        out_shape=jax.ShapeDtypeStruct((B, R, L, N), BF16),
        compiler_params=pltpu.CompilerParams(
            dimension_semantics=("parallel", "parallel", "parallel", "parallel"),
            vmem_limit_bytes=VMEM_LIMIT),
        name=name,
    )(u, w, bias.reshape(1, N))


def _mem_kv_kernel(x_ref, g_ref, w_ref, o_ref):
    u = _rms(x_ref[...], g_ref[...]).astype(BF16)
    o_ref[...] = jnp.dot(u, w_ref[...], preferred_element_type=F32).astype(o_ref.dtype)


def _mem_kv(mem, gain, w):
    B, M, D = mem.shape
    N = w.shape[1]
    return pl.pallas_call(
        _mem_kv_kernel,
        grid=(B,),
        in_specs=[pl.BlockSpec((None, M, D), lambda b: (b, 0, 0)),
                  pl.BlockSpec((1, D), lambda b: (0, 0)),
                  pl.BlockSpec((D, N), lambda b: (0, 0))],
        out_specs=pl.BlockSpec((None, M, N), lambda b: (b, 0, 0)),
        out_shape=jax.ShapeDtypeStruct((B, M, N), BF16),
        compiler_params=pltpu.CompilerParams(
            dimension_semantics=("parallel",), vmem_limit_bytes=VMEM_LIMIT),
        name="mem_kv",
    )(mem, gain.reshape(1, D), w)


def _attn_block(q_ref, kp_ref, kc_ref, vp_ref, vc_ref, tab_ref, first, r):
    lane = lax.broadcasted_iota(I32, (Q_BLOCK, LANES), 1)
    low = lane < HEAD_DIM
    lse_tile = jnp.zeros((Q_BLOCK, LANES), F32)
    pairs = []
    for pair in range(ATTN_HEADS // 2):
        cols = slice(pair * LANES, (pair + 1) * LANES)
        q2 = q_ref[r, :, cols] * (HEAD_DIM ** -0.5)
        k2 = jnp.concatenate([kp_ref[r, :, cols], kc_ref[r, :, cols]], axis=0)
        v2 = jnp.concatenate([vp_ref[r, :, cols], vc_ref[r, :, cols]], axis=0)
        outs = []
        for half in range(2):
            h = 2 * pair + half
            keep = low if half == 0 else jnp.logical_not(low)
            qm = jnp.where(keep, q2, jnp.zeros_like(q2))
            s = lax.dot_general(qm, k2, (((1,), (1,)), ((), ())), preferred_element_type=F32)
            s = s + tab_ref[first, h]
            m = jnp.max(s, axis=1, keepdims=True)
            p = jnp.exp(s - m)
            l = jnp.sum(p, axis=1, keepdims=True)
            o = jnp.dot(p.astype(BF16), v2, preferred_element_type=F32) * (1.0 / l)
            outs.append(o)
            lse_tile = jnp.where(lane == h, m + jnp.log(l), lse_tile)
        pairs.append(jnp.where(low, outs[0], outs[1]))
    return pairs, lse_tile


def _attn_kernel(q_ref, kp_ref, kc_ref, vp_ref, vc_ref, tab_ref, o_ref, lse_ref, *scratch, dil):
    first = jnp.minimum(pl.program_id(1), 1)
    blocks = (q_ref, kp_ref, kc_ref, vp_ref, vc_ref, tab_ref)
    n_pair = ATTN_HEADS // 2
    if dil == 1:
        pairs, lse_tile = _attn_block(*blocks, first, 0)
        for p in range(n_pair):
            o_ref[:, p * LANES:(p + 1) * LANES] = pairs[p].astype(o_ref.dtype)
        lse_ref[...] = lse_tile
        return

    o_scr, lse_scr = scratch

    def body(r, carry):
        pairs, lse_tile = _attn_block(*blocks, first, r)
        rows = pl.ds(r, Q_BLOCK, stride=dil)
        for p in range(n_pair):
            o_scr[p, rows, :] = pairs[p]
        lse_scr[rows, :] = lse_tile
        return carry

    lax.fori_loop(0, dil, body, 0)
    for p in range(n_pair):
        o_ref[:, p * LANES:(p + 1) * LANES] = o_scr[p].astype(o_ref.dtype)
    lse_ref[...] = lse_scr[...]


def _attn_bias_table(dil):
    slopes = jnp.power(2.0, -8.0 * jnp.arange(1, ATTN_HEADS + 1, dtype=F32) / ATTN_HEADS)
    iq = jnp.arange(Q_BLOCK)
    ik = jnp.arange(2 * Q_BLOCK)
    dist = iq[:, None] + Q_BLOCK - ik[None, :]
    band = (dist >= 0) & (dist <= Q_BLOCK)
    has_prev = jnp.stack([ik >= Q_BLOCK, jnp.ones_like(ik, dtype=bool)])
    mask = band[None] & has_prev[:, None, :]
    bias = -slopes[:, None, None] * (dist * dil).astype(F32)[None]
    return jnp.where(mask[:, None], bias[None], NEG_INF).astype(F32)


def _attention(qkv, dil, col0, name):
    B, _, L, _ = qkv.shape
    W = ATTN_WIDTH
    nb = L // Q_BLOCK
    span = Q_BLOCK * dil
    blk = lambda part, prev: pl.BlockSpec(
        (None, dil, Q_BLOCK, W),
        (lambda b, n: (b, 0, jnp.maximum(n - 1, 0), col0 + part)) if prev
        else (lambda b, n: (b, 0, n, col0 + part)))
    tab = _attn_bias_table(dil)
    scratch = [] if dil == 1 else [pltpu.VMEM((ATTN_HEADS // 2, span, LANES), F32),
                                   pltpu.VMEM((span, LANES), F32)]
    o, lse = pl.pallas_call(
        functools.partial(_attn_kernel, dil=dil),
        grid=(B, nb),
        in_specs=[blk(0, False), blk(1, True), blk(1, False), blk(2, True), blk(2, False),
                  pl.BlockSpec(tab.shape, lambda b, n: (0, 0, 0, 0))],
        out_specs=[pl.BlockSpec((None, span, W), lambda b, n: (b, n, 0)),
                   pl.BlockSpec((None, span, LANES), lambda b, n: (b, n, 0))],
        out_shape=[jax.ShapeDtypeStruct((B, L * dil, W), BF16),
                   jax.ShapeDtypeStruct((B, L * dil, LANES), F32)],
        scratch_shapes=scratch,
        compiler_params=pltpu.CompilerParams(
            dimension_semantics=("parallel", "parallel"), vmem_limit_bytes=VMEM_LIMIT),
        name=name,
    )(qkv, qkv, qkv, qkv, qkv, tab)
    return o.reshape(B * L * dil, W), lse.reshape(B * L * dil, LANES)


def _mix_kernel(o1_ref, o2_ref, o3_ref, l1_ref, l2_ref, l3_ref, gb_ref, gc_ref, xc_ref, gch_ref, xch_ref,
                ga_ref, gv_ref, x_ref, e_ref, wa_ref, wc_ref, wo_ref, cw_ref, h_ref, z_scr, *, tm, seq):
    a1, a2, a3 = l1_ref[...], l2_ref[...], l3_ref[...]
    m = jnp.maximum(jnp.maximum(a1, a2), a3)
    e1, e2, e3 = jnp.exp(a1 - m), jnp.exp(a2 - m), jnp.exp(a3 - m)
    inv = 1.0 / (e1 + e2 + e3)
    y = jnp.zeros((tm, ATTN_WIDTH), F32)
    for e, o_ref in ((e1, o1_ref), (e2, o2_ref), (e3, o3_ref)):
        w = e * inv
        w_hi = w.astype(BF16)
        w_lo = (w - w_hi.astype(F32)).astype(BF16)
        w_heads = (jnp.dot(w_hi, e_ref[...], preferred_element_type=F32)
                   + jnp.dot(w_lo, e_ref[...], preferred_element_type=F32))
        y = y + w_heads * o_ref[...].astype(F32)
    branch_a = jnp.dot(y.astype(BF16), wa_ref[...], preferred_element_type=F32)

    at_start = (pl.program_id(0) * tm) % seq == 0
    halo = gch_ref[...].astype(F32) * xch_ref[...].astype(F32)
    z_scr[0:CONV_HALO, :] = jnp.where(at_start, jnp.zeros_like(halo), halo)
    z_scr[CONV_HALO:, :] = gc_ref[...].astype(F32) * xc_ref[...].astype(F32)
    conv = (cw_ref[0:1, :] * z_scr[pl.ds(CONV_HALO - 2, tm), :]
            + cw_ref[1:2, :] * z_scr[pl.ds(CONV_HALO - 1, tm), :]
            + cw_ref[2:3, :] * z_scr[pl.ds(CONV_HALO, tm), :])
    y_conv = gb_ref[...].astype(F32) * conv
    branch_c = jnp.dot(y_conv.astype(BF16), wc_ref[...], preferred_element_type=F32)

    mixed = ga_ref[...].astype(F32) * branch_a + gv_ref[...].astype(F32) * branch_c
    h_ref[...] = x_ref[...] + jnp.dot(mixed.astype(BF16), wo_ref[...], preferred_element_type=F32)


def _mix(o_list, lse_list, proj, gates, x2d, head_expand, wa, wc, wo, conv_w, seq, tm=512):
    T, D = x2d.shape
    W = ATTN_WIDTH
    row = lambda width, col: pl.BlockSpec((tm, width), lambda i: (i, col))
    halo = lambda col: pl.BlockSpec(
        (CONV_HALO, D), lambda i: (jnp.maximum(i * (tm // CONV_HALO) - 1, 0), col))
    full = lambda a: pl.BlockSpec(a.shape, lambda i: (0,) * a.ndim)
    return pl.pallas_call(
        functools.partial(_mix_kernel, tm=tm, seq=seq),
        grid=(T // tm,),
        in_specs=[row(W, 0)] * 3 + [row(LANES, 0)] * 3
                 + [row(D, 0), row(D, 1), row(D, 2), halo(1), halo(2)]
                 + [row(D, 0), row(D, 1), row(D, 0)]
                 + [full(head_expand), full(wa), full(wc), full(wo), full(conv_w)],
        out_specs=pl.BlockSpec((tm, D), lambda i: (i, 0)),
        out_shape=jax.ShapeDtypeStruct((T, D), F32),
        scratch_shapes=[pltpu.VMEM((tm + CONV_HALO, D), F32)],
        compiler_params=pltpu.CompilerParams(
            dimension_semantics=("parallel",), vmem_limit_bytes=VMEM_LIMIT),
        name="branch_mix",
    )(*o_list, *lse_list, proj, proj, proj, proj, proj, gates, gates, x2d,
      head_expand, wa, wc, wo, conv_w)


def _cross_kernel(h_ref, k_ref, v_ref, g2_ref, wq_ref, wo_ref, g3_ref, rw_ref, rb_ref, tri_ref,
                  h2_ref, u3_ref, eidx_ref, rank_ref, wts_ref, cnt_ref, run_scr, *, tm):
    @pl.when(pl.program_id(0) == 0)
    def _():
        run_scr[...] = jnp.zeros_like(run_scr)

    h = h_ref[...]
    u = _rms(h, g2_ref[...]).astype(BF16)
    q = jnp.dot(u, wq_ref[...], preferred_element_type=F32) * (MEM_HEAD_DIM ** -0.5)
    heads = []
    for hd in range(MEM_HEADS):
        cols = slice(hd * MEM_HEAD_DIM, (hd + 1) * MEM_HEAD_DIM)
        s = lax.dot_general(q[:, cols].astype(BF16), k_ref[:, cols], (((1,), (1,)), ((), ())),
                            preferred_element_type=F32)
        m = jnp.max(s, axis=1, keepdims=True)
        p = jnp.exp(s - m)
        l = jnp.sum(p, axis=1, keepdims=True)
        heads.append(jnp.dot(p.astype(BF16), v_ref[:, cols], preferred_element_type=F32) * (1.0 / l))
    o = jnp.concatenate(heads, axis=1).astype(BF16)
    h2 = h + jnp.dot(o, wo_ref[...], preferred_element_type=F32)
    h2_ref[...] = h2
    u3 = _rms(h2, g3_ref[...])
    _pack_rows(u3, u3_ref, tm)

    logits = jnp.dot(u3.astype(BF16), rw_ref[...], preferred_element_type=F32) + rb_ref[...]
    lane = lax.broadcasted_iota(I32, (tm, LANES), 1).astype(F32)
    work = logits
    picked = jnp.zeros((tm, LANES), F32)
    tops, idxs = [], []
    for _ in range(TOP_K):
        mk = jnp.max(work, axis=1, keepdims=True)
        ik = jnp.min(jnp.where(work == mk, lane, float(LANES)), axis=1, keepdims=True)
        sel = lane == ik
        work = jnp.where(sel, -jnp.inf, work)
        picked = picked + sel.astype(F32)
        tops.append(mk)
        idxs.append(ik)
    exps = [jnp.exp(t - tops[0]) for t in tops]
    inv = 1.0 / (exps[0] + exps[1] + exps[2] + exps[3])

    before = jnp.dot(tri_ref[...], picked.astype(BF16), preferred_element_type=F32) + run_scr[...]
    eidx = jnp.zeros((tm, LANES), I32)
    rank = jnp.zeros((tm, LANES), I32)
    wts = jnp.zeros((tm, LANES), F32)
    for k in range(TOP_K):
        rk = jnp.sum(jnp.where(lane == idxs[k], before, 0.0), axis=1, keepdims=True)
        eidx = jnp.where(lane == k, idxs[k].astype(I32), eidx)
        rank = jnp.where(lane == k, rk.astype(I32), rank)
        wts = jnp.where(lane == k, exps[k] * inv, wts)
    eidx_ref[...] = eidx
    rank_ref[...] = rank
    wts_ref[...] = wts
    run_scr[...] = run_scr[...] + jnp.sum(picked, axis=0, keepdims=True)
    cnt_ref[...] = run_scr[...]


def _cross_router(h1, kv, g2, wq, wo, g3, rw, rb, seq, tm=512):
    T, D = h1.shape
    n_mem = kv.shape[1]
    tri = (jnp.arange(tm)[:, None] > jnp.arange(tm)[None, :]).astype(BF16)
    full = lambda a: pl.BlockSpec(a.shape, lambda i: (0,) * a.ndim)
    kvspec = lambda col: pl.BlockSpec((None, n_mem, MEM_WIDTH), lambda i: ((i * tm) // seq, 0, col))
    tile = lambda width: pl.BlockSpec((tm, width), lambda i: (i, 0))
    return pl.pallas_call(
        functools.partial(_cross_kernel, tm=tm),
        grid=(T // tm,),
        in_specs=[tile(D), kvspec(0), kvspec(1), full(g2), full(wq), full(wo), full(g3),
                  full(rw), full(rb), full(tri)],
        out_specs=[tile(D), pl.BlockSpec((tm * PACK_ROWS, LANES), lambda i: (i, 0)),
                   tile(LANES), tile(LANES), tile(LANES),
                   pl.BlockSpec((1, LANES), lambda i: (0, 0))],
        out_shape=[jax.ShapeDtypeStruct((T, D), F32), jax.ShapeDtypeStruct((T * PACK_ROWS, LANES), F32),
                   jax.ShapeDtypeStruct((T, LANES), I32), jax.ShapeDtypeStruct((T, LANES), I32),
                   jax.ShapeDtypeStruct((T, LANES), F32), jax.ShapeDtypeStruct((1, LANES), F32)],
        scratch_shapes=[pltpu.VMEM((1, LANES), F32)],
        compiler_params=pltpu.CompilerParams(
            dimension_semantics=("arbitrary",), vmem_limit_bytes=VMEM_LIMIT),
        name="cross_router",
    )(h1, kv, kv, g2, wq, wo, g3, rw, rb, tri)


def _wait_copies(src_hbm, dst, sem, rows):
    pltpu.make_async_copy(src_hbm.at[pl.ds(0, rows), :], dst.at[pl.ds(0, rows), :], sem).wait()


def _dispatch_kernel(pos_ref, u_hbm, x_hbm, stage, load_sem, row_sem, *, tm, n_token_steps):
    i = pl.program_id(0)
    n = pl.num_programs(0)
    batch = tm * TOP_K * PACK_ROWS
    tile_rows = tm * PACK_ROWS

    def tile_copy(step):
        first = pl.multiple_of(jnp.minimum(step, n_token_steps - 1) * tile_rows, tile_rows)
        return pltpu.make_async_copy(u_hbm.at[pl.ds(first, tile_rows), :], stage.at[step % 3],
                                     load_sem.at[step % 3])

    @pl.when(i == 0)
    def _():
        tile_copy(i).start()

    @pl.when(i + 1 < n)
    def _():
        tile_copy(i + 1).start()

    tile_copy(i).wait()
    src_tile = stage.at[i % 3]
    group = 4

    def body(g, carry):
        rows = [pos_ref[0, g * (group * TOP_K) + j] for j in range(group * TOP_K)]
        for j, p in enumerate(rows):
            t = g * group + j // TOP_K
            src = src_tile.at[pl.ds(pl.multiple_of(t * PACK_ROWS, PACK_ROWS), PACK_ROWS), :]
            dst = x_hbm.at[pl.ds(pl.multiple_of(p * PACK_ROWS, PACK_ROWS), PACK_ROWS), :]
            pltpu.make_async_copy(src, dst, row_sem.at[i % 2]).start(priority=j % 2)
        return carry

    lax.fori_loop(0, tm // group, body, 0)

    @pl.when(i > 0)
    def _():
        _wait_copies(u_hbm, x_hbm, row_sem.at[1 - i % 2], batch)

    @pl.when(i == n - 1)
    def _():
        _wait_copies(u_hbm, x_hbm, row_sem.at[i % 2], batch)


def _dispatch(u_packed, pos, fill_rows, tm=256):
    T = pos.shape[0]
    per_step = TOP_K * tm
    dst_rows = jnp.concatenate([pos.reshape(-1), fill_rows])
    n_rows = dst_rows.shape[0]
    assert n_rows % per_step == 0
    steps = n_rows // per_step
    return pl.pallas_call(
        functools.partial(_dispatch_kernel, tm=tm, n_token_steps=T // tm),
        grid=(steps,),
        in_specs=[pl.BlockSpec((None, 1, per_step), lambda i: (i, 0, 0), memory_space=pltpu.SMEM),
                  pl.BlockSpec(memory_space=pl.ANY)],
        out_specs=pl.BlockSpec(memory_space=pl.ANY),
        out_shape=jax.ShapeDtypeStruct((n_rows * PACK_ROWS, LANES), F32),
        scratch_shapes=[pltpu.VMEM((3, tm * PACK_ROWS, LANES), F32),
                        pltpu.SemaphoreType.DMA((3,)), pltpu.SemaphoreType.DMA((2,))],
        compiler_params=pltpu.CompilerParams(dimension_semantics=("arbitrary",)),
        name="dispatch_rows",
    )(dst_rows.reshape(steps, 1, per_step), u_packed)


def _expert_kernel(bexp_ref, nused_ref, x_ref, wg_ref, bg_ref, wu_ref, bu_ref,
                   wd_ref, bd_ref, y_ref, wg_s, wu_s, wd_s):
    i = pl.program_id(0)
    used = i < nused_ref[0]
    changed = jnp.logical_or(i == 0, bexp_ref[i] != bexp_ref[jnp.maximum(i - 1, 0)])

    @pl.when(jnp.logical_and(changed, used))
    def _():
        wg_s[...] = wg_ref[...].astype(BF16)
        wu_s[...] = wu_ref[...].astype(BF16)
        wd_s[...] = wd_ref[...].astype(BF16)

    @pl.when(used)
    def _():
        x = _unpack_rows(x_ref, 0, MOE_ROWS).astype(BF16)
        gate = jnp.minimum(jnp.dot(x, wg_s[...], preferred_element_type=F32) + bg_ref[...], SWIGLU_LIMIT)
        lin = jnp.clip(jnp.dot(x, wu_s[...], preferred_element_type=F32) + bu_ref[...],
                       -SWIGLU_LIMIT, SWIGLU_LIMIT)
        hdn = gate * jax.nn.sigmoid(SWIGLU_ALPHA * gate) * (lin + 1.0)
        y = jnp.dot(hdn.astype(BF16), wd_s[...], preferred_element_type=F32) + bd_ref[...]
        _pack_rows(y, y_ref, MOE_ROWS)

    @pl.when(jnp.logical_not(used))
    def _():
        y_ref[...] = jnp.zeros_like(y_ref)


def _experts(x_packed, block_expert, n_used, w_eg, b_eg, w_eu, b_eu, w_ed, b_ed):
    nb = block_expert.shape[0]
    E, D, F = w_eg.shape
    rows = lambda i, be, nu: (i, 0)
    wspec = lambda shape: pl.BlockSpec((None,) + shape, lambda i, be, nu: (be[i], 0, 0))
    grid_spec = pltpu.PrefetchScalarGridSpec(
        num_scalar_prefetch=2,
        grid=(nb,),
        in_specs=[pl.BlockSpec((MOE_ROWS * PACK_ROWS, LANES), rows),
                  wspec((D, F)), wspec((1, F)), wspec((D, F)), wspec((1, F)), wspec((F, D)), wspec((1, D))],
        out_specs=pl.BlockSpec((MOE_ROWS * PACK_ROWS, LANES), rows),
        scratch_shapes=[pltpu.VMEM((D, F), BF16), pltpu.VMEM((D, F), BF16), pltpu.VMEM((F, D), BF16)],
    )
    return pl.pallas_call(
        _expert_kernel,
        grid_spec=grid_spec,
        out_shape=jax.ShapeDtypeStruct(x_packed.shape, F32),
        compiler_params=pltpu.CompilerParams(
            dimension_semantics=("arbitrary",), vmem_limit_bytes=VMEM_LIMIT),
        name="expert_ffn",
    )(block_expert, n_used, x_packed, w_eg, b_eg.reshape(E, 1, F), w_eu,
      b_eu.reshape(E, 1, F), w_ed, b_ed.reshape(E, 1, D))


def _gather_packed(pos_ref, src_hbm, dst, sem, n):
    group = 16

    def body(g, carry):
        rows = [pos_ref[0, g * group + j] for j in range(group)]
        for j, p in enumerate(rows):
            r = g * group + j
            pltpu.make_async_copy(
                src_hbm.at[pl.ds(pl.multiple_of(p * PACK_ROWS, PACK_ROWS), PACK_ROWS), :],
                dst.at[pl.ds(pl.multiple_of(r * PACK_ROWS, PACK_ROWS), PACK_ROWS), :],
                sem).start(priority=j % 2)
        return carry
    lax.fori_loop(0, n // group, body, 0)


def _combine_kernel(pos0_ref, pos1_ref, y_hbm, h_ref, w_ref, g_ref, o_ref, ybuf, sem):
    i = pl.program_id(0)
    n = pl.num_programs(0)
    slot = i % 2
    rows = TOP_K * COMBINE_ROWS

    @pl.when(i == 0)
    def _():
        _gather_packed(pos0_ref, y_hbm, ybuf.at[0], sem.at[0], rows)

    @pl.when(i + 1 < n)
    def _():
        _gather_packed(pos1_ref, y_hbm, ybuf.at[1 - slot], sem.at[1 - slot], rows)

    _wait_copies(y_hbm, ybuf.at[slot], sem.at[slot], rows * PACK_ROWS)
    acc = h_ref[...]
    w = w_ref[...]
    for k in range(TOP_K):
        acc = acc + w[:, k:k + 1] * _unpack_rows(ybuf.at[slot], k * COMBINE_ROWS * PACK_ROWS, COMBINE_ROWS)
    o_ref[...] = _rms(acc, g_ref[...])


def _combine(y_rows, pos_kmajor, h2, wts, g_final):
    T, D = h2.shape
    tm = COMBINE_ROWS
    nt = T // tm
    posspec = lambda off: pl.BlockSpec(
        (None, 1, TOP_K * tm), lambda i: (jnp.minimum(i + off, nt - 1), 0, 0), memory_space=pltpu.SMEM)
    return pl.pallas_call(
        _combine_kernel,
        grid=(nt,),
        in_specs=[posspec(0), posspec(1), pl.BlockSpec(memory_space=pl.ANY),
                  pl.BlockSpec((tm, D), lambda i: (i, 0)), pl.BlockSpec((tm, LANES), lambda i: (i, 0)),
                  pl.BlockSpec((1, D), lambda i: (0, 0))],
        out_specs=pl.BlockSpec((tm, D), lambda i: (i, 0)),
        out_shape=jax.ShapeDtypeStruct((T, D), F32),
        scratch_shapes=[pltpu.VMEM((2, TOP_K * tm * PACK_ROWS, LANES), F32), pltpu.SemaphoreType.DMA((2,))],
        compiler_params=pltpu.CompilerParams(
            dimension_semantics=("arbitrary",), vmem_limit_bytes=VMEM_LIMIT),
        name="combine_norm",
    )(pos_kmajor, pos_kmajor, y_rows, h2, wts, g_final.reshape(1, D))


def _layer(h, mem, norm_mix, w_in, conv_w, w_branch_attn, w_branch_conv, w_gate, b_gate, w_out,
           norm_cross, norm_mem, w_cq, w_ckv, w_co, norm_moe, router_w, router_b,
           w_eg, b_eg, w_eu, b_eu, w_ed, b_ed, norm_final):
    B, S, D = h.shape
    T = B * S
    W3 = 3 * ATTN_WIDTH
    n_attn = len(ATTN_GROUPS) * W3
    zero_bias = lambda n: jnp.zeros((n,), F32)

    dils = tuple(dil for _, dil in ATTN_GROUPS)
    assert dils[0] == 1
    u_all = _norm_permute(h, norm_mix, dils[1:])
    w_in_b = w_in.astype(BF16)
    w_a = jnp.concatenate([w_in_b[:, n_attn:], w_in_b[:, :W3]], axis=1)
    proj = _inproj(u_all[0], w_a, zero_bias(w_a.shape[1]), 1024, 768, name="inproj_conv_g1")
    gates = _inproj(u_all[0], w_gate.astype(BF16), b_gate, 1024, 1024, act="sigmoid", name="inproj_gates")
    proj2d = proj.reshape(T, -1)
    gates2d = gates.reshape(T, -1)

    o_list, lse_list = [], []
    for g, (window, dil) in enumerate(ATTN_GROUPS):
        assert window // dil == Q_BLOCK
        if g == 0:
            qkv, col0 = proj, 3 * D // ATTN_WIDTH
        else:
            L = S // dil
            qkv = _inproj(u_all[g], w_in_b[:, g * W3:(g + 1) * W3], zero_bias(W3),
                          min(L, 1024), W3 if L < 1024 else 768, name=f"inproj_g{g + 1}")
            col0 = 0
        o, lse = _attention(qkv, dil, col0, name=f"dilated_attn_g{g + 1}")
        o_list.append(o)
        lse_list.append(lse)

    head_expand = (jnp.arange(LANES)[:, None] == jnp.arange(ATTN_WIDTH)[None, :] // HEAD_DIM).astype(BF16)
    h1 = _mix(o_list, lse_list, proj2d, gates2d, h.reshape(T, D), head_expand,
              w_branch_attn.astype(BF16), w_branch_conv.astype(BF16), w_out.astype(BF16), conv_w, S)

    kv = _mem_kv(mem, norm_mem, w_ckv.astype(BF16))
    rw = jnp.zeros((D, LANES), BF16).at[:, :N_EXPERTS].set(router_w.astype(BF16))
    rb = jnp.full((1, LANES), -jnp.inf, F32).at[0, :N_EXPERTS].set(router_b)
    h2, u3, eidx, rank, wts, counts = _cross_router(
        h1, kv, norm_cross.reshape(1, D), w_cq.astype(BF16), w_co.astype(BF16), norm_moe.reshape(1, D),
        rw, rb, S)

    nb = -(-(T * TOP_K) // MOE_ROWS) + N_EXPERTS
    cnt = counts[0, :N_EXPERTS].astype(I32)
    padded = (cnt + MOE_ROWS - 1) // MOE_ROWS * MOE_ROWS
    pend = jnp.cumsum(padded)
    pstart = pend - padded
    pos = pstart[eidx[:, :TOP_K]] + rank[:, :TOP_K]
    block_start = jnp.arange(nb, dtype=I32) * MOE_ROWS
    block_expert = jnp.minimum(jnp.sum(block_start[:, None] >= pend[None, :], axis=1), N_EXPERTS - 1).astype(I32)
    n_pad = padded - cnt
    pad_end = jnp.cumsum(n_pad)
    fill = jnp.arange(nb * MOE_ROWS - T * TOP_K, dtype=I32)
    fill_seg = jnp.sum(fill[:, None] >= pad_end[None, :], axis=1)
    seg_first_row = jnp.concatenate([pstart + cnt, pend[-1:]])
    seg_first_fill = jnp.concatenate([pad_end - n_pad, pad_end[-1:]])
    fill_rows = (seg_first_row[fill_seg] + fill - seg_first_fill[fill_seg]).astype(I32)

    x_rows = _dispatch(u3, pos, fill_rows)
    y_rows = _experts(x_rows, block_expert, (pend[-1:] // MOE_ROWS).astype(I32),
                      w_eg, b_eg, w_eu, b_eu, w_ed, b_ed)

    pos_kmajor = pos.reshape(T // COMBINE_ROWS, COMBINE_ROWS, TOP_K).transpose(0, 2, 1).reshape(
        T // COMBINE_ROWS, 1, TOP_K * COMBINE_ROWS)
    out = _combine(y_rows, pos_kmajor, h2, wts, norm_final)
    return out.reshape(B, S, D)


def kernel(x, mem, norm_mix, w_in, conv_w, w_branch_attn, w_branch_conv, w_gate, b_gate, w_out, norm_cross, norm_mem, w_cq, w_ckv, w_co, norm_moe, router_w, router_b, w_exp_gate, b_exp_gate, w_exp_up, b_exp_up, w_exp_down, b_exp_down, norm_final):
    depth = norm_mix.shape[0]
    assert depth == 1, "the final norm is fused into the last layer's combine step"
    return _layer(x, mem, norm_mix[0], w_in[0], conv_w[0], w_branch_attn[0], w_branch_conv[0], w_gate[0],
                  b_gate[0], w_out[0], norm_cross[0], norm_mem[0], w_cq[0], w_ckv[0], w_co[0], norm_moe[0],
                  router_w[0], router_b[0], w_exp_gate[0], b_exp_gate[0], w_exp_up[0], b_exp_up[0],
                  w_exp_down[0], b_exp_down[0], norm_final)
```

```python
import functools

import jax
import jax.numpy as jnp
from jax import lax
from jax.experimental import pallas as pl
from jax.experimental.pallas import tpu as pltpu

F32 = jnp.float32
BF16 = jnp.bfloat16
I32 = jnp.int32

D_MODEL = 1024
ATTN_GROUPS = ((128, 1), (512, 4), (2048, 16))
ATTN_HEADS = 8
HEAD_DIM = 64
ATTN_WIDTH = ATTN_HEADS * HEAD_DIM
Q_BLOCK = 128
CONV_K = 3
MEM_HEADS = 4
MEM_HEAD_DIM = 128
MEM_WIDTH = MEM_HEADS * MEM_HEAD_DIM
N_EXPERTS = 32
TOP_K = 4
SWIGLU_LIMIT = 7.0
SWIGLU_ALPHA = 1.702
EPS = 1e-6
NEG_INF = -1e30

LANES = 128
VMEM_LIMIT = 56 * 1024 * 1024

MOE_ROWS = 256
COMBINE_ROWS = 256
CONV_HALO = 16


def _rms(x, g):
    ms = jnp.mean(x * x, axis=-1, keepdims=True)
    return x * lax.rsqrt(ms + EPS) * g


PACK_ROWS = D_MODEL // LANES


def _pack_rows(y, out_ref, n):
    for c in range(PACK_ROWS):
        out_ref[pl.ds(c, n, stride=PACK_ROWS), :] = y[:, c * LANES:(c + 1) * LANES]


def _unpack_rows(ref, start, n):
    return jnp.concatenate(
        [ref[pl.ds(start + c, n, stride=PACK_ROWS), :] for c in range(PACK_ROWS)], axis=1)


def _norm_permute_kernel(x_ref, g_ref, *refs, dils, tm):
    out_refs, slab = refs[:-1], refs[-1]
    u = _rms(x_ref[...], g_ref[...])
    out_refs[0][...] = u.astype(BF16)
    n_slab = u.shape[1] // LANES
    for c in range(n_slab):
        slab[c] = u[:, c * LANES:(c + 1) * LANES]
    for o_ref, dil in zip(out_refs[1:], dils):
        for r in range(dil):
            for c in range(n_slab):
                o_ref[r, :, c * LANES:(c + 1) * LANES] = (
                    slab[c, pl.ds(r, tm // dil, stride=dil), :].astype(BF16))


def _norm_permute(x, gain, dils, tm=1024):
    B, S, D = x.shape
    outs = [jax.ShapeDtypeStruct((B, 1, S, D), BF16)]
    specs = [pl.BlockSpec((None, None, tm, D), lambda b, i: (b, 0, i, 0))]
    for dil in dils:
        outs.append(jax.ShapeDtypeStruct((B, dil, S // dil, D), BF16))
        specs.append(pl.BlockSpec((None, dil, tm // dil, D), lambda b, i: (b, 0, i, 0)))
    return pl.pallas_call(
        functools.partial(_norm_permute_kernel, dils=dils, tm=tm),
        grid=(B, S // tm),
        in_specs=[pl.BlockSpec((None, tm, D), lambda b, i: (b, i, 0)),
                  pl.BlockSpec((1, D), lambda b, i: (0, 0))],
        out_specs=specs,
        out_shape=outs,
        scratch_shapes=[pltpu.VMEM((D // LANES, tm, LANES), F32)],
        compiler_params=pltpu.CompilerParams(
            dimension_semantics=("parallel", "parallel"), vmem_limit_bytes=VMEM_LIMIT),
        name="norm_permute",
    )(x, gain.reshape(1, D))


def _matmul_kernel(u_ref, w_ref, b_ref, o_ref, *, act):
    acc = jnp.dot(u_ref[...], w_ref[...], preferred_element_type=F32)
    if act == "sigmoid":
        acc = jax.nn.sigmoid(acc + b_ref[...])
    o_ref[...] = acc.astype(o_ref.dtype)


def _inproj(u, w, bias, tm, tn, act=None, name="inproj"):
    B, R, L, D = u.shape
    N = w.shape[1]
    return pl.pallas_call(
        functools.partial(_matmul_kernel, act=act),
        grid=(B, R, L // tm, N // tn),
        in_specs=[
            pl.BlockSpec((None, None, tm, D), lambda b, r, i, j: (b, r, i, 0)),
            pl.BlockSpec((D, tn), lambda b, r, i, j: (0, j)),
            pl.BlockSpec((1, tn), lambda b, r, i, j: (0, j)),
        ],
        out_specs=pl.BlockSpec((None, None, tm, tn), lambda b, r, i, j: (b, r, i, j)),
        out_shape=jax.ShapeDtypeStruct((B, R, L, N), BF16),
        compiler_params=pltpu.CompilerParams(
            dimension_semantics=("parallel", "parallel", "parallel", "parallel"),
            vmem_limit_bytes=VMEM_LIMIT),
        name=name,
    )(u, w, bias.reshape(1, N))


def _mem_kv_kernel(x_ref, g_ref, w_ref, o_ref):
    u = _rms(x_ref[...], g_ref[...]).astype(BF16)
    o_ref[...] = jnp.dot(u, w_ref[...], preferred_element_type=F32).astype(o_ref.dtype)


def _mem_kv(mem, gain, w):
    B, M, D = mem.shape
    N = w.shape[1]
    return pl.pallas_call(
        _mem_kv_kernel,
        grid=(B,),
        in_specs=[pl.BlockSpec((None, M, D), lambda b: (b, 0, 0)),
                  pl.BlockSpec((1, D), lambda b: (0, 0)),
                  pl.BlockSpec((D, N), lambda b: (0, 0))],
        out_specs=pl.BlockSpec((None, M, N), lambda b: (b, 0, 0)),
        out_shape=jax.ShapeDtypeStruct((B, M, N), BF16),
        compiler_params=pltpu.CompilerParams(
            dimension_semantics=("parallel",), vmem_limit_bytes=VMEM_LIMIT),
        name="mem_kv",
    )(mem, gain.reshape(1, D), w)


def _attn_block(q_ref, kp_ref, kc_ref, vp_ref, vc_ref, tab_ref, first, r):
    lane = lax.broadcasted_iota(I32, (Q_BLOCK, LANES), 1)
    low = lane < HEAD_DIM
    lse_tile = jnp.zeros((Q_BLOCK, LANES), F32)
    pairs = []
    for pair in range(ATTN_HEADS // 2):
        cols = slice(pair * LANES, (pair + 1) * LANES)
        q2 = q_ref[r, :, cols] * (HEAD_DIM ** -0.5)
        k2 = jnp.concatenate([kp_ref[r, :, cols], kc_ref[r, :, cols]], axis=0)
        v2 = jnp.concatenate([vp_ref[r, :, cols], vc_ref[r, :, cols]], axis=0)
        outs = []
        for half in range(2):
            h = 2 * pair + half
            keep = low if half == 0 else jnp.logical_not(low)
            qm = jnp.where(keep, q2, jnp.zeros_like(q2))
            s = lax.dot_general(qm, k2, (((1,), (1,)), ((), ())), preferred_element_type=F32)
            s = s + tab_ref[first, h]
            m = jnp.max(s, axis=1, keepdims=True)
            p = jnp.exp(s - m)
            l = jnp.sum(p, axis=1, keepdims=True)
            o = jnp.dot(p.astype(BF16), v2, preferred_element_type=F32) * (1.0 / l)
            outs.append(o)
            lse_tile = jnp.where(lane == h, m + jnp.log(l), lse_tile)
        pairs.append(jnp.where(low, outs[0], outs[1]))
    return pairs, lse_tile


def _attn_kernel(q_ref, kp_ref, kc_ref, vp_ref, vc_ref, tab_ref, o_ref, lse_ref, *scratch, dil):
    first = jnp.minimum(pl.program_id(1), 1)
    blocks = (q_ref, kp_ref, kc_ref, vp_ref, vc_ref, tab_ref)
    n_pair = ATTN_HEADS // 2
    if dil == 1:
        pairs, lse_tile = _attn_block(*blocks, first, 0)
        for p in range(n_pair):
            o_ref[:, p * LANES:(p + 1) * LANES] = pairs[p].astype(o_ref.dtype)
        lse_ref[...] = lse_tile
        return

    o_scr, lse_scr = scratch

    def body(r, carry):
        pairs, lse_tile = _attn_block(*blocks, first, r)
        rows = pl.ds(r, Q_BLOCK, stride=dil)
        for p in range(n_pair):
            o_scr[p, rows, :] = pairs[p]
        lse_scr[rows, :] = lse_tile
        return carry

    lax.fori_loop(0, dil, body, 0)
    for p in range(n_pair):
        o_ref[:, p * LANES:(p + 1) * LANES] = o_scr[p].astype(o_ref.dtype)
    lse_ref[...] = lse_scr[...]


def _attn_bias_table(dil):
    slopes = jnp.power(2.0, -8.0 * jnp.arange(1, ATTN_HEADS + 1, dtype=F32) / ATTN_HEADS)
    iq = jnp.arange(Q_BLOCK)
    ik = jnp.arange(2 * Q_BLOCK)
    dist = iq[:, None] + Q_BLOCK - ik[None, :]
    band = (dist >= 0) & (dist <= Q_BLOCK)
    has_prev = jnp.stack([ik >= Q_BLOCK, jnp.ones_like(ik, dtype=bool)])
    mask = band[None] & has_prev[:, None, :]
    bias = -slopes[:, None, None] * (dist * dil).astype(F32)[None]
    return jnp.where(mask[:, None], bias[None], NEG_INF).astype(F32)


def _attention(qkv, dil, col0, name):
    B, _, L, _ = qkv.shape
    W = ATTN_WIDTH
    nb = L // Q_BLOCK
    span = Q_BLOCK * dil
    blk = lambda part, prev: pl.BlockSpec(
        (None, dil, Q_BLOCK, W),
        (lambda b, n: (b, 0, jnp.maximum(n - 1, 0), col0 + part)) if prev
        else (lambda b, n: (b, 0, n, col0 + part)))
    tab = _attn_bias_table(dil)
    scratch = [] if dil == 1 else [pltpu.VMEM((ATTN_HEADS // 2, span, LANES), F32),
                                   pltpu.VMEM((span, LANES), F32)]
    o, lse = pl.pallas_call(
        functools.partial(_attn_kernel, dil=dil),
        grid=(B, nb),
        in_specs=[blk(0, False), blk(1, True), blk(1, False), blk(2, True), blk(2, False),
                  pl.BlockSpec(tab.shape, lambda b, n: (0, 0, 0, 0))],
        out_specs=[pl.BlockSpec((None, span, W), lambda b, n: (b, n, 0)),
                   pl.BlockSpec((None, span, LANES), lambda b, n: (b, n, 0))],
        out_shape=[jax.ShapeDtypeStruct((B, L * dil, W), BF16),
                   jax.ShapeDtypeStruct((B, L * dil, LANES), F32)],
        scratch_shapes=scratch,
        compiler_params=pltpu.CompilerParams(
            dimension_semantics=("parallel", "parallel"), vmem_limit_bytes=VMEM_LIMIT),
        name=name,
    )(qkv, qkv, qkv, qkv, qkv, tab)
    return o.reshape(B * L * dil, W), lse.reshape(B * L * dil, LANES)


def _mix_kernel(o1_ref, o2_ref, o3_ref, l1_ref, l2_ref, l3_ref, gb_ref, gc_ref, xc_ref, gch_ref, xch_ref,
                ga_ref, gv_ref, x_ref, e_ref, wa_ref, wc_ref, wo_ref, cw_ref, h_ref, z_scr, *, tm, seq):
    a1, a2, a3 = l1_ref[...], l2_ref[...], l3_ref[...]
    m = jnp.maximum(jnp.maximum(a1, a2), a3)
    e1, e2, e3 = jnp.exp(a1 - m), jnp.exp(a2 - m), jnp.exp(a3 - m)
    inv = 1.0 / (e1 + e2 + e3)
    y = jnp.zeros((tm, ATTN_WIDTH), F32)
    for e, o_ref in ((e1, o1_ref), (e2, o2_ref), (e3, o3_ref)):
        w = e * inv
        w_hi = w.astype(BF16)
        w_lo = (w - w_hi.astype(F32)).astype(BF16)
        w_heads = (jnp.dot(w_hi, e_ref[...], preferred_element_type=F32)
                   + jnp.dot(w_lo, e_ref[...], preferred_element_type=F32))
        y = y + w_heads * o_ref[...].astype(F32)
    branch_a = jnp.dot(y.astype(BF16), wa_ref[...], preferred_element_type=F32)

    at_start = (pl.program_id(0) * tm) % seq == 0
    halo = gch_ref[...].astype(F32) * xch_ref[...].astype(F32)
    z_scr[0:CONV_HALO, :] = jnp.where(at_start, jnp.zeros_like(halo), halo)
    z_scr[CONV_HALO:, :] = gc_ref[...].astype(F32) * xc_ref[...].astype(F32)
    conv = (cw_ref[0:1, :] * z_scr[pl.ds(CONV_HALO - 2, tm), :]
            + cw_ref[1:2, :] * z_scr[pl.ds(CONV_HALO - 1, tm), :]
            + cw_ref[2:3, :] * z_scr[pl.ds(CONV_HALO, tm), :])
    y_conv = gb_ref[...].astype(F32) * conv
    branch_c = jnp.dot(y_conv.astype(BF16), wc_ref[...], preferred_element_type=F32)

    mixed = ga_ref[...].astype(F32) * branch_a + gv_ref[...].astype(F32) * branch_c
    h_ref[...] = x_ref[...] + jnp.dot(mixed.astype(BF16), wo_ref[...], preferred_element_type=F32)


def _mix(o_list, lse_list, proj, gates, x2d, head_expand, wa, wc, wo, conv_w, seq, tm=512):
    T, D = x2d.shape
    W = ATTN_WIDTH
    row = lambda width, col: pl.BlockSpec((tm, width), lambda i: (i, col))
    halo = lambda col: pl.BlockSpec(
        (CONV_HALO, D), lambda i: (jnp.maximum(i * (tm // CONV_HALO) - 1, 0), col))
    full = lambda a: pl.BlockSpec(a.shape, lambda i: (0,) * a.ndim)
    return pl.pallas_call(
        functools.partial(_mix_kernel, tm=tm, seq=seq),
        grid=(T // tm,),
        in_specs=[row(W, 0)] * 3 + [row(LANES, 0)] * 3
                 + [row(D, 0), row(D, 1), row(D, 2), halo(1), halo(2)]
                 + [row(D, 0), row(D, 1), row(D, 0)]
                 + [full(head_expand), full(wa), full(wc), full(wo), full(conv_w)],
        out_specs=pl.BlockSpec((tm, D), lambda i: (i, 0)),
        out_shape=jax.ShapeDtypeStruct((T, D), F32),
        scratch_shapes=[pltpu.VMEM((tm + CONV_HALO, D), F32)],
        compiler_params=pltpu.CompilerParams(
            dimension_semantics=("parallel",), vmem_limit_bytes=VMEM_LIMIT),
        name="branch_mix",
    )(*o_list, *lse_list, proj, proj, proj, proj, proj, gates, gates, x2d,
      head_expand, wa, wc, wo, conv_w)


def _cross_kernel(h_ref, k_ref, v_ref, g2_ref, wq_ref, wo_ref, g3_ref, rw_ref, rb_ref, tri_ref,
                  h2_ref, u3_ref, eidx_ref, rank_ref, wts_ref, cnt_ref, run_scr, *, tm):
    @pl.when(pl.program_id(0) == 0)
    def _():
        run_scr[...] = jnp.zeros_like(run_scr)

    h = h_ref[...]
    u = _rms(h, g2_ref[...]).astype(BF16)
    q = jnp.dot(u, wq_ref[...], preferred_element_type=F32) * (MEM_HEAD_DIM ** -0.5)
    heads = []
    for hd in range(MEM_HEADS):
        cols = slice(hd * MEM_HEAD_DIM, (hd + 1) * MEM_HEAD_DIM)
        s = lax.dot_general(q[:, cols].astype(BF16), k_ref[:, cols], (((1,), (1,)), ((), ())),
                            preferred_element_type=F32)
        m = jnp.max(s, axis=1, keepdims=True)
        p = jnp.exp(s - m)
        l = jnp.sum(p, axis=1, keepdims=True)
        heads.append(jnp.dot(p.astype(BF16), v_ref[:, cols], preferred_element_type=F32) * (1.0 / l))
    o = jnp.concatenate(heads, axis=1).astype(BF16)
    h2 = h + jnp.dot(o, wo_ref[...], preferred_element_type=F32)
    h2_ref[...] = h2
    u3 = _rms(h2, g3_ref[...])
    _pack_rows(u3, u3_ref, tm)

    logits = jnp.dot(u3.astype(BF16), rw_ref[...], preferred_element_type=F32) + rb_ref[...]
    lane = lax.broadcasted_iota(I32, (tm, LANES), 1).astype(F32)
    work = logits
    picked = jnp.zeros((tm, LANES), F32)
    tops, idxs = [], []
    for _ in range(TOP_K):
        mk = jnp.max(work, axis=1, keepdims=True)
        ik = jnp.min(jnp.where(work == mk, lane, float(LANES)), axis=1, keepdims=True)
        sel = lane == ik
        work = jnp.where(sel, -jnp.inf, work)
        picked = picked + sel.astype(F32)
        tops.append(mk)
        idxs.append(ik)
    exps = [jnp.exp(t - tops[0]) for t in tops]
    inv = 1.0 / (exps[0] + exps[1] + exps[2] + exps[3])

    before = jnp.dot(tri_ref[...], picked.astype(BF16), preferred_element_type=F32) + run_scr[...]
    eidx = jnp.zeros((tm, LANES), I32)
    rank = jnp.zeros((tm, LANES), I32)
    wts = jnp.zeros((tm, LANES), F32)
    for k in range(TOP_K):
        rk = jnp.sum(jnp.where(lane == idxs[k], before, 0.0), axis=1, keepdims=True)
        eidx = jnp.where(lane == k, idxs[k].astype(I32), eidx)
        rank = jnp.where(lane == k, rk.astype(I32), rank)
        wts = jnp.where(lane == k, exps[k] * inv, wts)
    eidx_ref[...] = eidx
    rank_ref[...] = rank
    wts_ref[...] = wts
    run_scr[...] = run_scr[...] + jnp.sum(picked, axis=0, keepdims=True)
    cnt_ref[...] = run_scr[...]


def _cross_router(h1, kv, g2, wq, wo, g3, rw, rb, seq, tm=512):
    T, D = h1.shape
    n_mem = kv.shape[1]
    tri = (jnp.arange(tm)[:, None] > jnp.arange(tm)[None, :]).astype(BF16)
    full = lambda a: pl.BlockSpec(a.shape, lambda i: (0,) * a.ndim)
    kvspec = lambda col: pl.BlockSpec((None, n_mem, MEM_WIDTH), lambda i: ((i * tm) // seq, 0, col))
    tile = lambda width: pl.BlockSpec((tm, width), lambda i: (i, 0))
    return pl.pallas_call(
        functools.partial(_cross_kernel, tm=tm),
        grid=(T // tm,),
        in_specs=[tile(D), kvspec(0), kvspec(1), full(g2), full(wq), full(wo), full(g3),
                  full(rw), full(rb), full(tri)],
        out_specs=[tile(D), pl.BlockSpec((tm * PACK_ROWS, LANES), lambda i: (i, 0)),
                   tile(LANES), tile(LANES), tile(LANES),
                   pl.BlockSpec((1, LANES), lambda i: (0, 0))],
        out_shape=[jax.ShapeDtypeStruct((T, D), F32), jax.ShapeDtypeStruct((T * PACK_ROWS, LANES), F32),
                   jax.ShapeDtypeStruct((T, LANES), I32), jax.ShapeDtypeStruct((T, LANES), I32),
                   jax.ShapeDtypeStruct((T, LANES), F32), jax.ShapeDtypeStruct((1, LANES), F32)],
        scratch_shapes=[pltpu.VMEM((1, LANES), F32)],
        compiler_params=pltpu.CompilerParams(
            dimension_semantics=("arbitrary",), vmem_limit_bytes=VMEM_LIMIT),
        name="cross_router",
    )(h1, kv, kv, g2, wq, wo, g3, rw, rb, tri)


def _wait_copies(src_hbm, dst, sem, rows):
    pltpu.make_async_copy(src_hbm.at[pl.ds(0, rows), :], dst.at[pl.ds(0, rows), :], sem).wait()


def _dispatch_kernel(pos_ref, u_hbm, x_hbm, stage, load_sem, row_sem, *, tm, n_token_steps):
    i = pl.program_id(0)
    n = pl.num_programs(0)
    batch = tm * TOP_K * PACK_ROWS
    tile_rows = tm * PACK_ROWS

    def tile_copy(step):
        first = pl.multiple_of(jnp.minimum(step, n_token_steps - 1) * tile_rows, tile_rows)
        return pltpu.make_async_copy(u_hbm.at[pl.ds(first, tile_rows), :], stage.at[step % 3],
                                     load_sem.at[step % 3])

    @pl.when(i == 0)
    def _():
        tile_copy(i).start()

    @pl.when(i + 1 < n)
    def _():
        tile_copy(i + 1).start()

    tile_copy(i).wait()
    src_tile = stage.at[i % 3]
    group = 4

    def body(g, carry):
        rows = [pos_ref[0, g * (group * TOP_K) + j] for j in range(group * TOP_K)]
        for j, p in enumerate(rows):
            t = g * group + j // TOP_K
            src = src_tile.at[pl.ds(pl.multiple_of(t * PACK_ROWS, PACK_ROWS), PACK_ROWS), :]
            dst = x_hbm.at[pl.ds(pl.multiple_of(p * PACK_ROWS, PACK_ROWS), PACK_ROWS), :]
            pltpu.make_async_copy(src, dst, row_sem.at[i % 2]).start(priority=j % 2)
        return carry

    lax.fori_loop(0, tm // group, body, 0)

    @pl.when(i > 0)
    def _():
        _wait_copies(u_hbm, x_hbm, row_sem.at[1 - i % 2], batch)

    @pl.when(i == n - 1)
    def _():
        _wait_copies(u_hbm, x_hbm, row_sem.at[i % 2], batch)


def _dispatch(u_packed, pos, fill_rows, tm=256):
    T = pos.shape[0]
    per_step = TOP_K * tm
    dst_rows = jnp.concatenate([pos.reshape(-1), fill_rows])
    n_rows = dst_rows.shape[0]
    assert n_rows % per_step == 0
    steps = n_rows // per_step
    return pl.pallas_call(
        functools.partial(_dispatch_kernel, tm=tm, n_token_steps=T // tm),
        grid=(steps,),
        in_specs=[pl.BlockSpec((None, 1, per_step), lambda i: (i, 0, 0), memory_space=pltpu.SMEM),
                  pl.BlockSpec(memory_space=pl.ANY)],
        out_specs=pl.BlockSpec(memory_space=pl.ANY),
        out_shape=jax.ShapeDtypeStruct((n_rows * PACK_ROWS, LANES), F32),
        scratch_shapes=[pltpu.VMEM((3, tm * PACK_ROWS, LANES), F32),
                        pltpu.SemaphoreType.DMA((3,)), pltpu.SemaphoreType.DMA((2,))],
        compiler_params=pltpu.CompilerParams(dimension_semantics=("arbitrary",)),
        name="dispatch_rows",
    )(dst_rows.reshape(steps, 1, per_step), u_packed)


def _expert_kernel(bexp_ref, slot_ref, next_ref, nused_ref, x_ref, wg_hbm, bg_ref, wu_hbm, bu_ref,
                   wd_hbm, bd_ref, y_ref, stage, w_bf, sem):
    i = pl.program_id(0)
    used = i < nused_ref[0]
    e = bexp_ref[i]
    slot = slot_ref[i]
    changed = jnp.logical_or(i == 0, e != bexp_ref[jnp.maximum(i - 1, 0)])
    weights = (wg_hbm, wu_hbm, wd_hbm)

    def fetch(expert, into):
        return [pltpu.make_async_copy(w.at[expert], stage.at[into, j], sem.at[into])
                for j, w in enumerate(weights)]

    @pl.when(jnp.logical_and(changed, used))
    def _():
        @pl.when(i == 0)
        def _():
            for c in fetch(e, slot):
                c.start()

        for c in fetch(e, slot):
            c.wait()
        for j in range(len(weights)):
            w_bf[j] = stage[slot, j].astype(BF16)

        @pl.when(next_ref[i] >= 0)
        def _():
            for c in fetch(next_ref[i], 1 - slot):
                c.start(priority=1)

    @pl.when(used)
    def _():
        x = _unpack_rows(x_ref, 0, MOE_ROWS).astype(BF16)
        gate = jnp.minimum(jnp.dot(x, w_bf[0], preferred_element_type=F32) + bg_ref[...], SWIGLU_LIMIT)
        lin = jnp.clip(jnp.dot(x, w_bf[1], preferred_element_type=F32) + bu_ref[...],
                       -SWIGLU_LIMIT, SWIGLU_LIMIT)
        hdn = gate * jax.nn.sigmoid(SWIGLU_ALPHA * gate) * (lin + 1.0)
        y = jnp.dot(hdn.astype(BF16), w_bf[2], preferred_element_type=F32) + bd_ref[...]
        _pack_rows(y, y_ref, MOE_ROWS)

    @pl.when(jnp.logical_not(used))
    def _():
        y_ref[...] = jnp.zeros_like(y_ref)


def _experts(x_packed, block_expert, block_slot, block_next, n_used, w_eg, b_eg, w_eu, b_eu, w_ed, b_ed):
    nb = block_expert.shape[0]
    E, D, F = w_eg.shape
    assert D == F
    rows = lambda i, be, sl, nx, nu: (i, 0)
    bspec = lambda n: pl.BlockSpec((None, 1, n), lambda i, be, sl, nx, nu: (be[i], 0, 0))
    anyspec = pl.BlockSpec(memory_space=pl.ANY)
    grid_spec = pltpu.PrefetchScalarGridSpec(
        num_scalar_prefetch=4,
        grid=(nb,),
        in_specs=[pl.BlockSpec((MOE_ROWS * PACK_ROWS, LANES), rows),
                  anyspec, bspec(F), anyspec, bspec(F), anyspec, bspec(D)],
        out_specs=pl.BlockSpec((MOE_ROWS * PACK_ROWS, LANES), rows),
        scratch_shapes=[pltpu.VMEM((2, 3, D, F), F32), pltpu.VMEM((3, D, F), BF16),
                        pltpu.SemaphoreType.DMA((2,))],
    )
    return pl.pallas_call(
        _expert_kernel,
        grid_spec=grid_spec,
        out_shape=jax.ShapeDtypeStruct(x_packed.shape, F32),
        compiler_params=pltpu.CompilerParams(
            dimension_semantics=("arbitrary",), vmem_limit_bytes=VMEM_LIMIT),
        name="expert_ffn",
    )(block_expert, block_slot, block_next, n_used, x_packed, w_eg, b_eg.reshape(E, 1, F), w_eu,
      b_eu.reshape(E, 1, F), w_ed, b_ed.reshape(E, 1, D))


def _gather_packed(pos_ref, src_hbm, dst, sem, n):
    group = 16

    def body(g, carry):
        rows = [pos_ref[0, g * group + j] for j in range(group)]
        for j, p in enumerate(rows):
            r = g * group + j
            pltpu.make_async_copy(
                src_hbm.at[pl.ds(pl.multiple_of(p * PACK_ROWS, PACK_ROWS), PACK_ROWS), :],
                dst.at[pl.ds(pl.multiple_of(r * PACK_ROWS, PACK_ROWS), PACK_ROWS), :],
                sem).start(priority=j % 2)
        return carry
    lax.fori_loop(0, n // group, body, 0)


def _combine_kernel(pos0_ref, pos1_ref, y_hbm, h_ref, w_ref, g_ref, o_ref, ybuf, sem):
    i = pl.program_id(0)
    n = pl.num_programs(0)
    slot = i % 2
    rows = TOP_K * COMBINE_ROWS

    @pl.when(i == 0)
    def _():
        _gather_packed(pos0_ref, y_hbm, ybuf.at[0], sem.at[0], rows)

    @pl.when(i + 1 < n)
    def _():
        _gather_packed(pos1_ref, y_hbm, ybuf.at[1 - slot], sem.at[1 - slot], rows)

    _wait_copies(y_hbm, ybuf.at[slot], sem.at[slot], rows * PACK_ROWS)
    acc = h_ref[...]
    w = w_ref[...]
    for k in range(TOP_K):
        acc = acc + w[:, k:k + 1] * _unpack_rows(ybuf.at[slot], k * COMBINE_ROWS * PACK_ROWS, COMBINE_ROWS)
    o_ref[...] = _rms(acc, g_ref[...])


def _combine(y_rows, pos_kmajor, h2, wts, g_final):
    T, D = h2.shape
    tm = COMBINE_ROWS
    nt = T // tm
    posspec = lambda off: pl.BlockSpec(
        (None, 1, TOP_K * tm), lambda i: (jnp.minimum(i + off, nt - 1), 0, 0), memory_space=pltpu.SMEM)
    return pl.pallas_call(
        _combine_kernel,
        grid=(nt,),
        in_specs=[posspec(0), posspec(1), pl.BlockSpec(memory_space=pl.ANY),
                  pl.BlockSpec((tm, D), lambda i: (i, 0)), pl.BlockSpec((tm, LANES), lambda i: (i, 0)),
                  pl.BlockSpec((1, D), lambda i: (0, 0))],
        out_specs=pl.BlockSpec((tm, D), lambda i: (i, 0)),
        out_shape=jax.ShapeDtypeStruct((T, D), F32),
        scratch_shapes=[pltpu.VMEM((2, TOP_K * tm * PACK_ROWS, LANES), F32), pltpu.SemaphoreType.DMA((2,))],
        compiler_params=pltpu.CompilerParams(
            dimension_semantics=("arbitrary",), vmem_limit_bytes=VMEM_LIMIT),
        name="combine_norm",
    )(pos_kmajor, pos_kmajor, y_rows, h2, wts, g_final.reshape(1, D))


def _layer(h, mem, norm_mix, w_in, conv_w, w_branch_attn, w_branch_conv, w_gate, b_gate, w_out,
           norm_cross, norm_mem, w_cq, w_ckv, w_co, norm_moe, router_w, router_b,
           w_eg, b_eg, w_eu, b_eu, w_ed, b_ed, norm_final):
    B, S, D = h.shape
    T = B * S
    W3 = 3 * ATTN_WIDTH
    n_attn = len(ATTN_GROUPS) * W3
    zero_bias = lambda n: jnp.zeros((n,), F32)

    dils = tuple(dil for _, dil in ATTN_GROUPS)
    assert dils[0] == 1
    u_all = _norm_permute(h, norm_mix, dils[1:])
    w_in_b = w_in.astype(BF16)
    w_a = jnp.concatenate([w_in_b[:, n_attn:], w_in_b[:, :W3]], axis=1)
    proj = _inproj(u_all[0], w_a, zero_bias(w_a.shape[1]), 1024, 768, name="inproj_conv_g1")
    gates = _inproj(u_all[0], w_gate.astype(BF16), b_gate, 1024, 1024, act="sigmoid", name="inproj_gates")
    proj2d = proj.reshape(T, -1)
    gates2d = gates.reshape(T, -1)

    o_list, lse_list = [], []
    for g, (window, dil) in enumerate(ATTN_GROUPS):
        assert window // dil == Q_BLOCK
        if g == 0:
            qkv, col0 = proj, 3 * D // ATTN_WIDTH
        else:
            L = S // dil
            qkv = _inproj(u_all[g], w_in_b[:, g * W3:(g + 1) * W3], zero_bias(W3),
                          min(L, 1024), W3 if L < 1024 else 768, name=f"inproj_g{g + 1}")
            col0 = 0
        o, lse = _attention(qkv, dil, col0, name=f"dilated_attn_g{g + 1}")
        o_list.append(o)
        lse_list.append(lse)

    head_expand = (jnp.arange(LANES)[:, None] == jnp.arange(ATTN_WIDTH)[None, :] // HEAD_DIM).astype(BF16)
    h1 = _mix(o_list, lse_list, proj2d, gates2d, h.reshape(T, D), head_expand,
              w_branch_attn.astype(BF16), w_branch_conv.astype(BF16), w_out.astype(BF16), conv_w, S)

    kv = _mem_kv(mem, norm_mem, w_ckv.astype(BF16))
    rw = jnp.zeros((D, LANES), BF16).at[:, :N_EXPERTS].set(router_w.astype(BF16))
    rb = jnp.full((1, LANES), -jnp.inf, F32).at[0, :N_EXPERTS].set(router_b)
    h2, u3, eidx, rank, wts, counts = _cross_router(
        h1, kv, norm_cross.reshape(1, D), w_cq.astype(BF16), w_co.astype(BF16), norm_moe.reshape(1, D),
        rw, rb, S)

    nb = -(-(T * TOP_K) // MOE_ROWS) + N_EXPERTS
    cnt = counts[0, :N_EXPERTS].astype(I32)
    padded = (cnt + MOE_ROWS - 1) // MOE_ROWS * MOE_ROWS
    pend = jnp.cumsum(padded)
    pstart = pend - padded
    experts = jnp.arange(N_EXPERTS, dtype=I32)
    lookup = lambda table, idx: jnp.sum(jnp.where(idx[..., None] == jnp.arange(table.shape[0], dtype=I32),
                                                  table, 0), axis=-1)
    pos = lookup(pstart, eidx[:, :TOP_K]) + rank[:, :TOP_K]
    block_start = jnp.arange(nb, dtype=I32) * MOE_ROWS
    block_expert = jnp.minimum(jnp.sum(block_start[:, None] >= pend[None, :], axis=1), N_EXPERTS - 1).astype(I32)
    active = cnt > 0
    slot_e = (jnp.cumsum(active.astype(I32)) - 1) % 2
    later = jnp.where(active[None, :] & (experts[None, :] > experts[:, None]), experts[None, :], N_EXPERTS)
    next_e = jnp.min(later, axis=1)
    next_e = jnp.where(next_e == N_EXPERTS, -1, next_e).astype(I32)
    n_pad = padded - cnt
    pad_end = jnp.cumsum(n_pad)
    fill = jnp.arange(nb * MOE_ROWS - T * TOP_K, dtype=I32)
    fill_seg = jnp.sum(fill[:, None] >= pad_end[None, :], axis=1)
    seg_first_row = jnp.concatenate([pstart + cnt, pend[-1:]])
    seg_first_fill = jnp.concatenate([pad_end - n_pad, pad_end[-1:]])
    fill_rows = (lookup(seg_first_row - seg_first_fill, fill_seg) + fill).astype(I32)

    x_rows = _dispatch(u3, pos, fill_rows)
    y_rows = _experts(x_rows, block_expert, slot_e[block_expert].astype(I32), next_e[block_expert],
                      (pend[-1:] // MOE_ROWS).astype(I32), w_eg, b_eg, w_eu, b_eu, w_ed, b_ed)

    pos_kmajor = pos.reshape(T // COMBINE_ROWS, COMBINE_ROWS, TOP_K).transpose(0, 2, 1).reshape(
        T // COMBINE_ROWS, 1, TOP_K * COMBINE_ROWS)
    out = _combine(y_rows, pos_kmajor, h2, wts, norm_final)
    return out.reshape(B, S, D)


def kernel(x, mem, norm_mix, w_in, conv_w, w_branch_attn, w_branch_conv, w_gate, b_gate, w_out, norm_cross, norm_mem, w_cq, w_ckv, w_co, norm_moe, router_w, router_b, w_exp_gate, b_exp_gate, w_exp_up, b_exp_up, w_exp_down, b_exp_down, norm_final):
    depth = norm_mix.shape[0]
    assert depth == 1, "the final norm is fused into the last layer's combine step"
    return _layer(x, mem, norm_mix[0], w_in[0], conv_w[0], w_branch_attn[0], w_branch_conv[0], w_gate[0],
                  b_gate[0], w_out[0], norm_cross[0], norm_mem[0], w_cq[0], w_ckv[0], w_co[0], norm_moe[0],
                  router_w[0], router_b[0], w_exp_gate[0], b_exp_gate[0], w_exp_up[0], b_exp_up[0],
                  w_exp_down[0], b_exp_down[0], norm_final)
```

```python
import functools

import jax
import jax.numpy as jnp
from jax import lax
from jax.experimental import pallas as pl
from jax.experimental.pallas import tpu as pltpu

F32 = jnp.float32
BF16 = jnp.bfloat16
I32 = jnp.int32

D_MODEL = 1024
ATTN_GROUPS = ((128, 1), (512, 4), (2048, 16))
ATTN_HEADS = 8
HEAD_DIM = 64
ATTN_WIDTH = ATTN_HEADS * HEAD_DIM
Q_BLOCK = 128
CONV_K = 3
MEM_HEADS = 4
MEM_HEAD_DIM = 128
MEM_WIDTH = MEM_HEADS * MEM_HEAD_DIM
N_EXPERTS = 32
TOP_K = 4
SWIGLU_LIMIT = 7.0
SWIGLU_ALPHA = 1.702
EPS = 1e-6
NEG_INF = -1e30

LANES = 128
VMEM_LIMIT = 56 * 1024 * 1024

MOE_ROWS = 256
COMBINE_ROWS = 256
CONV_HALO = 16


def _rms(x, g):
    ms = jnp.mean(x * x, axis=-1, keepdims=True)
    return x * lax.rsqrt(ms + EPS) * g


PACK_ROWS = D_MODEL // LANES


def _pack_rows(y, out_ref, n):
    for c in range(PACK_ROWS):
        out_ref[pl.ds(c, n, stride=PACK_ROWS), :] = y[:, c * LANES:(c + 1) * LANES]


def _unpack_rows(ref, start, n):
    return jnp.concatenate(
        [ref[pl.ds(start + c, n, stride=PACK_ROWS), :] for c in range(PACK_ROWS)], axis=1)


def _norm_permute_kernel(x_ref, g_ref, *refs, dils, tm):
    out_refs, slab = refs[:-1], refs[-1]
    u = _rms(x_ref[...], g_ref[...])
    out_refs[0][...] = u.astype(BF16)
    n_slab = u.shape[1] // LANES
    for c in range(n_slab):
        slab[c] = u[:, c * LANES:(c + 1) * LANES]
    for o_ref, dil in zip(out_refs[1:], dils):
        for r in range(dil):
            for c in range(n_slab):
                o_ref[r, :, c * LANES:(c + 1) * LANES] = (
                    slab[c, pl.ds(r, tm // dil, stride=dil), :].astype(BF16))


def _norm_permute(x, gain, dils, tm=1024):
    B, S, D = x.shape
    outs = [jax.ShapeDtypeStruct((B, 1, S, D), BF16)]
    specs = [pl.BlockSpec((None, None, tm, D), lambda b, i: (b, 0, i, 0))]
    for dil in dils:
        outs.append(jax.ShapeDtypeStruct((B, dil, S // dil, D), BF16))
        specs.append(pl.BlockSpec((None, dil, tm // dil, D), lambda b, i: (b, 0, i, 0)))
    return pl.pallas_call(
        functools.partial(_norm_permute_kernel, dils=dils, tm=tm),
        grid=(B, S // tm),
        in_specs=[pl.BlockSpec((None, tm, D), lambda b, i: (b, i, 0)),
                  pl.BlockSpec((1, D), lambda b, i: (0, 0))],
        out_specs=specs,
        out_shape=outs,
        scratch_shapes=[pltpu.VMEM((D // LANES, tm, LANES), F32)],
        compiler_params=pltpu.CompilerParams(
            dimension_semantics=("parallel", "parallel"), vmem_limit_bytes=VMEM_LIMIT),
        name="norm_permute",
    )(x, gain.reshape(1, D))


def _inproj_kernel(u_ref, *refs, gate_bias, rb, chunk):
    n_w = len(gate_bias)
    w_refs, b_refs, o_refs = refs[:n_w], refs[n_w:2 * n_w], refs[2 * n_w:]
    for r in range(rb):
        u = u_ref[r]
        for w_ref, b_ref, o_ref, sig in zip(w_refs, b_refs, o_refs, gate_bias):
            for c in range(w_ref.shape[1] // chunk):
                cols = slice(c * chunk, (c + 1) * chunk)
                acc = jnp.dot(u, w_ref[:, cols], preferred_element_type=F32)
                if sig:
                    acc = jax.nn.sigmoid(acc + b_ref[:, cols])
                o_ref[r, :, cols] = acc.astype(o_ref.dtype)


def _inproj(u, weights, rb, tm, chunk, name):
    B, R, L, D = u.shape
    ws = [w for w, _ in weights]
    bs = [jnp.zeros((1, w.shape[1]), F32) if b is None else b.reshape(1, -1) for w, b in weights]
    const = lambda a: pl.BlockSpec(a.shape, lambda b, r, i: (0, 0))
    return pl.pallas_call(
        functools.partial(_inproj_kernel, gate_bias=tuple(b is not None for _, b in weights),
                          rb=rb, chunk=chunk),
        grid=(B, R // rb, L // tm),
        in_specs=[pl.BlockSpec((None, rb, tm, D), lambda b, r, i: (b, r, i, 0))]
                 + [const(w) for w in ws] + [const(b) for b in bs],
        out_specs=[pl.BlockSpec((None, rb, tm, w.shape[1]), lambda b, r, i: (b, r, i, 0)) for w in ws],
        out_shape=[jax.ShapeDtypeStruct((B, R, L, w.shape[1]), BF16) for w in ws],
        compiler_params=pltpu.CompilerParams(
            dimension_semantics=("parallel", "parallel", "parallel"), vmem_limit_bytes=VMEM_LIMIT),
        name=name,
    )(u, *ws, *bs)


def _mem_kv_kernel(x_ref, g_ref, w_ref, o_ref):
    u = _rms(x_ref[...], g_ref[...]).astype(BF16)
    o_ref[...] = jnp.dot(u, w_ref[...], preferred_element_type=F32).astype(o_ref.dtype)


def _mem_kv(mem, gain, w):
    B, M, D = mem.shape
    N = w.shape[1]
    return pl.pallas_call(
        _mem_kv_kernel,
        grid=(B,),
        in_specs=[pl.BlockSpec((None, M, D), lambda b: (b, 0, 0)),
                  pl.BlockSpec((1, D), lambda b: (0, 0)),
                  pl.BlockSpec((D, N), lambda b: (0, 0))],
        out_specs=pl.BlockSpec((None, M, N), lambda b: (b, 0, 0)),
        out_shape=jax.ShapeDtypeStruct((B, M, N), BF16),
        compiler_params=pltpu.CompilerParams(
            dimension_semantics=("parallel",), vmem_limit_bytes=VMEM_LIMIT),
        name="mem_kv",
    )(mem, gain.reshape(1, D), w)


def _attn_block(q_ref, kp_ref, kc_ref, vp_ref, vc_ref, tab_ref, first, r, j):
    lane = lax.broadcasted_iota(I32, (Q_BLOCK, LANES), 1)
    low = lane < HEAD_DIM
    lse_tile = jnp.zeros((Q_BLOCK, LANES), F32)
    pairs = []
    if j > 0:
        first = 1
    for pair in range(ATTN_HEADS // 2):
        cols = slice(pair * LANES, (pair + 1) * LANES)
        q2 = q_ref[r, j * Q_BLOCK:(j + 1) * Q_BLOCK, cols] * (HEAD_DIM ** -0.5)
        if j == 0:
            k2 = jnp.concatenate([kp_ref[r, :, cols], kc_ref[r, :Q_BLOCK, cols]], axis=0)
            v2 = jnp.concatenate([vp_ref[r, :, cols], vc_ref[r, :Q_BLOCK, cols]], axis=0)
        else:
            k2 = kc_ref[r, (j - 1) * Q_BLOCK:(j + 1) * Q_BLOCK, cols]
            v2 = vc_ref[r, (j - 1) * Q_BLOCK:(j + 1) * Q_BLOCK, cols]
        outs = []
        for half in range(2):
            h = 2 * pair + half
            keep = low if half == 0 else jnp.logical_not(low)
            qm = jnp.where(keep, q2, jnp.zeros_like(q2))
            s = lax.dot_general(qm, k2, (((1,), (1,)), ((), ())), preferred_element_type=F32)
            s = s + tab_ref[first, h]
            m = jnp.max(s, axis=1, keepdims=True)
            p = jnp.exp(s - m)
            l = jnp.sum(p, axis=1, keepdims=True)
            o = jnp.dot(p.astype(BF16), v2, preferred_element_type=F32) * (1.0 / l)
            outs.append(o)
            lse_tile = jnp.where(lane == h, m + jnp.log(l), lse_tile)
        pairs.append(jnp.where(low, outs[0], outs[1]))
    return pairs, lse_tile


def _attn_kernel(q_ref, kp_ref, kc_ref, vp_ref, vc_ref, tab_ref, o_ref, lse_ref, *scratch, dil, qb):
    first = jnp.minimum(pl.program_id(1), 1)
    blocks = (q_ref, kp_ref, kc_ref, vp_ref, vc_ref, tab_ref)
    n_pair = ATTN_HEADS // 2
    if dil == 1:
        for j in range(qb):
            pairs, lse_tile = _attn_block(*blocks, first, 0, j)
            rows = slice(j * Q_BLOCK, (j + 1) * Q_BLOCK)
            for p in range(n_pair):
                o_ref[rows, p * LANES:(p + 1) * LANES] = pairs[p].astype(o_ref.dtype)
            lse_ref[rows, :] = lse_tile
        return

    o_scr, lse_scr = scratch

    def body(r, carry):
        for j in range(qb):
            pairs, lse_tile = _attn_block(*blocks, first, r, j)
            rows = pl.ds(j * Q_BLOCK * dil + r, Q_BLOCK, stride=dil)
            for p in range(n_pair):
                o_scr[p, rows, :] = pairs[p]
            lse_scr[rows, :] = lse_tile
        return carry

    lax.fori_loop(0, dil, body, 0)
    for p in range(n_pair):
        o_ref[:, p * LANES:(p + 1) * LANES] = o_scr[p].astype(o_ref.dtype)
    lse_ref[...] = lse_scr[...]


def _attn_bias_table(dil):
    slopes = jnp.power(2.0, -8.0 * jnp.arange(1, ATTN_HEADS + 1, dtype=F32) / ATTN_HEADS)
    iq = jnp.arange(Q_BLOCK)
    ik = jnp.arange(2 * Q_BLOCK)
    dist = iq[:, None] + Q_BLOCK - ik[None, :]
    band = (dist >= 0) & (dist <= Q_BLOCK)
    has_prev = jnp.stack([ik >= Q_BLOCK, jnp.ones_like(ik, dtype=bool)])
    mask = band[None] & has_prev[:, None, :]
    bias = -slopes[:, None, None] * (dist * dil).astype(F32)[None]
    return jnp.where(mask[:, None], bias[None], NEG_INF).astype(F32)


def _attention(qkv, dil, col0, name):
    B, _, L, _ = qkv.shape
    W = ATTN_WIDTH
    qb = max(1, 4 // dil)
    nb = L // (Q_BLOCK * qb)
    span = Q_BLOCK * qb * dil
    blk = lambda part, prev: (
        pl.BlockSpec((None, dil, Q_BLOCK, W), lambda b, n: (b, 0, jnp.maximum(n * qb - 1, 0), col0 + part))
        if prev else pl.BlockSpec((None, dil, Q_BLOCK * qb, W), lambda b, n: (b, 0, n, col0 + part)))
    tab = _attn_bias_table(dil)
    scratch = [] if dil == 1 else [pltpu.VMEM((ATTN_HEADS // 2, span, LANES), F32),
                                   pltpu.VMEM((span, LANES), F32)]
    o, lse = pl.pallas_call(
        functools.partial(_attn_kernel, dil=dil, qb=qb),
        grid=(B, nb),
        in_specs=[blk(0, False), blk(1, True), blk(1, False), blk(2, True), blk(2, False),
                  pl.BlockSpec(tab.shape, lambda b, n: (0, 0, 0, 0))],
        out_specs=[pl.BlockSpec((None, span, W), lambda b, n: (b, n, 0)),
                   pl.BlockSpec((None, span, LANES), lambda b, n: (b, n, 0))],
        out_shape=[jax.ShapeDtypeStruct((B, L * dil, W), BF16),
                   jax.ShapeDtypeStruct((B, L * dil, LANES), F32)],
        scratch_shapes=scratch,
        compiler_params=pltpu.CompilerParams(
            dimension_semantics=("parallel", "parallel"), vmem_limit_bytes=VMEM_LIMIT),
        name=name,
    )(qkv, qkv, qkv, qkv, qkv, tab)
    return o.reshape(B * L * dil, W), lse.reshape(B * L * dil, LANES)


def _mix_kernel(o1_ref, o2_ref, o3_ref, l1_ref, l2_ref, l3_ref, gb_ref, gc_ref, xc_ref, gch_ref, xch_ref,
                ga_ref, gv_ref, x_ref, e_ref, wa_ref, wc_ref, wo_ref, cw_ref, h_ref, z_scr, *, tm, seq):
    a1, a2, a3 = l1_ref[...], l2_ref[...], l3_ref[...]
    m = jnp.maximum(jnp.maximum(a1, a2), a3)
    e1, e2, e3 = jnp.exp(a1 - m), jnp.exp(a2 - m), jnp.exp(a3 - m)
    inv = 1.0 / (e1 + e2 + e3)
    y = jnp.zeros((tm, ATTN_WIDTH), F32)
    for e, o_ref in ((e1, o1_ref), (e2, o2_ref), (e3, o3_ref)):
        w = e * inv
        w_hi = w.astype(BF16)
        w_lo = (w - w_hi.astype(F32)).astype(BF16)
        w_heads = (jnp.dot(w_hi, e_ref[...], preferred_element_type=F32)
                   + jnp.dot(w_lo, e_ref[...], preferred_element_type=F32))
        y = y + w_heads * o_ref[...].astype(F32)
    branch_a = jnp.dot(y.astype(BF16), wa_ref[...], preferred_element_type=F32)

    at_start = (pl.program_id(0) * tm) % seq == 0
    halo = gch_ref[...].astype(F32) * xch_ref[...].astype(F32)
    z_scr[0:CONV_HALO, :] = jnp.where(at_start, jnp.zeros_like(halo), halo)
    z_scr[CONV_HALO:, :] = gc_ref[...].astype(F32) * xc_ref[...].astype(F32)
    conv = (cw_ref[0:1, :] * z_scr[pl.ds(CONV_HALO - 2, tm), :]
            + cw_ref[1:2, :] * z_scr[pl.ds(CONV_HALO - 1, tm), :]
            + cw_ref[2:3, :] * z_scr[pl.ds(CONV_HALO, tm), :])
    y_conv = gb_ref[...].astype(F32) * conv
    branch_c = jnp.dot(y_conv.astype(BF16), wc_ref[...], preferred_element_type=F32)

    mixed = ga_ref[...].astype(F32) * branch_a + gv_ref[...].astype(F32) * branch_c
    h_ref[...] = x_ref[...] + jnp.dot(mixed.astype(BF16), wo_ref[...], preferred_element_type=F32)


def _mix(o_list, lse_list, proj, gates, x2d, head_expand, wa, wc, wo, conv_w, seq, tm=512):
    T, D = x2d.shape
    W = ATTN_WIDTH
    row = lambda width, col: pl.BlockSpec((tm, width), lambda i: (i, col))
    halo = lambda col: pl.BlockSpec(
        (CONV_HALO, D), lambda i: (jnp.maximum(i * (tm // CONV_HALO) - 1, 0), col))
    full = lambda a: pl.BlockSpec(a.shape, lambda i: (0,) * a.ndim)
    return pl.pallas_call(
        functools.partial(_mix_kernel, tm=tm, seq=seq),
        grid=(T // tm,),
        in_specs=[row(W, 0)] * 3 + [row(LANES, 0)] * 3
                 + [row(D, 0), row(D, 1), row(D, 2), halo(1), halo(2)]
                 + [row(D, 0), row(D, 1), row(D, 0)]
                 + [full(head_expand), full(wa), full(wc), full(wo), full(conv_w)],
        out_specs=pl.BlockSpec((tm, D), lambda i: (i, 0)),
        out_shape=jax.ShapeDtypeStruct((T, D), F32),
        scratch_shapes=[pltpu.VMEM((tm + CONV_HALO, D), F32)],
        compiler_params=pltpu.CompilerParams(
            dimension_semantics=("parallel",), vmem_limit_bytes=VMEM_LIMIT),
        name="branch_mix",
    )(*o_list, *lse_list, proj, proj, proj, proj, proj, gates, gates, x2d,
      head_expand, wa, wc, wo, conv_w)


def _cross_kernel(h_ref, k_ref, v_ref, g2_ref, wq_ref, wo_ref, g3_ref, rw_ref, rb_ref, tri_ref,
                  h2_ref, u3_ref, eidx_ref, rank_ref, wts_ref, cnt_ref, run_scr, *, tm):
    @pl.when(pl.program_id(0) == 0)
    def _():
        run_scr[...] = jnp.zeros_like(run_scr)

    h = h_ref[...]
    u = _rms(h, g2_ref[...]).astype(BF16)
    q = jnp.dot(u, wq_ref[...], preferred_element_type=F32) * (MEM_HEAD_DIM ** -0.5)
    heads = []
    for hd in range(MEM_HEADS):
        cols = slice(hd * MEM_HEAD_DIM, (hd + 1) * MEM_HEAD_DIM)
        s = lax.dot_general(q[:, cols].astype(BF16), k_ref[:, cols], (((1,), (1,)), ((), ())),
                            preferred_element_type=F32)
        m = jnp.max(s, axis=1, keepdims=True)
        p = jnp.exp(s - m)
        l = jnp.sum(p, axis=1, keepdims=True)
        heads.append(jnp.dot(p.astype(BF16), v_ref[:, cols], preferred_element_type=F32) * (1.0 / l))
    o = jnp.concatenate(heads, axis=1).astype(BF16)
    h2 = h + jnp.dot(o, wo_ref[...], preferred_element_type=F32)
    h2_ref[...] = h2
    u3 = _rms(h2, g3_ref[...])
    _pack_rows(u3, u3_ref, tm)

    logits = jnp.dot(u3.astype(BF16), rw_ref[...], preferred_element_type=F32) + rb_ref[...]
    lane = lax.broadcasted_iota(I32, (tm, LANES), 1).astype(F32)
    work = logits
    picked = jnp.zeros((tm, LANES), F32)
    tops, idxs = [], []
    for _ in range(TOP_K):
        mk = jnp.max(work, axis=1, keepdims=True)
        ik = jnp.min(jnp.where(work == mk, lane, float(LANES)), axis=1, keepdims=True)
        sel = lane == ik
        work = jnp.where(sel, -jnp.inf, work)
        picked = picked + sel.astype(F32)
        tops.append(mk)
        idxs.append(ik)
    exps = [jnp.exp(t - tops[0]) for t in tops]
    inv = 1.0 / (exps[0] + exps[1] + exps[2] + exps[3])

    before = jnp.dot(tri_ref[...], picked.astype(BF16), preferred_element_type=F32) + run_scr[...]
    eidx = jnp.zeros((tm, LANES), I32)
    rank = jnp.zeros((tm, LANES), I32)
    wts = jnp.zeros((tm, LANES), F32)
    for k in range(TOP_K):
        rk = jnp.sum(jnp.where(lane == idxs[k], before, 0.0), axis=1, keepdims=True)
        eidx = jnp.where(lane == k, idxs[k].astype(I32), eidx)
        rank = jnp.where(lane == k, rk.astype(I32), rank)
        wts = jnp.where(lane == k, exps[k] * inv, wts)
    eidx_ref[...] = eidx
    rank_ref[...] = rank
    wts_ref[...] = wts
    run_scr[...] = run_scr[...] + jnp.sum(picked, axis=0, keepdims=True)
    cnt_ref[...] = run_scr[...]


def _cross_router(h1, kv, g2, wq, wo, g3, rw, rb, seq, tm=512):
    T, D = h1.shape
    n_mem = kv.shape[1]
    tri = (jnp.arange(tm)[:, None] > jnp.arange(tm)[None, :]).astype(BF16)
    full = lambda a: pl.BlockSpec(a.shape, lambda i: (0,) * a.ndim)
    kvspec = lambda col: pl.BlockSpec((None, n_mem, MEM_WIDTH), lambda i: ((i * tm) // seq, 0, col))
    tile = lambda width: pl.BlockSpec((tm, width), lambda i: (i, 0))
    return pl.pallas_call(
        functools.partial(_cross_kernel, tm=tm),
        grid=(T // tm,),
        in_specs=[tile(D), kvspec(0), kvspec(1), full(g2), full(wq), full(wo), full(g3),
                  full(rw), full(rb), full(tri)],
        out_specs=[tile(D), pl.BlockSpec((tm * PACK_ROWS, LANES), lambda i: (i, 0)),
                   tile(LANES), tile(LANES), tile(LANES),
                   pl.BlockSpec((1, LANES), lambda i: (0, 0))],
        out_shape=[jax.ShapeDtypeStruct((T, D), F32), jax.ShapeDtypeStruct((T * PACK_ROWS, LANES), F32),
                   jax.ShapeDtypeStruct((T, LANES), I32), jax.ShapeDtypeStruct((T, LANES), I32),
                   jax.ShapeDtypeStruct((T, LANES), F32), jax.ShapeDtypeStruct((1, LANES), F32)],
        scratch_shapes=[pltpu.VMEM((1, LANES), F32)],
        compiler_params=pltpu.CompilerParams(
            dimension_semantics=("arbitrary",), vmem_limit_bytes=VMEM_LIMIT),
        name="cross_router",
    )(h1, kv, kv, g2, wq, wo, g3, rw, rb, tri)


def _wait_copies(src_hbm, dst, sem, rows):
    pltpu.make_async_copy(src_hbm.at[pl.ds(0, rows), :], dst.at[pl.ds(0, rows), :], sem).wait()


def _dispatch_kernel(pos_ref, u_hbm, x_hbm, stage, load_sem, row_sem, *, tm, n_token_steps):
    i = pl.program_id(0)
    n = pl.num_programs(0)
    batch = tm * TOP_K * PACK_ROWS
    tile_rows = tm * PACK_ROWS

    def tile_copy(step):
        first = pl.multiple_of(jnp.minimum(step, n_token_steps - 1) * tile_rows, tile_rows)
        return pltpu.make_async_copy(u_hbm.at[pl.ds(first, tile_rows), :], stage.at[step % 3],
                                     load_sem.at[step % 3])

    @pl.when(i == 0)
    def _():
        tile_copy(i).start()

    @pl.when(i + 1 < n)
    def _():
        tile_copy(i + 1).start()

    tile_copy(i).wait()
    src_tile = stage.at[i % 3]
    group = 4

    def body(g, carry):
        rows = [pos_ref[0, g * (group * TOP_K) + j] for j in range(group * TOP_K)]
        for j, p in enumerate(rows):
            t = g * group + j // TOP_K
            src = src_tile.at[pl.ds(pl.multiple_of(t * PACK_ROWS, PACK_ROWS), PACK_ROWS), :]
            dst = x_hbm.at[pl.ds(pl.multiple_of(p * PACK_ROWS, PACK_ROWS), PACK_ROWS), :]
            pltpu.make_async_copy(src, dst, row_sem.at[i % 2]).start(priority=j % 2)
        return carry

    lax.fori_loop(0, tm // group, body, 0)

    @pl.when(i > 0)
    def _():
        _wait_copies(u_hbm, x_hbm, row_sem.at[1 - i % 2], batch)

    @pl.when(i == n - 1)
    def _():
        _wait_copies(u_hbm, x_hbm, row_sem.at[i % 2], batch)


def _dispatch(u_packed, pos, fill_rows, tm=256):
    T = pos.shape[0]
    per_step = TOP_K * tm
    dst_rows = jnp.concatenate([pos.reshape(-1), fill_rows])
    n_rows = dst_rows.shape[0]
    assert n_rows % per_step == 0
    steps = n_rows // per_step
    return pl.pallas_call(
        functools.partial(_dispatch_kernel, tm=tm, n_token_steps=T // tm),
        grid=(steps,),
        in_specs=[pl.BlockSpec((None, 1, per_step), lambda i: (i, 0, 0), memory_space=pltpu.SMEM),
                  pl.BlockSpec(memory_space=pl.ANY)],
        out_specs=pl.BlockSpec(memory_space=pl.ANY),
        out_shape=jax.ShapeDtypeStruct((n_rows * PACK_ROWS, LANES), F32),
        scratch_shapes=[pltpu.VMEM((3, tm * PACK_ROWS, LANES), F32),
                        pltpu.SemaphoreType.DMA((3,)), pltpu.SemaphoreType.DMA((2,))],
        compiler_params=pltpu.CompilerParams(dimension_semantics=("arbitrary",)),
        name="dispatch_rows",
    )(dst_rows.reshape(steps, 1, per_step), u_packed)


def _expert_kernel(bexp_ref, slot_ref, next_ref, nused_ref, x_ref, wg_hbm, bg_ref, wu_hbm, bu_ref,
                   wd_hbm, bd_ref, y_ref, stage, w_bf, sem):
    i = pl.program_id(0)
    used = i < nused_ref[0]
    e = bexp_ref[i]
    slot = slot_ref[i]
    changed = jnp.logical_or(i == 0, e != bexp_ref[jnp.maximum(i - 1, 0)])
    weights = (wg_hbm, wu_hbm, wd_hbm)

    def fetch(expert, into):
        return [pltpu.make_async_copy(w.at[expert], stage.at[into, j], sem.at[into])
                for j, w in enumerate(weights)]

    @pl.when(jnp.logical_and(changed, used))
    def _():
        @pl.when(i == 0)
        def _():
            for c in fetch(e, slot):
                c.start()

        for c in fetch(e, slot):
            c.wait()
        for j in range(len(weights)):
            w_bf[j] = stage[slot, j].astype(BF16)

        @pl.when(next_ref[i] >= 0)
        def _():
            for c in fetch(next_ref[i], 1 - slot):
                c.start(priority=1)

    @pl.when(used)
    def _():
        x = _unpack_rows(x_ref, 0, MOE_ROWS).astype(BF16)
        gate = jnp.minimum(jnp.dot(x, w_bf[0], preferred_element_type=F32) + bg_ref[...], SWIGLU_LIMIT)
        lin = jnp.clip(jnp.dot(x, w_bf[1], preferred_element_type=F32) + bu_ref[...],
                       -SWIGLU_LIMIT, SWIGLU_LIMIT)
        hdn = gate * jax.nn.sigmoid(SWIGLU_ALPHA * gate) * (lin + 1.0)
        y = jnp.dot(hdn.astype(BF16), w_bf[2], preferred_element_type=F32) + bd_ref[...]
        _pack_rows(y, y_ref, MOE_ROWS)

    @pl.when(jnp.logical_not(used))
    def _():
        y_ref[...] = jnp.zeros_like(y_ref)


def _experts(x_packed, block_expert, block_slot, block_next, n_used, w_eg, b_eg, w_eu, b_eu, w_ed, b_ed):
    nb = block_expert.shape[0]
    E, D, F = w_eg.shape
    assert D == F
    rows = lambda i, be, sl, nx, nu: (i, 0)
    bspec = lambda n: pl.BlockSpec((None, 1, n), lambda i, be, sl, nx, nu: (be[i], 0, 0))
    anyspec = pl.BlockSpec(memory_space=pl.ANY)
    grid_spec = pltpu.PrefetchScalarGridSpec(
        num_scalar_prefetch=4,
        grid=(nb,),
        in_specs=[pl.BlockSpec((MOE_ROWS * PACK_ROWS, LANES), rows),
                  anyspec, bspec(F), anyspec, bspec(F), anyspec, bspec(D)],
        out_specs=pl.BlockSpec((MOE_ROWS * PACK_ROWS, LANES), rows),
        scratch_shapes=[pltpu.VMEM((2, 3, D, F), F32), pltpu.VMEM((3, D, F), BF16),
                        pltpu.SemaphoreType.DMA((2,))],
    )
    return pl.pallas_call(
        _expert_kernel,
        grid_spec=grid_spec,
        out_shape=jax.ShapeDtypeStruct(x_packed.shape, F32),
        compiler_params=pltpu.CompilerParams(
            dimension_semantics=("arbitrary",), vmem_limit_bytes=VMEM_LIMIT),
        name="expert_ffn",
    )(block_expert, block_slot, block_next, n_used, x_packed, w_eg, b_eg.reshape(E, 1, F), w_eu,
      b_eu.reshape(E, 1, F), w_ed, b_ed.reshape(E, 1, D))


def _gather_packed(pos_ref, src_hbm, dst, sem, n):
    group = 16

    def body(g, carry):
        rows = [pos_ref[0, g * group + j] for j in range(group)]
        for j, p in enumerate(rows):
            r = g * group + j
            pltpu.make_async_copy(
                src_hbm.at[pl.ds(pl.multiple_of(p * PACK_ROWS, PACK_ROWS), PACK_ROWS), :],
                dst.at[pl.ds(pl.multiple_of(r * PACK_ROWS, PACK_ROWS), PACK_ROWS), :],
                sem).start(priority=j % 2)
        return carry
    lax.fori_loop(0, n // group, body, 0)


def _combine_kernel(pos0_ref, pos1_ref, y_hbm, h_ref, w_ref, g_ref, o_ref, ybuf, sem):
    i = pl.program_id(0)
    n = pl.num_programs(0)
    slot = i % 2
    rows = TOP_K * COMBINE_ROWS

    @pl.when(i == 0)
    def _():
        _gather_packed(pos0_ref, y_hbm, ybuf.at[0], sem.at[0], rows)

    @pl.when(i + 1 < n)
    def _():
        _gather_packed(pos1_ref, y_hbm, ybuf.at[1 - slot], sem.at[1 - slot], rows)

    _wait_copies(y_hbm, ybuf.at[slot], sem.at[slot], rows * PACK_ROWS)
    acc = h_ref[...]
    w = w_ref[...]
    for k in range(TOP_K):
        acc = acc + w[:, k:k + 1] * _unpack_rows(ybuf.at[slot], k * COMBINE_ROWS * PACK_ROWS, COMBINE_ROWS)
    o_ref[...] = _rms(acc, g_ref[...])


def _combine(y_rows, pos_kmajor, h2, wts, g_final):
    T, D = h2.shape
    tm = COMBINE_ROWS
    nt = T // tm
    posspec = lambda off: pl.BlockSpec(
        (None, 1, TOP_K * tm), lambda i: (jnp.minimum(i + off, nt - 1), 0, 0), memory_space=pltpu.SMEM)
    return pl.pallas_call(
        _combine_kernel,
        grid=(nt,),
        in_specs=[posspec(0), posspec(1), pl.BlockSpec(memory_space=pl.ANY),
                  pl.BlockSpec((tm, D), lambda i: (i, 0)), pl.BlockSpec((tm, LANES), lambda i: (i, 0)),
                  pl.BlockSpec((1, D), lambda i: (0, 0))],
        out_specs=pl.BlockSpec((tm, D), lambda i: (i, 0)),
        out_shape=jax.ShapeDtypeStruct((T, D), F32),
        scratch_shapes=[pltpu.VMEM((2, TOP_K * tm * PACK_ROWS, LANES), F32), pltpu.SemaphoreType.DMA((2,))],
        compiler_params=pltpu.CompilerParams(
            dimension_semantics=("arbitrary",), vmem_limit_bytes=VMEM_LIMIT),
        name="combine_norm",
    )(pos_kmajor, pos_kmajor, y_rows, h2, wts, g_final.reshape(1, D))


def _layer(h, mem, norm_mix, w_in, conv_w, w_branch_attn, w_branch_conv, w_gate, b_gate, w_out,
           norm_cross, norm_mem, w_cq, w_ckv, w_co, norm_moe, router_w, router_b,
           w_eg, b_eg, w_eu, b_eu, w_ed, b_ed, norm_final):
    B, S, D = h.shape
    T = B * S
    W3 = 3 * ATTN_WIDTH
    n_attn = len(ATTN_GROUPS) * W3

    dils = tuple(dil for _, dil in ATTN_GROUPS)
    assert dils[0] == 1
    u_all = _norm_permute(h, norm_mix, dils[1:])
    w_in_b = w_in.astype(BF16)
    w_a = jnp.concatenate([w_in_b[:, n_attn:], w_in_b[:, :W3]], axis=1)
    proj, gates = _inproj(u_all[0], [(w_a, None), (w_gate.astype(BF16), b_gate)], 1, 512, 512,
                          name="inproj_conv_g1_gates")
    proj2d = proj.reshape(T, -1)
    gates2d = gates.reshape(T, -1)

    o_list, lse_list = [], []
    for g, (window, dil) in enumerate(ATTN_GROUPS):
        assert window // dil == Q_BLOCK
        if g == 0:
            qkv, col0 = proj, 3 * D // ATTN_WIDTH
        else:
            L = S // dil
            tm = min(L, 1024)
            qkv, = _inproj(u_all[g], [(w_in_b[:, g * W3:(g + 1) * W3], None)], 1024 // tm, tm, 512,
                           name=f"inproj_g{g + 1}")
            col0 = 0
        o, lse = _attention(qkv, dil, col0, name=f"dilated_attn_g{g + 1}")
        o_list.append(o)
        lse_list.append(lse)

    head_expand = (jnp.arange(LANES)[:, None] == jnp.arange(ATTN_WIDTH)[None, :] // HEAD_DIM).astype(BF16)
    h1 = _mix(o_list, lse_list, proj2d, gates2d, h.reshape(T, D), head_expand,
              w_branch_attn.astype(BF16), w_branch_conv.astype(BF16), w_out.astype(BF16), conv_w, S)

    kv = _mem_kv(mem, norm_mem, w_ckv.astype(BF16))
    rw = jnp.zeros((D, LANES), BF16).at[:, :N_EXPERTS].set(router_w.astype(BF16))
    rb = jnp.full((1, LANES), -jnp.inf, F32).at[0, :N_EXPERTS].set(router_b)
    h2, u3, eidx, rank, wts, counts = _cross_router(
        h1, kv, norm_cross.reshape(1, D), w_cq.astype(BF16), w_co.astype(BF16), norm_moe.reshape(1, D),
        rw, rb, S)

    nb = -(-(T * TOP_K) // MOE_ROWS) + N_EXPERTS
    cnt = counts[0, :N_EXPERTS].astype(I32)
    padded = (cnt + MOE_ROWS - 1) // MOE_ROWS * MOE_ROWS
    pend = jnp.cumsum(padded)
    pstart = pend - padded
    experts = jnp.arange(N_EXPERTS, dtype=I32)
    lookup = lambda table, idx: jnp.sum(jnp.where(idx[..., None] == jnp.arange(table.shape[0], dtype=I32),
                                                  table, 0), axis=-1)
    pos = lookup(pstart, eidx[:, :TOP_K]) + rank[:, :TOP_K]
    block_start = jnp.arange(nb, dtype=I32) * MOE_ROWS
    block_expert = jnp.minimum(jnp.sum(block_start[:, None] >= pend[None, :], axis=1), N_EXPERTS - 1).astype(I32)
    active = cnt > 0
    slot_e = (jnp.cumsum(active.astype(I32)) - 1) % 2
    later = jnp.where(active[None, :] & (experts[None, :] > experts[:, None]), experts[None, :], N_EXPERTS)
    next_e = jnp.min(later, axis=1)
    next_e = jnp.where(next_e == N_EXPERTS, -1, next_e).astype(I32)
    n_pad = padded - cnt
    pad_end = jnp.cumsum(n_pad)
    fill = jnp.arange(nb * MOE_ROWS - T * TOP_K, dtype=I32)
    fill_seg = jnp.sum(fill[:, None] >= pad_end[None, :], axis=1)
    seg_first_row = jnp.concatenate([pstart + cnt, pend[-1:]])
    seg_first_fill = jnp.concatenate([pad_end - n_pad, pad_end[-1:]])
    fill_rows = (lookup(seg_first_row - seg_first_fill, fill_seg) + fill).astype(I32)

    x_rows = _dispatch(u3, pos, fill_rows)
    y_rows = _experts(x_rows, block_expert, slot_e[block_expert].astype(I32), next_e[block_expert],
                      (pend[-1:] // MOE_ROWS).astype(I32), w_eg, b_eg, w_eu, b_eu, w_ed, b_ed)

    pos_kmajor = pos.reshape(T // COMBINE_ROWS, COMBINE_ROWS, TOP_K).transpose(0, 2, 1).reshape(
        T // COMBINE_ROWS, 1, TOP_K * COMBINE_ROWS)
    out = _combine(y_rows, pos_kmajor, h2, wts, norm_final)
    return out.reshape(B, S, D)


def kernel(x, mem, norm_mix, w_in, conv_w, w_branch_attn, w_branch_conv, w_gate, b_gate, w_out, norm_cross, norm_mem, w_cq, w_ckv, w_co, norm_moe, router_w, router_b, w_exp_gate, b_exp_gate, w_exp_up, b_exp_up, w_exp_down, b_exp_down, norm_final):
    depth = norm_mix.shape[0]
    assert depth == 1, "the final norm is fused into the last layer's combine step"
    return _layer(x, mem, norm_mix[0], w_in[0], conv_w[0], w_branch_attn[0], w_branch_conv[0], w_gate[0],
                  b_gate[0], w_out[0], norm_cross[0], norm_mem[0], w_cq[0], w_ckv[0], w_co[0], norm_moe[0],
                  router_w[0], router_b[0], w_exp_gate[0], b_exp_gate[0], w_exp_up[0], b_exp_up[0],
                  w_exp_down[0], b_exp_down[0], norm_final)
```

```python
import functools

import jax
import jax.numpy as jnp
from jax import lax
from jax.experimental import pallas as pl
from jax.experimental.pallas import tpu as pltpu

F32 = jnp.float32
BF16 = jnp.bfloat16
I32 = jnp.int32

D_MODEL = 1024
ATTN_GROUPS = ((128, 1), (512, 4), (2048, 16))
ATTN_HEADS = 8
HEAD_DIM = 64
ATTN_WIDTH = ATTN_HEADS * HEAD_DIM
Q_BLOCK = 128
CONV_K = 3
MEM_HEADS = 4
MEM_HEAD_DIM = 128
MEM_WIDTH = MEM_HEADS * MEM_HEAD_DIM
N_EXPERTS = 32
TOP_K = 4
SWIGLU_LIMIT = 7.0
SWIGLU_ALPHA = 1.702
EPS = 1e-6
NEG_INF = -1e30

LANES = 128
VMEM_LIMIT = 56 * 1024 * 1024

MOE_ROWS = 256
COMBINE_ROWS = 512
CONV_HALO = 16


def _rms(x, g):
    ms = jnp.mean(x * x, axis=-1, keepdims=True)
    return x * lax.rsqrt(ms + EPS) * g


PACK_ROWS = D_MODEL // LANES


def _pack_rows(y, out_ref, n):
    for c in range(PACK_ROWS):
        out_ref[pl.ds(c, n, stride=PACK_ROWS), :] = y[:, c * LANES:(c + 1) * LANES]


def _unpack_rows(ref, start, n):
    return jnp.concatenate(
        [ref[pl.ds(start + c, n, stride=PACK_ROWS), :] for c in range(PACK_ROWS)], axis=1)


def _norm_permute_kernel(x_ref, g_ref, *refs, dils, tm):
    out_refs, slab = refs[:-1], refs[-1]
    u = _rms(x_ref[...], g_ref[...])
    out_refs[0][...] = u.astype(BF16)
    n_slab = u.shape[1] // LANES
    for c in range(n_slab):
        slab[c] = u[:, c * LANES:(c + 1) * LANES]
    for o_ref, dil in zip(out_refs[1:], dils):
        for r in range(dil):
            for c in range(n_slab):
                o_ref[r, :, c * LANES:(c + 1) * LANES] = (
                    slab[c, pl.ds(r, tm // dil, stride=dil), :].astype(BF16))


def _norm_permute(x, gain, dils, tm=1024):
    B, S, D = x.shape
    outs = [jax.ShapeDtypeStruct((B, 1, S, D), BF16)]
    specs = [pl.BlockSpec((None, None, tm, D), lambda b, i: (b, 0, i, 0))]
    for dil in dils:
        outs.append(jax.ShapeDtypeStruct((B, dil, S // dil, D), BF16))
        specs.append(pl.BlockSpec((None, dil, tm // dil, D), lambda b, i: (b, 0, i, 0)))
    return pl.pallas_call(
        functools.partial(_norm_permute_kernel, dils=dils, tm=tm),
        grid=(B, S // tm),
        in_specs=[pl.BlockSpec((None, tm, D), lambda b, i: (b, i, 0)),
                  pl.BlockSpec((1, D), lambda b, i: (0, 0))],
        out_specs=specs,
        out_shape=outs,
        scratch_shapes=[pltpu.VMEM((D // LANES, tm, LANES), F32)],
        compiler_params=pltpu.CompilerParams(
            dimension_semantics=("parallel", "parallel"), vmem_limit_bytes=VMEM_LIMIT),
        name="norm_permute",
    )(x, gain.reshape(1, D))


def _inproj_kernel(u_ref, *refs, gate_bias, rb, chunk):
    n_w = len(gate_bias)
    w_refs, b_refs, o_refs = refs[:n_w], refs[n_w:2 * n_w], refs[2 * n_w:]
    for r in range(rb):
        u = u_ref[r]
        for w_ref, b_ref, o_ref, sig in zip(w_refs, b_refs, o_refs, gate_bias):
            for c in range(w_ref.shape[1] // chunk):
                cols = slice(c * chunk, (c + 1) * chunk)
                acc = jnp.dot(u, w_ref[:, cols], preferred_element_type=F32)
                if sig:
                    acc = jax.nn.sigmoid(acc + b_ref[:, cols])
                o_ref[r, :, cols] = acc.astype(o_ref.dtype)


def _inproj(u, weights, rb, tm, chunk, name):
    B, R, L, D = u.shape
    ws = [w for w, _ in weights]
    bs = [jnp.zeros((1, w.shape[1]), F32) if b is None else b.reshape(1, -1) for w, b in weights]
    const = lambda a: pl.BlockSpec(a.shape, lambda b, r, i: (0, 0))
    return pl.pallas_call(
        functools.partial(_inproj_kernel, gate_bias=tuple(b is not None for _, b in weights),
                          rb=rb, chunk=chunk),
        grid=(B, R // rb, L // tm),
        in_specs=[pl.BlockSpec((None, rb, tm, D), lambda b, r, i: (b, r, i, 0))]
                 + [const(w) for w in ws] + [const(b) for b in bs],
        out_specs=[pl.BlockSpec((None, rb, tm, w.shape[1]), lambda b, r, i: (b, r, i, 0)) for w in ws],
        out_shape=[jax.ShapeDtypeStruct((B, R, L, w.shape[1]), BF16) for w in ws],
        compiler_params=pltpu.CompilerParams(
            dimension_semantics=("parallel", "parallel", "parallel"), vmem_limit_bytes=VMEM_LIMIT),
        name=name,
    )(u, *ws, *bs)


def _mem_kv_kernel(x_ref, g_ref, w_ref, o_ref):
    u = _rms(x_ref[...], g_ref[...]).astype(BF16)
    o_ref[...] = jnp.dot(u, w_ref[...], preferred_element_type=F32).astype(o_ref.dtype)


def _mem_kv(mem, gain, w):
    B, M, D = mem.shape
    N = w.shape[1]
    return pl.pallas_call(
        _mem_kv_kernel,
        grid=(B,),
        in_specs=[pl.BlockSpec((None, M, D), lambda b: (b, 0, 0)),
                  pl.BlockSpec((1, D), lambda b: (0, 0)),
                  pl.BlockSpec((D, N), lambda b: (0, 0))],
        out_specs=pl.BlockSpec((None, M, N), lambda b: (b, 0, 0)),
        out_shape=jax.ShapeDtypeStruct((B, M, N), BF16),
        compiler_params=pltpu.CompilerParams(
            dimension_semantics=("parallel",), vmem_limit_bytes=VMEM_LIMIT),
        name="mem_kv",
    )(mem, gain.reshape(1, D), w)


def _attn_block(q_ref, kp_ref, kc_ref, vp_ref, vc_ref, tab_ref, first, r, j):
    lane = lax.broadcasted_iota(I32, (Q_BLOCK, LANES), 1)
    low = lane < HEAD_DIM
    lse_tile = jnp.zeros((Q_BLOCK, LANES), F32)
    pairs = []
    if j > 0:
        first = 1
    for pair in range(ATTN_HEADS // 2):
        cols = slice(pair * LANES, (pair + 1) * LANES)
        q2 = q_ref[r, j * Q_BLOCK:(j + 1) * Q_BLOCK, cols] * (HEAD_DIM ** -0.5)
        if j == 0:
            k2 = jnp.concatenate([kp_ref[r, :, cols], kc_ref[r, :Q_BLOCK, cols]], axis=0)
            v2 = jnp.concatenate([vp_ref[r, :, cols], vc_ref[r, :Q_BLOCK, cols]], axis=0)
        else:
            k2 = kc_ref[r, (j - 1) * Q_BLOCK:(j + 1) * Q_BLOCK, cols]
            v2 = vc_ref[r, (j - 1) * Q_BLOCK:(j + 1) * Q_BLOCK, cols]
        outs = []
        for half in range(2):
            h = 2 * pair + half
            keep = low if half == 0 else jnp.logical_not(low)
            qm = jnp.where(keep, q2, jnp.zeros_like(q2))
            s = lax.dot_general(qm, k2, (((1,), (1,)), ((), ())), preferred_element_type=F32)
            s = s + tab_ref[first, h]
            m = jnp.max(s, axis=1, keepdims=True)
            p = jnp.exp(s - m)
            l = jnp.sum(p, axis=1, keepdims=True)
            o = jnp.dot(p.astype(BF16), v2, preferred_element_type=F32) * (1.0 / l)
            outs.append(o)
            lse_tile = jnp.where(lane == h, m + jnp.log(l), lse_tile)
        pairs.append(jnp.where(low, outs[0], outs[1]))
    return pairs, lse_tile


def _attn_kernel(q_ref, kp_ref, kc_ref, vp_ref, vc_ref, tab_ref, o_ref, lse_ref, *scratch, dil, qb):
    first = jnp.minimum(pl.program_id(1), 1)
    blocks = (q_ref, kp_ref, kc_ref, vp_ref, vc_ref, tab_ref)
    n_pair = ATTN_HEADS // 2
    if dil == 1:
        for j in range(qb):
            pairs, lse_tile = _attn_block(*blocks, first, 0, j)
            rows = slice(j * Q_BLOCK, (j + 1) * Q_BLOCK)
            for p in range(n_pair):
                o_ref[rows, p * LANES:(p + 1) * LANES] = pairs[p].astype(o_ref.dtype)
            lse_ref[rows, :] = lse_tile
        return

    o_scr, lse_scr = scratch

    unroll = 4

    def body(g, carry):
        for r in [g * unroll + t for t in range(unroll)]:
            for j in range(qb):
                pairs, lse_tile = _attn_block(*blocks, first, r, j)
                rows = pl.ds(j * Q_BLOCK * dil + r, Q_BLOCK, stride=dil)
                for p in range(n_pair):
                    o_scr[p, rows, :] = pairs[p]
                lse_scr[rows, :] = lse_tile
        return carry

    if dil == unroll:
        body(0, 0)
    else:
        lax.fori_loop(0, dil // unroll, body, 0)
    for p in range(n_pair):
        o_ref[:, p * LANES:(p + 1) * LANES] = o_scr[p].astype(o_ref.dtype)
    lse_ref[...] = lse_scr[...]


def _attn_bias_table(dil):
    slopes = jnp.power(2.0, -8.0 * jnp.arange(1, ATTN_HEADS + 1, dtype=F32) / ATTN_HEADS)
    iq = jnp.arange(Q_BLOCK)
    ik = jnp.arange(2 * Q_BLOCK)
    dist = iq[:, None] + Q_BLOCK - ik[None, :]
    band = (dist >= 0) & (dist <= Q_BLOCK)
    has_prev = jnp.stack([ik >= Q_BLOCK, jnp.ones_like(ik, dtype=bool)])
    mask = band[None] & has_prev[:, None, :]
    bias = -slopes[:, None, None] * (dist * dil).astype(F32)[None]
    return jnp.where(mask[:, None], bias[None], NEG_INF).astype(F32)


def _attention(qkv, dil, col0, name):
    B, _, L, _ = qkv.shape
    W = ATTN_WIDTH
    qb = max(1, 4 // dil)
    nb = L // (Q_BLOCK * qb)
    span = Q_BLOCK * qb * dil
    blk = lambda part, prev: (
        pl.BlockSpec((None, dil, Q_BLOCK, W), lambda b, n: (b, 0, jnp.maximum(n * qb - 1, 0), col0 + part))
        if prev else pl.BlockSpec((None, dil, Q_BLOCK * qb, W), lambda b, n: (b, 0, n, col0 + part)))
    tab = _attn_bias_table(dil)
    scratch = [] if dil == 1 else [pltpu.VMEM((ATTN_HEADS // 2, span, LANES), F32),
                                   pltpu.VMEM((span, LANES), F32)]
    o, lse = pl.pallas_call(
        functools.partial(_attn_kernel, dil=dil, qb=qb),
        grid=(B, nb),
        in_specs=[blk(0, False), blk(1, True), blk(1, False), blk(2, True), blk(2, False),
                  pl.BlockSpec(tab.shape, lambda b, n: (0, 0, 0, 0))],
        out_specs=[pl.BlockSpec((None, span, W), lambda b, n: (b, n, 0)),
                   pl.BlockSpec((None, span, LANES), lambda b, n: (b, n, 0))],
        out_shape=[jax.ShapeDtypeStruct((B, L * dil, W), BF16),
                   jax.ShapeDtypeStruct((B, L * dil, LANES), F32)],
        scratch_shapes=scratch,
        compiler_params=pltpu.CompilerParams(
            dimension_semantics=("parallel", "parallel"), vmem_limit_bytes=VMEM_LIMIT),
        name=name,
    )(qkv, qkv, qkv, qkv, qkv, tab)
    return o.reshape(B * L * dil, W), lse.reshape(B * L * dil, LANES)


def _mix_kernel(o1_ref, o2_ref, o3_ref, l1_ref, l2_ref, l3_ref, gb_ref, gc_ref, xc_ref, gch_ref, xch_ref,
                ga_ref, gv_ref, x_ref, e_ref, wa_ref, wc_ref, wo_ref, cw_ref, h_ref, z_scr, *, tm, seq):
    a1, a2, a3 = l1_ref[...], l2_ref[...], l3_ref[...]
    m = jnp.maximum(jnp.maximum(a1, a2), a3)
    e1, e2, e3 = jnp.exp(a1 - m), jnp.exp(a2 - m), jnp.exp(a3 - m)
    inv = 1.0 / (e1 + e2 + e3)
    y = jnp.zeros((tm, ATTN_WIDTH), F32)
    for e, o_ref in ((e1, o1_ref), (e2, o2_ref), (e3, o3_ref)):
        w = e * inv
        w_hi = w.astype(BF16)
        w_lo = (w - w_hi.astype(F32)).astype(BF16)
        w_heads = (jnp.dot(w_hi, e_ref[...], preferred_element_type=F32)
                   + jnp.dot(w_lo, e_ref[...], preferred_element_type=F32))
        y = y + w_heads * o_ref[...].astype(F32)
    branch_a = jnp.dot(y.astype(BF16), wa_ref[...], preferred_element_type=F32)

    at_start = (pl.program_id(0) * tm) % seq == 0
    halo = gch_ref[...].astype(F32) * xch_ref[...].astype(F32)
    z_scr[0:CONV_HALO, :] = jnp.where(at_start, jnp.zeros_like(halo), halo)
    z_scr[CONV_HALO:, :] = gc_ref[...].astype(F32) * xc_ref[...].astype(F32)
    conv = (cw_ref[0:1, :] * z_scr[pl.ds(CONV_HALO - 2, tm), :]
            + cw_ref[1:2, :] * z_scr[pl.ds(CONV_HALO - 1, tm), :]
            + cw_ref[2:3, :] * z_scr[pl.ds(CONV_HALO, tm), :])
    y_conv = gb_ref[...].astype(F32) * conv
    branch_c = jnp.dot(y_conv.astype(BF16), wc_ref[...], preferred_element_type=F32)

    mixed = ga_ref[...].astype(F32) * branch_a + gv_ref[...].astype(F32) * branch_c
    h_ref[...] = x_ref[...] + jnp.dot(mixed.astype(BF16), wo_ref[...], preferred_element_type=F32)


def _mix(o_list, lse_list, proj, gates, x2d, head_expand, wa, wc, wo, conv_w, seq, tm=512):
    T, D = x2d.shape
    W = ATTN_WIDTH
    row = lambda width, col: pl.BlockSpec((tm, width), lambda i: (i, col))
    halo = lambda col: pl.BlockSpec(
        (CONV_HALO, D), lambda i: (jnp.maximum(i * (tm // CONV_HALO) - 1, 0), col))
    full = lambda a: pl.BlockSpec(a.shape, lambda i: (0,) * a.ndim)
    return pl.pallas_call(
        functools.partial(_mix_kernel, tm=tm, seq=seq),
        grid=(T // tm,),
        in_specs=[row(W, 0)] * 3 + [row(LANES, 0)] * 3
                 + [row(D, 0), row(D, 1), row(D, 2), halo(1), halo(2)]
                 + [row(D, 0), row(D, 1), row(D, 0)]
                 + [full(head_expand), full(wa), full(wc), full(wo), full(conv_w)],
        out_specs=pl.BlockSpec((tm, D), lambda i: (i, 0)),
        out_shape=jax.ShapeDtypeStruct((T, D), F32),
        scratch_shapes=[pltpu.VMEM((tm + CONV_HALO, D), F32)],
        compiler_params=pltpu.CompilerParams(
            dimension_semantics=("parallel",), vmem_limit_bytes=VMEM_LIMIT),
        name="branch_mix",
    )(*o_list, *lse_list, proj, proj, proj, proj, proj, gates, gates, x2d,
      head_expand, wa, wc, wo, conv_w)


def _cross_kernel(h_ref, k_ref, v_ref, g2_ref, wq_ref, wo_ref, g3_ref, rw_ref, rb_ref, tri_ref,
                  h2_ref, u3_ref, eidx_ref, rank_ref, wts_ref, cnt_ref, run_scr, *, tm):
    @pl.when(pl.program_id(0) == 0)
    def _():
        run_scr[...] = jnp.zeros_like(run_scr)

    h = h_ref[...]
    u = _rms(h, g2_ref[...]).astype(BF16)
    q = jnp.dot(u, wq_ref[...], preferred_element_type=F32) * (MEM_HEAD_DIM ** -0.5)
    heads = []
    for hd in range(MEM_HEADS):
        cols = slice(hd * MEM_HEAD_DIM, (hd + 1) * MEM_HEAD_DIM)
        s = lax.dot_general(q[:, cols].astype(BF16), k_ref[:, cols], (((1,), (1,)), ((), ())),
                            preferred_element_type=F32)
        m = jnp.max(s, axis=1, keepdims=True)
        p = jnp.exp(s - m)
        l = jnp.sum(p, axis=1, keepdims=True)
        heads.append(jnp.dot(p.astype(BF16), v_ref[:, cols], preferred_element_type=F32) * (1.0 / l))
    o = jnp.concatenate(heads, axis=1).astype(BF16)
    h2 = h + jnp.dot(o, wo_ref[...], preferred_element_type=F32)
    h2_ref[...] = h2
    u3 = _rms(h2, g3_ref[...])
    _pack_rows(u3, u3_ref, tm)

    logits = jnp.dot(u3.astype(BF16), rw_ref[...], preferred_element_type=F32) + rb_ref[...]
    lane = lax.broadcasted_iota(I32, (tm, LANES), 1).astype(F32)
    work = logits
    picked = jnp.zeros((tm, LANES), F32)
    tops, idxs = [], []
    for _ in range(TOP_K):
        mk = jnp.max(work, axis=1, keepdims=True)
        ik = jnp.min(jnp.where(work == mk, lane, float(LANES)), axis=1, keepdims=True)
        sel = lane == ik
        work = jnp.where(sel, -jnp.inf, work)
        picked = picked + sel.astype(F32)
        tops.append(mk)
        idxs.append(ik)
    exps = [jnp.exp(t - tops[0]) for t in tops]
    inv = 1.0 / (exps[0] + exps[1] + exps[2] + exps[3])

    before = jnp.dot(tri_ref[...], picked.astype(BF16), preferred_element_type=F32) + run_scr[...]
    eidx = jnp.zeros((tm, LANES), I32)
    rank = jnp.zeros((tm, LANES), I32)
    wts = jnp.zeros((tm, LANES), F32)
    for k in range(TOP_K):
        rk = jnp.sum(jnp.where(lane == idxs[k], before, 0.0), axis=1, keepdims=True)
        eidx = jnp.where(lane == k, idxs[k].astype(I32), eidx)
        rank = jnp.where(lane == k, rk.astype(I32), rank)
        wts = jnp.where(lane == k, exps[k] * inv, wts)
    eidx_ref[...] = eidx
    rank_ref[...] = rank
    wts_ref[...] = wts
    run_scr[...] = run_scr[...] + jnp.sum(picked, axis=0, keepdims=True)
    cnt_ref[...] = run_scr[...]


def _cross_router(h1, kv, g2, wq, wo, g3, rw, rb, seq, tm=512):
    T, D = h1.shape
    n_mem = kv.shape[1]
    tri = (jnp.arange(tm)[:, None] > jnp.arange(tm)[None, :]).astype(BF16)
    full = lambda a: pl.BlockSpec(a.shape, lambda i: (0,) * a.ndim)
    kvspec = lambda col: pl.BlockSpec((None, n_mem, MEM_WIDTH), lambda i: ((i * tm) // seq, 0, col))
    tile = lambda width: pl.BlockSpec((tm, width), lambda i: (i, 0))
    return pl.pallas_call(
        functools.partial(_cross_kernel, tm=tm),
        grid=(T // tm,),
        in_specs=[tile(D), kvspec(0), kvspec(1), full(g2), full(wq), full(wo), full(g3),
                  full(rw), full(rb), full(tri)],
        out_specs=[tile(D), pl.BlockSpec((tm * PACK_ROWS, LANES), lambda i: (i, 0)),
                   tile(LANES), tile(LANES), tile(LANES),
                   pl.BlockSpec((1, LANES), lambda i: (0, 0))],
        out_shape=[jax.ShapeDtypeStruct((T, D), F32), jax.ShapeDtypeStruct((T * PACK_ROWS, LANES), F32),
                   jax.ShapeDtypeStruct((T, LANES), I32), jax.ShapeDtypeStruct((T, LANES), I32),
                   jax.ShapeDtypeStruct((T, LANES), F32), jax.ShapeDtypeStruct((1, LANES), F32)],
        scratch_shapes=[pltpu.VMEM((1, LANES), F32)],
        compiler_params=pltpu.CompilerParams(
            dimension_semantics=("arbitrary",), vmem_limit_bytes=VMEM_LIMIT),
        name="cross_router",
    )(h1, kv, kv, g2, wq, wo, g3, rw, rb, tri)


def _wait_copies(src_hbm, dst, sem, rows):
    pltpu.make_async_copy(src_hbm.at[pl.ds(0, rows), :], dst.at[pl.ds(0, rows), :], sem).wait()


def _dispatch_kernel(pos_ref, u_hbm, x_hbm, stage, load_sem, row_sem, *, tm, n_token_steps):
    i = pl.program_id(0)
    n = pl.num_programs(0)
    batch = tm * TOP_K * PACK_ROWS
    tile_rows = tm * PACK_ROWS

    def tile_copy(step):
        first = pl.multiple_of(jnp.minimum(step, n_token_steps - 1) * tile_rows, tile_rows)
        return pltpu.make_async_copy(u_hbm.at[pl.ds(first, tile_rows), :], stage.at[step % 3],
                                     load_sem.at[step % 3])

    @pl.when(i == 0)
    def _():
        tile_copy(i).start()

    @pl.when(i + 1 < n)
    def _():
        tile_copy(i + 1).start()

    tile_copy(i).wait()
    src_tile = stage.at[i % 3]
    group = 4

    def body(g, carry):
        rows = [pos_ref[0, g * (group * TOP_K) + j] for j in range(group * TOP_K)]
        for j, p in enumerate(rows):
            t = g * group + j // TOP_K
            src = src_tile.at[pl.ds(pl.multiple_of(t * PACK_ROWS, PACK_ROWS), PACK_ROWS), :]
            dst = x_hbm.at[pl.ds(pl.multiple_of(p * PACK_ROWS, PACK_ROWS), PACK_ROWS), :]
            pltpu.make_async_copy(src, dst, row_sem.at[i % 2]).start(priority=j % 2)
        return carry

    lax.fori_loop(0, tm // group, body, 0)

    @pl.when(i > 0)
    def _():
        _wait_copies(u_hbm, x_hbm, row_sem.at[1 - i % 2], batch)

    @pl.when(i == n - 1)
    def _():
        _wait_copies(u_hbm, x_hbm, row_sem.at[i % 2], batch)


def _dispatch(u_packed, pos, fill_rows, tm=512):
    T = pos.shape[0]
    per_step = TOP_K * tm
    dst_rows = jnp.concatenate([pos.reshape(-1), fill_rows])
    n_rows = dst_rows.shape[0]
    assert n_rows % per_step == 0
    steps = n_rows // per_step
    return pl.pallas_call(
        functools.partial(_dispatch_kernel, tm=tm, n_token_steps=T // tm),
        grid=(steps,),
        in_specs=[pl.BlockSpec((None, 1, per_step), lambda i: (i, 0, 0), memory_space=pltpu.SMEM),
                  pl.BlockSpec(memory_space=pl.ANY)],
        out_specs=pl.BlockSpec(memory_space=pl.ANY),
        out_shape=jax.ShapeDtypeStruct((n_rows * PACK_ROWS, LANES), F32),
        scratch_shapes=[pltpu.VMEM((3, tm * PACK_ROWS, LANES), F32),
                        pltpu.SemaphoreType.DMA((3,)), pltpu.SemaphoreType.DMA((2,))],
        compiler_params=pltpu.CompilerParams(dimension_semantics=("arbitrary",)),
        name="dispatch_rows",
    )(dst_rows.reshape(steps, 1, per_step), u_packed)


def _expert_kernel(bexp_ref, slot_ref, next_ref, nused_ref, x_ref, wg_hbm, bg_ref, wu_hbm, bu_ref,
                   wd_hbm, bd_ref, y_ref, stage, w_bf, sem):
    i = pl.program_id(0)
    used = i < nused_ref[0]
    e = bexp_ref[i]
    slot = slot_ref[i]
    changed = jnp.logical_or(i == 0, e != bexp_ref[jnp.maximum(i - 1, 0)])
    weights = (wg_hbm, wu_hbm, wd_hbm)

    def fetch(expert, into):
        return [pltpu.make_async_copy(w.at[expert], stage.at[into, j], sem.at[into])
                for j, w in enumerate(weights)]

    @pl.when(jnp.logical_and(changed, used))
    def _():
        @pl.when(i == 0)
        def _():
            for c in fetch(e, slot):
                c.start()

        for c in fetch(e, slot):
            c.wait()
        for j in range(len(weights)):
            w_bf[j] = stage[slot, j].astype(BF16)

        @pl.when(next_ref[i] >= 0)
        def _():
            for c in fetch(next_ref[i], 1 - slot):
                c.start(priority=1)

    @pl.when(used)
    def _():
        x = _unpack_rows(x_ref, 0, MOE_ROWS).astype(BF16)
        gate = jnp.minimum(jnp.dot(x, w_bf[0], preferred_element_type=F32) + bg_ref[...], SWIGLU_LIMIT)
        lin = jnp.clip(jnp.dot(x, w_bf[1], preferred_element_type=F32) + bu_ref[...],
                       -SWIGLU_LIMIT, SWIGLU_LIMIT)
        hdn = gate * jax.nn.sigmoid(SWIGLU_ALPHA * gate) * (lin + 1.0)
        y = jnp.dot(hdn.astype(BF16), w_bf[2], preferred_element_type=F32) + bd_ref[...]
        _pack_rows(y, y_ref, MOE_ROWS)

    @pl.when(jnp.logical_not(used))
    def _():
        y_ref[...] = jnp.zeros_like(y_ref)


def _experts(x_packed, block_expert, block_slot, block_next, n_used, w_eg, b_eg, w_eu, b_eu, w_ed, b_ed):
    nb = block_expert.shape[0]
    E, D, F = w_eg.shape
    assert D == F
    rows = lambda i, be, sl, nx, nu: (i, 0)
    bspec = lambda n: pl.BlockSpec((None, 1, n), lambda i, be, sl, nx, nu: (be[i], 0, 0))
    anyspec = pl.BlockSpec(memory_space=pl.ANY)
    grid_spec = pltpu.PrefetchScalarGridSpec(
        num_scalar_prefetch=4,
        grid=(nb,),
        in_specs=[pl.BlockSpec((MOE_ROWS * PACK_ROWS, LANES), rows),
                  anyspec, bspec(F), anyspec, bspec(F), anyspec, bspec(D)],
        out_specs=pl.BlockSpec((MOE_ROWS * PACK_ROWS, LANES), rows),
        scratch_shapes=[pltpu.VMEM((2, 3, D, F), F32), pltpu.VMEM((3, D, F), BF16),
                        pltpu.SemaphoreType.DMA((2,))],
    )
    return pl.pallas_call(
        _expert_kernel,
        grid_spec=grid_spec,
        out_shape=jax.ShapeDtypeStruct(x_packed.shape, F32),
        compiler_params=pltpu.CompilerParams(
            dimension_semantics=("arbitrary",), vmem_limit_bytes=VMEM_LIMIT),
        name="expert_ffn",
    )(block_expert, block_slot, block_next, n_used, x_packed, w_eg, b_eg.reshape(E, 1, F), w_eu,
      b_eu.reshape(E, 1, F), w_ed, b_ed.reshape(E, 1, D))


def _gather_packed(pos_ref, src_hbm, dst, sem, n):
    group = 16

    def body(g, carry):
        rows = [pos_ref[0, g * group + j] for j in range(group)]
        for j, p in enumerate(rows):
            r = g * group + j
            pltpu.make_async_copy(
                src_hbm.at[pl.ds(pl.multiple_of(p * PACK_ROWS, PACK_ROWS), PACK_ROWS), :],
                dst.at[pl.ds(pl.multiple_of(r * PACK_ROWS, PACK_ROWS), PACK_ROWS), :],
                sem).start(priority=j % 2)
        return carry
    lax.fori_loop(0, n // group, body, 0)


def _combine_kernel(pos0_ref, pos1_ref, y_hbm, h_ref, w_ref, g_ref, o_ref, ybuf, sem):
    i = pl.program_id(0)
    n = pl.num_programs(0)
    slot = i % 2
    rows = TOP_K * COMBINE_ROWS

    @pl.when(i == 0)
    def _():
        _gather_packed(pos0_ref, y_hbm, ybuf.at[0], sem.at[0], rows)

    @pl.when(i + 1 < n)
    def _():
        _gather_packed(pos1_ref, y_hbm, ybuf.at[1 - slot], sem.at[1 - slot], rows)

    _wait_copies(y_hbm, ybuf.at[slot], sem.at[slot], rows * PACK_ROWS)
    acc = h_ref[...]
    w = w_ref[...]
    for k in range(TOP_K):
        acc = acc + w[:, k:k + 1] * _unpack_rows(ybuf.at[slot], k * COMBINE_ROWS * PACK_ROWS, COMBINE_ROWS)
    o_ref[...] = _rms(acc, g_ref[...])


def _combine(y_rows, pos_kmajor, h2, wts, g_final):
    T, D = h2.shape
    tm = COMBINE_ROWS
    nt = T // tm
    posspec = lambda off: pl.BlockSpec(
        (None, 1, TOP_K * tm), lambda i: (jnp.minimum(i + off, nt - 1), 0, 0), memory_space=pltpu.SMEM)
    return pl.pallas_call(
        _combine_kernel,
        grid=(nt,),
        in_specs=[posspec(0), posspec(1), pl.BlockSpec(memory_space=pl.ANY),
                  pl.BlockSpec((tm, D), lambda i: (i, 0)), pl.BlockSpec((tm, LANES), lambda i: (i, 0)),
                  pl.BlockSpec((1, D), lambda i: (0, 0))],
        out_specs=pl.BlockSpec((tm, D), lambda i: (i, 0)),
        out_shape=jax.ShapeDtypeStruct((T, D), F32),
        scratch_shapes=[pltpu.VMEM((2, TOP_K * tm * PACK_ROWS, LANES), F32), pltpu.SemaphoreType.DMA((2,))],
        compiler_params=pltpu.CompilerParams(
            dimension_semantics=("arbitrary",), vmem_limit_bytes=VMEM_LIMIT),
        name="combine_norm",
    )(pos_kmajor, pos_kmajor, y_rows, h2, wts, g_final.reshape(1, D))


def _layer(h, mem, norm_mix, w_in, conv_w, w_branch_attn, w_branch_conv, w_gate, b_gate, w_out,
           norm_cross, norm_mem, w_cq, w_ckv, w_co, norm_moe, router_w, router_b,
           w_eg, b_eg, w_eu, b_eu, w_ed, b_ed, norm_final):
    B, S, D = h.shape
    T = B * S
    W3 = 3 * ATTN_WIDTH
    n_attn = len(ATTN_GROUPS) * W3

    dils = tuple(dil for _, dil in ATTN_GROUPS)
    assert dils[0] == 1
    u_all = _norm_permute(h, norm_mix, dils[1:])
    w_in_b = w_in.astype(BF16)
    w_a = jnp.concatenate([w_in_b[:, n_attn:], w_in_b[:, :W3]], axis=1)
    proj, gates = _inproj(u_all[0], [(w_a, None), (w_gate.astype(BF16), b_gate)], 1, 512, 512,
                          name="inproj_conv_g1_gates")
    proj2d = proj.reshape(T, -1)
    gates2d = gates.reshape(T, -1)

    o_list, lse_list = [], []
    for g, (window, dil) in enumerate(ATTN_GROUPS):
        assert window // dil == Q_BLOCK
        if g == 0:
            qkv, col0 = proj, 3 * D // ATTN_WIDTH
        else:
            L = S // dil
            tm = min(L, 1024)
            qkv, = _inproj(u_all[g], [(w_in_b[:, g * W3:(g + 1) * W3], None)], 1024 // tm, tm, 512,
                           name=f"inproj_g{g + 1}")
            col0 = 0
        o, lse = _attention(qkv, dil, col0, name=f"dilated_attn_g{g + 1}")
        o_list.append(o)
        lse_list.append(lse)

    head_expand = (jnp.arange(LANES)[:, None] == jnp.arange(ATTN_WIDTH)[None, :] // HEAD_DIM).astype(BF16)
    h1 = _mix(o_list, lse_list, proj2d, gates2d, h.reshape(T, D), head_expand,
              w_branch_attn.astype(BF16), w_branch_conv.astype(BF16), w_out.astype(BF16), conv_w, S)

    kv = _mem_kv(mem, norm_mem, w_ckv.astype(BF16))
    rw = jnp.zeros((D, LANES), BF16).at[:, :N_EXPERTS].set(router_w.astype(BF16))
    rb = jnp.full((1, LANES), -jnp.inf, F32).at[0, :N_EXPERTS].set(router_b)
    h2, u3, eidx, rank, wts, counts = _cross_router(
        h1, kv, norm_cross.reshape(1, D), w_cq.astype(BF16), w_co.astype(BF16), norm_moe.reshape(1, D),
        rw, rb, S)

    nb = -(-(T * TOP_K) // MOE_ROWS) + N_EXPERTS
    cnt = counts[0, :N_EXPERTS].astype(I32)
    padded = (cnt + MOE_ROWS - 1) // MOE_ROWS * MOE_ROWS
    pend = jnp.cumsum(padded)
    pstart = pend - padded
    experts = jnp.arange(N_EXPERTS, dtype=I32)
    lookup = lambda table, idx: jnp.sum(jnp.where(idx[..., None] == jnp.arange(table.shape[0], dtype=I32),
                                                  table, 0), axis=-1)
    pos = lookup(pstart, eidx[:, :TOP_K]) + rank[:, :TOP_K]
    block_start = jnp.arange(nb, dtype=I32) * MOE_ROWS
    block_expert = jnp.minimum(jnp.sum(block_start[:, None] >= pend[None, :], axis=1), N_EXPERTS - 1).astype(I32)
    active = cnt > 0
    slot_e = (jnp.cumsum(active.astype(I32)) - 1) % 2
    later = jnp.where(active[None, :] & (experts[None, :] > experts[:, None]), experts[None, :], N_EXPERTS)
    next_e = jnp.min(later, axis=1)
    next_e = jnp.where(next_e == N_EXPERTS, -1, next_e).astype(I32)
    n_pad = padded - cnt
    pad_end = jnp.cumsum(n_pad)
    fill = jnp.arange(nb * MOE_ROWS - T * TOP_K, dtype=I32)
    fill_seg = jnp.sum(fill[:, None] >= pad_end[None, :], axis=1)
    seg_first_row = jnp.concatenate([pstart + cnt, pend[-1:]])
    seg_first_fill = jnp.concatenate([pad_end - n_pad, pad_end[-1:]])
    fill_rows = (lookup(seg_first_row - seg_first_fill, fill_seg) + fill).astype(I32)

    x_rows = _dispatch(u3, pos, fill_rows)
    y_rows = _experts(x_rows, block_expert, slot_e[block_expert].astype(I32), next_e[block_expert],
                      (pend[-1:] // MOE_ROWS).astype(I32), w_eg, b_eg, w_eu, b_eu, w_ed, b_ed)

    pos_kmajor = pos.reshape(T // COMBINE_ROWS, COMBINE_ROWS, TOP_K).transpose(0, 2, 1).reshape(
        T // COMBINE_ROWS, 1, TOP_K * COMBINE_ROWS)
    out = _combine(y_rows, pos_kmajor, h2, wts, norm_final)
    return out.reshape(B, S, D)


def kernel(x, mem, norm_mix, w_in, conv_w, w_branch_attn, w_branch_conv, w_gate, b_gate, w_out, norm_cross, norm_mem, w_cq, w_ckv, w_co, norm_moe, router_w, router_b, w_exp_gate, b_exp_gate, w_exp_up, b_exp_up, w_exp_down, b_exp_down, norm_final):
    depth = norm_mix.shape[0]
    assert depth == 1, "the final norm is fused into the last layer's combine step"
    return _layer(x, mem, norm_mix[0], w_in[0], conv_w[0], w_branch_attn[0], w_branch_conv[0], w_gate[0],
                  b_gate[0], w_out[0], norm_cross[0], norm_mem[0], w_cq[0], w_ckv[0], w_co[0], norm_moe[0],
                  router_w[0], router_b[0], w_exp_gate[0], b_exp_gate[0], w_exp_up[0], b_exp_up[0],
                  w_exp_down[0], b_exp_down[0], norm_final)
```

```python
import functools

import jax
import jax.numpy as jnp
from jax import lax
from jax.experimental import pallas as pl
from jax.experimental.pallas import tpu as pltpu

F32 = jnp.float32
BF16 = jnp.bfloat16
I32 = jnp.int32

D_MODEL = 1024
ATTN_GROUPS = ((128, 1), (512, 4), (2048, 16))
ATTN_HEADS = 8
HEAD_DIM = 64
ATTN_WIDTH = ATTN_HEADS * HEAD_DIM
Q_BLOCK = 128
CONV_K = 3
MEM_HEADS = 4
MEM_HEAD_DIM = 128
MEM_WIDTH = MEM_HEADS * MEM_HEAD_DIM
N_EXPERTS = 32
TOP_K = 4
SWIGLU_LIMIT = 7.0
SWIGLU_ALPHA = 1.702
EPS = 1e-6
NEG_INF = -1e30

LANES = 128
VMEM_LIMIT = 56 * 1024 * 1024

MOE_ROWS = 512
COMBINE_ROWS = 512
CONV_HALO = 16
META_ROWS = 16


def _rms(x, g):
    ms = jnp.mean(x * x, axis=-1, keepdims=True)
    return x * lax.rsqrt(ms + EPS) * g


PACK_ROWS = D_MODEL // LANES


def _pack_rows(y, out_ref, n):
    for c in range(PACK_ROWS):
        out_ref[pl.ds(c, n, stride=PACK_ROWS), :] = y[:, c * LANES:(c + 1) * LANES]


def _unpack_rows(ref, start, n):
    return jnp.concatenate(
        [ref[pl.ds(start + c, n, stride=PACK_ROWS), :] for c in range(PACK_ROWS)], axis=1)


def _norm_permute_kernel(x_ref, g_ref, *refs, dils, tm):
    out_refs, slab = refs[:-1], refs[-1]
    u = _rms(x_ref[...], g_ref[...])
    out_refs[0][...] = u.astype(BF16)
    n_slab = u.shape[1] // LANES
    for c in range(n_slab):
        slab[c] = u[:, c * LANES:(c + 1) * LANES]
    for o_ref, dil in zip(out_refs[1:], dils):
        for r in range(dil):
            for c in range(n_slab):
                o_ref[r, :, c * LANES:(c + 1) * LANES] = (
                    slab[c, pl.ds(r, tm // dil, stride=dil), :].astype(BF16))


def _norm_permute(x, gain, dils, tm=1024):
    B, S, D = x.shape
    outs = [jax.ShapeDtypeStruct((B, 1, S, D), BF16)]
    specs = [pl.BlockSpec((None, None, tm, D), lambda b, i: (b, 0, i, 0))]
    for dil in dils:
        outs.append(jax.ShapeDtypeStruct((B, dil, S // dil, D), BF16))
        specs.append(pl.BlockSpec((None, dil, tm // dil, D), lambda b, i: (b, 0, i, 0)))
    return pl.pallas_call(
        functools.partial(_norm_permute_kernel, dils=dils, tm=tm),
        grid=(B, S // tm),
        in_specs=[pl.BlockSpec((None, tm, D), lambda b, i: (b, i, 0)),
                  pl.BlockSpec((1, D), lambda b, i: (0, 0))],
        out_specs=specs,
        out_shape=outs,
        scratch_shapes=[pltpu.VMEM((D // LANES, tm, LANES), F32)],
        compiler_params=pltpu.CompilerParams(
            dimension_semantics=("parallel", "parallel"), vmem_limit_bytes=VMEM_LIMIT),
        name="norm_permute",
    )(x, gain.reshape(1, D))


def _inproj_kernel(u_ref, *refs, gate_bias, rb, chunk):
    n_w = len(gate_bias)
    w_refs, b_refs, o_refs = refs[:n_w], refs[n_w:2 * n_w], refs[2 * n_w:]
    for r in range(rb):
        u = u_ref[r]
        for w_ref, b_ref, o_ref, sig in zip(w_refs, b_refs, o_refs, gate_bias):
            for c in range(w_ref.shape[1] // chunk):
                cols = slice(c * chunk, (c + 1) * chunk)
                acc = jnp.dot(u, w_ref[:, cols], preferred_element_type=F32)
                if sig:
                    acc = jax.nn.sigmoid(acc + b_ref[:, cols])
                o_ref[r, :, cols] = acc.astype(o_ref.dtype)


def _inproj(u, weights, rb, tm, chunk, name):
    B, R, L, D = u.shape
    ws = [w for w, _ in weights]
    bs = [jnp.zeros((1, w.shape[1]), F32) if b is None else b.reshape(1, -1) for w, b in weights]
    const = lambda a: pl.BlockSpec(a.shape, lambda b, r, i: (0, 0))
    return pl.pallas_call(
        functools.partial(_inproj_kernel, gate_bias=tuple(b is not None for _, b in weights),
                          rb=rb, chunk=chunk),
        grid=(B, R // rb, L // tm),
        in_specs=[pl.BlockSpec((None, rb, tm, D), lambda b, r, i: (b, r, i, 0))]
                 + [const(w) for w in ws] + [const(b) for b in bs],
        out_specs=[pl.BlockSpec((None, rb, tm, w.shape[1]), lambda b, r, i: (b, r, i, 0)) for w in ws],
        out_shape=[jax.ShapeDtypeStruct((B, R, L, w.shape[1]), BF16) for w in ws],
        compiler_params=pltpu.CompilerParams(
            dimension_semantics=("parallel", "parallel", "parallel"), vmem_limit_bytes=VMEM_LIMIT),
        name=name,
    )(u, *ws, *bs)


def _mem_kv_kernel(x_ref, g_ref, w_ref, o_ref):
    u = _rms(x_ref[...], g_ref[...]).astype(BF16)
    o_ref[...] = jnp.dot(u, w_ref[...], preferred_element_type=F32).astype(o_ref.dtype)


def _mem_kv(mem, gain, w):
    B, M, D = mem.shape
    N = w.shape[1]
    return pl.pallas_call(
        _mem_kv_kernel,
        grid=(B,),
        in_specs=[pl.BlockSpec((None, M, D), lambda b: (b, 0, 0)),
                  pl.BlockSpec((1, D), lambda b: (0, 0)),
                  pl.BlockSpec((D, N), lambda b: (0, 0))],
        out_specs=pl.BlockSpec((None, M, N), lambda b: (b, 0, 0)),
        out_shape=jax.ShapeDtypeStruct((B, M, N), BF16),
        compiler_params=pltpu.CompilerParams(
            dimension_semantics=("parallel",), vmem_limit_bytes=VMEM_LIMIT),
        name="mem_kv",
    )(mem, gain.reshape(1, D), w)


def _attn_block(q_ref, kp_ref, kc_ref, vp_ref, vc_ref, tab_ref, first, r, j):
    lane = lax.broadcasted_iota(I32, (Q_BLOCK, LANES), 1)
    low = lane < HEAD_DIM
    lse_tile = jnp.zeros((Q_BLOCK, LANES), F32)
    pairs = []
    if j > 0:
        first = 1
    for pair in range(ATTN_HEADS // 2):
        cols = slice(pair * LANES, (pair + 1) * LANES)
        q2 = q_ref[r, j * Q_BLOCK:(j + 1) * Q_BLOCK, cols] * (HEAD_DIM ** -0.5)
        if j == 0:
            k2 = jnp.concatenate([kp_ref[r, :, cols], kc_ref[r, :Q_BLOCK, cols]], axis=0)
            v2 = jnp.concatenate([vp_ref[r, :, cols], vc_ref[r, :Q_BLOCK, cols]], axis=0)
        else:
            k2 = kc_ref[r, (j - 1) * Q_BLOCK:(j + 1) * Q_BLOCK, cols]
            v2 = vc_ref[r, (j - 1) * Q_BLOCK:(j + 1) * Q_BLOCK, cols]
        outs = []
        for half in range(2):
            h = 2 * pair + half
            keep = low if half == 0 else jnp.logical_not(low)
            qm = jnp.where(keep, q2, jnp.zeros_like(q2))
            s = lax.dot_general(qm, k2, (((1,), (1,)), ((), ())), preferred_element_type=F32)
            s = s + tab_ref[first, h]
            m = jnp.max(s, axis=1, keepdims=True)
            p = jnp.exp(s - m)
            l = jnp.sum(p, axis=1, keepdims=True)
            o = jnp.dot(p.astype(BF16), v2, preferred_element_type=F32) * (1.0 / l)
            outs.append(o)
            lse_tile = jnp.where(lane == h, m + jnp.log(l), lse_tile)
        pairs.append(jnp.where(low, outs[0], outs[1]))
    return pairs, lse_tile


def _attn_kernel(q_ref, kp_ref, kc_ref, vp_ref, vc_ref, tab_ref, o_ref, lse_ref, *scratch, dil, qb):
    first = jnp.minimum(pl.program_id(1), 1)
    blocks = (q_ref, kp_ref, kc_ref, vp_ref, vc_ref, tab_ref)
    n_pair = ATTN_HEADS // 2
    if dil == 1:
        for j in range(qb):
            pairs, lse_tile = _attn_block(*blocks, first, 0, j)
            rows = slice(j * Q_BLOCK, (j + 1) * Q_BLOCK)
            for p in range(n_pair):
                o_ref[rows, p * LANES:(p + 1) * LANES] = pairs[p].astype(o_ref.dtype)
            lse_ref[rows, :] = lse_tile
        return

    o_scr, lse_scr = scratch

    unroll = min(dil, 8)

    def body(g, carry):
        for r in [g * unroll + t for t in range(unroll)]:
            for j in range(qb):
                pairs, lse_tile = _attn_block(*blocks, first, r, j)
                rows = pl.ds(j * Q_BLOCK * dil + r, Q_BLOCK, stride=dil)
                for p in range(n_pair):
                    o_scr[p, rows, :] = pairs[p]
                lse_scr[rows, :] = lse_tile
        return carry

    if dil == unroll:
        body(0, 0)
    else:
        lax.fori_loop(0, dil // unroll, body, 0)
    for p in range(n_pair):
        o_ref[:, p * LANES:(p + 1) * LANES] = o_scr[p].astype(o_ref.dtype)
    lse_ref[...] = lse_scr[...]


def _attn_bias_table(dil):
    slopes = jnp.power(2.0, -8.0 * jnp.arange(1, ATTN_HEADS + 1, dtype=F32) / ATTN_HEADS)
    iq = jnp.arange(Q_BLOCK)
    ik = jnp.arange(2 * Q_BLOCK)
    dist = iq[:, None] + Q_BLOCK - ik[None, :]
    band = (dist >= 0) & (dist <= Q_BLOCK)
    has_prev = jnp.stack([ik >= Q_BLOCK, jnp.ones_like(ik, dtype=bool)])
    mask = band[None] & has_prev[:, None, :]
    bias = -slopes[:, None, None] * (dist * dil).astype(F32)[None]
    return jnp.where(mask[:, None], bias[None], NEG_INF).astype(F32)


def _attention(qkv, dil, col0, name):
    B, _, L, _ = qkv.shape
    W = ATTN_WIDTH
    qb = max(1, 4 // dil)
    nb = L // (Q_BLOCK * qb)
    span = Q_BLOCK * qb * dil
    blk = lambda part, prev: (
        pl.BlockSpec((None, dil, Q_BLOCK, W), lambda b, n: (b, 0, jnp.maximum(n * qb - 1, 0), col0 + part))
        if prev else pl.BlockSpec((None, dil, Q_BLOCK * qb, W), lambda b, n: (b, 0, n, col0 + part)))
    tab = _attn_bias_table(dil)
    scratch = [] if dil == 1 else [pltpu.VMEM((ATTN_HEADS // 2, span, LANES), F32),
                                   pltpu.VMEM((span, LANES), F32)]
    o, lse = pl.pallas_call(
        functools.partial(_attn_kernel, dil=dil, qb=qb),
        grid=(B, nb),
        in_specs=[blk(0, False), blk(1, True), blk(1, False), blk(2, True), blk(2, False),
                  pl.BlockSpec(tab.shape, lambda b, n: (0, 0, 0, 0))],
        out_specs=[pl.BlockSpec((None, span, W), lambda b, n: (b, n, 0)),
                   pl.BlockSpec((None, span, LANES), lambda b, n: (b, n, 0))],
        out_shape=[jax.ShapeDtypeStruct((B, L * dil, W), BF16),
                   jax.ShapeDtypeStruct((B, L * dil, LANES), F32)],
        scratch_shapes=scratch,
        compiler_params=pltpu.CompilerParams(
            dimension_semantics=("parallel", "parallel"), vmem_limit_bytes=VMEM_LIMIT),
        name=name,
    )(qkv, qkv, qkv, qkv, qkv, tab)
    return o.reshape(B * L * dil, W), lse.reshape(B * L * dil, LANES)


def _mix_kernel(o1_ref, o2_ref, o3_ref, l1_ref, l2_ref, l3_ref, gb_ref, gc_ref, xc_ref, gch_ref, xch_ref,
                ga_ref, gv_ref, x_ref, e_ref, wa_ref, wc_ref, wo_ref, cw_ref, h_ref, z_scr, *, tm, seq):
    a1, a2, a3 = l1_ref[...], l2_ref[...], l3_ref[...]
    m = jnp.maximum(jnp.maximum(a1, a2), a3)
    e1, e2, e3 = jnp.exp(a1 - m), jnp.exp(a2 - m), jnp.exp(a3 - m)
    inv = 1.0 / (e1 + e2 + e3)
    y = jnp.zeros((tm, ATTN_WIDTH), F32)
    for e, o_ref in ((e1, o1_ref), (e2, o2_ref), (e3, o3_ref)):
        w = e * inv
        w_hi = w.astype(BF16)
        w_lo = (w - w_hi.astype(F32)).astype(BF16)
        w_heads = (jnp.dot(w_hi, e_ref[...], preferred_element_type=F32)
                   + jnp.dot(w_lo, e_ref[...], preferred_element_type=F32))
        y = y + w_heads * o_ref[...].astype(F32)
    branch_a = jnp.dot(y.astype(BF16), wa_ref[...], preferred_element_type=F32)

    at_start = (pl.program_id(0) * tm) % seq == 0
    halo = gch_ref[...].astype(F32) * xch_ref[...].astype(F32)
    z_scr[0:CONV_HALO, :] = jnp.where(at_start, jnp.zeros_like(halo), halo)
    z_scr[CONV_HALO:, :] = gc_ref[...].astype(F32) * xc_ref[...].astype(F32)
    conv = (cw_ref[0:1, :] * z_scr[pl.ds(CONV_HALO - 2, tm), :]
            + cw_ref[1:2, :] * z_scr[pl.ds(CONV_HALO - 1, tm), :]
            + cw_ref[2:3, :] * z_scr[pl.ds(CONV_HALO, tm), :])
    y_conv = gb_ref[...].astype(F32) * conv
    branch_c = jnp.dot(y_conv.astype(BF16), wc_ref[...], preferred_element_type=F32)

    mixed = ga_ref[...].astype(F32) * branch_a + gv_ref[...].astype(F32) * branch_c
    h_ref[...] = x_ref[...] + jnp.dot(mixed.astype(BF16), wo_ref[...], preferred_element_type=F32)


def _mix(o_list, lse_list, proj, gates, x2d, head_expand, wa, wc, wo, conv_w, seq, tm=512):
    T, D = x2d.shape
    W = ATTN_WIDTH
    row = lambda width, col: pl.BlockSpec((tm, width), lambda i: (i, col))
    halo = lambda col: pl.BlockSpec(
        (CONV_HALO, D), lambda i: (jnp.maximum(i * (tm // CONV_HALO) - 1, 0), col))
    full = lambda a: pl.BlockSpec(a.shape, lambda i: (0,) * a.ndim)
    return pl.pallas_call(
        functools.partial(_mix_kernel, tm=tm, seq=seq),
        grid=(T // tm,),
        in_specs=[row(W, 0)] * 3 + [row(LANES, 0)] * 3
                 + [row(D, 0), row(D, 1), row(D, 2), halo(1), halo(2)]
                 + [row(D, 0), row(D, 1), row(D, 0)]
                 + [full(head_expand), full(wa), full(wc), full(wo), full(conv_w)],
        out_specs=pl.BlockSpec((tm, D), lambda i: (i, 0)),
        out_shape=jax.ShapeDtypeStruct((T, D), F32),
        scratch_shapes=[pltpu.VMEM((tm + CONV_HALO, D), F32)],
        compiler_params=pltpu.CompilerParams(
            dimension_semantics=("parallel",), vmem_limit_bytes=VMEM_LIMIT),
        name="branch_mix",
    )(*o_list, *lse_list, proj, proj, proj, proj, proj, gates, gates, x2d,
      head_expand, wa, wc, wo, conv_w)


def _cross_kernel(h_ref, k_ref, v_ref, g2_ref, wq_ref, wo_ref, g3_ref, rw_ref, rb_ref, tri_ref,
                  h2_ref, u3_ref, meta_ref, wts_ref, cnt_ref, run_scr, *, tm):
    @pl.when(pl.program_id(0) == 0)
    def _():
        run_scr[...] = jnp.zeros_like(run_scr)

    h = h_ref[...]
    u = _rms(h, g2_ref[...]).astype(BF16)
    q = jnp.dot(u, wq_ref[...], preferred_element_type=F32) * (MEM_HEAD_DIM ** -0.5)
    heads = []
    for hd in range(MEM_HEADS):
        cols = slice(hd * MEM_HEAD_DIM, (hd + 1) * MEM_HEAD_DIM)
        s = lax.dot_general(q[:, cols].astype(BF16), k_ref[:, cols], (((1,), (1,)), ((), ())),
                            preferred_element_type=F32)
        m = jnp.max(s, axis=1, keepdims=True)
        p = jnp.exp(s - m)
        l = jnp.sum(p, axis=1, keepdims=True)
        heads.append(jnp.dot(p.astype(BF16), v_ref[:, cols], preferred_element_type=F32) * (1.0 / l))
    o = jnp.concatenate(heads, axis=1).astype(BF16)
    h2 = h + jnp.dot(o, wo_ref[...], preferred_element_type=F32)
    h2_ref[...] = h2
    u3 = _rms(h2, g3_ref[...])
    _pack_rows(u3, u3_ref, tm)

    logits = lax.dot_general(rw_ref[...], u3.astype(BF16), (((1,), (1,)), ((), ())),
                             preferred_element_type=F32) + rb_ref[:, 0:1]
    expert = lax.broadcasted_iota(I32, (N_EXPERTS, tm), 0).astype(F32)
    work = logits
    picked = jnp.zeros((N_EXPERTS, tm), F32)
    tops, idxs, sels = [], [], []
    for _ in range(TOP_K):
        mk = jnp.max(work, axis=0, keepdims=True)
        ik = jnp.min(jnp.where(work == mk, expert, float(N_EXPERTS)), axis=0, keepdims=True)
        sel = expert == ik
        work = jnp.where(sel, -jnp.inf, work)
        picked = picked + sel.astype(F32)
        tops.append(mk)
        idxs.append(ik)
        sels.append(sel)
    exps = [jnp.exp(t - tops[0]) for t in tops]
    inv = 1.0 / (exps[0] + exps[1] + exps[2] + exps[3])
    wts = [e * inv for e in exps]

    before = jnp.dot(picked.astype(BF16), tri_ref[...], preferred_element_type=F32) + run_scr[:, 0:1]
    ranks = [jnp.sum(jnp.where(sel, before, 0.0), axis=0, keepdims=True) for sel in sels]

    row = lax.broadcasted_iota(I32, (META_ROWS, tm), 0)
    meta = jnp.zeros((META_ROWS, tm), F32)
    for r, v in enumerate(idxs + ranks + wts):
        meta = jnp.where(row == r, v, meta)
    meta_ref[...] = meta
    row = lax.broadcasted_iota(I32, (LANES, tm), 0)
    wpad = jnp.zeros((LANES, tm), F32)
    for k in range(TOP_K):
        wpad = jnp.where(row == k, wts[k], wpad)
    wts_ref[...] = wpad.T
    run_scr[...] = run_scr[...] + jnp.sum(picked, axis=1, keepdims=True)
    cnt_ref[...] = run_scr[...]


def _cross_router(h1, kv, g2, wq, wo, g3, rw_t, rb, seq, tm=512):
    T, D = h1.shape
    n_mem = kv.shape[1]
    tri = (jnp.arange(tm)[:, None] < jnp.arange(tm)[None, :]).astype(BF16)
    full = lambda a: pl.BlockSpec(a.shape, lambda i: (0,) * a.ndim)
    kvspec = lambda col: pl.BlockSpec((None, n_mem, MEM_WIDTH), lambda i: ((i * tm) // seq, 0, col))
    tile = lambda width: pl.BlockSpec((tm, width), lambda i: (i, 0))
    return pl.pallas_call(
        functools.partial(_cross_kernel, tm=tm),
        grid=(T // tm,),
        in_specs=[tile(D), kvspec(0), kvspec(1), full(g2), full(wq), full(wo), full(g3),
                  full(rw_t), full(rb), full(tri)],
        out_specs=[tile(D), pl.BlockSpec((tm * PACK_ROWS, LANES), lambda i: (i, 0)),
                   pl.BlockSpec((META_ROWS, tm), lambda i: (0, i)), tile(LANES),
                   pl.BlockSpec((N_EXPERTS, LANES), lambda i: (0, 0))],
        out_shape=[jax.ShapeDtypeStruct((T, D), F32), jax.ShapeDtypeStruct((T * PACK_ROWS, LANES), F32),
                   jax.ShapeDtypeStruct((META_ROWS, T), F32), jax.ShapeDtypeStruct((T, LANES), F32),
                   jax.ShapeDtypeStruct((N_EXPERTS, LANES), F32)],
        scratch_shapes=[pltpu.VMEM((N_EXPERTS, LANES), F32)],
        compiler_params=pltpu.CompilerParams(
            dimension_semantics=("arbitrary",), vmem_limit_bytes=VMEM_LIMIT),
        name="cross_router",
    )(h1, kv, kv, g2, wq, wo, g3, rw_t, rb, tri)


def _wait_copies(src_hbm, dst, sem, rows):
    pltpu.make_async_copy(src_hbm.at[pl.ds(0, rows), :], dst.at[pl.ds(0, rows), :], sem).wait()


def _dispatch_kernel(pos_ref, u_hbm, x_hbm, stage, load_sem, row_sem, *, tm, n_token_steps):
    i = pl.program_id(0)
    n = pl.num_programs(0)
    batch = tm * TOP_K * PACK_ROWS
    tile_rows = tm * PACK_ROWS

    def tile_copy(step):
        first = pl.multiple_of(jnp.minimum(step, n_token_steps - 1) * tile_rows, tile_rows)
        return pltpu.make_async_copy(u_hbm.at[pl.ds(first, tile_rows), :], stage.at[step % 3],
                                     load_sem.at[step % 3])

    @pl.when(i == 0)
    def _():
        tile_copy(i).start()

    @pl.when(i + 1 < n)
    def _():
        tile_copy(i + 1).start()

    tile_copy(i).wait()
    src_tile = stage.at[i % 3]
    group = 4

    def body(g, carry):
        rows = [pos_ref[0, g * (group * TOP_K) + j] for j in range(group * TOP_K)]
        for j, p in enumerate(rows):
            t = g * group + j // TOP_K
            src = src_tile.at[pl.ds(pl.multiple_of(t * PACK_ROWS, PACK_ROWS), PACK_ROWS), :]
            dst = x_hbm.at[pl.ds(pl.multiple_of(p * PACK_ROWS, PACK_ROWS), PACK_ROWS), :]
            pltpu.make_async_copy(src, dst, row_sem.at[i % 2]).start(priority=j % 2)
        return carry

    lax.fori_loop(0, tm // group, body, 0)

    @pl.when(i > 0)
    def _():
        _wait_copies(u_hbm, x_hbm, row_sem.at[1 - i % 2], batch)

    @pl.when(i == n - 1)
    def _():
        _wait_copies(u_hbm, x_hbm, row_sem.at[i % 2], batch)


def _dispatch(u_packed, pos, fill_rows, tm=512):
    T = pos.shape[0]
    per_step = TOP_K * tm
    dst_rows = jnp.concatenate([pos.reshape(-1), fill_rows])
    n_rows = dst_rows.shape[0]
    assert n_rows % per_step == 0
    steps = n_rows // per_step
    return pl.pallas_call(
        functools.partial(_dispatch_kernel, tm=tm, n_token_steps=T // tm),
        grid=(steps,),
        in_specs=[pl.BlockSpec((None, 1, per_step), lambda i: (i, 0, 0), memory_space=pltpu.SMEM),
                  pl.BlockSpec(memory_space=pl.ANY)],
        out_specs=pl.BlockSpec(memory_space=pl.ANY),
        out_shape=jax.ShapeDtypeStruct((n_rows * PACK_ROWS, LANES), F32),
        scratch_shapes=[pltpu.VMEM((3, tm * PACK_ROWS, LANES), F32),
                        pltpu.SemaphoreType.DMA((3,)), pltpu.SemaphoreType.DMA((2,))],
        compiler_params=pltpu.CompilerParams(dimension_semantics=("arbitrary",)),
        name="dispatch_rows",
    )(dst_rows.reshape(steps, 1, per_step), u_packed)


def _expert_kernel(bexp_ref, slot_ref, next_ref, nused_ref, x_ref, wg_hbm, bg_ref, wu_hbm, bu_ref,
                   wd_hbm, bd_ref, y_ref, stage, w_bf, sem):
    i = pl.program_id(0)
    used = i < nused_ref[0]
    e = bexp_ref[i]
    slot = slot_ref[i]
    changed = jnp.logical_or(i == 0, e != bexp_ref[jnp.maximum(i - 1, 0)])
    weights = (wg_hbm, wu_hbm, wd_hbm)

    def fetch(expert, into):
        return [pltpu.make_async_copy(w.at[expert], stage.at[into, j], sem.at[into])
                for j, w in enumerate(weights)]

    @pl.when(jnp.logical_and(changed, used))
    def _():
        @pl.when(i == 0)
        def _():
            for c in fetch(e, slot):
                c.start()

        for c in fetch(e, slot):
            c.wait()
        for j in range(len(weights)):
            w_bf[j] = stage[slot, j].astype(BF16)

        @pl.when(next_ref[i] >= 0)
        def _():
            for c in fetch(next_ref[i], 1 - slot):
                c.start(priority=1)

    @pl.when(used)
    def _():
        x = _unpack_rows(x_ref, 0, MOE_ROWS).astype(BF16)
        gate = jnp.minimum(jnp.dot(x, w_bf[0], preferred_element_type=F32) + bg_ref[...], SWIGLU_LIMIT)
        lin = jnp.clip(jnp.dot(x, w_bf[1], preferred_element_type=F32) + bu_ref[...],
                       -SWIGLU_LIMIT, SWIGLU_LIMIT)
        hdn = gate * jax.nn.sigmoid(SWIGLU_ALPHA * gate) * (lin + 1.0)
        y = jnp.dot(hdn.astype(BF16), w_bf[2], preferred_element_type=F32) + bd_ref[...]
        _pack_rows(y, y_ref, MOE_ROWS)

    @pl.when(jnp.logical_not(used))
    def _():
        y_ref[...] = jnp.zeros_like(y_ref)


def _experts(x_packed, block_expert, block_slot, block_next, n_used, w_eg, b_eg, w_eu, b_eu, w_ed, b_ed):
    nb = block_expert.shape[0]
    E, D, F = w_eg.shape
    assert D == F
    rows = lambda i, be, sl, nx, nu: (i, 0)
    bspec = lambda n: pl.BlockSpec((None, 1, n), lambda i, be, sl, nx, nu: (be[i], 0, 0))
    anyspec = pl.BlockSpec(memory_space=pl.ANY)
    grid_spec = pltpu.PrefetchScalarGridSpec(
        num_scalar_prefetch=4,
        grid=(nb,),
        in_specs=[pl.BlockSpec((MOE_ROWS * PACK_ROWS, LANES), rows),
                  anyspec, bspec(F), anyspec, bspec(F), anyspec, bspec(D)],
        out_specs=pl.BlockSpec((MOE_ROWS * PACK_ROWS, LANES), rows),
        scratch_shapes=[pltpu.VMEM((2, 3, D, F), F32), pltpu.VMEM((3, D, F), BF16),
                        pltpu.SemaphoreType.DMA((2,))],
    )
    return pl.pallas_call(
        _expert_kernel,
        grid_spec=grid_spec,
        out_shape=jax.ShapeDtypeStruct(x_packed.shape, F32),
        compiler_params=pltpu.CompilerParams(
            dimension_semantics=("arbitrary",), vmem_limit_bytes=VMEM_LIMIT),
        name="expert_ffn",
    )(block_expert, block_slot, block_next, n_used, x_packed, w_eg, b_eg.reshape(E, 1, F), w_eu,
      b_eu.reshape(E, 1, F), w_ed, b_ed.reshape(E, 1, D))


def _gather_packed(pos_ref, src_hbm, dst, sem, n):
    group = 16

    def body(g, carry):
        rows = [pos_ref[0, g * group + j] for j in range(group)]
        for j, p in enumerate(rows):
            r = g * group + j
            pltpu.make_async_copy(
                src_hbm.at[pl.ds(pl.multiple_of(p * PACK_ROWS, PACK_ROWS), PACK_ROWS), :],
                dst.at[pl.ds(pl.multiple_of(r * PACK_ROWS, PACK_ROWS), PACK_ROWS), :],
                sem).start(priority=j % 2)
        return carry
    lax.fori_loop(0, n // group, body, 0)


def _combine_kernel(pos0_ref, pos1_ref, y_hbm, h_ref, w_ref, g_ref, o_ref, ybuf, sem):
    i = pl.program_id(0)
    n = pl.num_programs(0)
    slot = i % 2
    rows = TOP_K * COMBINE_ROWS

    @pl.when(i == 0)
    def _():
        _gather_packed(pos0_ref, y_hbm, ybuf.at[0], sem.at[0], rows)

    @pl.when(i + 1 < n)
    def _():
        _gather_packed(pos1_ref, y_hbm, ybuf.at[1 - slot], sem.at[1 - slot], rows)

    _wait_copies(y_hbm, ybuf.at[slot], sem.at[slot], rows * PACK_ROWS)
    acc = h_ref[...]
    w = w_ref[...]
    for k in range(TOP_K):
        acc = acc + w[:, k:k + 1] * _unpack_rows(ybuf.at[slot], k * COMBINE_ROWS * PACK_ROWS, COMBINE_ROWS)
    o_ref[...] = _rms(acc, g_ref[...])


def _combine(y_rows, pos_kmajor, h2, wts, g_final):
    T, D = h2.shape
    tm = COMBINE_ROWS
    nt = T // tm
    posspec = lambda off: pl.BlockSpec(
        (None, 1, TOP_K * tm), lambda i: (jnp.minimum(i + off, nt - 1), 0, 0), memory_space=pltpu.SMEM)
    return pl.pallas_call(
        _combine_kernel,
        grid=(nt,),
        in_specs=[posspec(0), posspec(1), pl.BlockSpec(memory_space=pl.ANY),
                  pl.BlockSpec((tm, D), lambda i: (i, 0)), pl.BlockSpec((tm, LANES), lambda i: (i, 0)),
                  pl.BlockSpec((1, D), lambda i: (0, 0))],
        out_specs=pl.BlockSpec((tm, D), lambda i: (i, 0)),
        out_shape=jax.ShapeDtypeStruct((T, D), F32),
        scratch_shapes=[pltpu.VMEM((2, TOP_K * tm * PACK_ROWS, LANES), F32), pltpu.SemaphoreType.DMA((2,))],
        compiler_params=pltpu.CompilerParams(
            dimension_semantics=("arbitrary",), vmem_limit_bytes=VMEM_LIMIT),
        name="combine_norm",
    )(pos_kmajor, pos_kmajor, y_rows, h2, wts, g_final.reshape(1, D))


def _layer(h, mem, norm_mix, w_in, conv_w, w_branch_attn, w_branch_conv, w_gate, b_gate, w_out,
           norm_cross, norm_mem, w_cq, w_ckv, w_co, norm_moe, router_w, router_b,
           w_eg, b_eg, w_eu, b_eu, w_ed, b_ed, norm_final):
    B, S, D = h.shape
    T = B * S
    W3 = 3 * ATTN_WIDTH
    n_attn = len(ATTN_GROUPS) * W3

    dils = tuple(dil for _, dil in ATTN_GROUPS)
    assert dils[0] == 1
    u_all = _norm_permute(h, norm_mix, dils[1:])
    w_in_b = w_in.astype(BF16)
    w_a = jnp.concatenate([w_in_b[:, n_attn:], w_in_b[:, :W3]], axis=1)
    proj, gates = _inproj(u_all[0], [(w_a, None), (w_gate.astype(BF16), b_gate)], 1, 512, 512,
                          name="inproj_conv_g1_gates")
    proj2d = proj.reshape(T, -1)
    gates2d = gates.reshape(T, -1)

    o_list, lse_list = [], []
    for g, (window, dil) in enumerate(ATTN_GROUPS):
        assert window // dil == Q_BLOCK
        if g == 0:
            qkv, col0 = proj, 3 * D // ATTN_WIDTH
        else:
            L = S // dil
            tm = min(L, 1024)
            qkv, = _inproj(u_all[g], [(w_in_b[:, g * W3:(g + 1) * W3], None)], 1024 // tm, tm, 512,
                           name=f"inproj_g{g + 1}")
            col0 = 0
        o, lse = _attention(qkv, dil, col0, name=f"dilated_attn_g{g + 1}")
        o_list.append(o)
        lse_list.append(lse)

    head_expand = (jnp.arange(LANES)[:, None] == jnp.arange(ATTN_WIDTH)[None, :] // HEAD_DIM).astype(BF16)
    h1 = _mix(o_list, lse_list, proj2d, gates2d, h.reshape(T, D), head_expand,
              w_branch_attn.astype(BF16), w_branch_conv.astype(BF16), w_out.astype(BF16), conv_w, S)

    kv = _mem_kv(mem, norm_mem, w_ckv.astype(BF16))
    rb = jnp.broadcast_to(router_b[:, None], (N_EXPERTS, LANES))
    h2, u3, meta, wts, counts = _cross_router(
        h1, kv, norm_cross.reshape(1, D), w_cq.astype(BF16), w_co.astype(BF16), norm_moe.reshape(1, D),
        router_w.T.astype(BF16), rb, S)

    nb = -(-(T * TOP_K) // MOE_ROWS) + N_EXPERTS
    cnt = counts[:, 0].astype(I32)
    padded = (cnt + MOE_ROWS - 1) // MOE_ROWS * MOE_ROWS
    pend = jnp.cumsum(padded)
    pstart = pend - padded
    experts = jnp.arange(N_EXPERTS, dtype=I32)

    def lookup(table, idx):
        out = jnp.zeros(idx.shape, table.dtype)
        for j in range(table.shape[0]):
            out = jnp.where(idx == j, table[j], out)
        return out

    pos_kt = lookup(pstart, meta[:TOP_K].astype(I32)) + meta[TOP_K:2 * TOP_K].astype(I32)
    pos = pos_kt.T
    block_start = jnp.arange(nb, dtype=I32) * MOE_ROWS
    block_expert = jnp.minimum(jnp.sum(block_start[:, None] >= pend[None, :], axis=1), N_EXPERTS - 1).astype(I32)
    active = cnt > 0
    slot_e = (jnp.cumsum(active.astype(I32)) - 1) % 2
    later = jnp.where(active[None, :] & (experts[None, :] > experts[:, None]), experts[None, :], N_EXPERTS)
    next_e = jnp.min(later, axis=1)
    next_e = jnp.where(next_e == N_EXPERTS, -1, next_e).astype(I32)
    n_pad = padded - cnt
    pad_end = jnp.cumsum(n_pad)
    fill = jnp.arange(nb * MOE_ROWS - T * TOP_K, dtype=I32)
    fill_seg = jnp.sum(fill[:, None] >= pad_end[None, :], axis=1)
    seg_first_row = jnp.concatenate([pstart + cnt, pend[-1:]])
    seg_first_fill = jnp.concatenate([pad_end - n_pad, pad_end[-1:]])
    fill_rows = (lookup(seg_first_row - seg_first_fill, fill_seg) + fill).astype(I32)

    x_rows = _dispatch(u3, pos, fill_rows)
    y_rows = _experts(x_rows, block_expert, slot_e[block_expert].astype(I32), next_e[block_expert],
                      (pend[-1:] // MOE_ROWS).astype(I32), w_eg, b_eg, w_eu, b_eu, w_ed, b_ed)

    pos_kmajor = pos_kt.reshape(TOP_K, T // COMBINE_ROWS, COMBINE_ROWS).transpose(1, 0, 2).reshape(
        T // COMBINE_ROWS, 1, TOP_K * COMBINE_ROWS)
    out = _combine(y_rows, pos_kmajor, h2, wts, norm_final)
    return out.reshape(B, S, D)


def kernel(x, mem, norm_mix, w_in, conv_w, w_branch_attn, w_branch_conv, w_gate, b_gate, w_out, norm_cross, norm_mem, w_cq, w_ckv, w_co, norm_moe, router_w, router_b, w_exp_gate, b_exp_gate, w_exp_up, b_exp_up, w_exp_down, b_exp_down, norm_final):
    depth = norm_mix.shape[0]
    assert depth == 1, "the final norm is fused into the last layer's combine step"
    return _layer(x, mem, norm_mix[0], w_in[0], conv_w[0], w_branch_attn[0], w_branch_conv[0], w_gate[0],
                  b_gate[0], w_out[0], norm_cross[0], norm_mem[0], w_cq[0], w_ckv[0], w_co[0], norm_moe[0],
                  router_w[0], router_b[0], w_exp_gate[0], b_exp_gate[0], w_exp_up[0], b_exp_up[0],
                  w_exp_down[0], b_exp_down[0], norm_final)
```

```python
import functools

import jax
import jax.numpy as jnp
from jax import lax
from jax.experimental import pallas as pl
from jax.experimental.pallas import tpu as pltpu

F32 = jnp.float32
BF16 = jnp.bfloat16
I32 = jnp.int32

D_MODEL = 1024
ATTN_GROUPS = ((128, 1), (512, 4), (2048, 16))
ATTN_HEADS = 8
HEAD_DIM = 64
ATTN_WIDTH = ATTN_HEADS * HEAD_DIM
Q_BLOCK = 128
CONV_K = 3
MEM_HEADS = 4
MEM_HEAD_DIM = 128
MEM_WIDTH = MEM_HEADS * MEM_HEAD_DIM
N_EXPERTS = 32
TOP_K = 4
SWIGLU_LIMIT = 7.0
SWIGLU_ALPHA = 1.702
EPS = 1e-6
NEG_INF = -1e30

LANES = 128
VMEM_LIMIT = 56 * 1024 * 1024

MOE_ROWS = 256
COMBINE_ROWS = 512
CONV_HALO = 16
META_ROWS = 16


def _rms(x, g):
    ms = jnp.mean(x * x, axis=-1, keepdims=True)
    return x * lax.rsqrt(ms + EPS) * g


PACK_ROWS = D_MODEL // LANES


def _pack_rows(y, out_ref, n):
    for c in range(PACK_ROWS):
        out_ref[pl.ds(c, n, stride=PACK_ROWS), :] = y[:, c * LANES:(c + 1) * LANES]


def _unpack_rows(ref, start, n):
    return jnp.concatenate(
        [ref[pl.ds(start + c, n, stride=PACK_ROWS), :] for c in range(PACK_ROWS)], axis=1)


def _norm_permute_kernel(x_ref, g_ref, *refs, dils, tm):
    out_refs, slab = refs[:-1], refs[-1]
    u = _rms(x_ref[...], g_ref[...])
    out_refs[0][...] = u.astype(BF16)
    n_slab = u.shape[1] // LANES
    for c in range(n_slab):
        slab[c] = u[:, c * LANES:(c + 1) * LANES]
    for o_ref, dil in zip(out_refs[1:], dils):
        for r in range(dil):
            for c in range(n_slab):
                o_ref[r, :, c * LANES:(c + 1) * LANES] = (
                    slab[c, pl.ds(r, tm // dil, stride=dil), :].astype(BF16))


def _norm_permute(x, gain, dils, tm=1024):
    B, S, D = x.shape
    outs = [jax.ShapeDtypeStruct((B, 1, S, D), BF16)]
    specs = [pl.BlockSpec((None, None, tm, D), lambda b, i: (b, 0, i, 0))]
    for dil in dils:
        outs.append(jax.ShapeDtypeStruct((B, dil, S // dil, D), BF16))
        specs.append(pl.BlockSpec((None, dil, tm // dil, D), lambda b, i: (b, 0, i, 0)))
    return pl.pallas_call(
        functools.partial(_norm_permute_kernel, dils=dils, tm=tm),
        grid=(B, S // tm),
        in_specs=[pl.BlockSpec((None, tm, D), lambda b, i: (b, i, 0)),
                  pl.BlockSpec((1, D), lambda b, i: (0, 0))],
        out_specs=specs,
        out_shape=outs,
        scratch_shapes=[pltpu.VMEM((D // LANES, tm, LANES), F32)],
        compiler_params=pltpu.CompilerParams(
            dimension_semantics=("parallel", "parallel"), vmem_limit_bytes=VMEM_LIMIT),
        name="norm_permute",
    )(x, gain.reshape(1, D))


def _inproj_kernel(u_ref, *refs, gate_bias, rb, chunk):
    n_w = len(gate_bias)
    w_refs, b_refs, o_refs = refs[:n_w], refs[n_w:2 * n_w], refs[2 * n_w:]
    for r in range(rb):
        u = u_ref[r]
        for w_ref, b_ref, o_ref, sig in zip(w_refs, b_refs, o_refs, gate_bias):
            for c in range(w_ref.shape[1] // chunk):
                cols = slice(c * chunk, (c + 1) * chunk)
                acc = jnp.dot(u, w_ref[:, cols], preferred_element_type=F32)
                if sig:
                    acc = jax.nn.sigmoid(acc + b_ref[:, cols])
                o_ref[r, :, cols] = acc.astype(o_ref.dtype)


def _inproj(u, weights, rb, tm, chunk, name):
    B, R, L, D = u.shape
    ws = [w for w, _ in weights]
    bs = [jnp.zeros((1, w.shape[1]), F32) if b is None else b.reshape(1, -1) for w, b in weights]
    const = lambda a: pl.BlockSpec(a.shape, lambda b, r, i: (0, 0))
    return pl.pallas_call(
        functools.partial(_inproj_kernel, gate_bias=tuple(b is not None for _, b in weights),
                          rb=rb, chunk=chunk),
        grid=(B, R // rb, L // tm),
        in_specs=[pl.BlockSpec((None, rb, tm, D), lambda b, r, i: (b, r, i, 0))]
                 + [const(w) for w in ws] + [const(b) for b in bs],
        out_specs=[pl.BlockSpec((None, rb, tm, w.shape[1]), lambda b, r, i: (b, r, i, 0)) for w in ws],
        out_shape=[jax.ShapeDtypeStruct((B, R, L, w.shape[1]), BF16) for w in ws],
        compiler_params=pltpu.CompilerParams(
            dimension_semantics=("parallel", "parallel", "parallel"), vmem_limit_bytes=VMEM_LIMIT),
        name=name,
    )(u, *ws, *bs)


def _mem_kv_kernel(x_ref, g_ref, w_ref, o_ref):
    u = _rms(x_ref[...], g_ref[...]).astype(BF16)
    o_ref[...] = jnp.dot(u, w_ref[...], preferred_element_type=F32).astype(o_ref.dtype)


def _mem_kv(mem, gain, w):
    B, M, D = mem.shape
    N = w.shape[1]
    return pl.pallas_call(
        _mem_kv_kernel,
        grid=(B,),
        in_specs=[pl.BlockSpec((None, M, D), lambda b: (b, 0, 0)),
                  pl.BlockSpec((1, D), lambda b: (0, 0)),
                  pl.BlockSpec((D, N), lambda b: (0, 0))],
        out_specs=pl.BlockSpec((None, M, N), lambda b: (b, 0, 0)),
        out_shape=jax.ShapeDtypeStruct((B, M, N), BF16),
        compiler_params=pltpu.CompilerParams(
            dimension_semantics=("parallel",), vmem_limit_bytes=VMEM_LIMIT),
        name="mem_kv",
    )(mem, gain.reshape(1, D), w)


def _attn_block(q_ref, kp_ref, kc_ref, vp_ref, vc_ref, tab_ref, first, r, j):
    lane = lax.broadcasted_iota(I32, (Q_BLOCK, LANES), 1)
    low = lane < HEAD_DIM
    lse_tile = jnp.zeros((Q_BLOCK, LANES), F32)
    pairs = []
    if j > 0:
        first = 1
    for pair in range(ATTN_HEADS // 2):
        cols = slice(pair * LANES, (pair + 1) * LANES)
        q2 = q_ref[r, j * Q_BLOCK:(j + 1) * Q_BLOCK, cols] * (HEAD_DIM ** -0.5)
        if j == 0:
            k2 = jnp.concatenate([kp_ref[r, :, cols], kc_ref[r, :Q_BLOCK, cols]], axis=0)
            v2 = jnp.concatenate([vp_ref[r, :, cols], vc_ref[r, :Q_BLOCK, cols]], axis=0)
        else:
            k2 = kc_ref[r, (j - 1) * Q_BLOCK:(j + 1) * Q_BLOCK, cols]
            v2 = vc_ref[r, (j - 1) * Q_BLOCK:(j + 1) * Q_BLOCK, cols]
        outs = []
        for half in range(2):
            h = 2 * pair + half
            keep = low if half == 0 else jnp.logical_not(low)
            qm = jnp.where(keep, q2, jnp.zeros_like(q2))
            s = lax.dot_general(qm, k2, (((1,), (1,)), ((), ())), preferred_element_type=F32)
            s = s + tab_ref[first, h]
            m = jnp.max(s, axis=1, keepdims=True)
            p = jnp.exp(s - m)
            l = jnp.sum(p, axis=1, keepdims=True)
            o = jnp.dot(p.astype(BF16), v2, preferred_element_type=F32) * (1.0 / l)
            outs.append(o)
            lse_tile = jnp.where(lane == h, m + jnp.log(l), lse_tile)
        pairs.append(jnp.where(low, outs[0], outs[1]))
    return pairs, lse_tile


def _attn_kernel(q_ref, kp_ref, kc_ref, vp_ref, vc_ref, tab_ref, o_ref, lse_ref, *scratch, dil, qb):
    first = jnp.minimum(pl.program_id(1), 1)
    blocks = (q_ref, kp_ref, kc_ref, vp_ref, vc_ref, tab_ref)
    n_pair = ATTN_HEADS // 2
    if dil == 1:
        for j in range(qb):
            pairs, lse_tile = _attn_block(*blocks, first, 0, j)
            rows = slice(j * Q_BLOCK, (j + 1) * Q_BLOCK)
            for p in range(n_pair):
                o_ref[rows, p * LANES:(p + 1) * LANES] = pairs[p].astype(o_ref.dtype)
            lse_ref[rows, :] = lse_tile
        return

    o_scr, lse_scr = scratch

    unroll = min(dil, 8)

    def body(g, carry):
        for r in [g * unroll + t for t in range(unroll)]:
            for j in range(qb):
                pairs, lse_tile = _attn_block(*blocks, first, r, j)
                rows = pl.ds(j * Q_BLOCK * dil + r, Q_BLOCK, stride=dil)
                for p in range(n_pair):
                    o_scr[p, rows, :] = pairs[p]
                lse_scr[rows, :] = lse_tile
        return carry

    if dil == unroll:
        body(0, 0)
    else:
        lax.fori_loop(0, dil // unroll, body, 0)
    for p in range(n_pair):
        o_ref[:, p * LANES:(p + 1) * LANES] = o_scr[p].astype(o_ref.dtype)
    lse_ref[...] = lse_scr[...]


def _attn_bias_table(dil):
    slopes = jnp.power(2.0, -8.0 * jnp.arange(1, ATTN_HEADS + 1, dtype=F32) / ATTN_HEADS)
    iq = jnp.arange(Q_BLOCK)
    ik = jnp.arange(2 * Q_BLOCK)
    dist = iq[:, None] + Q_BLOCK - ik[None, :]
    band = (dist >= 0) & (dist <= Q_BLOCK)
    has_prev = jnp.stack([ik >= Q_BLOCK, jnp.ones_like(ik, dtype=bool)])
    mask = band[None] & has_prev[:, None, :]
    bias = -slopes[:, None, None] * (dist * dil).astype(F32)[None]
    return jnp.where(mask[:, None], bias[None], NEG_INF).astype(F32)


def _attention(qkv, dil, col0, name):
    B, _, L, _ = qkv.shape
    W = ATTN_WIDTH
    qb = max(1, 4 // dil)
    nb = L // (Q_BLOCK * qb)
    span = Q_BLOCK * qb * dil
    blk = lambda part, prev: (
        pl.BlockSpec((None, dil, Q_BLOCK, W), lambda b, n: (b, 0, jnp.maximum(n * qb - 1, 0), col0 + part))
        if prev else pl.BlockSpec((None, dil, Q_BLOCK * qb, W), lambda b, n: (b, 0, n, col0 + part)))
    tab = _attn_bias_table(dil)
    scratch = [] if dil == 1 else [pltpu.VMEM((ATTN_HEADS // 2, span, LANES), F32),
                                   pltpu.VMEM((span, LANES), F32)]
    o, lse = pl.pallas_call(
        functools.partial(_attn_kernel, dil=dil, qb=qb),
        grid=(B, nb),
        in_specs=[blk(0, False), blk(1, True), blk(1, False), blk(2, True), blk(2, False),
                  pl.BlockSpec(tab.shape, lambda b, n: (0, 0, 0, 0))],
        out_specs=[pl.BlockSpec((None, span, W), lambda b, n: (b, n, 0)),
                   pl.BlockSpec((None, span, LANES), lambda b, n: (b, n, 0))],
        out_shape=[jax.ShapeDtypeStruct((B, L * dil, W), BF16),
                   jax.ShapeDtypeStruct((B, L * dil, LANES), F32)],
        scratch_shapes=scratch,
        compiler_params=pltpu.CompilerParams(
            dimension_semantics=("parallel", "parallel"), vmem_limit_bytes=VMEM_LIMIT),
        name=name,
    )(qkv, qkv, qkv, qkv, qkv, tab)
    return o.reshape(B * L * dil, W), lse.reshape(B * L * dil, LANES)


def _mix_kernel(o1_ref, o2_ref, o3_ref, l1_ref, l2_ref, l3_ref, gb_ref, gc_ref, xc_ref, gch_ref, xch_ref,
                ga_ref, gv_ref, x_ref, e_ref, wa_ref, wc_ref, wo_ref, cw_ref, h_ref, z_scr, *, tm, seq):
    a1, a2, a3 = l1_ref[...], l2_ref[...], l3_ref[...]
    m = jnp.maximum(jnp.maximum(a1, a2), a3)
    e1, e2, e3 = jnp.exp(a1 - m), jnp.exp(a2 - m), jnp.exp(a3 - m)
    inv = 1.0 / (e1 + e2 + e3)
    y = jnp.zeros((tm, ATTN_WIDTH), F32)
    for e, o_ref in ((e1, o1_ref), (e2, o2_ref), (e3, o3_ref)):
        w = e * inv
        w_hi = w.astype(BF16)
        w_lo = (w - w_hi.astype(F32)).astype(BF16)
        w_heads = (jnp.dot(w_hi, e_ref[...], preferred_element_type=F32)
                   + jnp.dot(w_lo, e_ref[...], preferred_element_type=F32))
        y = y + w_heads * o_ref[...].astype(F32)
    branch_a = jnp.dot(y.astype(BF16), wa_ref[...], preferred_element_type=F32)

    at_start = (pl.program_id(0) * tm) % seq == 0
    halo = gch_ref[...].astype(F32) * xch_ref[...].astype(F32)
    z_scr[0:CONV_HALO, :] = jnp.where(at_start, jnp.zeros_like(halo), halo)
    z_scr[CONV_HALO:, :] = gc_ref[...].astype(F32) * xc_ref[...].astype(F32)
    conv = (cw_ref[0:1, :] * z_scr[pl.ds(CONV_HALO - 2, tm), :]
            + cw_ref[1:2, :] * z_scr[pl.ds(CONV_HALO - 1, tm), :]
            + cw_ref[2:3, :] * z_scr[pl.ds(CONV_HALO, tm), :])
    y_conv = gb_ref[...].astype(F32) * conv
    branch_c = jnp.dot(y_conv.astype(BF16), wc_ref[...], preferred_element_type=F32)

    mixed = ga_ref[...].astype(F32) * branch_a + gv_ref[...].astype(F32) * branch_c
    h_ref[...] = x_ref[...] + jnp.dot(mixed.astype(BF16), wo_ref[...], preferred_element_type=F32)


def _mix(o_list, lse_list, proj, gates, x2d, head_expand, wa, wc, wo, conv_w, seq, tm=512):
    T, D = x2d.shape
    W = ATTN_WIDTH
    row = lambda width, col: pl.BlockSpec((tm, width), lambda i: (i, col))
    halo = lambda col: pl.BlockSpec(
        (CONV_HALO, D), lambda i: (jnp.maximum(i * (tm // CONV_HALO) - 1, 0), col))
    full = lambda a: pl.BlockSpec(a.shape, lambda i: (0,) * a.ndim)
    return pl.pallas_call(
        functools.partial(_mix_kernel, tm=tm, seq=seq),
        grid=(T // tm,),
        in_specs=[row(W, 0)] * 3 + [row(LANES, 0)] * 3
                 + [row(D, 0), row(D, 1), row(D, 2), halo(1), halo(2)]
                 + [row(D, 0), row(D, 1), row(D, 0)]
                 + [full(head_expand), full(wa), full(wc), full(wo), full(conv_w)],
        out_specs=pl.BlockSpec((tm, D), lambda i: (i, 0)),
        out_shape=jax.ShapeDtypeStruct((T, D), F32),
        scratch_shapes=[pltpu.VMEM((tm + CONV_HALO, D), F32)],
        compiler_params=pltpu.CompilerParams(
            dimension_semantics=("parallel",), vmem_limit_bytes=VMEM_LIMIT),
        name="branch_mix",
    )(*o_list, *lse_list, proj, proj, proj, proj, proj, gates, gates, x2d,
      head_expand, wa, wc, wo, conv_w)


def _cross_kernel(h_ref, k_ref, v_ref, g2_ref, wq_ref, wo_ref, g3_ref, rw_ref, rb_ref, tri_ref,
                  h2_ref, u3_ref, meta_ref, wts_ref, cnt_ref, run_scr, *, tm):
    @pl.when(pl.program_id(0) == 0)
    def _():
        run_scr[...] = jnp.zeros_like(run_scr)

    h = h_ref[...]
    u = _rms(h, g2_ref[...]).astype(BF16)
    q = jnp.dot(u, wq_ref[...], preferred_element_type=F32) * (MEM_HEAD_DIM ** -0.5)
    heads = []
    for hd in range(MEM_HEADS):
        cols = slice(hd * MEM_HEAD_DIM, (hd + 1) * MEM_HEAD_DIM)
        s = lax.dot_general(q[:, cols].astype(BF16), k_ref[:, cols], (((1,), (1,)), ((), ())),
                            preferred_element_type=F32)
        m = jnp.max(s, axis=1, keepdims=True)
        p = jnp.exp(s - m)
        l = jnp.sum(p, axis=1, keepdims=True)
        heads.append(jnp.dot(p.astype(BF16), v_ref[:, cols], preferred_element_type=F32) * (1.0 / l))
    o = jnp.concatenate(heads, axis=1).astype(BF16)
    h2 = h + jnp.dot(o, wo_ref[...], preferred_element_type=F32)
    h2_ref[...] = h2
    u3 = _rms(h2, g3_ref[...])
    _pack_rows(u3, u3_ref, tm)

    logits = lax.dot_general(rw_ref[...], u3.astype(BF16), (((1,), (1,)), ((), ())),
                             preferred_element_type=F32) + rb_ref[:, 0:1]
    expert = lax.broadcasted_iota(I32, (N_EXPERTS, tm), 0).astype(F32)
    work = logits
    picked = jnp.zeros((N_EXPERTS, tm), F32)
    tops, idxs, sels = [], [], []
    for _ in range(TOP_K):
        mk = jnp.max(work, axis=0, keepdims=True)
        ik = jnp.min(jnp.where(work == mk, expert, float(N_EXPERTS)), axis=0, keepdims=True)
        sel = expert == ik
        work = jnp.where(sel, -jnp.inf, work)
        picked = picked + sel.astype(F32)
        tops.append(mk)
        idxs.append(ik)
        sels.append(sel)
    exps = [jnp.exp(t - tops[0]) for t in tops]
    inv = 1.0 / (exps[0] + exps[1] + exps[2] + exps[3])
    wts = [e * inv for e in exps]

    before = jnp.dot(picked.astype(BF16), tri_ref[...], preferred_element_type=F32) + run_scr[:, 0:1]
    ranks = [jnp.sum(jnp.where(sel, before, 0.0), axis=0, keepdims=True) for sel in sels]

    row = lax.broadcasted_iota(I32, (META_ROWS, tm), 0)
    meta = jnp.zeros((META_ROWS, tm), F32)
    for r, v in enumerate(idxs + ranks + wts):
        meta = jnp.where(row == r, v, meta)
    meta_ref[...] = meta
    row = lax.broadcasted_iota(I32, (LANES, tm), 0)
    wpad = jnp.zeros((LANES, tm), F32)
    for k in range(TOP_K):
        wpad = jnp.where(row == k, wts[k], wpad)
    wts_ref[...] = wpad.T
    run_scr[...] = run_scr[...] + jnp.sum(picked, axis=1, keepdims=True)
    cnt_ref[...] = run_scr[...]


def _cross_router(h1, kv, g2, wq, wo, g3, rw_t, rb, seq, tm=512):
    T, D = h1.shape
    n_mem = kv.shape[1]
    tri = (jnp.arange(tm)[:, None] < jnp.arange(tm)[None, :]).astype(BF16)
    full = lambda a: pl.BlockSpec(a.shape, lambda i: (0,) * a.ndim)
    kvspec = lambda col: pl.BlockSpec((None, n_mem, MEM_WIDTH), lambda i: ((i * tm) // seq, 0, col))
    tile = lambda width: pl.BlockSpec((tm, width), lambda i: (i, 0))
    return pl.pallas_call(
        functools.partial(_cross_kernel, tm=tm),
        grid=(T // tm,),
        in_specs=[tile(D), kvspec(0), kvspec(1), full(g2), full(wq), full(wo), full(g3),
                  full(rw_t), full(rb), full(tri)],
        out_specs=[tile(D), pl.BlockSpec((tm * PACK_ROWS, LANES), lambda i: (i, 0)),
                   pl.BlockSpec((META_ROWS, tm), lambda i: (0, i)), tile(LANES),
                   pl.BlockSpec((N_EXPERTS, LANES), lambda i: (0, 0))],
        out_shape=[jax.ShapeDtypeStruct((T, D), F32), jax.ShapeDtypeStruct((T * PACK_ROWS, LANES), F32),
                   jax.ShapeDtypeStruct((META_ROWS, T), F32), jax.ShapeDtypeStruct((T, LANES), F32),
                   jax.ShapeDtypeStruct((N_EXPERTS, LANES), F32)],
        scratch_shapes=[pltpu.VMEM((N_EXPERTS, LANES), F32)],
        compiler_params=pltpu.CompilerParams(
            dimension_semantics=("arbitrary",), vmem_limit_bytes=VMEM_LIMIT),
        name="cross_router",
    )(h1, kv, kv, g2, wq, wo, g3, rw_t, rb, tri)


def _wait_copies(src_hbm, dst, sem, rows):
    pltpu.make_async_copy(src_hbm.at[pl.ds(0, rows), :], dst.at[pl.ds(0, rows), :], sem).wait()


def _dispatch_kernel(pos_ref, u_hbm, x_hbm, stage, load_sem, row_sem, *, tm, n_token_steps):
    i = pl.program_id(0)
    n = pl.num_programs(0)
    batch = tm * TOP_K * PACK_ROWS
    tile_rows = tm * PACK_ROWS

    def tile_copy(step):
        first = pl.multiple_of(jnp.minimum(step, n_token_steps - 1) * tile_rows, tile_rows)
        return pltpu.make_async_copy(u_hbm.at[pl.ds(first, tile_rows), :], stage.at[step % 3],
                                     load_sem.at[step % 3])

    @pl.when(i == 0)
    def _():
        tile_copy(i).start()

    @pl.when(i + 1 < n)
    def _():
        tile_copy(i + 1).start()

    tile_copy(i).wait()
    src_tile = stage.at[i % 3]
    group = 16

    def body(g, carry):
        rows = [pos_ref[0, g * group + j] for j in range(group)]
        t0 = (g * group) & (tm - 1)
        for j, p in enumerate(rows):
            src = src_tile.at[pl.ds(pl.multiple_of((t0 + j) * PACK_ROWS, PACK_ROWS), PACK_ROWS), :]
            dst = x_hbm.at[pl.ds(pl.multiple_of(p * PACK_ROWS, PACK_ROWS), PACK_ROWS), :]
            pltpu.make_async_copy(src, dst, row_sem.at[i % 2]).start(priority=j % 2)
        return carry

    lax.fori_loop(0, tm * TOP_K // group, body, 0)

    @pl.when(i > 0)
    def _():
        _wait_copies(u_hbm, x_hbm, row_sem.at[1 - i % 2], batch)

    @pl.when(i == n - 1)
    def _():
        _wait_copies(u_hbm, x_hbm, row_sem.at[i % 2], batch)


def _dispatch(u_packed, pos_kmajor, fill_rows, tm):
    T = pos_kmajor.shape[0] * tm
    per_step = TOP_K * tm
    dst_rows = jnp.concatenate([pos_kmajor.reshape(-1), fill_rows])
    n_rows = dst_rows.shape[0]
    assert n_rows % per_step == 0
    steps = n_rows // per_step
    return pl.pallas_call(
        functools.partial(_dispatch_kernel, tm=tm, n_token_steps=T // tm),
        grid=(steps,),
        in_specs=[pl.BlockSpec((None, 1, per_step), lambda i: (i, 0, 0), memory_space=pltpu.SMEM),
                  pl.BlockSpec(memory_space=pl.ANY)],
        out_specs=pl.BlockSpec(memory_space=pl.ANY),
        out_shape=jax.ShapeDtypeStruct((n_rows * PACK_ROWS, LANES), F32),
        scratch_shapes=[pltpu.VMEM((3, tm * PACK_ROWS, LANES), F32),
                        pltpu.SemaphoreType.DMA((3,)), pltpu.SemaphoreType.DMA((2,))],
        compiler_params=pltpu.CompilerParams(dimension_semantics=("arbitrary",)),
        name="dispatch_rows",
    )(dst_rows.reshape(steps, 1, per_step), u_packed)


def _expert_kernel(bexp_ref, slot_ref, next_ref, nused_ref, x_ref, wg_hbm, bg_ref, wu_hbm, bu_ref,
                   wd_hbm, bd_ref, y_ref, stage, w_bf, sem):
    i = pl.program_id(0)
    used = i < nused_ref[0]
    e = bexp_ref[i]
    slot = slot_ref[i]
    changed = jnp.logical_or(i == 0, e != bexp_ref[jnp.maximum(i - 1, 0)])
    weights = (wg_hbm, wu_hbm, wd_hbm)

    def fetch(expert, into):
        return [pltpu.make_async_copy(w.at[expert], stage.at[into, j], sem.at[into])
                for j, w in enumerate(weights)]

    @pl.when(jnp.logical_and(changed, used))
    def _():
        @pl.when(i == 0)
        def _():
            for c in fetch(e, slot):
                c.start()

        for c in fetch(e, slot):
            c.wait()
        for j in range(len(weights)):
            w_bf[j] = stage[slot, j].astype(BF16)

        @pl.when(next_ref[i] >= 0)
        def _():
            for c in fetch(next_ref[i], 1 - slot):
                c.start(priority=1)

    @pl.when(used)
    def _():
        x = _unpack_rows(x_ref, 0, MOE_ROWS).astype(BF16)
        gate = jnp.minimum(jnp.dot(x, w_bf[0], preferred_element_type=F32) + bg_ref[...], SWIGLU_LIMIT)
        lin = jnp.clip(jnp.dot(x, w_bf[1], preferred_element_type=F32) + bu_ref[...],
                       -SWIGLU_LIMIT, SWIGLU_LIMIT)
        hdn = gate * jax.nn.sigmoid(SWIGLU_ALPHA * gate) * (lin + 1.0)
        y = jnp.dot(hdn.astype(BF16), w_bf[2], preferred_element_type=F32) + bd_ref[...]
        _pack_rows(y, y_ref, MOE_ROWS)

    @pl.when(jnp.logical_not(used))
    def _():
        y_ref[...] = jnp.zeros_like(y_ref)


def _experts(x_packed, block_expert, block_slot, block_next, n_used, w_eg, b_eg, w_eu, b_eu, w_ed, b_ed):
    nb = block_expert.shape[0]
    E, D, F = w_eg.shape
    assert D == F
    rows = lambda i, be, sl, nx, nu: (i, 0)
    bspec = lambda n: pl.BlockSpec((None, 1, n), lambda i, be, sl, nx, nu: (be[i], 0, 0))
    anyspec = pl.BlockSpec(memory_space=pl.ANY)
    grid_spec = pltpu.PrefetchScalarGridSpec(
        num_scalar_prefetch=4,
        grid=(nb,),
        in_specs=[pl.BlockSpec((MOE_ROWS * PACK_ROWS, LANES), rows),
                  anyspec, bspec(F), anyspec, bspec(F), anyspec, bspec(D)],
        out_specs=pl.BlockSpec((MOE_ROWS * PACK_ROWS, LANES), rows),
        scratch_shapes=[pltpu.VMEM((2, 3, D, F), F32), pltpu.VMEM((3, D, F), BF16),
                        pltpu.SemaphoreType.DMA((2,))],
    )
    return pl.pallas_call(
        _expert_kernel,
        grid_spec=grid_spec,
        out_shape=jax.ShapeDtypeStruct(x_packed.shape, F32),
        compiler_params=pltpu.CompilerParams(
            dimension_semantics=("arbitrary",), vmem_limit_bytes=VMEM_LIMIT),
        name="expert_ffn",
    )(block_expert, block_slot, block_next, n_used, x_packed, w_eg, b_eg.reshape(E, 1, F), w_eu,
      b_eu.reshape(E, 1, F), w_ed, b_ed.reshape(E, 1, D))


def _gather_packed(pos_ref, src_hbm, dst, sem, n):
    group = 16

    def body(g, carry):
        rows = [pos_ref[0, g * group + j] for j in range(group)]
        for j, p in enumerate(rows):
            r = g * group + j
            pltpu.make_async_copy(
                src_hbm.at[pl.ds(pl.multiple_of(p * PACK_ROWS, PACK_ROWS), PACK_ROWS), :],
                dst.at[pl.ds(pl.multiple_of(r * PACK_ROWS, PACK_ROWS), PACK_ROWS), :],
                sem).start(priority=j % 2)
        return carry
    lax.fori_loop(0, n // group, body, 0)


def _combine_kernel(pos0_ref, pos1_ref, y_hbm, h_ref, w_ref, g_ref, o_ref, ybuf, sem):
    i = pl.program_id(0)
    n = pl.num_programs(0)
    slot = i % 2
    rows = TOP_K * COMBINE_ROWS

    @pl.when(i == 0)
    def _():
        _gather_packed(pos0_ref, y_hbm, ybuf.at[0], sem.at[0], rows)

    @pl.when(i + 1 < n)
    def _():
        _gather_packed(pos1_ref, y_hbm, ybuf.at[1 - slot], sem.at[1 - slot], rows)

    _wait_copies(y_hbm, ybuf.at[slot], sem.at[slot], rows * PACK_ROWS)
    acc = h_ref[...]
    w = w_ref[...]
    for k in range(TOP_K):
        acc = acc + w[:, k:k + 1] * _unpack_rows(ybuf.at[slot], k * COMBINE_ROWS * PACK_ROWS, COMBINE_ROWS)
    o_ref[...] = _rms(acc, g_ref[...])


def _combine(y_rows, pos_kmajor, h2, wts, g_final):
    T, D = h2.shape
    tm = COMBINE_ROWS
    nt = T // tm
    posspec = lambda off: pl.BlockSpec(
        (None, 1, TOP_K * tm), lambda i: (jnp.minimum(i + off, nt - 1), 0, 0), memory_space=pltpu.SMEM)
    return pl.pallas_call(
        _combine_kernel,
        grid=(nt,),
        in_specs=[posspec(0), posspec(1), pl.BlockSpec(memory_space=pl.ANY),
                  pl.BlockSpec((tm, D), lambda i: (i, 0)), pl.BlockSpec((tm, LANES), lambda i: (i, 0)),
                  pl.BlockSpec((1, D), lambda i: (0, 0))],
        out_specs=pl.BlockSpec((tm, D), lambda i: (i, 0)),
        out_shape=jax.ShapeDtypeStruct((T, D), F32),
        scratch_shapes=[pltpu.VMEM((2, TOP_K * tm * PACK_ROWS, LANES), F32), pltpu.SemaphoreType.DMA((2,))],
        compiler_params=pltpu.CompilerParams(
            dimension_semantics=("arbitrary",), vmem_limit_bytes=VMEM_LIMIT),
        name="combine_norm",
    )(pos_kmajor, pos_kmajor, y_rows, h2, wts, g_final.reshape(1, D))


def _layer(h, mem, norm_mix, w_in, conv_w, w_branch_attn, w_branch_conv, w_gate, b_gate, w_out,
           norm_cross, norm_mem, w_cq, w_ckv, w_co, norm_moe, router_w, router_b,
           w_eg, b_eg, w_eu, b_eu, w_ed, b_ed, norm_final):
    B, S, D = h.shape
    T = B * S
    W3 = 3 * ATTN_WIDTH
    n_attn = len(ATTN_GROUPS) * W3

    dils = tuple(dil for _, dil in ATTN_GROUPS)
    assert dils[0] == 1
    u_all = _norm_permute(h, norm_mix, dils[1:])
    w_in_b = w_in.astype(BF16)
    w_a = jnp.concatenate([w_in_b[:, n_attn:], w_in_b[:, :W3]], axis=1)
    proj, gates = _inproj(u_all[0], [(w_a, None), (w_gate.astype(BF16), b_gate)], 1, 512, 512,
                          name="inproj_conv_g1_gates")
    proj2d = proj.reshape(T, -1)
    gates2d = gates.reshape(T, -1)

    o_list, lse_list = [], []
    for g, (window, dil) in enumerate(ATTN_GROUPS):
        assert window // dil == Q_BLOCK
        if g == 0:
            qkv, col0 = proj, 3 * D // ATTN_WIDTH
        else:
            L = S // dil
            tm = min(L, 1024)
            qkv, = _inproj(u_all[g], [(w_in_b[:, g * W3:(g + 1) * W3], None)], 1024 // tm, tm, 512,
                           name=f"inproj_g{g + 1}")
            col0 = 0
        o, lse = _attention(qkv, dil, col0, name=f"dilated_attn_g{g + 1}")
        o_list.append(o)
        lse_list.append(lse)

    head_expand = (jnp.arange(LANES)[:, None] == jnp.arange(ATTN_WIDTH)[None, :] // HEAD_DIM).astype(BF16)
    h1 = _mix(o_list, lse_list, proj2d, gates2d, h.reshape(T, D), head_expand,
              w_branch_attn.astype(BF16), w_branch_conv.astype(BF16), w_out.astype(BF16), conv_w, S)

    kv = _mem_kv(mem, norm_mem, w_ckv.astype(BF16))
    rb = jnp.broadcast_to(router_b[:, None], (N_EXPERTS, LANES))
    h2, u3, meta, wts, counts = _cross_router(
        h1, kv, norm_cross.reshape(1, D), w_cq.astype(BF16), w_co.astype(BF16), norm_moe.reshape(1, D),
        router_w.T.astype(BF16), rb, S)

    nb = -(-(T * TOP_K) // MOE_ROWS) + N_EXPERTS
    cnt = counts[:, 0].astype(I32)
    padded = (cnt + MOE_ROWS - 1) // MOE_ROWS * MOE_ROWS
    pend = jnp.cumsum(padded)
    pstart = pend - padded
    experts = jnp.arange(N_EXPERTS, dtype=I32)

    def lookup(table, idx):
        out = jnp.zeros(idx.shape, table.dtype)
        for j in range(table.shape[0]):
            out = jnp.where(idx == j, table[j], out)
        return out

    tm = COMBINE_ROWS
    assert tm & (tm - 1) == 0 and T % tm == 0
    pos_kt = lookup(pstart, meta[:TOP_K].astype(I32)) + meta[TOP_K:2 * TOP_K].astype(I32)
    pos_kmajor = pos_kt.reshape(TOP_K, T // tm, tm).transpose(1, 0, 2).reshape(T // tm, 1, TOP_K * tm)
    block_start = jnp.arange(nb, dtype=I32) * MOE_ROWS
    block_expert = jnp.minimum(jnp.sum(block_start[:, None] >= pend[None, :], axis=1), N_EXPERTS - 1).astype(I32)
    active = cnt > 0
    slot_e = (jnp.cumsum(active.astype(I32)) - 1) % 2
    later = jnp.where(active[None, :] & (experts[None, :] > experts[:, None]), experts[None, :], N_EXPERTS)
    next_e = jnp.min(later, axis=1)
    next_e = jnp.where(next_e == N_EXPERTS, -1, next_e).astype(I32)
    n_pad = padded - cnt
    pad_end = jnp.cumsum(n_pad)
    fill = jnp.arange(nb * MOE_ROWS - T * TOP_K, dtype=I32)
    fill_seg = jnp.sum(fill[:, None] >= pad_end[None, :], axis=1)
    seg_first_row = jnp.concatenate([pstart + cnt, pend[-1:]])
    seg_first_fill = jnp.concatenate([pad_end - n_pad, pad_end[-1:]])
    fill_rows = (lookup(seg_first_row - seg_first_fill, fill_seg) + fill).astype(I32)

    x_rows = _dispatch(u3, pos_kmajor, fill_rows, tm)
    y_rows = _experts(x_rows, block_expert, slot_e[block_expert].astype(I32), next_e[block_expert],
                      (pend[-1:] // MOE_ROWS).astype(I32), w_eg, b_eg, w_eu, b_eu, w_ed, b_ed)

    out = _combine(y_rows, pos_kmajor, h2, wts, norm_final)
    return out.reshape(B, S, D)


def kernel(x, mem, norm_mix, w_in, conv_w, w_branch_attn, w_branch_conv, w_gate, b_gate, w_out, norm_cross, norm_mem, w_cq, w_ckv, w_co, norm_moe, router_w, router_b, w_exp_gate, b_exp_gate, w_exp_up, b_exp_up, w_exp_down, b_exp_down, norm_final):
    depth = norm_mix.shape[0]
    assert depth == 1, "the final norm is fused into the last layer's combine step"
    return _layer(x, mem, norm_mix[0], w_in[0], conv_w[0], w_branch_attn[0], w_branch_conv[0], w_gate[0],
                  b_gate[0], w_out[0], norm_cross[0], norm_mem[0], w_cq[0], w_ckv[0], w_co[0], norm_moe[0],
                  router_w[0], router_b[0], w_exp_gate[0], b_exp_gate[0], w_exp_up[0], b_exp_up[0],
                  w_exp_down[0], b_exp_down[0], norm_final)
```

```python
import functools

import jax
import jax.numpy as jnp
from jax import lax
from jax.experimental import pallas as pl
from jax.experimental.pallas import tpu as pltpu

F32 = jnp.float32
BF16 = jnp.bfloat16
I32 = jnp.int32

D_MODEL = 1024
ATTN_GROUPS = ((128, 1), (512, 4), (2048, 16))
ATTN_HEADS = 8
HEAD_DIM = 64
ATTN_WIDTH = ATTN_HEADS * HEAD_DIM
Q_BLOCK = 128
CONV_K = 3
MEM_HEADS = 4
MEM_HEAD_DIM = 128
MEM_WIDTH = MEM_HEADS * MEM_HEAD_DIM
N_EXPERTS = 32
TOP_K = 4
SWIGLU_LIMIT = 7.0
SWIGLU_ALPHA = 1.702
EPS = 1e-6
NEG_INF = -1e30

LANES = 128
VMEM_LIMIT = 56 * 1024 * 1024

MOE_ROWS = 256
MOE_STEP_BLOCKS = 4
COMBINE_ROWS = 512
CONV_HALO = 16
META_ROWS = 16


def _rms(x, g):
    ms = jnp.mean(x * x, axis=-1, keepdims=True)
    return x * lax.rsqrt(ms + EPS) * g


PACK_ROWS = D_MODEL // LANES


def _pack_rows(y, out_ref, n):
    for c in range(PACK_ROWS):
        out_ref[pl.ds(c, n, stride=PACK_ROWS), :] = y[:, c * LANES:(c + 1) * LANES]


def _unpack_rows(ref, start, n):
    return jnp.concatenate(
        [ref[pl.ds(start + c, n, stride=PACK_ROWS), :] for c in range(PACK_ROWS)], axis=1)


def _norm_permute_kernel(x_ref, g_ref, *refs, dils, tm):
    out_refs, slab = refs[:-1], refs[-1]
    u = _rms(x_ref[...], g_ref[...])
    out_refs[0][...] = u.astype(BF16)
    n_slab = u.shape[1] // LANES
    for c in range(n_slab):
        slab[c] = u[:, c * LANES:(c + 1) * LANES]
    for o_ref, dil in zip(out_refs[1:], dils):
        for r in range(dil):
            for c in range(n_slab):
                o_ref[r, :, c * LANES:(c + 1) * LANES] = (
                    slab[c, pl.ds(r, tm // dil, stride=dil), :].astype(BF16))


def _norm_permute(x, gain, dils, tm=1024):
    B, S, D = x.shape
    outs = [jax.ShapeDtypeStruct((B, 1, S, D), BF16)]
    specs = [pl.BlockSpec((None, None, tm, D), lambda b, i: (b, 0, i, 0))]
    for dil in dils:
        outs.append(jax.ShapeDtypeStruct((B, dil, S // dil, D), BF16))
        specs.append(pl.BlockSpec((None, dil, tm // dil, D), lambda b, i: (b, 0, i, 0)))
    return pl.pallas_call(
        functools.partial(_norm_permute_kernel, dils=dils, tm=tm),
        grid=(B, S // tm),
        in_specs=[pl.BlockSpec((None, tm, D), lambda b, i: (b, i, 0)),
                  pl.BlockSpec((1, D), lambda b, i: (0, 0))],
        out_specs=specs,
        out_shape=outs,
        scratch_shapes=[pltpu.VMEM((D // LANES, tm, LANES), F32)],
        compiler_params=pltpu.CompilerParams(
            dimension_semantics=("parallel", "parallel"), vmem_limit_bytes=VMEM_LIMIT),
        name="norm_permute",
    )(x, gain.reshape(1, D))


def _inproj_kernel(u_ref, *refs, gate_bias, rb, chunk):
    n_w = len(gate_bias)
    w_refs, b_refs, o_refs = refs[:n_w], refs[n_w:2 * n_w], refs[2 * n_w:]
    for r in range(rb):
        u = u_ref[r]
        for w_ref, b_ref, o_ref, sig in zip(w_refs, b_refs, o_refs, gate_bias):
            for c in range(w_ref.shape[1] // chunk):
                cols = slice(c * chunk, (c + 1) * chunk)
                acc = jnp.dot(u, w_ref[:, cols], preferred_element_type=F32)
                if sig:
                    acc = jax.nn.sigmoid(acc + b_ref[:, cols])
                o_ref[r, :, cols] = acc.astype(o_ref.dtype)


def _inproj(u, weights, rb, tm, chunk, name):
    B, R, L, D = u.shape
    ws = [w for w, _ in weights]
    bs = [jnp.zeros((1, w.shape[1]), F32) if b is None else b.reshape(1, -1) for w, b in weights]
    const = lambda a: pl.BlockSpec(a.shape, lambda b, r, i: (0, 0))
    return pl.pallas_call(
        functools.partial(_inproj_kernel, gate_bias=tuple(b is not None for _, b in weights),
                          rb=rb, chunk=chunk),
        grid=(B, R // rb, L // tm),
        in_specs=[pl.BlockSpec((None, rb, tm, D), lambda b, r, i: (b, r, i, 0))]
                 + [const(w) for w in ws] + [const(b) for b in bs],
        out_specs=[pl.BlockSpec((None, rb, tm, w.shape[1]), lambda b, r, i: (b, r, i, 0)) for w in ws],
        out_shape=[jax.ShapeDtypeStruct((B, R, L, w.shape[1]), BF16) for w in ws],
        compiler_params=pltpu.CompilerParams(
            dimension_semantics=("parallel", "parallel", "parallel"), vmem_limit_bytes=VMEM_LIMIT),
        name=name,
    )(u, *ws, *bs)


def _mem_kv_kernel(x_ref, g_ref, w_ref, o_ref):
    u = _rms(x_ref[...], g_ref[...]).astype(BF16)
    o_ref[...] = jnp.dot(u, w_ref[...], preferred_element_type=F32).astype(o_ref.dtype)


def _mem_kv(mem, gain, w):
    B, M, D = mem.shape
    N = w.shape[1]
    return pl.pallas_call(
        _mem_kv_kernel,
        grid=(B,),
        in_specs=[pl.BlockSpec((None, M, D), lambda b: (b, 0, 0)),
                  pl.BlockSpec((1, D), lambda b: (0, 0)),
                  pl.BlockSpec((D, N), lambda b: (0, 0))],
        out_specs=pl.BlockSpec((None, M, N), lambda b: (b, 0, 0)),
        out_shape=jax.ShapeDtypeStruct((B, M, N), BF16),
        compiler_params=pltpu.CompilerParams(
            dimension_semantics=("parallel",), vmem_limit_bytes=VMEM_LIMIT),
        name="mem_kv",
    )(mem, gain.reshape(1, D), w)


def _attn_block(q_ref, kp_ref, kc_ref, vp_ref, vc_ref, tab_ref, first, r, j):
    lane = lax.broadcasted_iota(I32, (Q_BLOCK, LANES), 1)
    low = lane < HEAD_DIM
    lse_tile = jnp.zeros((Q_BLOCK, LANES), F32)
    pairs = []
    if j > 0:
        first = 1
    for pair in range(ATTN_HEADS // 2):
        cols = slice(pair * LANES, (pair + 1) * LANES)
        q2 = q_ref[r, j * Q_BLOCK:(j + 1) * Q_BLOCK, cols] * (HEAD_DIM ** -0.5)
        if j == 0:
            k2 = jnp.concatenate([kp_ref[r, :, cols], kc_ref[r, :Q_BLOCK, cols]], axis=0)
            v2 = jnp.concatenate([vp_ref[r, :, cols], vc_ref[r, :Q_BLOCK, cols]], axis=0)
        else:
            k2 = kc_ref[r, (j - 1) * Q_BLOCK:(j + 1) * Q_BLOCK, cols]
            v2 = vc_ref[r, (j - 1) * Q_BLOCK:(j + 1) * Q_BLOCK, cols]
        outs = []
        for half in range(2):
            h = 2 * pair + half
            keep = low if half == 0 else jnp.logical_not(low)
            qm = jnp.where(keep, q2, jnp.zeros_like(q2))
            s = lax.dot_general(qm, k2, (((1,), (1,)), ((), ())), preferred_element_type=F32)
            s = s + tab_ref[first, h]
            m = jnp.max(s, axis=1, keepdims=True)
            p = jnp.exp(s - m)
            l = jnp.sum(p, axis=1, keepdims=True)
            o = jnp.dot(p.astype(BF16), v2, preferred_element_type=F32) * (1.0 / l)
            outs.append(o)
            lse_tile = jnp.where(lane == h, m + jnp.log(l), lse_tile)
        pairs.append(jnp.where(low, outs[0], outs[1]))
    return pairs, lse_tile


def _attn_kernel(q_ref, kp_ref, kc_ref, vp_ref, vc_ref, tab_ref, o_ref, lse_ref, *scratch, dil, qb):
    first = jnp.minimum(pl.program_id(1), 1)
    blocks = (q_ref, kp_ref, kc_ref, vp_ref, vc_ref, tab_ref)
    n_pair = ATTN_HEADS // 2
    if dil == 1:
        for j in range(qb):
            pairs, lse_tile = _attn_block(*blocks, first, 0, j)
            rows = slice(j * Q_BLOCK, (j + 1) * Q_BLOCK)
            for p in range(n_pair):
                o_ref[rows, p * LANES:(p + 1) * LANES] = pairs[p].astype(o_ref.dtype)
            lse_ref[rows, :] = lse_tile
        return

    o_scr, lse_scr = scratch

    unroll = min(dil, 8)

    def body(g, carry):
        for r in [g * unroll + t for t in range(unroll)]:
            for j in range(qb):
                pairs, lse_tile = _attn_block(*blocks, first, r, j)
                rows = pl.ds(j * Q_BLOCK * dil + r, Q_BLOCK, stride=dil)
                for p in range(n_pair):
                    o_scr[p, rows, :] = pairs[p]
                lse_scr[rows, :] = lse_tile
        return carry

    if dil == unroll:
        body(0, 0)
    else:
        lax.fori_loop(0, dil // unroll, body, 0)
    for p in range(n_pair):
        o_ref[:, p * LANES:(p + 1) * LANES] = o_scr[p].astype(o_ref.dtype)
    lse_ref[...] = lse_scr[...]


def _attn_bias_table(dil):
    slopes = jnp.power(2.0, -8.0 * jnp.arange(1, ATTN_HEADS + 1, dtype=F32) / ATTN_HEADS)
    iq = jnp.arange(Q_BLOCK)
    ik = jnp.arange(2 * Q_BLOCK)
    dist = iq[:, None] + Q_BLOCK - ik[None, :]
    band = (dist >= 0) & (dist <= Q_BLOCK)
    has_prev = jnp.stack([ik >= Q_BLOCK, jnp.ones_like(ik, dtype=bool)])
    mask = band[None] & has_prev[:, None, :]
    bias = -slopes[:, None, None] * (dist * dil).astype(F32)[None]
    return jnp.where(mask[:, None], bias[None], NEG_INF).astype(F32)


def _attention(qkv, dil, col0, name):
    B, _, L, _ = qkv.shape
    W = ATTN_WIDTH
    qb = max(1, 4 // dil)
    nb = L // (Q_BLOCK * qb)
    span = Q_BLOCK * qb * dil
    blk = lambda part, prev: (
        pl.BlockSpec((None, dil, Q_BLOCK, W), lambda b, n: (b, 0, jnp.maximum(n * qb - 1, 0), col0 + part))
        if prev else pl.BlockSpec((None, dil, Q_BLOCK * qb, W), lambda b, n: (b, 0, n, col0 + part)))
    tab = _attn_bias_table(dil)
    scratch = [] if dil == 1 else [pltpu.VMEM((ATTN_HEADS // 2, span, LANES), F32),
                                   pltpu.VMEM((span, LANES), F32)]
    o, lse = pl.pallas_call(
        functools.partial(_attn_kernel, dil=dil, qb=qb),
        grid=(B, nb),
        in_specs=[blk(0, False), blk(1, True), blk(1, False), blk(2, True), blk(2, False),
                  pl.BlockSpec(tab.shape, lambda b, n: (0, 0, 0, 0))],
        out_specs=[pl.BlockSpec((None, span, W), lambda b, n: (b, n, 0)),
                   pl.BlockSpec((None, span, LANES), lambda b, n: (b, n, 0))],
        out_shape=[jax.ShapeDtypeStruct((B, L * dil, W), BF16),
                   jax.ShapeDtypeStruct((B, L * dil, LANES), F32)],
        scratch_shapes=scratch,
        compiler_params=pltpu.CompilerParams(
            dimension_semantics=("parallel", "parallel"), vmem_limit_bytes=VMEM_LIMIT),
        name=name,
    )(qkv, qkv, qkv, qkv, qkv, tab)
    return o.reshape(B * L * dil, W), lse.reshape(B * L * dil, LANES)


def _mix_kernel(o1_ref, o2_ref, o3_ref, l1_ref, l2_ref, l3_ref, gb_ref, gc_ref, xc_ref, gch_ref, xch_ref,
                ga_ref, gv_ref, x_ref, e_ref, wa_ref, wc_ref, wo_ref, cw_ref, h_ref, z_scr, *, tm, seq):
    a1, a2, a3 = l1_ref[...], l2_ref[...], l3_ref[...]
    m = jnp.maximum(jnp.maximum(a1, a2), a3)
    e1, e2, e3 = jnp.exp(a1 - m), jnp.exp(a2 - m), jnp.exp(a3 - m)
    inv = 1.0 / (e1 + e2 + e3)
    y = jnp.zeros((tm, ATTN_WIDTH), F32)
    for e, o_ref in ((e1, o1_ref), (e2, o2_ref), (e3, o3_ref)):
        w = e * inv
        w_hi = w.astype(BF16)
        w_lo = (w - w_hi.astype(F32)).astype(BF16)
        w_heads = (jnp.dot(w_hi, e_ref[...], preferred_element_type=F32)
                   + jnp.dot(w_lo, e_ref[...], preferred_element_type=F32))
        y = y + w_heads * o_ref[...].astype(F32)
    branch_a = jnp.dot(y.astype(BF16), wa_ref[...], preferred_element_type=F32)

    at_start = (pl.program_id(0) * tm) % seq == 0
    halo = gch_ref[...].astype(F32) * xch_ref[...].astype(F32)
    z_scr[0:CONV_HALO, :] = jnp.where(at_start, jnp.zeros_like(halo), halo)
    z_scr[CONV_HALO:, :] = gc_ref[...].astype(F32) * xc_ref[...].astype(F32)
    conv = (cw_ref[0:1, :] * z_scr[pl.ds(CONV_HALO - 2, tm), :]
            + cw_ref[1:2, :] * z_scr[pl.ds(CONV_HALO - 1, tm), :]
            + cw_ref[2:3, :] * z_scr[pl.ds(CONV_HALO, tm), :])
    y_conv = gb_ref[...].astype(F32) * conv
    branch_c = jnp.dot(y_conv.astype(BF16), wc_ref[...], preferred_element_type=F32)

    mixed = ga_ref[...].astype(F32) * branch_a + gv_ref[...].astype(F32) * branch_c
    h_ref[...] = x_ref[...] + jnp.dot(mixed.astype(BF16), wo_ref[...], preferred_element_type=F32)


def _mix(o_list, lse_list, proj, gates, x2d, head_expand, wa, wc, wo, conv_w, seq, tm=512):
    T, D = x2d.shape
    W = ATTN_WIDTH
    row = lambda width, col: pl.BlockSpec((tm, width), lambda i: (i, col))
    halo = lambda col: pl.BlockSpec(
        (CONV_HALO, D), lambda i: (jnp.maximum(i * (tm // CONV_HALO) - 1, 0), col))
    full = lambda a: pl.BlockSpec(a.shape, lambda i: (0,) * a.ndim)
    return pl.pallas_call(
        functools.partial(_mix_kernel, tm=tm, seq=seq),
        grid=(T // tm,),
        in_specs=[row(W, 0)] * 3 + [row(LANES, 0)] * 3
                 + [row(D, 0), row(D, 1), row(D, 2), halo(1), halo(2)]
                 + [row(D, 0), row(D, 1), row(D, 0)]
                 + [full(head_expand), full(wa), full(wc), full(wo), full(conv_w)],
        out_specs=pl.BlockSpec((tm, D), lambda i: (i, 0)),
        out_shape=jax.ShapeDtypeStruct((T, D), F32),
        scratch_shapes=[pltpu.VMEM((tm + CONV_HALO, D), F32)],
        compiler_params=pltpu.CompilerParams(
            dimension_semantics=("parallel",), vmem_limit_bytes=VMEM_LIMIT),
        name="branch_mix",
    )(*o_list, *lse_list, proj, proj, proj, proj, proj, gates, gates, x2d,
      head_expand, wa, wc, wo, conv_w)


def _cross_kernel(h_ref, k_ref, v_ref, g2_ref, wq_ref, wo_ref, g3_ref, rw_ref, rb_ref, tri_ref,
                  h2_ref, u3_ref, meta_ref, wts_ref, cnt_ref, run_scr, *, tm):
    @pl.when(pl.program_id(0) == 0)
    def _():
        run_scr[...] = jnp.zeros_like(run_scr)

    h = h_ref[...]
    u = _rms(h, g2_ref[...]).astype(BF16)
    q = jnp.dot(u, wq_ref[...], preferred_element_type=F32) * (MEM_HEAD_DIM ** -0.5)
    heads = []
    for hd in range(MEM_HEADS):
        cols = slice(hd * MEM_HEAD_DIM, (hd + 1) * MEM_HEAD_DIM)
        s = lax.dot_general(q[:, cols].astype(BF16), k_ref[:, cols], (((1,), (1,)), ((), ())),
                            preferred_element_type=F32)
        m = jnp.max(s, axis=1, keepdims=True)
        p = jnp.exp(s - m)
        l = jnp.sum(p, axis=1, keepdims=True)
        heads.append(jnp.dot(p.astype(BF16), v_ref[:, cols], preferred_element_type=F32) * (1.0 / l))
    o = jnp.concatenate(heads, axis=1).astype(BF16)
    h2 = h + jnp.dot(o, wo_ref[...], preferred_element_type=F32)
    h2_ref[...] = h2
    u3 = _rms(h2, g3_ref[...])
    _pack_rows(u3, u3_ref, tm)

    logits = lax.dot_general(rw_ref[...], u3.astype(BF16), (((1,), (1,)), ((), ())),
                             preferred_element_type=F32) + rb_ref[:, 0:1]
    expert = lax.broadcasted_iota(I32, (N_EXPERTS, tm), 0).astype(F32)
    work = logits
    picked = jnp.zeros((N_EXPERTS, tm), F32)
    tops, idxs, sels = [], [], []
    for _ in range(TOP_K):
        mk = jnp.max(work, axis=0, keepdims=True)
        ik = jnp.min(jnp.where(work == mk, expert, float(N_EXPERTS)), axis=0, keepdims=True)
        sel = expert == ik
        work = jnp.where(sel, -jnp.inf, work)
        picked = picked + sel.astype(F32)
        tops.append(mk)
        idxs.append(ik)
        sels.append(sel)
    exps = [jnp.exp(t - tops[0]) for t in tops]
    inv = 1.0 / (exps[0] + exps[1] + exps[2] + exps[3])
    wts = [e * inv for e in exps]

    before = jnp.dot(picked.astype(BF16), tri_ref[...], preferred_element_type=F32) + run_scr[:, 0:1]
    ranks = [jnp.sum(jnp.where(sel, before, 0.0), axis=0, keepdims=True) for sel in sels]

    row = lax.broadcasted_iota(I32, (META_ROWS, tm), 0)
    meta = jnp.zeros((META_ROWS, tm), F32)
    for r, v in enumerate(idxs + ranks + wts):
        meta = jnp.where(row == r, v, meta)
    meta_ref[...] = meta
    row = lax.broadcasted_iota(I32, (LANES, tm), 0)
    wpad = jnp.zeros((LANES, tm), F32)
    for k in range(TOP_K):
        wpad = jnp.where(row == k, wts[k], wpad)
    wts_ref[...] = wpad.T
    run_scr[...] = run_scr[...] + jnp.sum(picked, axis=1, keepdims=True)
    cnt_ref[...] = run_scr[...]


def _cross_router(h1, kv, g2, wq, wo, g3, rw_t, rb, seq, tm=512):
    T, D = h1.shape
    n_mem = kv.shape[1]
    tri = (jnp.arange(tm)[:, None] < jnp.arange(tm)[None, :]).astype(BF16)
    full = lambda a: pl.BlockSpec(a.shape, lambda i: (0,) * a.ndim)
    kvspec = lambda col: pl.BlockSpec((None, n_mem, MEM_WIDTH), lambda i: ((i * tm) // seq, 0, col))
    tile = lambda width: pl.BlockSpec((tm, width), lambda i: (i, 0))
    return pl.pallas_call(
        functools.partial(_cross_kernel, tm=tm),
        grid=(T // tm,),
        in_specs=[tile(D), kvspec(0), kvspec(1), full(g2), full(wq), full(wo), full(g3),
                  full(rw_t), full(rb), full(tri)],
        out_specs=[tile(D), pl.BlockSpec((tm * PACK_ROWS, LANES), lambda i: (i, 0)),
                   pl.BlockSpec((META_ROWS, tm), lambda i: (0, i)), tile(LANES),
                   pl.BlockSpec((N_EXPERTS, LANES), lambda i: (0, 0))],
        out_shape=[jax.ShapeDtypeStruct((T, D), F32), jax.ShapeDtypeStruct((T * PACK_ROWS, LANES), F32),
                   jax.ShapeDtypeStruct((META_ROWS, T), F32), jax.ShapeDtypeStruct((T, LANES), F32),
                   jax.ShapeDtypeStruct((N_EXPERTS, LANES), F32)],
        scratch_shapes=[pltpu.VMEM((N_EXPERTS, LANES), F32)],
        compiler_params=pltpu.CompilerParams(
            dimension_semantics=("arbitrary",), vmem_limit_bytes=VMEM_LIMIT),
        name="cross_router",
    )(h1, kv, kv, g2, wq, wo, g3, rw_t, rb, tri)


def _wait_copies(src_hbm, dst, sem, rows):
    pltpu.make_async_copy(src_hbm.at[pl.ds(0, rows), :], dst.at[pl.ds(0, rows), :], sem).wait()


def _dispatch_kernel(pos_ref, u_hbm, x_hbm, stage, load_sem, row_sem, *, tm, n_token_steps):
    i = pl.program_id(0)
    n = pl.num_programs(0)
    batch = tm * TOP_K * PACK_ROWS
    tile_rows = tm * PACK_ROWS

    def tile_copy(step):
        first = pl.multiple_of(jnp.minimum(step, n_token_steps - 1) * tile_rows, tile_rows)
        return pltpu.make_async_copy(u_hbm.at[pl.ds(first, tile_rows), :], stage.at[step % 3],
                                     load_sem.at[step % 3])

    @pl.when(i == 0)
    def _():
        tile_copy(i).start()

    @pl.when(i + 1 < n)
    def _():
        tile_copy(i + 1).start()

    tile_copy(i).wait()
    src_tile = stage.at[i % 3]
    group = 16

    def body(g, carry):
        rows = [pos_ref[0, g * group + j] for j in range(group)]
        t0 = (g * group) & (tm - 1)
        for j, p in enumerate(rows):
            src = src_tile.at[pl.ds(pl.multiple_of((t0 + j) * PACK_ROWS, PACK_ROWS), PACK_ROWS), :]
            dst = x_hbm.at[pl.ds(pl.multiple_of(p * PACK_ROWS, PACK_ROWS), PACK_ROWS), :]
            pltpu.make_async_copy(src, dst, row_sem.at[i % 2]).start(priority=j % 2)
        return carry

    lax.fori_loop(0, tm * TOP_K // group, body, 0)

    @pl.when(i > 0)
    def _():
        _wait_copies(u_hbm, x_hbm, row_sem.at[1 - i % 2], batch)

    @pl.when(i == n - 1)
    def _():
        _wait_copies(u_hbm, x_hbm, row_sem.at[i % 2], batch)


def _dispatch(u_packed, pos_kmajor, fill_rows, tm):
    T = pos_kmajor.shape[0] * tm
    per_step = TOP_K * tm
    dst_rows = jnp.concatenate([pos_kmajor.reshape(-1), fill_rows])
    n_rows = dst_rows.shape[0]
    assert n_rows % per_step == 0
    steps = n_rows // per_step
    return pl.pallas_call(
        functools.partial(_dispatch_kernel, tm=tm, n_token_steps=T // tm),
        grid=(steps,),
        in_specs=[pl.BlockSpec((None, 1, per_step), lambda i: (i, 0, 0), memory_space=pltpu.SMEM),
                  pl.BlockSpec(memory_space=pl.ANY)],
        out_specs=pl.BlockSpec(memory_space=pl.ANY),
        out_shape=jax.ShapeDtypeStruct((n_rows * PACK_ROWS, LANES), F32),
        scratch_shapes=[pltpu.VMEM((3, tm * PACK_ROWS, LANES), F32),
                        pltpu.SemaphoreType.DMA((3,)), pltpu.SemaphoreType.DMA((2,))],
        compiler_params=pltpu.CompilerParams(dimension_semantics=("arbitrary",)),
        name="dispatch_rows",
    )(dst_rows.reshape(steps, 1, per_step), u_packed)


def _expert_kernel(bexp_ref, slot_ref, next_ref, nused_ref, x_ref, wg_hbm, bg_ref, wu_hbm, bu_ref,
                   wd_hbm, bd_ref, y_ref, stage, w_bf, sem):
    weights = (wg_hbm, wu_hbm, wd_hbm)
    block_rows = MOE_ROWS * PACK_ROWS

    def fetch(expert, into):
        return [pltpu.make_async_copy(w.at[expert], stage.at[into, j], sem.at[into])
                for j, w in enumerate(weights)]

    for sub in range(MOE_STEP_BLOCKS):
        i = pl.program_id(0) * MOE_STEP_BLOCKS + sub
        used = i < nused_ref[0]
        e = bexp_ref[i]
        slot = slot_ref[i]
        changed = jnp.logical_or(i == 0, e != bexp_ref[jnp.maximum(i - 1, 0)])
        y_rows = y_ref.at[pl.ds(sub * block_rows, block_rows), :]

        @pl.when(jnp.logical_and(changed, used))
        def _():
            @pl.when(i == 0)
            def _():
                for c in fetch(e, slot):
                    c.start()

            for c in fetch(e, slot):
                c.wait()
            for j in range(len(weights)):
                w_bf[j] = stage[slot, j].astype(BF16)

            @pl.when(next_ref[i] >= 0)
            def _():
                for c in fetch(next_ref[i], 1 - slot):
                    c.start(priority=1)

        @pl.when(used)
        def _():
            x = _unpack_rows(x_ref, sub * block_rows, MOE_ROWS).astype(BF16)
            gate = jnp.minimum(jnp.dot(x, w_bf[0], preferred_element_type=F32) + bg_ref[e], SWIGLU_LIMIT)
            lin = jnp.clip(jnp.dot(x, w_bf[1], preferred_element_type=F32) + bu_ref[e],
                           -SWIGLU_LIMIT, SWIGLU_LIMIT)
            hdn = gate * jax.nn.sigmoid(SWIGLU_ALPHA * gate) * (lin + 1.0)
            y = jnp.dot(hdn.astype(BF16), w_bf[2], preferred_element_type=F32) + bd_ref[e]
            _pack_rows(y, y_rows, MOE_ROWS)

        @pl.when(jnp.logical_not(used))
        def _():
            y_rows[...] = jnp.zeros(y_rows.shape, y_rows.dtype)


def _experts(x_packed, block_expert, block_slot, block_next, n_used, w_eg, b_eg, w_eu, b_eu, w_ed, b_ed):
    nb = block_expert.shape[0]
    assert nb % MOE_STEP_BLOCKS == 0
    E, D, F = w_eg.shape
    assert D == F
    step_rows = MOE_STEP_BLOCKS * MOE_ROWS * PACK_ROWS
    rows = lambda i, be, sl, nx, nu: (i, 0)
    bspec = lambda n: pl.BlockSpec((E, 1, n), lambda i, be, sl, nx, nu: (0, 0, 0))
    anyspec = pl.BlockSpec(memory_space=pl.ANY)
    grid_spec = pltpu.PrefetchScalarGridSpec(
        num_scalar_prefetch=4,
        grid=(nb // MOE_STEP_BLOCKS,),
        in_specs=[pl.BlockSpec((step_rows, LANES), rows),
                  anyspec, bspec(F), anyspec, bspec(F), anyspec, bspec(D)],
        out_specs=pl.BlockSpec((step_rows, LANES), rows),
        scratch_shapes=[pltpu.VMEM((2, 3, D, F), F32), pltpu.VMEM((3, D, F), BF16),
                        pltpu.SemaphoreType.DMA((2,))],
    )
    return pl.pallas_call(
        _expert_kernel,
        grid_spec=grid_spec,
        out_shape=jax.ShapeDtypeStruct(x_packed.shape, F32),
        compiler_params=pltpu.CompilerParams(
            dimension_semantics=("arbitrary",), vmem_limit_bytes=VMEM_LIMIT),
        name="expert_ffn",
    )(block_expert, block_slot, block_next, n_used, x_packed, w_eg, b_eg.reshape(E, 1, F), w_eu,
      b_eu.reshape(E, 1, F), w_ed, b_ed.reshape(E, 1, D))


def _gather_packed(pos_ref, src_hbm, dst, sem, n):
    group = 16

    def body(g, carry):
        rows = [pos_ref[0, g * group + j] for j in range(group)]
        for j, p in enumerate(rows):
            r = g * group + j
            pltpu.make_async_copy(
                src_hbm.at[pl.ds(pl.multiple_of(p * PACK_ROWS, PACK_ROWS), PACK_ROWS), :],
                dst.at[pl.ds(pl.multiple_of(r * PACK_ROWS, PACK_ROWS), PACK_ROWS), :],
                sem).start(priority=j % 2)
        return carry
    lax.fori_loop(0, n // group, body, 0)


def _combine_kernel(pos0_ref, pos1_ref, y_hbm, h_ref, w_ref, g_ref, o_ref, ybuf, sem):
    i = pl.program_id(0)
    n = pl.num_programs(0)
    slot = i % 2
    rows = TOP_K * COMBINE_ROWS

    @pl.when(i == 0)
    def _():
        _gather_packed(pos0_ref, y_hbm, ybuf.at[0], sem.at[0], rows)

    @pl.when(i + 1 < n)
    def _():
        _gather_packed(pos1_ref, y_hbm, ybuf.at[1 - slot], sem.at[1 - slot], rows)

    _wait_copies(y_hbm, ybuf.at[slot], sem.at[slot], rows * PACK_ROWS)
    acc = h_ref[...]
    w = w_ref[...]
    for k in range(TOP_K):
        acc = acc + w[:, k:k + 1] * _unpack_rows(ybuf.at[slot], k * COMBINE_ROWS * PACK_ROWS, COMBINE_ROWS)
    o_ref[...] = _rms(acc, g_ref[...])


def _combine(y_rows, pos_kmajor, h2, wts, g_final):
    T, D = h2.shape
    tm = COMBINE_ROWS
    nt = T // tm
    posspec = lambda off: pl.BlockSpec(
        (None, 1, TOP_K * tm), lambda i: (jnp.minimum(i + off, nt - 1), 0, 0), memory_space=pltpu.SMEM)
    return pl.pallas_call(
        _combine_kernel,
        grid=(nt,),
        in_specs=[posspec(0), posspec(1), pl.BlockSpec(memory_space=pl.ANY),
                  pl.BlockSpec((tm, D), lambda i: (i, 0)), pl.BlockSpec((tm, LANES), lambda i: (i, 0)),
                  pl.BlockSpec((1, D), lambda i: (0, 0))],
        out_specs=pl.BlockSpec((tm, D), lambda i: (i, 0)),
        out_shape=jax.ShapeDtypeStruct((T, D), F32),
        scratch_shapes=[pltpu.VMEM((2, TOP_K * tm * PACK_ROWS, LANES), F32), pltpu.SemaphoreType.DMA((2,))],
        compiler_params=pltpu.CompilerParams(
            dimension_semantics=("arbitrary",), vmem_limit_bytes=VMEM_LIMIT),
        name="combine_norm",
    )(pos_kmajor, pos_kmajor, y_rows, h2, wts, g_final.reshape(1, D))


def _layer(h, mem, norm_mix, w_in, conv_w, w_branch_attn, w_branch_conv, w_gate, b_gate, w_out,
           norm_cross, norm_mem, w_cq, w_ckv, w_co, norm_moe, router_w, router_b,
           w_eg, b_eg, w_eu, b_eu, w_ed, b_ed, norm_final):
    B, S, D = h.shape
    T = B * S
    W3 = 3 * ATTN_WIDTH
    n_attn = len(ATTN_GROUPS) * W3

    dils = tuple(dil for _, dil in ATTN_GROUPS)
    assert dils[0] == 1
    u_all = _norm_permute(h, norm_mix, dils[1:])
    w_in_b = w_in.astype(BF16)
    w_a = jnp.concatenate([w_in_b[:, n_attn:], w_in_b[:, :W3]], axis=1)
    proj, gates = _inproj(u_all[0], [(w_a, None), (w_gate.astype(BF16), b_gate)], 1, 512, 512,
                          name="inproj_conv_g1_gates")
    proj2d = proj.reshape(T, -1)
    gates2d = gates.reshape(T, -1)

    o_list, lse_list = [], []
    for g, (window, dil) in enumerate(ATTN_GROUPS):
        assert window // dil == Q_BLOCK
        if g == 0:
            qkv, col0 = proj, 3 * D // ATTN_WIDTH
        else:
            L = S // dil
            tm = min(L, 1024)
            qkv, = _inproj(u_all[g], [(w_in_b[:, g * W3:(g + 1) * W3], None)], 1024 // tm, tm, 512,
                           name=f"inproj_g{g + 1}")
            col0 = 0
        o, lse = _attention(qkv, dil, col0, name=f"dilated_attn_g{g + 1}")
        o_list.append(o)
        lse_list.append(lse)

    head_expand = (jnp.arange(LANES)[:, None] == jnp.arange(ATTN_WIDTH)[None, :] // HEAD_DIM).astype(BF16)
    h1 = _mix(o_list, lse_list, proj2d, gates2d, h.reshape(T, D), head_expand,
              w_branch_attn.astype(BF16), w_branch_conv.astype(BF16), w_out.astype(BF16), conv_w, S)

    kv = _mem_kv(mem, norm_mem, w_ckv.astype(BF16))
    rb = jnp.broadcast_to(router_b[:, None], (N_EXPERTS, LANES))
    h2, u3, meta, wts, counts = _cross_router(
        h1, kv, norm_cross.reshape(1, D), w_cq.astype(BF16), w_co.astype(BF16), norm_moe.reshape(1, D),
        router_w.T.astype(BF16), rb, S)

    nb = -(-(T * TOP_K) // MOE_ROWS) + N_EXPERTS
    cnt = counts[:, 0].astype(I32)
    padded = (cnt + MOE_ROWS - 1) // MOE_ROWS * MOE_ROWS
    pend = jnp.cumsum(padded)
    pstart = pend - padded
    experts = jnp.arange(N_EXPERTS, dtype=I32)

    def lookup(table, idx):
        out = jnp.zeros(idx.shape, table.dtype)
        for j in range(table.shape[0]):
            out = jnp.where(idx == j, table[j], out)
        return out

    tm = COMBINE_ROWS
    assert tm & (tm - 1) == 0 and T % tm == 0
    pos_kt = lookup(pstart, meta[:TOP_K].astype(I32)) + meta[TOP_K:2 * TOP_K].astype(I32)
    pos_kmajor = pos_kt.reshape(TOP_K, T // tm, tm).transpose(1, 0, 2).reshape(T // tm, 1, TOP_K * tm)
    block_start = jnp.arange(nb, dtype=I32) * MOE_ROWS
    block_expert = jnp.minimum(jnp.sum(block_start[:, None] >= pend[None, :], axis=1), N_EXPERTS - 1).astype(I32)
    active = cnt > 0
    slot_e = (jnp.cumsum(active.astype(I32)) - 1) % 2
    later = jnp.where(active[None, :] & (experts[None, :] > experts[:, None]), experts[None, :], N_EXPERTS)
    next_e = jnp.min(later, axis=1)
    next_e = jnp.where(next_e == N_EXPERTS, -1, next_e).astype(I32)
    n_pad = padded - cnt
    pad_end = jnp.cumsum(n_pad)
    fill = jnp.arange(nb * MOE_ROWS - T * TOP_K, dtype=I32)
    fill_seg = jnp.sum(fill[:, None] >= pad_end[None, :], axis=1)
    seg_first_row = jnp.concatenate([pstart + cnt, pend[-1:]])
    seg_first_fill = jnp.concatenate([pad_end - n_pad, pad_end[-1:]])
    fill_rows = (lookup(seg_first_row - seg_first_fill, fill_seg) + fill).astype(I32)

    x_rows = _dispatch(u3, pos_kmajor, fill_rows, tm)
    y_rows = _experts(x_rows, block_expert, slot_e[block_expert].astype(I32), next_e[block_expert],
                      (pend[-1:] // MOE_ROWS).astype(I32), w_eg, b_eg, w_eu, b_eu, w_ed, b_ed)

    out = _combine(y_rows, pos_kmajor, h2, wts, norm_final)
    return out.reshape(B, S, D)


def kernel(x, mem, norm_mix, w_in, conv_w, w_branch_attn, w_branch_conv, w_gate, b_gate, w_out, norm_cross, norm_mem, w_cq, w_ckv, w_co, norm_moe, router_w, router_b, w_exp_gate, b_exp_gate, w_exp_up, b_exp_up, w_exp_down, b_exp_down, norm_final):
    depth = norm_mix.shape[0]
    assert depth == 1, "the final norm is fused into the last layer's combine step"
    return _layer(x, mem, norm_mix[0], w_in[0], conv_w[0], w_branch_attn[0], w_branch_conv[0], w_gate[0],
                  b_gate[0], w_out[0], norm_cross[0], norm_mem[0], w_cq[0], w_ckv[0], w_co[0], norm_moe[0],
                  router_w[0], router_b[0], w_exp_gate[0], b_exp_gate[0], w_exp_up[0], b_exp_up[0],
                  w_exp_down[0], b_exp_down[0], norm_final)
```

```python
import functools

import jax
import jax.numpy as jnp
from jax import lax
from jax.experimental import pallas as pl
from jax.experimental.pallas import tpu as pltpu

F32 = jnp.float32
BF16 = jnp.bfloat16
I32 = jnp.int32

D_MODEL = 1024
ATTN_GROUPS = ((128, 1), (512, 4), (2048, 16))
ATTN_HEADS = 8
HEAD_DIM = 64
ATTN_WIDTH = ATTN_HEADS * HEAD_DIM
Q_BLOCK = 128
CONV_K = 3
MEM_HEADS = 4
MEM_HEAD_DIM = 128
MEM_WIDTH = MEM_HEADS * MEM_HEAD_DIM
N_EXPERTS = 32
TOP_K = 4
SWIGLU_LIMIT = 7.0
SWIGLU_ALPHA = 1.702
EPS = 1e-6
NEG_INF = -1e30

LANES = 128
VMEM_LIMIT = 56 * 1024 * 1024

MOE_ROWS = 256
MOE_STEP_BLOCKS = 4
COMBINE_ROWS = 512
CONV_HALO = 16
META_ROWS = 16


def _rms(x, g):
    ms = jnp.mean(x * x, axis=-1, keepdims=True)
    return x * lax.rsqrt(ms + EPS) * g


PACK_ROWS = D_MODEL // LANES


def _pack_rows(y, out_ref, n):
    for c in range(PACK_ROWS):
        out_ref[pl.ds(c, n, stride=PACK_ROWS), :] = y[:, c * LANES:(c + 1) * LANES]


def _unpack_rows(ref, start, n):
    return jnp.concatenate(
        [ref[pl.ds(start + c, n, stride=PACK_ROWS), :] for c in range(PACK_ROWS)], axis=1)


def _norm_permute_kernel(x_ref, g_ref, *refs, dils, tm):
    out_refs, slab = refs[:-1], refs[-1]
    u = _rms(x_ref[...], g_ref[...])
    out_refs[0][...] = u.astype(BF16)
    n_slab = u.shape[1] // LANES
    for c in range(n_slab):
        slab[c] = u[:, c * LANES:(c + 1) * LANES]
    for o_ref, dil in zip(out_refs[1:], dils):
        for r in range(dil):
            for c in range(n_slab):
                o_ref[r, :, c * LANES:(c + 1) * LANES] = (
                    slab[c, pl.ds(r, tm // dil, stride=dil), :].astype(BF16))


def _norm_permute(x, gain, dils, tm=1024):
    B, S, D = x.shape
    outs = [jax.ShapeDtypeStruct((B, 1, S, D), BF16)]
    specs = [pl.BlockSpec((None, None, tm, D), lambda b, i: (b, 0, i, 0))]
    for dil in dils:
        outs.append(jax.ShapeDtypeStruct((B, dil, S // dil, D), BF16))
        specs.append(pl.BlockSpec((None, dil, tm // dil, D), lambda b, i: (b, 0, i, 0)))
    return pl.pallas_call(
        functools.partial(_norm_permute_kernel, dils=dils, tm=tm),
        grid=(B, S // tm),
        in_specs=[pl.BlockSpec((None, tm, D), lambda b, i: (b, i, 0)),
                  pl.BlockSpec((1, D), lambda b, i: (0, 0))],
        out_specs=specs,
        out_shape=outs,
        scratch_shapes=[pltpu.VMEM((D // LANES, tm, LANES), F32)],
        compiler_params=pltpu.CompilerParams(
            dimension_semantics=("parallel", "parallel"), vmem_limit_bytes=VMEM_LIMIT),
        name="norm_permute",
    )(x, gain.reshape(1, D))


def _inproj_kernel(u_ref, *refs, gate_bias, rb, chunk):
    n_w = len(gate_bias)
    w_refs, b_refs, o_refs = refs[:n_w], refs[n_w:2 * n_w], refs[2 * n_w:]
    for r in range(rb):
        u = u_ref[r]
        for w_ref, b_ref, o_ref, sig in zip(w_refs, b_refs, o_refs, gate_bias):
            for c in range(w_ref.shape[1] // chunk):
                cols = slice(c * chunk, (c + 1) * chunk)
                acc = jnp.dot(u, w_ref[:, cols], preferred_element_type=F32)
                if sig:
                    acc = jax.nn.sigmoid(acc + b_ref[:, cols])
                o_ref[r, :, cols] = acc.astype(o_ref.dtype)


def _inproj(u, weights, rb, tm, chunk, name):
    B, R, L, D = u.shape
    ws = [w for w, _ in weights]
    bs = [jnp.zeros((1, w.shape[1]), F32) if b is None else b.reshape(1, -1) for w, b in weights]
    const = lambda a: pl.BlockSpec(a.shape, lambda b, r, i: (0, 0))
    return pl.pallas_call(
        functools.partial(_inproj_kernel, gate_bias=tuple(b is not None for _, b in weights),
                          rb=rb, chunk=chunk),
        grid=(B, R // rb, L // tm),
        in_specs=[pl.BlockSpec((None, rb, tm, D), lambda b, r, i: (b, r, i, 0))]
                 + [const(w) for w in ws] + [const(b) for b in bs],
        out_specs=[pl.BlockSpec((None, rb, tm, w.shape[1]), lambda b, r, i: (b, r, i, 0)) for w in ws],
        out_shape=[jax.ShapeDtypeStruct((B, R, L, w.shape[1]), BF16) for w in ws],
        compiler_params=pltpu.CompilerParams(
            dimension_semantics=("parallel", "parallel", "parallel"), vmem_limit_bytes=VMEM_LIMIT),
        name=name,
    )(u, *ws, *bs)


def _mem_kv_kernel(x_ref, g_ref, w_ref, o_ref):
    u = _rms(x_ref[...], g_ref[...]).astype(BF16)
    o_ref[...] = jnp.dot(u, w_ref[...], preferred_element_type=F32).astype(o_ref.dtype)


def _mem_kv(mem, gain, w):
    B, M, D = mem.shape
    N = w.shape[1]
    return pl.pallas_call(
        _mem_kv_kernel,
        grid=(B,),
        in_specs=[pl.BlockSpec((None, M, D), lambda b: (b, 0, 0)),
                  pl.BlockSpec((1, D), lambda b: (0, 0)),
                  pl.BlockSpec((D, N), lambda b: (0, 0))],
        out_specs=pl.BlockSpec((None, M, N), lambda b: (b, 0, 0)),
        out_shape=jax.ShapeDtypeStruct((B, M, N), BF16),
        compiler_params=pltpu.CompilerParams(
            dimension_semantics=("parallel",), vmem_limit_bytes=VMEM_LIMIT),
        name="mem_kv",
    )(mem, gain.reshape(1, D), w)


def _attn_block(q_ref, kp_ref, kc_ref, vp_ref, vc_ref, tab_ref, first, r, j):
    lane = lax.broadcasted_iota(I32, (Q_BLOCK, LANES), 1)
    low = lane < HEAD_DIM
    lse_tile = jnp.zeros((Q_BLOCK, LANES), F32)
    pairs = []
    if j > 0:
        first = 1
    for pair in range(ATTN_HEADS // 2):
        cols = slice(pair * LANES, (pair + 1) * LANES)
        q2 = q_ref[r, j * Q_BLOCK:(j + 1) * Q_BLOCK, cols] * (HEAD_DIM ** -0.5)
        if j == 0:
            k2 = jnp.concatenate([kp_ref[r, :, cols], kc_ref[r, :Q_BLOCK, cols]], axis=0)
            v2 = jnp.concatenate([vp_ref[r, :, cols], vc_ref[r, :Q_BLOCK, cols]], axis=0)
        else:
            k2 = kc_ref[r, (j - 1) * Q_BLOCK:(j + 1) * Q_BLOCK, cols]
            v2 = vc_ref[r, (j - 1) * Q_BLOCK:(j + 1) * Q_BLOCK, cols]
        outs = []
        for half in range(2):
            h = 2 * pair + half
            keep = low if half == 0 else jnp.logical_not(low)
            qm = jnp.where(keep, q2, jnp.zeros_like(q2))
            s = lax.dot_general(qm, k2, (((1,), (1,)), ((), ())), preferred_element_type=F32)
            s = s + tab_ref[first, h]
            m = jnp.max(s, axis=1, keepdims=True)
            p = jnp.exp(s - m)
            l = jnp.sum(p, axis=1, keepdims=True)
            o = jnp.dot(p.astype(BF16), v2, preferred_element_type=F32) * (1.0 / l)
            outs.append(o)
            lse_tile = jnp.where(lane == h, m + jnp.log(l), lse_tile)
        pairs.append(jnp.where(low, outs[0], outs[1]))
    return pairs, lse_tile


def _attn_kernel(q_ref, kp_ref, kc_ref, vp_ref, vc_ref, tab_ref, o_ref, lse_ref, *scratch, dil, qb):
    first = jnp.minimum(pl.program_id(1), 1)
    blocks = (q_ref, kp_ref, kc_ref, vp_ref, vc_ref, tab_ref)
    n_pair = ATTN_HEADS // 2
    if dil == 1:
        for j in range(qb):
            pairs, lse_tile = _attn_block(*blocks, first, 0, j)
            rows = slice(j * Q_BLOCK, (j + 1) * Q_BLOCK)
            for p in range(n_pair):
                o_ref[rows, p * LANES:(p + 1) * LANES] = pairs[p].astype(o_ref.dtype)
            lse_ref[rows, :] = lse_tile
        return

    o_scr, lse_scr = scratch

    unroll = min(dil, 8)

    def body(g, carry):
        for r in [g * unroll + t for t in range(unroll)]:
            for j in range(qb):
                pairs, lse_tile = _attn_block(*blocks, first, r, j)
                rows = pl.ds(j * Q_BLOCK * dil + r, Q_BLOCK, stride=dil)
                for p in range(n_pair):
                    o_scr[p, rows, :] = pairs[p]
                lse_scr[rows, :] = lse_tile
        return carry

    if dil == unroll:
        body(0, 0)
    else:
        lax.fori_loop(0, dil // unroll, body, 0)
    for p in range(n_pair):
        o_ref[:, p * LANES:(p + 1) * LANES] = o_scr[p].astype(o_ref.dtype)
    lse_ref[...] = lse_scr[...]


def _attn_bias_table(dil):
    slopes = jnp.power(2.0, -8.0 * jnp.arange(1, ATTN_HEADS + 1, dtype=F32) / ATTN_HEADS)
    iq = jnp.arange(Q_BLOCK)
    ik = jnp.arange(2 * Q_BLOCK)
    dist = iq[:, None] + Q_BLOCK - ik[None, :]
    band = (dist >= 0) & (dist <= Q_BLOCK)
    has_prev = jnp.stack([ik >= Q_BLOCK, jnp.ones_like(ik, dtype=bool)])
    mask = band[None] & has_prev[:, None, :]
    bias = -slopes[:, None, None] * (dist * dil).astype(F32)[None]
    return jnp.where(mask[:, None], bias[None], NEG_INF).astype(F32)


def _attention(qkv, dil, col0, name):
    B, _, L, _ = qkv.shape
    W = ATTN_WIDTH
    qb = max(1, 4 // dil)
    nb = L // (Q_BLOCK * qb)
    span = Q_BLOCK * qb * dil
    blk = lambda part, prev: (
        pl.BlockSpec((None, dil, Q_BLOCK, W), lambda b, n: (b, 0, jnp.maximum(n * qb - 1, 0), col0 + part))
        if prev else pl.BlockSpec((None, dil, Q_BLOCK * qb, W), lambda b, n: (b, 0, n, col0 + part)))
    tab = _attn_bias_table(dil)
    scratch = [] if dil == 1 else [pltpu.VMEM((ATTN_HEADS // 2, span, LANES), F32),
                                   pltpu.VMEM((span, LANES), F32)]
    o, lse = pl.pallas_call(
        functools.partial(_attn_kernel, dil=dil, qb=qb),
        grid=(B, nb),
        in_specs=[blk(0, False), blk(1, True), blk(1, False), blk(2, True), blk(2, False),
                  pl.BlockSpec(tab.shape, lambda b, n: (0, 0, 0, 0))],
        out_specs=[pl.BlockSpec((None, span, W), lambda b, n: (b, n, 0)),
                   pl.BlockSpec((None, span, LANES), lambda b, n: (b, n, 0))],
        out_shape=[jax.ShapeDtypeStruct((B, L * dil, W), BF16),
                   jax.ShapeDtypeStruct((B, L * dil, LANES), F32)],
        scratch_shapes=scratch,
        compiler_params=pltpu.CompilerParams(
            dimension_semantics=("parallel", "parallel"), vmem_limit_bytes=VMEM_LIMIT),
        name=name,
    )(qkv, qkv, qkv, qkv, qkv, tab)
    return o.reshape(B * L * dil, W), lse.reshape(B * L * dil, LANES)


def _mix_kernel(o1_ref, o2_ref, o3_ref, l1_ref, l2_ref, l3_ref, gb_ref, gc_ref, xc_ref, gch_ref, xch_ref,
                ga_ref, gv_ref, x_ref, e_ref, wa_ref, wc_ref, wo_ref, cw_ref, h_ref, z_scr, *, tm, seq):
    a1, a2, a3 = l1_ref[...], l2_ref[...], l3_ref[...]
    m = jnp.maximum(jnp.maximum(a1, a2), a3)
    e1, e2, e3 = jnp.exp(a1 - m), jnp.exp(a2 - m), jnp.exp(a3 - m)
    inv = 1.0 / (e1 + e2 + e3)
    y = jnp.zeros((tm, ATTN_WIDTH), F32)
    for e, o_ref in ((e1, o1_ref), (e2, o2_ref), (e3, o3_ref)):
        w = e * inv
        w_hi = w.astype(BF16)
        w_lo = (w - w_hi.astype(F32)).astype(BF16)
        w_heads = (jnp.dot(w_hi, e_ref[...], preferred_element_type=F32)
                   + jnp.dot(w_lo, e_ref[...], preferred_element_type=F32))
        y = y + w_heads * o_ref[...].astype(F32)
    branch_a = jnp.dot(y.astype(BF16), wa_ref[...], preferred_element_type=F32)

    at_start = (pl.program_id(0) * tm) % seq == 0
    halo = gch_ref[...].astype(F32) * xch_ref[...].astype(F32)
    z_scr[0:CONV_HALO, :] = jnp.where(at_start, jnp.zeros_like(halo), halo)
    z_scr[CONV_HALO:, :] = gc_ref[...].astype(F32) * xc_ref[...].astype(F32)
    conv = (cw_ref[0:1, :] * z_scr[pl.ds(CONV_HALO - 2, tm), :]
            + cw_ref[1:2, :] * z_scr[pl.ds(CONV_HALO - 1, tm), :]
            + cw_ref[2:3, :] * z_scr[pl.ds(CONV_HALO, tm), :])
    y_conv = gb_ref[...].astype(F32) * conv
    branch_c = jnp.dot(y_conv.astype(BF16), wc_ref[...], preferred_element_type=F32)

    mixed = ga_ref[...].astype(F32) * branch_a + gv_ref[...].astype(F32) * branch_c
    h_ref[...] = x_ref[...] + jnp.dot(mixed.astype(BF16), wo_ref[...], preferred_element_type=F32)


def _mix(o_list, lse_list, proj, gates, x2d, head_expand, wa, wc, wo, conv_w, seq, tm=512):
    T, D = x2d.shape
    W = ATTN_WIDTH
    row = lambda width, col: pl.BlockSpec((tm, width), lambda i: (i, col))
    halo = lambda col: pl.BlockSpec(
        (CONV_HALO, D), lambda i: (jnp.maximum(i * (tm // CONV_HALO) - 1, 0), col))
    full = lambda a: pl.BlockSpec(a.shape, lambda i: (0,) * a.ndim)
    return pl.pallas_call(
        functools.partial(_mix_kernel, tm=tm, seq=seq),
        grid=(T // tm,),
        in_specs=[row(W, 0)] * 3 + [row(LANES, 0)] * 3
                 + [row(D, 0), row(D, 1), row(D, 2), halo(1), halo(2)]
                 + [row(D, 0), row(D, 1), row(D, 0)]
                 + [full(head_expand), full(wa), full(wc), full(wo), full(conv_w)],
        out_specs=pl.BlockSpec((tm, D), lambda i: (i, 0)),
        out_shape=jax.ShapeDtypeStruct((T, D), F32),
        scratch_shapes=[pltpu.VMEM((tm + CONV_HALO, D), F32)],
        compiler_params=pltpu.CompilerParams(
            dimension_semantics=("parallel",), vmem_limit_bytes=VMEM_LIMIT),
        name="branch_mix",
    )(*o_list, *lse_list, proj, proj, proj, proj, proj, gates, gates, x2d,
      head_expand, wa, wc, wo, conv_w)


def _cross_kernel(h_ref, k_ref, v_ref, g2_ref, wq_ref, wo_ref, g3_ref, rw_ref, rb_ref, tri_ref,
                  h2_ref, u3_ref, meta_ref, wts_ref, cnt_ref, run_scr, *, tm):
    @pl.when(pl.program_id(0) == 0)
    def _():
        run_scr[...] = jnp.zeros_like(run_scr)

    h = h_ref[...]
    u = _rms(h, g2_ref[...]).astype(BF16)
    q = jnp.dot(u, wq_ref[...], preferred_element_type=F32) * (MEM_HEAD_DIM ** -0.5)
    heads = []
    for hd in range(MEM_HEADS):
        cols = slice(hd * MEM_HEAD_DIM, (hd + 1) * MEM_HEAD_DIM)
        s = lax.dot_general(q[:, cols].astype(BF16), k_ref[:, cols], (((1,), (1,)), ((), ())),
                            preferred_element_type=F32)
        m = jnp.max(s, axis=1, keepdims=True)
        p = jnp.exp(s - m)
        l = jnp.sum(p, axis=1, keepdims=True)
        heads.append(jnp.dot(p.astype(BF16), v_ref[:, cols], preferred_element_type=F32) * (1.0 / l))
    o = jnp.concatenate(heads, axis=1).astype(BF16)
    h2 = h + jnp.dot(o, wo_ref[...], preferred_element_type=F32)
    h2_ref[...] = h2
    u3 = _rms(h2, g3_ref[...])
    _pack_rows(u3, u3_ref, tm)

    logits = lax.dot_general(rw_ref[...], u3.astype(BF16), (((1,), (1,)), ((), ())),
                             preferred_element_type=F32) + rb_ref[:, 0:1]
    expert = lax.broadcasted_iota(I32, (N_EXPERTS, tm), 0).astype(F32)
    work = logits
    picked = jnp.zeros((N_EXPERTS, tm), F32)
    tops, idxs, sels = [], [], []
    for _ in range(TOP_K):
        mk = jnp.max(work, axis=0, keepdims=True)
        ik = jnp.min(jnp.where(work == mk, expert, float(N_EXPERTS)), axis=0, keepdims=True)
        sel = expert == ik
        work = jnp.where(sel, -jnp.inf, work)
        picked = picked + sel.astype(F32)
        tops.append(mk)
        idxs.append(ik)
        sels.append(sel)
    exps = [jnp.exp(t - tops[0]) for t in tops]
    inv = 1.0 / (exps[0] + exps[1] + exps[2] + exps[3])
    wts = [e * inv for e in exps]

    before = jnp.dot(picked.astype(BF16), tri_ref[...], preferred_element_type=F32) + run_scr[:, 0:1]
    ranks = [jnp.sum(jnp.where(sel, before, 0.0), axis=0, keepdims=True) for sel in sels]

    row = lax.broadcasted_iota(I32, (META_ROWS, tm), 0)
    meta = jnp.zeros((META_ROWS, tm), F32)
    for r, v in enumerate(idxs + ranks + wts):
        meta = jnp.where(row == r, v, meta)
    meta_ref[...] = meta
    row = lax.broadcasted_iota(I32, (LANES, tm), 0)
    wpad = jnp.zeros((LANES, tm), F32)
    for k in range(TOP_K):
        wpad = jnp.where(row == k, wts[k], wpad)
    wts_ref[...] = wpad.T
    run_scr[...] = run_scr[...] + jnp.sum(picked, axis=1, keepdims=True)
    cnt_ref[...] = run_scr[...]


def _cross_router(h1, kv, g2, wq, wo, g3, rw_t, rb, seq, tm=512):
    T, D = h1.shape
    n_mem = kv.shape[1]
    tri = (jnp.arange(tm)[:, None] < jnp.arange(tm)[None, :]).astype(BF16)
    full = lambda a: pl.BlockSpec(a.shape, lambda i: (0,) * a.ndim)
    kvspec = lambda col: pl.BlockSpec((None, n_mem, MEM_WIDTH), lambda i: ((i * tm) // seq, 0, col))
    tile = lambda width: pl.BlockSpec((tm, width), lambda i: (i, 0))
    return pl.pallas_call(
        functools.partial(_cross_kernel, tm=tm),
        grid=(T // tm,),
        in_specs=[tile(D), kvspec(0), kvspec(1), full(g2), full(wq), full(wo), full(g3),
                  full(rw_t), full(rb), full(tri)],
        out_specs=[tile(D), pl.BlockSpec((tm * PACK_ROWS, LANES), lambda i: (i, 0)),
                   pl.BlockSpec((META_ROWS, tm), lambda i: (0, i)), tile(LANES),
                   pl.BlockSpec((N_EXPERTS, LANES), lambda i: (0, 0))],
        out_shape=[jax.ShapeDtypeStruct((T, D), F32), jax.ShapeDtypeStruct((T * PACK_ROWS, LANES), F32),
                   jax.ShapeDtypeStruct((META_ROWS, T), F32), jax.ShapeDtypeStruct((T, LANES), F32),
                   jax.ShapeDtypeStruct((N_EXPERTS, LANES), F32)],
        scratch_shapes=[pltpu.VMEM((N_EXPERTS, LANES), F32)],
        compiler_params=pltpu.CompilerParams(
            dimension_semantics=("arbitrary",), vmem_limit_bytes=VMEM_LIMIT),
        name="cross_router",
    )(h1, kv, kv, g2, wq, wo, g3, rw_t, rb, tri)


def _plan_kernel(pstart_ref, pad_end_ref, fill_base_ref, meta_ref, pos_ref, fill_ref):
    ei = meta_ref[0:2 * TOP_K, :].astype(I32)
    start = jnp.zeros(ei.shape, I32)
    for j in range(N_EXPERTS):
        start = jnp.where(ei == j, pstart_ref[j], start)
    pos_ref[...] = start + pltpu.roll(ei, TOP_K, axis=0)

    rows, cols = fill_ref.shape
    f = lax.broadcasted_iota(I32, (rows, cols), 0) * cols + lax.broadcasted_iota(I32, (rows, cols), 1)
    base = jnp.full((rows, cols), fill_base_ref[0], I32)
    for j in range(1, N_EXPERTS + 1):
        base = jnp.where(f >= pad_end_ref[j - 1], fill_base_ref[j], base)
    fill_ref[...] = base + f


def _plan(meta, pstart, pad_end, fill_base, n_fill):
    T = meta.shape[1]
    rows = 2 * TOP_K
    assert n_fill % (rows * LANES) == 0
    grid_spec = pltpu.PrefetchScalarGridSpec(
        num_scalar_prefetch=3,
        grid=(1,),
        in_specs=[pl.BlockSpec(meta.shape, lambda i, a, b, c: (0, 0))],
        out_specs=[pl.BlockSpec((rows, T), lambda i, a, b, c: (0, 0)),
                   pl.BlockSpec((rows, n_fill // rows), lambda i, a, b, c: (0, 0))],
    )
    pos, fill = pl.pallas_call(
        _plan_kernel,
        grid_spec=grid_spec,
        out_shape=[jax.ShapeDtypeStruct((rows, T), I32), jax.ShapeDtypeStruct((rows, n_fill // rows), I32)],
        compiler_params=pltpu.CompilerParams(dimension_semantics=("arbitrary",)),
        name="row_plan",
    )(pstart, pad_end, fill_base, meta)
    return pos[:TOP_K], fill.reshape(-1)


def _wait_copies(src_hbm, dst, sem, rows):
    pltpu.make_async_copy(src_hbm.at[pl.ds(0, rows), :], dst.at[pl.ds(0, rows), :], sem).wait()


def _dispatch_kernel(pos_ref, u_hbm, x_hbm, stage, load_sem, row_sem, *, tm, n_token_steps):
    i = pl.program_id(0)
    n = pl.num_programs(0)
    batch = tm * TOP_K * PACK_ROWS
    tile_rows = tm * PACK_ROWS

    def tile_copy(step):
        first = pl.multiple_of(jnp.minimum(step, n_token_steps - 1) * tile_rows, tile_rows)
        return pltpu.make_async_copy(u_hbm.at[pl.ds(first, tile_rows), :], stage.at[step % 3],
                                     load_sem.at[step % 3])

    @pl.when(i == 0)
    def _():
        tile_copy(i).start()

    @pl.when(i + 1 < n)
    def _():
        tile_copy(i + 1).start()

    tile_copy(i).wait()
    src_tile = stage.at[i % 3]
    group = 16

    def body(g, carry):
        rows = [pos_ref[0, g * group + j] for j in range(group)]
        t0 = (g * group) & (tm - 1)
        for j, p in enumerate(rows):
            src = src_tile.at[pl.ds(pl.multiple_of((t0 + j) * PACK_ROWS, PACK_ROWS), PACK_ROWS), :]
            dst = x_hbm.at[pl.ds(pl.multiple_of(p * PACK_ROWS, PACK_ROWS), PACK_ROWS), :]
            pltpu.make_async_copy(src, dst, row_sem.at[i % 2]).start(priority=j % 2)
        return carry

    lax.fori_loop(0, tm * TOP_K // group, body, 0)

    @pl.when(i > 0)
    def _():
        _wait_copies(u_hbm, x_hbm, row_sem.at[1 - i % 2], batch)

    @pl.when(i == n - 1)
    def _():
        _wait_copies(u_hbm, x_hbm, row_sem.at[i % 2], batch)


def _dispatch(u_packed, pos_kmajor, fill_rows, tm):
    T = pos_kmajor.shape[0] * tm
    per_step = TOP_K * tm
    dst_rows = jnp.concatenate([pos_kmajor.reshape(-1), fill_rows])
    n_rows = dst_rows.shape[0]
    assert n_rows % per_step == 0
    steps = n_rows // per_step
    return pl.pallas_call(
        functools.partial(_dispatch_kernel, tm=tm, n_token_steps=T // tm),
        grid=(steps,),
        in_specs=[pl.BlockSpec((None, 1, per_step), lambda i: (i, 0, 0), memory_space=pltpu.SMEM),
                  pl.BlockSpec(memory_space=pl.ANY)],
        out_specs=pl.BlockSpec(memory_space=pl.ANY),
        out_shape=jax.ShapeDtypeStruct((n_rows * PACK_ROWS, LANES), F32),
        scratch_shapes=[pltpu.VMEM((3, tm * PACK_ROWS, LANES), F32),
                        pltpu.SemaphoreType.DMA((3,)), pltpu.SemaphoreType.DMA((2,))],
        compiler_params=pltpu.CompilerParams(dimension_semantics=("arbitrary",)),
        name="dispatch_rows",
    )(dst_rows.reshape(steps, 1, per_step), u_packed)


def _expert_kernel(bexp_ref, slot_ref, next_ref, nused_ref, x_ref, wg_hbm, bg_ref, wu_hbm, bu_ref,
                   wd_hbm, bd_ref, y_ref, stage, w_bf, sem):
    weights = (wg_hbm, wu_hbm, wd_hbm)
    block_rows = MOE_ROWS * PACK_ROWS

    def fetch(expert, into):
        return [pltpu.make_async_copy(w.at[expert], stage.at[into, j], sem.at[into])
                for j, w in enumerate(weights)]

    for sub in range(MOE_STEP_BLOCKS):
        i = pl.program_id(0) * MOE_STEP_BLOCKS + sub
        used = i < nused_ref[0]
        e = bexp_ref[i]
        slot = slot_ref[i]
        changed = jnp.logical_or(i == 0, e != bexp_ref[jnp.maximum(i - 1, 0)])
        y_rows = y_ref.at[pl.ds(sub * block_rows, block_rows), :]

        @pl.when(jnp.logical_and(changed, used))
        def _():
            @pl.when(i == 0)
            def _():
                for c in fetch(e, slot):
                    c.start()

            for c in fetch(e, slot):
                c.wait()
            for s in range(2):
                @pl.when(slot == s)
                def _():
                    for j in range(len(weights)):
                        w_bf[j] = stage[s, j].astype(BF16)

            @pl.when(next_ref[i] >= 0)
            def _():
                for c in fetch(next_ref[i], 1 - slot):
                    c.start(priority=1)

        @pl.when(used)
        def _():
            x = _unpack_rows(x_ref, sub * block_rows, MOE_ROWS).astype(BF16)
            gate = jnp.minimum(jnp.dot(x, w_bf[0], preferred_element_type=F32) + bg_ref[e], SWIGLU_LIMIT)
            lin = jnp.clip(jnp.dot(x, w_bf[1], preferred_element_type=F32) + bu_ref[e],
                           -SWIGLU_LIMIT, SWIGLU_LIMIT)
            hdn = gate * jax.nn.sigmoid(SWIGLU_ALPHA * gate) * (lin + 1.0)
            y = jnp.dot(hdn.astype(BF16), w_bf[2], preferred_element_type=F32) + bd_ref[e]
            _pack_rows(y, y_rows, MOE_ROWS)

        @pl.when(jnp.logical_not(used))
        def _():
            y_rows[...] = jnp.zeros(y_rows.shape, y_rows.dtype)


def _experts(x_packed, block_expert, block_slot, block_next, n_used, w_eg, b_eg, w_eu, b_eu, w_ed, b_ed):
    nb = block_expert.shape[0]
    assert nb % MOE_STEP_BLOCKS == 0
    E, D, F = w_eg.shape
    assert D == F
    step_rows = MOE_STEP_BLOCKS * MOE_ROWS * PACK_ROWS
    rows = lambda i, be, sl, nx, nu: (i, 0)
    bspec = lambda n: pl.BlockSpec((E, 1, n), lambda i, be, sl, nx, nu: (0, 0, 0))
    anyspec = pl.BlockSpec(memory_space=pl.ANY)
    grid_spec = pltpu.PrefetchScalarGridSpec(
        num_scalar_prefetch=4,
        grid=(nb // MOE_STEP_BLOCKS,),
        in_specs=[pl.BlockSpec((step_rows, LANES), rows),
                  anyspec, bspec(F), anyspec, bspec(F), anyspec, bspec(D)],
        out_specs=pl.BlockSpec((step_rows, LANES), rows),
        scratch_shapes=[pltpu.VMEM((2, 3, D, F), F32), pltpu.VMEM((3, D, F), BF16),
                        pltpu.SemaphoreType.DMA((2,))],
    )
    return pl.pallas_call(
        _expert_kernel,
        grid_spec=grid_spec,
        out_shape=jax.ShapeDtypeStruct(x_packed.shape, F32),
        compiler_params=pltpu.CompilerParams(
            dimension_semantics=("arbitrary",), vmem_limit_bytes=VMEM_LIMIT),
        name="expert_ffn",
    )(block_expert, block_slot, block_next, n_used, x_packed, w_eg, b_eg.reshape(E, 1, F), w_eu,
      b_eu.reshape(E, 1, F), w_ed, b_ed.reshape(E, 1, D))


def _gather_packed(pos_ref, src_hbm, dst, sem, n):
    group = 16

    def body(g, carry):
        rows = [pos_ref[0, g * group + j] for j in range(group)]
        for j, p in enumerate(rows):
            r = g * group + j
            pltpu.make_async_copy(
                src_hbm.at[pl.ds(pl.multiple_of(p * PACK_ROWS, PACK_ROWS), PACK_ROWS), :],
                dst.at[pl.ds(pl.multiple_of(r * PACK_ROWS, PACK_ROWS), PACK_ROWS), :],
                sem).start(priority=j % 2)
        return carry
    lax.fori_loop(0, n // group, body, 0)


def _combine_kernel(pos0_ref, pos1_ref, y_hbm, h_ref, w_ref, g_ref, o_ref, ybuf, sem):
    i = pl.program_id(0)
    n = pl.num_programs(0)
    slot = i % 2
    rows = TOP_K * COMBINE_ROWS

    @pl.when(i == 0)
    def _():
        _gather_packed(pos0_ref, y_hbm, ybuf.at[0], sem.at[0], rows)

    @pl.when(i + 1 < n)
    def _():
        _gather_packed(pos1_ref, y_hbm, ybuf.at[1 - slot], sem.at[1 - slot], rows)

    _wait_copies(y_hbm, ybuf.at[slot], sem.at[slot], rows * PACK_ROWS)
    acc = h_ref[...]
    w = w_ref[...]
    for k in range(TOP_K):
        acc = acc + w[:, k:k + 1] * _unpack_rows(ybuf.at[slot], k * COMBINE_ROWS * PACK_ROWS, COMBINE_ROWS)
    o_ref[...] = _rms(acc, g_ref[...])


def _combine(y_rows, pos_kmajor, h2, wts, g_final):
    T, D = h2.shape
    tm = COMBINE_ROWS
    nt = T // tm
    posspec = lambda off: pl.BlockSpec(
        (None, 1, TOP_K * tm), lambda i: (jnp.minimum(i + off, nt - 1), 0, 0), memory_space=pltpu.SMEM)
    return pl.pallas_call(
        _combine_kernel,
        grid=(nt,),
        in_specs=[posspec(0), posspec(1), pl.BlockSpec(memory_space=pl.ANY),
                  pl.BlockSpec((tm, D), lambda i: (i, 0)), pl.BlockSpec((tm, LANES), lambda i: (i, 0)),
                  pl.BlockSpec((1, D), lambda i: (0, 0))],
        out_specs=pl.BlockSpec((tm, D), lambda i: (i, 0)),
        out_shape=jax.ShapeDtypeStruct((T, D), F32),
        scratch_shapes=[pltpu.VMEM((2, TOP_K * tm * PACK_ROWS, LANES), F32), pltpu.SemaphoreType.DMA((2,))],
        compiler_params=pltpu.CompilerParams(
            dimension_semantics=("arbitrary",), vmem_limit_bytes=VMEM_LIMIT),
        name="combine_norm",
    )(pos_kmajor, pos_kmajor, y_rows, h2, wts, g_final.reshape(1, D))


def _layer(h, mem, norm_mix, w_in, conv_w, w_branch_attn, w_branch_conv, w_gate, b_gate, w_out,
           norm_cross, norm_mem, w_cq, w_ckv, w_co, norm_moe, router_w, router_b,
           w_eg, b_eg, w_eu, b_eu, w_ed, b_ed, norm_final):
    B, S, D = h.shape
    T = B * S
    W3 = 3 * ATTN_WIDTH
    n_attn = len(ATTN_GROUPS) * W3

    dils = tuple(dil for _, dil in ATTN_GROUPS)
    assert dils[0] == 1
    u_all = _norm_permute(h, norm_mix, dils[1:])
    w_in_b = w_in.astype(BF16)
    w_a = jnp.concatenate([w_in_b[:, n_attn:], w_in_b[:, :W3]], axis=1)
    proj, gates = _inproj(u_all[0], [(w_a, None), (w_gate.astype(BF16), b_gate)], 1, 512, 512,
                          name="inproj_conv_g1_gates")
    proj2d = proj.reshape(T, -1)
    gates2d = gates.reshape(T, -1)

    o_list, lse_list = [], []
    for g, (window, dil) in enumerate(ATTN_GROUPS):
        assert window // dil == Q_BLOCK
        if g == 0:
            qkv, col0 = proj, 3 * D // ATTN_WIDTH
        else:
            L = S // dil
            tm = min(L, 1024)
            qkv, = _inproj(u_all[g], [(w_in_b[:, g * W3:(g + 1) * W3], None)], 1024 // tm, tm, 512,
                           name=f"inproj_g{g + 1}")
            col0 = 0
        o, lse = _attention(qkv, dil, col0, name=f"dilated_attn_g{g + 1}")
        o_list.append(o)
        lse_list.append(lse)

    head_expand = (jnp.arange(LANES)[:, None] == jnp.arange(ATTN_WIDTH)[None, :] // HEAD_DIM).astype(BF16)
    h1 = _mix(o_list, lse_list, proj2d, gates2d, h.reshape(T, D), head_expand,
              w_branch_attn.astype(BF16), w_branch_conv.astype(BF16), w_out.astype(BF16), conv_w, S)

    kv = _mem_kv(mem, norm_mem, w_ckv.astype(BF16))
    rb = jnp.broadcast_to(router_b[:, None], (N_EXPERTS, LANES))
    h2, u3, meta, wts, counts = _cross_router(
        h1, kv, norm_cross.reshape(1, D), w_cq.astype(BF16), w_co.astype(BF16), norm_moe.reshape(1, D),
        router_w.T.astype(BF16), rb, S)

    nb = -(-(T * TOP_K) // MOE_ROWS) + N_EXPERTS
    cnt = counts[:, 0].astype(I32)
    padded = (cnt + MOE_ROWS - 1) // MOE_ROWS * MOE_ROWS
    pend = jnp.cumsum(padded)
    pstart = pend - padded
    experts = jnp.arange(N_EXPERTS, dtype=I32)
    n_pad = padded - cnt
    pad_end = jnp.cumsum(n_pad)
    fill_base = jnp.concatenate([pstart + cnt - (pad_end - n_pad), pend[-1:] - pad_end[-1:]])
    pos_kt, fill_rows = _plan(meta, pstart, pad_end, fill_base, nb * MOE_ROWS - T * TOP_K)

    tm = COMBINE_ROWS
    assert tm & (tm - 1) == 0 and T % tm == 0
    pos_kmajor = pos_kt.reshape(TOP_K, T // tm, tm).transpose(1, 0, 2).reshape(T // tm, 1, TOP_K * tm)
    block_start = jnp.arange(nb, dtype=I32) * MOE_ROWS
    block_expert = jnp.minimum(jnp.sum(block_start[:, None] >= pend[None, :], axis=1), N_EXPERTS - 1).astype(I32)
    active = cnt > 0
    slot_e = (jnp.cumsum(active.astype(I32)) - 1) % 2
    later = jnp.where(active[None, :] & (experts[None, :] > experts[:, None]), experts[None, :], N_EXPERTS)
    next_e = jnp.min(later, axis=1)
    next_e = jnp.where(next_e == N_EXPERTS, -1, next_e).astype(I32)
    x_rows = _dispatch(u3, pos_kmajor, fill_rows, tm)
    y_rows = _experts(x_rows, block_expert, slot_e[block_expert].astype(I32), next_e[block_expert],
                      (pend[-1:] // MOE_ROWS).astype(I32), w_eg, b_eg, w_eu, b_eu, w_ed, b_ed)

    out = _combine(y_rows, pos_kmajor, h2, wts, norm_final)
    return out.reshape(B, S, D)


def kernel(x, mem, norm_mix, w_in, conv_w, w_branch_attn, w_branch_conv, w_gate, b_gate, w_out, norm_cross, norm_mem, w_cq, w_ckv, w_co, norm_moe, router_w, router_b, w_exp_gate, b_exp_gate, w_exp_up, b_exp_up, w_exp_down, b_exp_down, norm_final):
    depth = norm_mix.shape[0]
    assert depth == 1, "the final norm is fused into the last layer's combine step"
    return _layer(x, mem, norm_mix[0], w_in[0], conv_w[0], w_branch_attn[0], w_branch_conv[0], w_gate[0],
                  b_gate[0], w_out[0], norm_cross[0], norm_mem[0], w_cq[0], w_ckv[0], w_co[0], norm_moe[0],
                  router_w[0], router_b[0], w_exp_gate[0], b_exp_gate[0], w_exp_up[0], b_exp_up[0],
                  w_exp_down[0], b_exp_down[0], norm_final)
```

```python
import functools

import jax
import jax.numpy as jnp
import numpy as np
from jax import lax
from jax.experimental import pallas as pl
from jax.experimental.pallas import tpu as pltpu

F32 = jnp.float32
BF16 = jnp.bfloat16
I32 = jnp.int32

D_MODEL = 1024
ATTN_GROUPS = ((128, 1), (512, 4), (2048, 16))
ATTN_HEADS = 8
HEAD_DIM = 64
ATTN_WIDTH = ATTN_HEADS * HEAD_DIM
Q_BLOCK = 128
CONV_K = 3
MEM_HEADS = 4
MEM_HEAD_DIM = 128
MEM_WIDTH = MEM_HEADS * MEM_HEAD_DIM
N_EXPERTS = 32
TOP_K = 4
SWIGLU_LIMIT = 7.0
SWIGLU_ALPHA = 1.702
EPS = 1e-6
NEG_INF = -1e30

LANES = 128
VMEM_LIMIT = 56 * 1024 * 1024

MOE_ROWS = 256
MOE_STEP_BLOCKS = 4
COMBINE_ROWS = 512
CONV_HALO = 16
META_ROWS = 16


def _rms(x, g):
    ms = jnp.mean(x * x, axis=-1, keepdims=True)
    return x * lax.rsqrt(ms + EPS) * g


PACK_ROWS = D_MODEL // LANES


def _pack_rows(y, out_ref, n):
    for c in range(PACK_ROWS):
        out_ref[pl.ds(c, n, stride=PACK_ROWS), :] = y[:, c * LANES:(c + 1) * LANES]


def _unpack_rows(ref, start, n):
    return jnp.concatenate(
        [ref[pl.ds(start + c, n, stride=PACK_ROWS), :] for c in range(PACK_ROWS)], axis=1)


def _inproj_kernel(x_ref, g_ref, *refs, gate_bias, dil, chunk):
    n_w = len(gate_bias)
    w_refs, b_refs, o_refs = refs[:n_w], refs[n_w:2 * n_w], refs[2 * n_w:3 * n_w]
    u_scr = refs[3 * n_w]
    tm = x_ref.shape[0]
    per = tm // dil
    u = _rms(x_ref[...], g_ref[...])
    if dil == 1:
        u_scr[...] = u.astype(BF16)
    else:
        slab = refs[3 * n_w + 1]
        n_slab = u.shape[1] // LANES
        for c in range(n_slab):
            slab[c] = u[:, c * LANES:(c + 1) * LANES]
        for r in range(dil):
            for c in range(n_slab):
                u_scr[r * per:(r + 1) * per, c * LANES:(c + 1) * LANES] = (
                    slab[c, pl.ds(r, per, stride=dil), :].astype(BF16))
    ub = u_scr[...]
    for w_ref, b_ref, o_ref, sig in zip(w_refs, b_refs, o_refs, gate_bias):
        for c in range(w_ref.shape[1] // chunk):
            cols = slice(c * chunk, (c + 1) * chunk)
            acc = jnp.dot(ub, w_ref[:, cols], preferred_element_type=F32)
            if sig:
                acc = jax.nn.sigmoid(acc + b_ref[:, cols])
            for r in range(dil):
                o_ref[r, :, cols] = acc[r * per:(r + 1) * per].astype(o_ref.dtype)


def _inproj(x, gain, weights, dil, tm, chunk, name):
    B, S, D = x.shape
    ws = [w for w, _ in weights]
    bs = [jnp.zeros((1, w.shape[1]), F32) if b is None else b.reshape(1, -1) for w, b in weights]
    const = lambda a: pl.BlockSpec(a.shape, lambda b, i: (0, 0))
    scratch = [pltpu.VMEM((tm, D), BF16)]
    if dil > 1:
        scratch.append(pltpu.VMEM((D // LANES, tm, LANES), F32))
    return pl.pallas_call(
        functools.partial(_inproj_kernel, gate_bias=tuple(b is not None for _, b in weights),
                          dil=dil, chunk=chunk),
        grid=(B, S // tm),
        in_specs=[pl.BlockSpec((None, tm, D), lambda b, i: (b, i, 0)), const(gain.reshape(1, D))]
                 + [const(w) for w in ws] + [const(b) for b in bs],
        out_specs=[pl.BlockSpec((None, dil, tm // dil, w.shape[1]), lambda b, i: (b, 0, i, 0)) for w in ws],
        out_shape=[jax.ShapeDtypeStruct((B, dil, S // dil, w.shape[1]), BF16) for w in ws],
        scratch_shapes=scratch,
        compiler_params=pltpu.CompilerParams(
            dimension_semantics=("parallel", "parallel"), vmem_limit_bytes=VMEM_LIMIT),
        name=name,
    )(x, gain.reshape(1, D), *ws, *bs)


def _mem_kv_kernel(x_ref, g_ref, w_ref, o_ref):
    u = _rms(x_ref[...], g_ref[...]).astype(BF16)
    o_ref[...] = jnp.dot(u, w_ref[...], preferred_element_type=F32).astype(o_ref.dtype)


def _mem_kv(mem, gain, w):
    B, M, D = mem.shape
    N = w.shape[1]
    return pl.pallas_call(
        _mem_kv_kernel,
        grid=(B,),
        in_specs=[pl.BlockSpec((None, M, D), lambda b: (b, 0, 0)),
                  pl.BlockSpec((1, D), lambda b: (0, 0)),
                  pl.BlockSpec((D, N), lambda b: (0, 0))],
        out_specs=pl.BlockSpec((None, M, N), lambda b: (b, 0, 0)),
        out_shape=jax.ShapeDtypeStruct((B, M, N), BF16),
        compiler_params=pltpu.CompilerParams(
            dimension_semantics=("parallel",), vmem_limit_bytes=VMEM_LIMIT),
        name="mem_kv",
    )(mem, gain.reshape(1, D), w)


def _attn_block(q_ref, kp_ref, kc_ref, vp_ref, vc_ref, tab_ref, first, r, j):
    lane = lax.broadcasted_iota(I32, (Q_BLOCK, LANES), 1)
    low = lane < HEAD_DIM
    lse_tile = jnp.zeros((Q_BLOCK, LANES), F32)
    pairs = []
    if j > 0:
        first = 1
    for pair in range(ATTN_HEADS // 2):
        cols = slice(pair * LANES, (pair + 1) * LANES)
        q2 = q_ref[r, j * Q_BLOCK:(j + 1) * Q_BLOCK, cols] * (HEAD_DIM ** -0.5)
        if j == 0:
            k2 = jnp.concatenate([kp_ref[r, :, cols], kc_ref[r, :Q_BLOCK, cols]], axis=0)
            v2 = jnp.concatenate([vp_ref[r, :, cols], vc_ref[r, :Q_BLOCK, cols]], axis=0)
        else:
            k2 = kc_ref[r, (j - 1) * Q_BLOCK:(j + 1) * Q_BLOCK, cols]
            v2 = vc_ref[r, (j - 1) * Q_BLOCK:(j + 1) * Q_BLOCK, cols]
        outs = []
        for half in range(2):
            h = 2 * pair + half
            keep = low if half == 0 else jnp.logical_not(low)
            qm = jnp.where(keep, q2, jnp.zeros_like(q2))
            s = lax.dot_general(qm, k2, (((1,), (1,)), ((), ())), preferred_element_type=F32)
            s = s + tab_ref[first, h]
            m = jnp.max(s, axis=1, keepdims=True)
            p = jnp.exp(s - m)
            l = jnp.sum(p, axis=1, keepdims=True)
            o = jnp.dot(p.astype(BF16), v2, preferred_element_type=F32) * (1.0 / l)
            outs.append(o)
            lse_tile = jnp.where(lane == h, m + jnp.log(l), lse_tile)
        pairs.append(jnp.where(low, outs[0], outs[1]))
    return pairs, lse_tile


def _attn_kernel(q_ref, kp_ref, kc_ref, vp_ref, vc_ref, tab_ref, o_ref, lse_ref, *scratch, dil, qb):
    first = jnp.minimum(pl.program_id(1), 1)
    blocks = (q_ref, kp_ref, kc_ref, vp_ref, vc_ref, tab_ref)
    n_pair = ATTN_HEADS // 2
    if dil == 1:
        for j in range(qb):
            pairs, lse_tile = _attn_block(*blocks, first, 0, j)
            rows = slice(j * Q_BLOCK, (j + 1) * Q_BLOCK)
            for p in range(n_pair):
                o_ref[rows, p * LANES:(p + 1) * LANES] = pairs[p].astype(o_ref.dtype)
            lse_ref[rows, :] = lse_tile
        return

    o_scr, lse_scr = scratch

    unroll = min(dil, 8)

    def body(g, carry):
        for r in [g * unroll + t for t in range(unroll)]:
            for j in range(qb):
                pairs, lse_tile = _attn_block(*blocks, first, r, j)
                rows = pl.ds(j * Q_BLOCK * dil + r, Q_BLOCK, stride=dil)
                for p in range(n_pair):
                    o_scr[p, rows, :] = pairs[p]
                lse_scr[rows, :] = lse_tile
        return carry

    if dil == unroll:
        body(0, 0)
    else:
        lax.fori_loop(0, dil // unroll, body, 0)
    for p in range(n_pair):
        o_ref[:, p * LANES:(p + 1) * LANES] = o_scr[p].astype(o_ref.dtype)
    lse_ref[...] = lse_scr[...]


def _attn_bias_table(dil):
    slopes = np.power(2.0, -8.0 * np.arange(1, ATTN_HEADS + 1, dtype=np.float32) / ATTN_HEADS)
    iq = np.arange(Q_BLOCK)
    ik = np.arange(2 * Q_BLOCK)
    dist = iq[:, None] + Q_BLOCK - ik[None, :]
    band = (dist >= 0) & (dist <= Q_BLOCK)
    has_prev = np.stack([ik >= Q_BLOCK, np.ones_like(ik, dtype=bool)])
    mask = band[None] & has_prev[:, None, :]
    bias = -slopes[:, None, None] * (dist * dil).astype(np.float32)[None]
    return np.where(mask[:, None], bias[None], np.float32(NEG_INF)).astype(np.float32)


def _attention(qkv, dil, col0, name):
    B, _, L, _ = qkv.shape
    W = ATTN_WIDTH
    qb = max(1, 4 // dil)
    nb = L // (Q_BLOCK * qb)
    span = Q_BLOCK * qb * dil
    blk = lambda part, prev: (
        pl.BlockSpec((None, dil, Q_BLOCK, W), lambda b, n: (b, 0, jnp.maximum(n * qb - 1, 0), col0 + part))
        if prev else pl.BlockSpec((None, dil, Q_BLOCK * qb, W), lambda b, n: (b, 0, n, col0 + part)))
    tab = _attn_bias_table(dil)
    scratch = [] if dil == 1 else [pltpu.VMEM((ATTN_HEADS // 2, span, LANES), F32),
                                   pltpu.VMEM((span, LANES), F32)]
    o, lse = pl.pallas_call(
        functools.partial(_attn_kernel, dil=dil, qb=qb),
        grid=(B, nb),
        in_specs=[blk(0, False), blk(1, True), blk(1, False), blk(2, True), blk(2, False),
                  pl.BlockSpec(tab.shape, lambda b, n: (0, 0, 0, 0))],
        out_specs=[pl.BlockSpec((None, span, W), lambda b, n: (b, n, 0)),
                   pl.BlockSpec((None, span, LANES), lambda b, n: (b, n, 0))],
        out_shape=[jax.ShapeDtypeStruct((B, L * dil, W), BF16),
                   jax.ShapeDtypeStruct((B, L * dil, LANES), F32)],
        scratch_shapes=scratch,
        compiler_params=pltpu.CompilerParams(
            dimension_semantics=("parallel", "parallel"), vmem_limit_bytes=VMEM_LIMIT),
        name=name,
    )(qkv, qkv, qkv, qkv, qkv, tab)
    return o.reshape(B * L * dil, W), lse.reshape(B * L * dil, LANES)


def _mix_kernel(o1_ref, o2_ref, o3_ref, l1_ref, l2_ref, l3_ref, gb_ref, gc_ref, xc_ref, gch_ref, xch_ref,
                ga_ref, gv_ref, x_ref, e_ref, wa_ref, wc_ref, wo_ref, cw_ref, h_ref, z_scr, *, tm, seq):
    a1, a2, a3 = l1_ref[...], l2_ref[...], l3_ref[...]
    m = jnp.maximum(jnp.maximum(a1, a2), a3)
    e1, e2, e3 = jnp.exp(a1 - m), jnp.exp(a2 - m), jnp.exp(a3 - m)
    inv = 1.0 / (e1 + e2 + e3)
    y = jnp.zeros((tm, ATTN_WIDTH), F32)
    for e, o_ref in ((e1, o1_ref), (e2, o2_ref), (e3, o3_ref)):
        w = e * inv
        w_hi = w.astype(BF16)
        w_lo = (w - w_hi.astype(F32)).astype(BF16)
        w_heads = (jnp.dot(w_hi, e_ref[...], preferred_element_type=F32)
                   + jnp.dot(w_lo, e_ref[...], preferred_element_type=F32))
        y = y + w_heads * o_ref[...].astype(F32)
    branch_a = jnp.dot(y.astype(BF16), wa_ref[...], preferred_element_type=F32)

    at_start = (pl.program_id(0) * tm) % seq == 0
    halo = gch_ref[...].astype(F32) * xch_ref[...].astype(F32)
    z_scr[0:CONV_HALO, :] = jnp.where(at_start, jnp.zeros_like(halo), halo)
    z_scr[CONV_HALO:, :] = gc_ref[...].astype(F32) * xc_ref[...].astype(F32)
    conv = (cw_ref[0:1, :] * z_scr[pl.ds(CONV_HALO - 2, tm), :]
            + cw_ref[1:2, :] * z_scr[pl.ds(CONV_HALO - 1, tm), :]
            + cw_ref[2:3, :] * z_scr[pl.ds(CONV_HALO, tm), :])
    y_conv = gb_ref[...].astype(F32) * conv
    branch_c = jnp.dot(y_conv.astype(BF16), wc_ref[...], preferred_element_type=F32)

    mixed = ga_ref[...].astype(F32) * branch_a + gv_ref[...].astype(F32) * branch_c
    h_ref[...] = x_ref[...] + jnp.dot(mixed.astype(BF16), wo_ref[...], preferred_element_type=F32)


def _mix(o_list, lse_list, proj, gates, x2d, head_expand, wa, wc, wo, conv_w, seq, tm=512):
    T, D = x2d.shape
    W = ATTN_WIDTH
    row = lambda width, col: pl.BlockSpec((tm, width), lambda i: (i, col))
    halo = lambda col: pl.BlockSpec(
        (CONV_HALO, D), lambda i: (jnp.maximum(i * (tm // CONV_HALO) - 1, 0), col))
    full = lambda a: pl.BlockSpec(a.shape, lambda i: (0,) * a.ndim)
    return pl.pallas_call(
        functools.partial(_mix_kernel, tm=tm, seq=seq),
        grid=(T // tm,),
        in_specs=[row(W, 0)] * 3 + [row(LANES, 0)] * 3
                 + [row(D, 0), row(D, 1), row(D, 2), halo(1), halo(2)]
                 + [row(D, 0), row(D, 1), row(D, 0)]
                 + [full(head_expand), full(wa), full(wc), full(wo), full(conv_w)],
        out_specs=pl.BlockSpec((tm, D), lambda i: (i, 0)),
        out_shape=jax.ShapeDtypeStruct((T, D), F32),
        scratch_shapes=[pltpu.VMEM((tm + CONV_HALO, D), F32)],
        compiler_params=pltpu.CompilerParams(
            dimension_semantics=("parallel",), vmem_limit_bytes=VMEM_LIMIT),
        name="branch_mix",
    )(*o_list, *lse_list, proj, proj, proj, proj, proj, gates, gates, x2d,
      head_expand, wa, wc, wo, conv_w)


def _cross_kernel(h_ref, k_ref, v_ref, g2_ref, wq_ref, wo_ref, g3_ref, rw_ref, rb_ref, tri_ref,
                  h2_ref, u3_ref, meta_ref, wts_ref, cnt_ref, run_scr, *, tm):
    @pl.when(pl.program_id(0) == 0)
    def _():
        run_scr[...] = jnp.zeros_like(run_scr)

    h = h_ref[...]
    u = _rms(h, g2_ref[...]).astype(BF16)
    q = jnp.dot(u, wq_ref[...], preferred_element_type=F32) * (MEM_HEAD_DIM ** -0.5)
    heads = []
    for hd in range(MEM_HEADS):
        cols = slice(hd * MEM_HEAD_DIM, (hd + 1) * MEM_HEAD_DIM)
        s = lax.dot_general(q[:, cols].astype(BF16), k_ref[:, cols], (((1,), (1,)), ((), ())),
                            preferred_element_type=F32)
        m = jnp.max(s, axis=1, keepdims=True)
        p = jnp.exp(s - m)
        l = jnp.sum(p, axis=1, keepdims=True)
        heads.append(jnp.dot(p.astype(BF16), v_ref[:, cols], preferred_element_type=F32) * (1.0 / l))
    o = jnp.concatenate(heads, axis=1).astype(BF16)
    h2 = h + jnp.dot(o, wo_ref[...], preferred_element_type=F32)
    h2_ref[...] = h2
    u3 = _rms(h2, g3_ref[...])
    _pack_rows(u3, u3_ref, tm)

    logits = lax.dot_general(rw_ref[...], u3.astype(BF16), (((1,), (1,)), ((), ())),
                             preferred_element_type=F32) + rb_ref[:, 0:1]
    expert = lax.broadcasted_iota(I32, (N_EXPERTS, tm), 0).astype(F32)
    work = logits
    picked = jnp.zeros((N_EXPERTS, tm), F32)
    tops, idxs, sels = [], [], []
    for _ in range(TOP_K):
        mk = jnp.max(work, axis=0, keepdims=True)
        ik = jnp.min(jnp.where(work == mk, expert, float(N_EXPERTS)), axis=0, keepdims=True)
        sel = expert == ik
        work = jnp.where(sel, -jnp.inf, work)
        picked = picked + sel.astype(F32)
        tops.append(mk)
        idxs.append(ik)
        sels.append(sel)
    exps = [jnp.exp(t - tops[0]) for t in tops]
    inv = 1.0 / (exps[0] + exps[1] + exps[2] + exps[3])
    wts = [e * inv for e in exps]

    before = jnp.dot(picked.astype(BF16), tri_ref[...], preferred_element_type=F32) + run_scr[:, 0:1]
    ranks = [jnp.sum(jnp.where(sel, before, 0.0), axis=0, keepdims=True) for sel in sels]

    row = lax.broadcasted_iota(I32, (META_ROWS, tm), 0)
    meta = jnp.zeros((META_ROWS, tm), F32)
    for r, v in enumerate(idxs + ranks + wts):
        meta = jnp.where(row == r, v, meta)
    meta_ref[...] = meta
    row = lax.broadcasted_iota(I32, (LANES, tm), 0)
    wpad = jnp.zeros((LANES, tm), F32)
    for k in range(TOP_K):
        wpad = jnp.where(row == k, wts[k], wpad)
    wts_ref[...] = wpad.T
    run_scr[...] = run_scr[...] + jnp.sum(picked, axis=1, keepdims=True)
    cnt_ref[...] = run_scr[...]


def _cross_router(h1, kv, g2, wq, wo, g3, rw_t, rb, seq, tm=512):
    T, D = h1.shape
    n_mem = kv.shape[1]
    tri = (jnp.arange(tm)[:, None] < jnp.arange(tm)[None, :]).astype(BF16)
    full = lambda a: pl.BlockSpec(a.shape, lambda i: (0,) * a.ndim)
    kvspec = lambda col: pl.BlockSpec((None, n_mem, MEM_WIDTH), lambda i: ((i * tm) // seq, 0, col))
    tile = lambda width: pl.BlockSpec((tm, width), lambda i: (i, 0))
    return pl.pallas_call(
        functools.partial(_cross_kernel, tm=tm),
        grid=(T // tm,),
        in_specs=[tile(D), kvspec(0), kvspec(1), full(g2), full(wq), full(wo), full(g3),
                  full(rw_t), full(rb), full(tri)],
        out_specs=[tile(D), pl.BlockSpec((tm * PACK_ROWS, LANES), lambda i: (i, 0)),
                   pl.BlockSpec((META_ROWS, tm), lambda i: (0, i)), tile(LANES),
                   pl.BlockSpec((N_EXPERTS, LANES), lambda i: (0, 0))],
        out_shape=[jax.ShapeDtypeStruct((T, D), F32), jax.ShapeDtypeStruct((T * PACK_ROWS, LANES), F32),
                   jax.ShapeDtypeStruct((META_ROWS, T), F32), jax.ShapeDtypeStruct((T, LANES), F32),
                   jax.ShapeDtypeStruct((N_EXPERTS, LANES), F32)],
        scratch_shapes=[pltpu.VMEM((N_EXPERTS, LANES), F32)],
        compiler_params=pltpu.CompilerParams(
            dimension_semantics=("arbitrary",), vmem_limit_bytes=VMEM_LIMIT),
        name="cross_router",
    )(h1, kv, kv, g2, wq, wo, g3, rw_t, rb, tri)


def _plan_kernel(pstart_ref, pad_end_ref, fill_base_ref, meta_ref, pos_ref, fill_ref):
    ei = meta_ref[0:2 * TOP_K, :].astype(I32)
    start = jnp.zeros(ei.shape, I32)
    for j in range(N_EXPERTS):
        start = jnp.where(ei == j, pstart_ref[j], start)
    pos_ref[...] = start + pltpu.roll(ei, TOP_K, axis=0)

    rows, cols = fill_ref.shape
    f = lax.broadcasted_iota(I32, (rows, cols), 0) * cols + lax.broadcasted_iota(I32, (rows, cols), 1)
    base = jnp.full((rows, cols), fill_base_ref[0], I32)
    for j in range(1, N_EXPERTS + 1):
        base = jnp.where(f >= pad_end_ref[j - 1], fill_base_ref[j], base)
    fill_ref[...] = base + f


def _plan(meta, pstart, pad_end, fill_base, n_fill):
    T = meta.shape[1]
    rows = 2 * TOP_K
    assert n_fill % (rows * LANES) == 0
    grid_spec = pltpu.PrefetchScalarGridSpec(
        num_scalar_prefetch=3,
        grid=(1,),
        in_specs=[pl.BlockSpec(meta.shape, lambda i, a, b, c: (0, 0))],
        out_specs=[pl.BlockSpec((rows, T), lambda i, a, b, c: (0, 0)),
                   pl.BlockSpec((rows, n_fill // rows), lambda i, a, b, c: (0, 0))],
    )
    pos, fill = pl.pallas_call(
        _plan_kernel,
        grid_spec=grid_spec,
        out_shape=[jax.ShapeDtypeStruct((rows, T), I32), jax.ShapeDtypeStruct((rows, n_fill // rows), I32)],
        compiler_params=pltpu.CompilerParams(dimension_semantics=("arbitrary",)),
        name="row_plan",
    )(pstart, pad_end, fill_base, meta)
    return pos[:TOP_K], fill.reshape(-1)


def _wait_copies(src_hbm, dst, sem, rows):
    pltpu.make_async_copy(src_hbm.at[pl.ds(0, rows), :], dst.at[pl.ds(0, rows), :], sem).wait()


def _dispatch_kernel(pos_ref, u_hbm, x_hbm, stage, load_sem, row_sem, *, tm, n_token_steps):
    i = pl.program_id(0)
    n = pl.num_programs(0)
    batch = tm * TOP_K * PACK_ROWS
    tile_rows = tm * PACK_ROWS

    def tile_copy(step):
        first = pl.multiple_of(jnp.minimum(step, n_token_steps - 1) * tile_rows, tile_rows)
        return pltpu.make_async_copy(u_hbm.at[pl.ds(first, tile_rows), :], stage.at[step % 3],
                                     load_sem.at[step % 3])

    @pl.when(i == 0)
    def _():
        tile_copy(i).start()

    @pl.when(i + 1 < n)
    def _():
        tile_copy(i + 1).start()

    tile_copy(i).wait()
    src_tile = stage.at[i % 3]
    group = 16

    def body(g, carry):
        rows = [pos_ref[0, g * group + j] for j in range(group)]
        t0 = (g * group) & (tm - 1)
        for j, p in enumerate(rows):
            src = src_tile.at[pl.ds(pl.multiple_of((t0 + j) * PACK_ROWS, PACK_ROWS), PACK_ROWS), :]
            dst = x_hbm.at[pl.ds(pl.multiple_of(p * PACK_ROWS, PACK_ROWS), PACK_ROWS), :]
            pltpu.make_async_copy(src, dst, row_sem.at[i % 2]).start(priority=j % 2)
        return carry

    lax.fori_loop(0, tm * TOP_K // group, body, 0)

    @pl.when(i > 0)
    def _():
        _wait_copies(u_hbm, x_hbm, row_sem.at[1 - i % 2], batch)

    @pl.when(i == n - 1)
    def _():
        _wait_copies(u_hbm, x_hbm, row_sem.at[i % 2], batch)


def _dispatch(u_packed, pos_kmajor, fill_rows, tm):
    T = pos_kmajor.shape[0] * tm
    per_step = TOP_K * tm
    dst_rows = jnp.concatenate([pos_kmajor.reshape(-1), fill_rows])
    n_rows = dst_rows.shape[0]
    assert n_rows % per_step == 0
    steps = n_rows // per_step
    return pl.pallas_call(
        functools.partial(_dispatch_kernel, tm=tm, n_token_steps=T // tm),
        grid=(steps,),
        in_specs=[pl.BlockSpec((None, 1, per_step), lambda i: (i, 0, 0), memory_space=pltpu.SMEM),
                  pl.BlockSpec(memory_space=pl.ANY)],
        out_specs=pl.BlockSpec(memory_space=pl.ANY),
        out_shape=jax.ShapeDtypeStruct((n_rows * PACK_ROWS, LANES), F32),
        scratch_shapes=[pltpu.VMEM((3, tm * PACK_ROWS, LANES), F32),
                        pltpu.SemaphoreType.DMA((3,)), pltpu.SemaphoreType.DMA((2,))],
        compiler_params=pltpu.CompilerParams(dimension_semantics=("arbitrary",)),
        name="dispatch_rows",
    )(dst_rows.reshape(steps, 1, per_step), u_packed)


def _expert_kernel(bexp_ref, slot_ref, next_ref, nused_ref, x_ref, wg_hbm, bg_ref, wu_hbm, bu_ref,
                   wd_hbm, bd_ref, y_ref, stage, w_bf, sem):
    weights = (wg_hbm, wu_hbm, wd_hbm)
    block_rows = MOE_ROWS * PACK_ROWS

    def fetch(expert, into):
        return [pltpu.make_async_copy(w.at[expert], stage.at[into, j], sem.at[into])
                for j, w in enumerate(weights)]

    for sub in range(MOE_STEP_BLOCKS):
        i = pl.program_id(0) * MOE_STEP_BLOCKS + sub
        used = i < nused_ref[0]
        e = bexp_ref[i]
        slot = slot_ref[i]
        changed = jnp.logical_or(i == 0, e != bexp_ref[jnp.maximum(i - 1, 0)])
        y_rows = y_ref.at[pl.ds(sub * block_rows, block_rows), :]

        @pl.when(jnp.logical_and(changed, used))
        def _():
            @pl.when(i == 0)
            def _():
                for c in fetch(e, slot):
                    c.start()

            for c in fetch(e, slot):
                c.wait()
            for s in range(2):
                @pl.when(slot == s)
                def _():
                    for j in range(len(weights)):
                        w_bf[j] = stage[s, j].astype(BF16)

            @pl.when(next_ref[i] >= 0)
            def _():
                for c in fetch(next_ref[i], 1 - slot):
                    c.start(priority=1)

        @pl.when(used)
        def _():
            x = _unpack_rows(x_ref, sub * block_rows, MOE_ROWS).astype(BF16)
            gate = jnp.minimum(jnp.dot(x, w_bf[0], preferred_element_type=F32) + bg_ref[e], SWIGLU_LIMIT)
            lin = jnp.clip(jnp.dot(x, w_bf[1], preferred_element_type=F32) + bu_ref[e],
                           -SWIGLU_LIMIT, SWIGLU_LIMIT)
            hdn = gate * jax.nn.sigmoid(SWIGLU_ALPHA * gate) * (lin + 1.0)
            y = jnp.dot(hdn.astype(BF16), w_bf[2], preferred_element_type=F32) + bd_ref[e]
            _pack_rows(y, y_rows, MOE_ROWS)

        @pl.when(jnp.logical_not(used))
        def _():
            y_rows[...] = jnp.zeros(y_rows.shape, y_rows.dtype)


def _experts(x_packed, block_expert, block_slot, block_next, n_used, w_eg, b_eg, w_eu, b_eu, w_ed, b_ed):
    nb = block_expert.shape[0]
    assert nb % MOE_STEP_BLOCKS == 0
    E, D, F = w_eg.shape
    assert D == F
    step_rows = MOE_STEP_BLOCKS * MOE_ROWS * PACK_ROWS
    rows = lambda i, be, sl, nx, nu: (i, 0)
    bspec = lambda n: pl.BlockSpec((E, 1, n), lambda i, be, sl, nx, nu: (0, 0, 0))
    anyspec = pl.BlockSpec(memory_space=pl.ANY)
    grid_spec = pltpu.PrefetchScalarGridSpec(
        num_scalar_prefetch=4,
        grid=(nb // MOE_STEP_BLOCKS,),
        in_specs=[pl.BlockSpec((step_rows, LANES), rows),
                  anyspec, bspec(F), anyspec, bspec(F), anyspec, bspec(D)],
        out_specs=pl.BlockSpec((step_rows, LANES), rows),
        scratch_shapes=[pltpu.VMEM((2, 3, D, F), F32), pltpu.VMEM((3, D, F), BF16),
                        pltpu.SemaphoreType.DMA((2,))],
    )
    return pl.pallas_call(
        _expert_kernel,
        grid_spec=grid_spec,
        out_shape=jax.ShapeDtypeStruct(x_packed.shape, F32),
        compiler_params=pltpu.CompilerParams(
            dimension_semantics=("arbitrary",), vmem_limit_bytes=VMEM_LIMIT),
        name="expert_ffn",
    )(block_expert, block_slot, block_next, n_used, x_packed, w_eg, b_eg.reshape(E, 1, F), w_eu,
      b_eu.reshape(E, 1, F), w_ed, b_ed.reshape(E, 1, D))


def _gather_packed(pos_ref, src_hbm, dst, sem, n):
    group = 16

    def body(g, carry):
        rows = [pos_ref[0, g * group + j] for j in range(group)]
        for j, p in enumerate(rows):
            r = g * group + j
            pltpu.make_async_copy(
                src_hbm.at[pl.ds(pl.multiple_of(p * PACK_ROWS, PACK_ROWS), PACK_ROWS), :],
                dst.at[pl.ds(pl.multiple_of(r * PACK_ROWS, PACK_ROWS), PACK_ROWS), :],
                sem).start(priority=j % 2)
        return carry
    lax.fori_loop(0, n // group, body, 0)


def _combine_kernel(pos0_ref, pos1_ref, y_hbm, h_ref, w_ref, g_ref, o_ref, ybuf, sem):
    i = pl.program_id(0)
    n = pl.num_programs(0)
    slot = i % 2
    rows = TOP_K * COMBINE_ROWS

    @pl.when(i == 0)
    def _():
        _gather_packed(pos0_ref, y_hbm, ybuf.at[0], sem.at[0], rows)

    @pl.when(i + 1 < n)
    def _():
        _gather_packed(pos1_ref, y_hbm, ybuf.at[1 - slot], sem.at[1 - slot], rows)

    _wait_copies(y_hbm, ybuf.at[slot], sem.at[slot], rows * PACK_ROWS)
    acc = h_ref[...]
    w = w_ref[...]
    for k in range(TOP_K):
        acc = acc + w[:, k:k + 1] * _unpack_rows(ybuf.at[slot], k * COMBINE_ROWS * PACK_ROWS, COMBINE_ROWS)
    o_ref[...] = _rms(acc, g_ref[...])


def _combine(y_rows, pos_kmajor, h2, wts, g_final):
    T, D = h2.shape
    tm = COMBINE_ROWS
    nt = T // tm
    posspec = lambda off: pl.BlockSpec(
        (None, 1, TOP_K * tm), lambda i: (jnp.minimum(i + off, nt - 1), 0, 0), memory_space=pltpu.SMEM)
    return pl.pallas_call(
        _combine_kernel,
        grid=(nt,),
        in_specs=[posspec(0), posspec(1), pl.BlockSpec(memory_space=pl.ANY),
                  pl.BlockSpec((tm, D), lambda i: (i, 0)), pl.BlockSpec((tm, LANES), lambda i: (i, 0)),
                  pl.BlockSpec((1, D), lambda i: (0, 0))],
        out_specs=pl.BlockSpec((tm, D), lambda i: (i, 0)),
        out_shape=jax.ShapeDtypeStruct((T, D), F32),
        scratch_shapes=[pltpu.VMEM((2, TOP_K * tm * PACK_ROWS, LANES), F32), pltpu.SemaphoreType.DMA((2,))],
        compiler_params=pltpu.CompilerParams(
            dimension_semantics=("arbitrary",), vmem_limit_bytes=VMEM_LIMIT),
        name="combine_norm",
    )(pos_kmajor, pos_kmajor, y_rows, h2, wts, g_final.reshape(1, D))


def _layer(h, mem, norm_mix, w_in, conv_w, w_branch_attn, w_branch_conv, w_gate, b_gate, w_out,
           norm_cross, norm_mem, w_cq, w_ckv, w_co, norm_moe, router_w, router_b,
           w_eg, b_eg, w_eu, b_eu, w_ed, b_ed, norm_final):
    B, S, D = h.shape
    T = B * S
    W3 = 3 * ATTN_WIDTH
    n_attn = len(ATTN_GROUPS) * W3

    assert ATTN_GROUPS[0][1] == 1
    w_in_b = w_in.astype(BF16)
    w_a = jnp.concatenate([w_in_b[:, n_attn:], w_in_b[:, :W3]], axis=1)
    proj, gates = _inproj(h, norm_mix, [(w_a, None), (w_gate.astype(BF16), b_gate)], 1, 512, 512,
                          name="inproj_conv_g1_gates")
    proj2d = proj.reshape(T, -1)
    gates2d = gates.reshape(T, -1)

    o_list, lse_list = [], []
    for g, (window, dil) in enumerate(ATTN_GROUPS):
        assert window // dil == Q_BLOCK
        if g == 0:
            qkv, col0 = proj, 3 * D // ATTN_WIDTH
        else:
            qkv, = _inproj(h, norm_mix, [(w_in_b[:, g * W3:(g + 1) * W3], None)], dil, 1024, 512,
                           name=f"inproj_g{g + 1}")
            col0 = 0
        o, lse = _attention(qkv, dil, col0, name=f"dilated_attn_g{g + 1}")
        o_list.append(o)
        lse_list.append(lse)

    head_expand = (jnp.arange(LANES)[:, None] == jnp.arange(ATTN_WIDTH)[None, :] // HEAD_DIM).astype(BF16)
    h1 = _mix(o_list, lse_list, proj2d, gates2d, h.reshape(T, D), head_expand,
              w_branch_attn.astype(BF16), w_branch_conv.astype(BF16), w_out.astype(BF16), conv_w, S)

    kv = _mem_kv(mem, norm_mem, w_ckv.astype(BF16))
    rb = jnp.broadcast_to(router_b[:, None], (N_EXPERTS, LANES))
    h2, u3, meta, wts, counts = _cross_router(
        h1, kv, norm_cross.reshape(1, D), w_cq.astype(BF16), w_co.astype(BF16), norm_moe.reshape(1, D),
        router_w.T.astype(BF16), rb, S)

    nb = -(-(T * TOP_K) // MOE_ROWS) + N_EXPERTS
    cnt = counts[:, 0].astype(I32)
    padded = (cnt + MOE_ROWS - 1) // MOE_ROWS * MOE_ROWS
    pend = jnp.cumsum(padded)
    pstart = pend - padded
    experts = jnp.arange(N_EXPERTS, dtype=I32)
    n_pad = padded - cnt
    pad_end = jnp.cumsum(n_pad)
    fill_base = jnp.concatenate([pstart + cnt - (pad_end - n_pad), pend[-1:] - pad_end[-1:]])
    pos_kt, fill_rows = _plan(meta, pstart, pad_end, fill_base, nb * MOE_ROWS - T * TOP_K)

    tm = COMBINE_ROWS
    assert tm & (tm - 1) == 0 and T % tm == 0
    pos_kmajor = pos_kt.reshape(TOP_K, T // tm, tm).transpose(1, 0, 2).reshape(T // tm, 1, TOP_K * tm)
    block_start = jnp.arange(nb, dtype=I32) * MOE_ROWS
    block_expert = jnp.minimum(jnp.sum(block_start[:, None] >= pend[None, :], axis=1), N_EXPERTS - 1).astype(I32)
    active = cnt > 0
    slot_e = (jnp.cumsum(active.astype(I32)) - 1) % 2
    later = jnp.where(active[None, :] & (experts[None, :] > experts[:, None]), experts[None, :], N_EXPERTS)
    next_e = jnp.min(later, axis=1)
    next_e = jnp.where(next_e == N_EXPERTS, -1, next_e).astype(I32)
    x_rows = _dispatch(u3, pos_kmajor, fill_rows, tm)
    block_is = block_expert[:, None] == experts[None, :]
    per_block = lambda table: jnp.sum(jnp.where(block_is, table[None, :], 0), axis=1).astype(I32)
    y_rows = _experts(x_rows, block_expert, per_block(slot_e), per_block(next_e),
                      (pend[-1:] // MOE_ROWS).astype(I32), w_eg, b_eg, w_eu, b_eu, w_ed, b_ed)

    out = _combine(y_rows, pos_kmajor, h2, wts, norm_final)
    return out.reshape(B, S, D)


def kernel(x, mem, norm_mix, w_in, conv_w, w_branch_attn, w_branch_conv, w_gate, b_gate, w_out, norm_cross, norm_mem, w_cq, w_ckv, w_co, norm_moe, router_w, router_b, w_exp_gate, b_exp_gate, w_exp_up, b_exp_up, w_exp_down, b_exp_down, norm_final):
    depth = norm_mix.shape[0]
    assert depth == 1, "the final norm is fused into the last layer's combine step"
    return _layer(x, mem, norm_mix[0], w_in[0], conv_w[0], w_branch_attn[0], w_branch_conv[0], w_gate[0],
                  b_gate[0], w_out[0], norm_cross[0], norm_mem[0], w_cq[0], w_ckv[0], w_co[0], norm_moe[0],
                  router_w[0], router_b[0], w_exp_gate[0], b_exp_gate[0], w_exp_up[0], b_exp_up[0],
                  w_exp_down[0], b_exp_down[0], norm_final)
```

```python
import functools

import jax
import jax.numpy as jnp
import numpy as np
from jax import lax
from jax.experimental import pallas as pl
from jax.experimental.pallas import tpu as pltpu

F32 = jnp.float32
BF16 = jnp.bfloat16
I32 = jnp.int32

D_MODEL = 1024
ATTN_GROUPS = ((128, 1), (512, 4), (2048, 16))
ATTN_HEADS = 8
HEAD_DIM = 64
ATTN_WIDTH = ATTN_HEADS * HEAD_DIM
Q_BLOCK = 128
CONV_K = 3
MEM_HEADS = 4
MEM_HEAD_DIM = 128
MEM_WIDTH = MEM_HEADS * MEM_HEAD_DIM
N_EXPERTS = 32
TOP_K = 4
SWIGLU_LIMIT = 7.0
SWIGLU_ALPHA = 1.702
EPS = 1e-6
NEG_INF = -1e30

LANES = 128
VMEM_LIMIT = 56 * 1024 * 1024

MOE_ROWS = 256
MOE_STEP_BLOCKS = 4
COMBINE_ROWS = 512
CONV_HALO = 16
META_ROWS = 16


def _rms(x, g):
    ms = jnp.mean(x * x, axis=-1, keepdims=True)
    return x * lax.rsqrt(ms + EPS) * g


PACK_ROWS = D_MODEL // LANES


def _pack_rows(y, out_ref, n):
    for c in range(PACK_ROWS):
        out_ref[pl.ds(c, n, stride=PACK_ROWS), :] = y[:, c * LANES:(c + 1) * LANES]


def _unpack_rows(ref, start, n):
    return jnp.concatenate(
        [ref[pl.ds(start + c, n, stride=PACK_ROWS), :] for c in range(PACK_ROWS)], axis=1)


def _inproj_kernel(x_ref, g_ref, *refs, gate_bias, dil, chunk):
    n_w = len(gate_bias)
    w_refs, b_refs, o_refs = refs[:n_w], refs[n_w:2 * n_w], refs[2 * n_w:3 * n_w]
    u_scr = refs[3 * n_w]
    tm = x_ref.shape[0]
    per = tm // dil
    u = _rms(x_ref[...], g_ref[...])
    if dil == 1:
        u_scr[...] = u.astype(BF16)
    else:
        slab = refs[3 * n_w + 1]
        n_slab = u.shape[1] // LANES
        for c in range(n_slab):
            slab[c] = u[:, c * LANES:(c + 1) * LANES]
        for r in range(dil):
            for c in range(n_slab):
                u_scr[r * per:(r + 1) * per, c * LANES:(c + 1) * LANES] = (
                    slab[c, pl.ds(r, per, stride=dil), :].astype(BF16))
    ub = u_scr[...]
    for w_ref, b_ref, o_ref, sig in zip(w_refs, b_refs, o_refs, gate_bias):
        for c in range(w_ref.shape[1] // chunk):
            cols = slice(c * chunk, (c + 1) * chunk)
            acc = jnp.dot(ub, w_ref[:, cols], preferred_element_type=F32)
            if sig:
                acc = jax.nn.sigmoid(acc + b_ref[:, cols])
            for r in range(dil):
                o_ref[r, :, cols] = acc[r * per:(r + 1) * per].astype(o_ref.dtype)


def _inproj(x, gain, weights, dil, tm, chunk, name):
    B, S, D = x.shape
    ws = [w for w, _ in weights]
    bs = [jnp.zeros((1, w.shape[1]), F32) if b is None else b.reshape(1, -1) for w, b in weights]
    const = lambda a: pl.BlockSpec(a.shape, lambda b, i: (0, 0))
    scratch = [pltpu.VMEM((tm, D), BF16)]
    if dil > 1:
        scratch.append(pltpu.VMEM((D // LANES, tm, LANES), F32))
    return pl.pallas_call(
        functools.partial(_inproj_kernel, gate_bias=tuple(b is not None for _, b in weights),
                          dil=dil, chunk=chunk),
        grid=(B, S // tm),
        in_specs=[pl.BlockSpec((None, tm, D), lambda b, i: (b, i, 0)), const(gain.reshape(1, D))]
                 + [const(w) for w in ws] + [const(b) for b in bs],
        out_specs=[pl.BlockSpec((None, dil, tm // dil, w.shape[1]), lambda b, i: (b, 0, i, 0)) for w in ws],
        out_shape=[jax.ShapeDtypeStruct((B, dil, S // dil, w.shape[1]), BF16) for w in ws],
        scratch_shapes=scratch,
        compiler_params=pltpu.CompilerParams(
            dimension_semantics=("parallel", "parallel"), vmem_limit_bytes=VMEM_LIMIT),
        name=name,
    )(x, gain.reshape(1, D), *ws, *bs)


def _mem_kv_kernel(x_ref, g_ref, w_ref, o_ref):
    u = _rms(x_ref[...], g_ref[...]).astype(BF16)
    o_ref[...] = jnp.dot(u, w_ref[...], preferred_element_type=F32).astype(o_ref.dtype)


def _mem_kv(mem, gain, w):
    B, M, D = mem.shape
    N = w.shape[1]
    return pl.pallas_call(
        _mem_kv_kernel,
        grid=(B,),
        in_specs=[pl.BlockSpec((None, M, D), lambda b: (b, 0, 0)),
                  pl.BlockSpec((1, D), lambda b: (0, 0)),
                  pl.BlockSpec((D, N), lambda b: (0, 0))],
        out_specs=pl.BlockSpec((None, M, N), lambda b: (b, 0, 0)),
        out_shape=jax.ShapeDtypeStruct((B, M, N), BF16),
        compiler_params=pltpu.CompilerParams(
            dimension_semantics=("parallel",), vmem_limit_bytes=VMEM_LIMIT),
        name="mem_kv",
    )(mem, gain.reshape(1, D), w)


def _attn_block(q_ref, kp_ref, kc_ref, vp_ref, vc_ref, tab_ref, first, r, j):
    lane = lax.broadcasted_iota(I32, (Q_BLOCK, LANES), 1)
    low = lane < HEAD_DIM
    lse_tile = jnp.zeros((Q_BLOCK, LANES), F32)
    pairs = []
    if j > 0:
        first = 1
    for pair in range(ATTN_HEADS // 2):
        cols = slice(pair * LANES, (pair + 1) * LANES)
        q2 = q_ref[r, j * Q_BLOCK:(j + 1) * Q_BLOCK, cols] * (HEAD_DIM ** -0.5)
        if j == 0:
            k2 = jnp.concatenate([kp_ref[r, :, cols], kc_ref[r, :Q_BLOCK, cols]], axis=0)
            v2 = jnp.concatenate([vp_ref[r, :, cols], vc_ref[r, :Q_BLOCK, cols]], axis=0)
        else:
            k2 = kc_ref[r, (j - 1) * Q_BLOCK:(j + 1) * Q_BLOCK, cols]
            v2 = vc_ref[r, (j - 1) * Q_BLOCK:(j + 1) * Q_BLOCK, cols]
        outs = []
        for half in range(2):
            h = 2 * pair + half
            keep = low if half == 0 else jnp.logical_not(low)
            qm = jnp.where(keep, q2, jnp.zeros_like(q2))
            s = lax.dot_general(qm, k2, (((1,), (1,)), ((), ())), preferred_element_type=F32)
            s = s + tab_ref[first, h]
            m = jnp.max(s, axis=1, keepdims=True)
            p = jnp.exp(s - m)
            l = jnp.sum(p, axis=1, keepdims=True)
            o = jnp.dot(p.astype(BF16), v2, preferred_element_type=F32) * (1.0 / l)
            outs.append(o)
            lse_tile = jnp.where(lane == h, m + jnp.log(l), lse_tile)
        pairs.append(jnp.where(low, outs[0], outs[1]))
    return pairs, lse_tile


def _attn_kernel(q_ref, kp_ref, kc_ref, vp_ref, vc_ref, tab_ref, o_ref, lse_ref, *scratch, dil, qb):
    first = jnp.minimum(pl.program_id(1), 1)
    blocks = (q_ref, kp_ref, kc_ref, vp_ref, vc_ref, tab_ref)
    n_pair = ATTN_HEADS // 2
    if dil == 1:
        for j in range(qb):
            pairs, lse_tile = _attn_block(*blocks, first, 0, j)
            rows = slice(j * Q_BLOCK, (j + 1) * Q_BLOCK)
            for p in range(n_pair):
                o_ref[rows, p * LANES:(p + 1) * LANES] = pairs[p].astype(o_ref.dtype)
            lse_ref[rows, :] = lse_tile
        return

    o_scr, lse_scr = scratch

    unroll = min(dil, 8)

    def body(g, carry):
        for r in [g * unroll + t for t in range(unroll)]:
            for j in range(qb):
                pairs, lse_tile = _attn_block(*blocks, first, r, j)
                rows = pl.ds(j * Q_BLOCK * dil + r, Q_BLOCK, stride=dil)
                for p in range(n_pair):
                    o_scr[p, rows, :] = pairs[p]
                lse_scr[rows, :] = lse_tile
        return carry

    if dil == unroll:
        body(0, 0)
    else:
        lax.fori_loop(0, dil // unroll, body, 0)
    for p in range(n_pair):
        o_ref[:, p * LANES:(p + 1) * LANES] = o_scr[p].astype(o_ref.dtype)
    lse_ref[...] = lse_scr[...]


def _attn_bias_table(dil):
    slopes = np.power(2.0, -8.0 * np.arange(1, ATTN_HEADS + 1, dtype=np.float32) / ATTN_HEADS)
    iq = np.arange(Q_BLOCK)
    ik = np.arange(2 * Q_BLOCK)
    dist = iq[:, None] + Q_BLOCK - ik[None, :]
    band = (dist >= 0) & (dist <= Q_BLOCK)
    has_prev = np.stack([ik >= Q_BLOCK, np.ones_like(ik, dtype=bool)])
    mask = band[None] & has_prev[:, None, :]
    bias = -slopes[:, None, None] * (dist * dil).astype(np.float32)[None]
    return np.where(mask[:, None], bias[None], np.float32(NEG_INF)).astype(np.float32)


def _attention(qkv, dil, col0, name):
    B, _, L, _ = qkv.shape
    W = ATTN_WIDTH
    qb = max(1, 4 // dil)
    nb = L // (Q_BLOCK * qb)
    span = Q_BLOCK * qb * dil
    blk = lambda part, prev: (
        pl.BlockSpec((None, dil, Q_BLOCK, W), lambda b, n: (b, 0, jnp.maximum(n * qb - 1, 0), col0 + part))
        if prev else pl.BlockSpec((None, dil, Q_BLOCK * qb, W), lambda b, n: (b, 0, n, col0 + part)))
    tab = _attn_bias_table(dil)
    scratch = [] if dil == 1 else [pltpu.VMEM((ATTN_HEADS // 2, span, LANES), F32),
                                   pltpu.VMEM((span, LANES), F32)]
    o, lse = pl.pallas_call(
        functools.partial(_attn_kernel, dil=dil, qb=qb),
        grid=(B, nb),
        in_specs=[blk(0, False), blk(1, True), blk(1, False), blk(2, True), blk(2, False),
                  pl.BlockSpec(tab.shape, lambda b, n: (0, 0, 0, 0))],
        out_specs=[pl.BlockSpec((None, span, W), lambda b, n: (b, n, 0)),
                   pl.BlockSpec((None, span, LANES), lambda b, n: (b, n, 0))],
        out_shape=[jax.ShapeDtypeStruct((B, L * dil, W), BF16),
                   jax.ShapeDtypeStruct((B, L * dil, LANES), F32)],
        scratch_shapes=scratch,
        compiler_params=pltpu.CompilerParams(
            dimension_semantics=("parallel", "parallel"), vmem_limit_bytes=VMEM_LIMIT),
        name=name,
    )(qkv, qkv, qkv, qkv, qkv, tab)
    return o.reshape(B * L * dil, W), lse.reshape(B * L * dil, LANES)


def _mix_kernel(o1_ref, o2_ref, o3_ref, l1_ref, l2_ref, l3_ref, gb_ref, gc_ref, xc_ref, gch_ref, xch_ref,
                ga_ref, gv_ref, x_ref, e_ref, wa_ref, wc_ref, wo_ref, cw_ref, h_ref, *, tm, seq):
    a1, a2, a3 = l1_ref[...], l2_ref[...], l3_ref[...]
    m = jnp.maximum(jnp.maximum(a1, a2), a3)
    e1, e2, e3 = jnp.exp(a1 - m), jnp.exp(a2 - m), jnp.exp(a3 - m)
    inv = 1.0 / (e1 + e2 + e3)
    y = jnp.zeros((tm, ATTN_WIDTH), F32)
    for e, o_ref in ((e1, o1_ref), (e2, o2_ref), (e3, o3_ref)):
        w = e * inv
        w_hi = w.astype(BF16)
        w_lo = (w - w_hi.astype(F32)).astype(BF16)
        w_heads = (jnp.dot(w_hi, e_ref[...], preferred_element_type=F32)
                   + jnp.dot(w_lo, e_ref[...], preferred_element_type=F32))
        y = y + w_heads * o_ref[...].astype(F32)
    branch_a = jnp.dot(y.astype(BF16), wa_ref[...], preferred_element_type=F32)

    at_start = (pl.program_id(0) * tm) % seq == 0
    halo = gch_ref[...].astype(F32) * xch_ref[...].astype(F32)
    halo = jnp.where(at_start, jnp.zeros_like(halo), halo)[CONV_HALO - 8:, :]
    z = gc_ref[...].astype(F32) * xc_ref[...].astype(F32)
    row = lax.broadcasted_iota(I32, (8, z.shape[1]), 0)
    taps = []
    for shift in (2, 1):
        zs = pltpu.roll(z, shift, axis=0)
        head = jnp.where(row < shift, pltpu.roll(halo, shift, axis=0), zs[0:8, :])
        taps.append(jnp.concatenate([head, zs[8:, :]], axis=0))
    conv = cw_ref[0:1, :] * taps[0] + cw_ref[1:2, :] * taps[1] + cw_ref[2:3, :] * z
    y_conv = gb_ref[...].astype(F32) * conv
    branch_c = jnp.dot(y_conv.astype(BF16), wc_ref[...], preferred_element_type=F32)

    mixed = ga_ref[...].astype(F32) * branch_a + gv_ref[...].astype(F32) * branch_c
    h_ref[...] = x_ref[...] + jnp.dot(mixed.astype(BF16), wo_ref[...], preferred_element_type=F32)


def _mix(o_list, lse_list, proj, gates, x2d, head_expand, wa, wc, wo, conv_w, seq, tm=512):
    T, D = x2d.shape
    W = ATTN_WIDTH
    row = lambda width, col: pl.BlockSpec((tm, width), lambda i: (i, col))
    halo = lambda col: pl.BlockSpec(
        (CONV_HALO, D), lambda i: (jnp.maximum(i * (tm // CONV_HALO) - 1, 0), col))
    full = lambda a: pl.BlockSpec(a.shape, lambda i: (0,) * a.ndim)
    return pl.pallas_call(
        functools.partial(_mix_kernel, tm=tm, seq=seq),
        grid=(T // tm,),
        in_specs=[row(W, 0)] * 3 + [row(LANES, 0)] * 3
                 + [row(D, 0), row(D, 1), row(D, 2), halo(1), halo(2)]
                 + [row(D, 0), row(D, 1), row(D, 0)]
                 + [full(head_expand), full(wa), full(wc), full(wo), full(conv_w)],
        out_specs=pl.BlockSpec((tm, D), lambda i: (i, 0)),
        out_shape=jax.ShapeDtypeStruct((T, D), F32),
        compiler_params=pltpu.CompilerParams(
            dimension_semantics=("parallel",), vmem_limit_bytes=VMEM_LIMIT),
        name="branch_mix",
    )(*o_list, *lse_list, proj, proj, proj, proj, proj, gates, gates, x2d,
      head_expand, wa, wc, wo, conv_w)


def _cross_kernel(h_ref, k_ref, v_ref, g2_ref, wq_ref, wo_ref, g3_ref, rw_ref, rb_ref, tri_ref,
                  h2_ref, u3_ref, meta_ref, wts_ref, cnt_ref, run_scr, *, tm):
    @pl.when(pl.program_id(0) == 0)
    def _():
        run_scr[...] = jnp.zeros_like(run_scr)

    h = h_ref[...]
    u = _rms(h, g2_ref[...]).astype(BF16)
    q = jnp.dot(u, wq_ref[...], preferred_element_type=F32) * (MEM_HEAD_DIM ** -0.5)
    heads = []
    for hd in range(MEM_HEADS):
        cols = slice(hd * MEM_HEAD_DIM, (hd + 1) * MEM_HEAD_DIM)
        s = lax.dot_general(q[:, cols].astype(BF16), k_ref[:, cols], (((1,), (1,)), ((), ())),
                            preferred_element_type=F32)
        m = jnp.max(s, axis=1, keepdims=True)
        p = jnp.exp(s - m)
        l = jnp.sum(p, axis=1, keepdims=True)
        heads.append(jnp.dot(p.astype(BF16), v_ref[:, cols], preferred_element_type=F32) * (1.0 / l))
    o = jnp.concatenate(heads, axis=1).astype(BF16)
    h2 = h + jnp.dot(o, wo_ref[...], preferred_element_type=F32)
    h2_ref[...] = h2
    u3 = _rms(h2, g3_ref[...])
    _pack_rows(u3, u3_ref, tm)

    logits = lax.dot_general(rw_ref[...], u3.astype(BF16), (((1,), (1,)), ((), ())),
                             preferred_element_type=F32) + rb_ref[:, 0:1]
    expert = lax.broadcasted_iota(I32, (N_EXPERTS, tm), 0).astype(F32)
    work = logits
    picked = jnp.zeros((N_EXPERTS, tm), F32)
    tops, idxs, sels = [], [], []
    for _ in range(TOP_K):
        mk = jnp.max(work, axis=0, keepdims=True)
        ik = jnp.min(jnp.where(work == mk, expert, float(N_EXPERTS)), axis=0, keepdims=True)
        sel = expert == ik
        work = jnp.where(sel, -jnp.inf, work)
        picked = picked + sel.astype(F32)
        tops.append(mk)
        idxs.append(ik)
        sels.append(sel)
    exps = [jnp.exp(t - tops[0]) for t in tops]
    inv = 1.0 / (exps[0] + exps[1] + exps[2] + exps[3])
    wts = [e * inv for e in exps]

    before = jnp.dot(picked.astype(BF16), tri_ref[...], preferred_element_type=F32) + run_scr[:, 0:1]
    ranks = [jnp.sum(jnp.where(sel, before, 0.0), axis=0, keepdims=True) for sel in sels]

    row = lax.broadcasted_iota(I32, (META_ROWS, tm), 0)
    meta = jnp.zeros((META_ROWS, tm), F32)
    for r, v in enumerate(idxs + ranks + wts):
        meta = jnp.where(row == r, v, meta)
    meta_ref[...] = meta
    row = lax.broadcasted_iota(I32, (LANES, tm), 0)
    wpad = jnp.zeros((LANES, tm), F32)
    for k in range(TOP_K):
        wpad = jnp.where(row == k, wts[k], wpad)
    wts_ref[...] = wpad.T
    run_scr[...] = run_scr[...] + jnp.sum(picked, axis=1, keepdims=True)
    cnt_ref[...] = run_scr[...]


def _cross_router(h1, kv, g2, wq, wo, g3, rw_t, rb, seq, tm=512):
    T, D = h1.shape
    n_mem = kv.shape[1]
    tri = (jnp.arange(tm)[:, None] < jnp.arange(tm)[None, :]).astype(BF16)
    full = lambda a: pl.BlockSpec(a.shape, lambda i: (0,) * a.ndim)
    kvspec = lambda col: pl.BlockSpec((None, n_mem, MEM_WIDTH), lambda i: ((i * tm) // seq, 0, col))
    tile = lambda width: pl.BlockSpec((tm, width), lambda i: (i, 0))
    return pl.pallas_call(
        functools.partial(_cross_kernel, tm=tm),
        grid=(T // tm,),
        in_specs=[tile(D), kvspec(0), kvspec(1), full(g2), full(wq), full(wo), full(g3),
                  full(rw_t), full(rb), full(tri)],
        out_specs=[tile(D), pl.BlockSpec((tm * PACK_ROWS, LANES), lambda i: (i, 0)),
                   pl.BlockSpec((META_ROWS, tm), lambda i: (0, i)), tile(LANES),
                   pl.BlockSpec((N_EXPERTS, LANES), lambda i: (0, 0))],
        out_shape=[jax.ShapeDtypeStruct((T, D), F32), jax.ShapeDtypeStruct((T * PACK_ROWS, LANES), F32),
                   jax.ShapeDtypeStruct((META_ROWS, T), F32), jax.ShapeDtypeStruct((T, LANES), F32),
                   jax.ShapeDtypeStruct((N_EXPERTS, LANES), F32)],
        scratch_shapes=[pltpu.VMEM((N_EXPERTS, LANES), F32)],
        compiler_params=pltpu.CompilerParams(
            dimension_semantics=("arbitrary",), vmem_limit_bytes=VMEM_LIMIT),
        name="cross_router",
    )(h1, kv, kv, g2, wq, wo, g3, rw_t, rb, tri)


def _plan_kernel(pstart_ref, pad_end_ref, fill_base_ref, meta_ref, pos_ref, fill_ref):
    ei = meta_ref[0:2 * TOP_K, :].astype(I32)
    start = jnp.zeros(ei.shape, I32)
    for j in range(N_EXPERTS):
        start = jnp.where(ei == j, pstart_ref[j], start)
    pos_ref[...] = start + pltpu.roll(ei, TOP_K, axis=0)

    rows, cols = fill_ref.shape
    f = lax.broadcasted_iota(I32, (rows, cols), 0) * cols + lax.broadcasted_iota(I32, (rows, cols), 1)
    base = jnp.full((rows, cols), fill_base_ref[0], I32)
    for j in range(1, N_EXPERTS + 1):
        base = jnp.where(f >= pad_end_ref[j - 1], fill_base_ref[j], base)
    fill_ref[...] = base + f


def _plan(meta, pstart, pad_end, fill_base, n_fill):
    T = meta.shape[1]
    rows = 2 * TOP_K
    assert n_fill % (rows * LANES) == 0
    grid_spec = pltpu.PrefetchScalarGridSpec(
        num_scalar_prefetch=3,
        grid=(1,),
        in_specs=[pl.BlockSpec(meta.shape, lambda i, a, b, c: (0, 0))],
        out_specs=[pl.BlockSpec((rows, T), lambda i, a, b, c: (0, 0)),
                   pl.BlockSpec((rows, n_fill // rows), lambda i, a, b, c: (0, 0))],
    )
    pos, fill = pl.pallas_call(
        _plan_kernel,
        grid_spec=grid_spec,
        out_shape=[jax.ShapeDtypeStruct((rows, T), I32), jax.ShapeDtypeStruct((rows, n_fill // rows), I32)],
        compiler_params=pltpu.CompilerParams(dimension_semantics=("arbitrary",)),
        name="row_plan",
    )(pstart, pad_end, fill_base, meta)
    return pos[:TOP_K], fill.reshape(-1)


def _wait_copies(src_hbm, dst, sem, rows):
    pltpu.make_async_copy(src_hbm.at[pl.ds(0, rows), :], dst.at[pl.ds(0, rows), :], sem).wait()


def _dispatch_kernel(pos_ref, u_hbm, x_hbm, stage, load_sem, row_sem, *, tm, n_token_steps):
    i = pl.program_id(0)
    n = pl.num_programs(0)
    batch = tm * TOP_K * PACK_ROWS
    tile_rows = tm * PACK_ROWS

    def tile_copy(step):
        first = pl.multiple_of(jnp.minimum(step, n_token_steps - 1) * tile_rows, tile_rows)
        return pltpu.make_async_copy(u_hbm.at[pl.ds(first, tile_rows), :], stage.at[step % 3],
                                     load_sem.at[step % 3])

    @pl.when(i == 0)
    def _():
        tile_copy(i).start()

    @pl.when(i + 1 < n)
    def _():
        tile_copy(i + 1).start()

    tile_copy(i).wait()
    src_tile = stage.at[i % 3]
    group = 16

    for k in range(TOP_K):
        def body(g, carry, k=k):
            rows = [pos_ref[0, k * tm + g * group + j] for j in range(group)]
            for j, p in enumerate(rows):
                src = src_tile.at[pl.ds(pl.multiple_of((g * group + j) * PACK_ROWS, PACK_ROWS), PACK_ROWS), :]
                dst = x_hbm.at[pl.ds(pl.multiple_of(p * PACK_ROWS, PACK_ROWS), PACK_ROWS), :]
                pltpu.make_async_copy(src, dst, row_sem.at[i % 2]).start(priority=j % 2)
            return carry

        lax.fori_loop(0, tm // group, body, 0)

    @pl.when(i > 0)
    def _():
        _wait_copies(u_hbm, x_hbm, row_sem.at[1 - i % 2], batch)

    @pl.when(i == n - 1)
    def _():
        _wait_copies(u_hbm, x_hbm, row_sem.at[i % 2], batch)


def _dispatch(u_packed, pos_kmajor, fill_rows, tm):
    T = pos_kmajor.shape[0] * tm
    per_step = TOP_K * tm
    dst_rows = jnp.concatenate([pos_kmajor.reshape(-1), fill_rows])
    n_rows = dst_rows.shape[0]
    assert n_rows % per_step == 0
    steps = n_rows // per_step
    return pl.pallas_call(
        functools.partial(_dispatch_kernel, tm=tm, n_token_steps=T // tm),
        grid=(steps,),
        in_specs=[pl.BlockSpec((None, 1, per_step), lambda i: (i, 0, 0), memory_space=pltpu.SMEM),
                  pl.BlockSpec(memory_space=pl.ANY)],
        out_specs=pl.BlockSpec(memory_space=pl.ANY),
        out_shape=jax.ShapeDtypeStruct((n_rows * PACK_ROWS, LANES), F32),
        scratch_shapes=[pltpu.VMEM((3, tm * PACK_ROWS, LANES), F32),
                        pltpu.SemaphoreType.DMA((3,)), pltpu.SemaphoreType.DMA((2,))],
        compiler_params=pltpu.CompilerParams(dimension_semantics=("arbitrary",)),
        name="dispatch_rows",
    )(dst_rows.reshape(steps, 1, per_step), u_packed)


def _expert_kernel(bexp_ref, slot_ref, next_ref, nused_ref, x_ref, wg_hbm, bg_ref, wu_hbm, bu_ref,
                   wd_hbm, bd_ref, y_ref, stage, w_bf, sem):
    weights = (wg_hbm, wu_hbm, wd_hbm)
    block_rows = MOE_ROWS * PACK_ROWS

    def fetch(expert, into):
        return [pltpu.make_async_copy(w.at[expert], stage.at[into, j], sem.at[into])
                for j, w in enumerate(weights)]

    for sub in range(MOE_STEP_BLOCKS):
        i = pl.program_id(0) * MOE_STEP_BLOCKS + sub
        used = i < nused_ref[0]
        e = bexp_ref[i]
        slot = slot_ref[i]
        changed = jnp.logical_or(i == 0, e != bexp_ref[jnp.maximum(i - 1, 0)])
        y_rows = y_ref.at[pl.ds(sub * block_rows, block_rows), :]

        @pl.when(jnp.logical_and(changed, used))
        def _():
            @pl.when(i == 0)
            def _():
                for c in fetch(e, slot):
                    c.start()

            for c in fetch(e, slot):
                c.wait()
            for s in range(2):
                @pl.when(slot == s)
                def _():
                    for j in range(len(weights)):
                        w_bf[j] = stage[s, j].astype(BF16)

            @pl.when(next_ref[i] >= 0)
            def _():
                for c in fetch(next_ref[i], 1 - slot):
                    c.start(priority=1)

        @pl.when(used)
        def _():
            x = _unpack_rows(x_ref, sub * block_rows, MOE_ROWS).astype(BF16)
            gate = jnp.minimum(jnp.dot(x, w_bf[0], preferred_element_type=F32) + bg_ref[e], SWIGLU_LIMIT)
            lin = jnp.clip(jnp.dot(x, w_bf[1], preferred_element_type=F32) + bu_ref[e],
                           -SWIGLU_LIMIT, SWIGLU_LIMIT)
            hdn = gate * jax.nn.sigmoid(SWIGLU_ALPHA * gate) * (lin + 1.0)
            y = jnp.dot(hdn.astype(BF16), w_bf[2], preferred_element_type=F32) + bd_ref[e]
            _pack_rows(y, y_rows, MOE_ROWS)

        @pl.when(jnp.logical_not(used))
        def _():
            y_rows[...] = jnp.zeros(y_rows.shape, y_rows.dtype)


def _experts(x_packed, block_expert, block_slot, block_next, n_used, w_eg, b_eg, w_eu, b_eu, w_ed, b_ed):
    nb = block_expert.shape[0]
    assert nb % MOE_STEP_BLOCKS == 0
    E, D, F = w_eg.shape
    assert D == F
    step_rows = MOE_STEP_BLOCKS * MOE_ROWS * PACK_ROWS
    rows = lambda i, be, sl, nx, nu: (i, 0)
    bspec = lambda n: pl.BlockSpec((E, 1, n), lambda i, be, sl, nx, nu: (0, 0, 0))
    anyspec = pl.BlockSpec(memory_space=pl.ANY)
    grid_spec = pltpu.PrefetchScalarGridSpec(
        num_scalar_prefetch=4,
        grid=(nb // MOE_STEP_BLOCKS,),
        in_specs=[pl.BlockSpec((step_rows, LANES), rows),
                  anyspec, bspec(F), anyspec, bspec(F), anyspec, bspec(D)],
        out_specs=pl.BlockSpec((step_rows, LANES), rows),
        scratch_shapes=[pltpu.VMEM((2, 3, D, F), F32), pltpu.VMEM((3, D, F), BF16),
                        pltpu.SemaphoreType.DMA((2,))],
    )
    return pl.pallas_call(
        _expert_kernel,
        grid_spec=grid_spec,
        out_shape=jax.ShapeDtypeStruct(x_packed.shape, F32),
        compiler_params=pltpu.CompilerParams(
            dimension_semantics=("arbitrary",), vmem_limit_bytes=VMEM_LIMIT),
        name="expert_ffn",
    )(block_expert, block_slot, block_next, n_used, x_packed, w_eg, b_eg.reshape(E, 1, F), w_eu,
      b_eu.reshape(E, 1, F), w_ed, b_ed.reshape(E, 1, D))


def _gather_packed(pos_ref, src_hbm, dst, sem, n):
    group = 16

    def body(g, carry):
        rows = [pos_ref[0, g * group + j] for j in range(group)]
        for j, p in enumerate(rows):
            r = g * group + j
            pltpu.make_async_copy(
                src_hbm.at[pl.ds(pl.multiple_of(p * PACK_ROWS, PACK_ROWS), PACK_ROWS), :],
                dst.at[pl.ds(pl.multiple_of(r * PACK_ROWS, PACK_ROWS), PACK_ROWS), :],
                sem).start(priority=j % 2)
        return carry
    lax.fori_loop(0, n // group, body, 0)


def _combine_kernel(pos0_ref, pos1_ref, y_hbm, h_ref, w_ref, g_ref, o_ref, ybuf, sem):
    i = pl.program_id(0)
    n = pl.num_programs(0)
    slot = i % 2
    rows = TOP_K * COMBINE_ROWS

    @pl.when(i == 0)
    def _():
        _gather_packed(pos0_ref, y_hbm, ybuf.at[0], sem.at[0], rows)

    @pl.when(i + 1 < n)
    def _():
        _gather_packed(pos1_ref, y_hbm, ybuf.at[1 - slot], sem.at[1 - slot], rows)

    _wait_copies(y_hbm, ybuf.at[slot], sem.at[slot], rows * PACK_ROWS)
    acc = h_ref[...]
    w = w_ref[...]
    for k in range(TOP_K):
        acc = acc + w[:, k:k + 1] * _unpack_rows(ybuf.at[slot], k * COMBINE_ROWS * PACK_ROWS, COMBINE_ROWS)
    o_ref[...] = _rms(acc, g_ref[...])


def _combine(y_rows, pos_kmajor, h2, wts, g_final):
    T, D = h2.shape
    tm = COMBINE_ROWS
    nt = T // tm
    posspec = lambda off: pl.BlockSpec(
        (None, 1, TOP_K * tm), lambda i: (jnp.minimum(i + off, nt - 1), 0, 0), memory_space=pltpu.SMEM)
    return pl.pallas_call(
        _combine_kernel,
        grid=(nt,),
        in_specs=[posspec(0), posspec(1), pl.BlockSpec(memory_space=pl.ANY),
                  pl.BlockSpec((tm, D), lambda i: (i, 0)), pl.BlockSpec((tm, LANES), lambda i: (i, 0)),
                  pl.BlockSpec((1, D), lambda i: (0, 0))],
        out_specs=pl.BlockSpec((tm, D), lambda i: (i, 0)),
        out_shape=jax.ShapeDtypeStruct((T, D), F32),
        scratch_shapes=[pltpu.VMEM((2, TOP_K * tm * PACK_ROWS, LANES), F32), pltpu.SemaphoreType.DMA((2,))],
        compiler_params=pltpu.CompilerParams(
            dimension_semantics=("arbitrary",), vmem_limit_bytes=VMEM_LIMIT),
        name="combine_norm",
    )(pos_kmajor, pos_kmajor, y_rows, h2, wts, g_final.reshape(1, D))


def _layer(h, mem, norm_mix, w_in, conv_w, w_branch_attn, w_branch_conv, w_gate, b_gate, w_out,
           norm_cross, norm_mem, w_cq, w_ckv, w_co, norm_moe, router_w, router_b,
           w_eg, b_eg, w_eu, b_eu, w_ed, b_ed, norm_final):
    B, S, D = h.shape
    T = B * S
    W3 = 3 * ATTN_WIDTH
    n_attn = len(ATTN_GROUPS) * W3

    assert ATTN_GROUPS[0][1] == 1
    w_in_b = w_in.astype(BF16)
    w_a = jnp.concatenate([w_in_b[:, n_attn:], w_in_b[:, :W3]], axis=1)
    proj, gates = _inproj(h, norm_mix, [(w_a, None), (w_gate.astype(BF16), b_gate)], 1, 512, 512,
                          name="inproj_conv_g1_gates")
    proj2d = proj.reshape(T, -1)
    gates2d = gates.reshape(T, -1)

    o_list, lse_list = [], []
    for g, (window, dil) in enumerate(ATTN_GROUPS):
        assert window // dil == Q_BLOCK
        if g == 0:
            qkv, col0 = proj, 3 * D // ATTN_WIDTH
        else:
            qkv, = _inproj(h, norm_mix, [(w_in_b[:, g * W3:(g + 1) * W3], None)], dil, 1024, 512,
                           name=f"inproj_g{g + 1}")
            col0 = 0
        o, lse = _attention(qkv, dil, col0, name=f"dilated_attn_g{g + 1}")
        o_list.append(o)
        lse_list.append(lse)

    head_expand = (jnp.arange(LANES)[:, None] == jnp.arange(ATTN_WIDTH)[None, :] // HEAD_DIM).astype(BF16)
    h1 = _mix(o_list, lse_list, proj2d, gates2d, h.reshape(T, D), head_expand,
              w_branch_attn.astype(BF16), w_branch_conv.astype(BF16), w_out.astype(BF16), conv_w, S)

    kv = _mem_kv(mem, norm_mem, w_ckv.astype(BF16))
    rb = jnp.broadcast_to(router_b[:, None], (N_EXPERTS, LANES))
    h2, u3, meta, wts, counts = _cross_router(
        h1, kv, norm_cross.reshape(1, D), w_cq.astype(BF16), w_co.astype(BF16), norm_moe.reshape(1, D),
        router_w.T.astype(BF16), rb, S)

    nb = -(-(T * TOP_K) // MOE_ROWS) + N_EXPERTS
    cnt = counts[:, 0].astype(I32)
    padded = (cnt + MOE_ROWS - 1) // MOE_ROWS * MOE_ROWS
    pend = jnp.cumsum(padded)
    pstart = pend - padded
    experts = jnp.arange(N_EXPERTS, dtype=I32)
    n_pad = padded - cnt
    pad_end = jnp.cumsum(n_pad)
    fill_base = jnp.concatenate([pstart + cnt - (pad_end - n_pad), pend[-1:] - pad_end[-1:]])
    pos_kt, fill_rows = _plan(meta, pstart, pad_end, fill_base, nb * MOE_ROWS - T * TOP_K)

    tm = COMBINE_ROWS
    assert tm & (tm - 1) == 0 and T % tm == 0
    pos_kmajor = pos_kt.reshape(TOP_K, T // tm, tm).transpose(1, 0, 2).reshape(T // tm, 1, TOP_K * tm)
    block_start = jnp.arange(nb, dtype=I32) * MOE_ROWS
    block_expert = jnp.minimum(jnp.sum(block_start[:, None] >= pend[None, :], axis=1), N_EXPERTS - 1).astype(I32)
    active = cnt > 0
    slot_e = (jnp.cumsum(active.astype(I32)) - 1) % 2
    later = jnp.where(active[None, :] & (experts[None, :] > experts[:, None]), experts[None, :], N_EXPERTS)
    next_e = jnp.min(later, axis=1)
    next_e = jnp.where(next_e == N_EXPERTS, -1, next_e).astype(I32)
    x_rows = _dispatch(u3, pos_kmajor, fill_rows, tm)
    block_is = block_expert[:, None] == experts[None, :]
    per_block = lambda table: jnp.sum(jnp.where(block_is, table[None, :], 0), axis=1).astype(I32)
    y_rows = _experts(x_rows, block_expert, per_block(slot_e), per_block(next_e),
                      (pend[-1:] // MOE_ROWS).astype(I32), w_eg, b_eg, w_eu, b_eu, w_ed, b_ed)

    out = _combine(y_rows, pos_kmajor, h2, wts, norm_final)
    return out.reshape(B, S, D)


def kernel(x, mem, norm_mix, w_in, conv_w, w_branch_attn, w_branch_conv, w_gate, b_gate, w_out, norm_cross, norm_mem, w_cq, w_ckv, w_co, norm_moe, router_w, router_b, w_exp_gate, b_exp_gate, w_exp_up, b_exp_up, w_exp_down, b_exp_down, norm_final):
    depth = norm_mix.shape[0]
    assert depth == 1, "the final norm is fused into the last layer's combine step"
    return _layer(x, mem, norm_mix[0], w_in[0], conv_w[0], w_branch_attn[0], w_branch_conv[0], w_gate[0],
                  b_gate[0], w_out[0], norm_cross[0], norm_mem[0], w_cq[0], w_ckv[0], w_co[0], norm_moe[0],
                  router_w[0], router_b[0], w_exp_gate[0], b_exp_gate[0], w_exp_up[0], b_exp_up[0],
                  w_exp_down[0], b_exp_down[0], norm_final)
```

```python
import functools

import jax
import jax.numpy as jnp
import numpy as np
from jax import lax
from jax.experimental import pallas as pl
from jax.experimental.pallas import tpu as pltpu

F32 = jnp.float32
BF16 = jnp.bfloat16
I32 = jnp.int32

D_MODEL = 1024
ATTN_GROUPS = ((128, 1), (512, 4), (2048, 16))
ATTN_HEADS = 8
HEAD_DIM = 64
ATTN_WIDTH = ATTN_HEADS * HEAD_DIM
Q_BLOCK = 128
CONV_K = 3
MEM_HEADS = 4
MEM_HEAD_DIM = 128
MEM_WIDTH = MEM_HEADS * MEM_HEAD_DIM
N_EXPERTS = 32
TOP_K = 4
SWIGLU_LIMIT = 7.0
SWIGLU_ALPHA = 1.702
EPS = 1e-6
NEG_INF = -1e30

LANES = 128
VMEM_LIMIT = 56 * 1024 * 1024

MOE_ROWS = 256
MOE_STEP_BLOCKS = 4
COMBINE_ROWS = 512
CONV_HALO = 16
META_ROWS = 16
REGROUP_STRIDE = 4


def _rms(x, g):
    ms = jnp.mean(x * x, axis=-1, keepdims=True)
    return x * lax.rsqrt(ms + EPS) * g


PACK_ROWS = D_MODEL // LANES


def _pack_rows(y, out_ref, n):
    for c in range(PACK_ROWS):
        out_ref[pl.ds(c, n, stride=PACK_ROWS), :] = y[:, c * LANES:(c + 1) * LANES]


def _unpack_rows(ref, start, n):
    return jnp.concatenate(
        [ref[pl.ds(start + c, n, stride=PACK_ROWS), :] for c in range(PACK_ROWS)], axis=1)


def _inproj_kernel(x_ref, g_ref, *refs, gate_bias, dil, chunk):
    n_w = len(gate_bias)
    w_refs, b_refs, o_refs = refs[:n_w], refs[n_w:2 * n_w], refs[2 * n_w:3 * n_w]
    u_scr = refs[3 * n_w]
    tm = x_ref.shape[0]
    per = tm // dil
    u = _rms(x_ref[...], g_ref[...])
    if dil == 1:
        u_scr[...] = u.astype(BF16)
    else:
        slab = refs[3 * n_w + 1]
        n_slab = u.shape[1] // LANES
        for c in range(n_slab):
            slab[c] = u[:, c * LANES:(c + 1) * LANES]
        if dil <= REGROUP_STRIDE:
            for r in range(dil):
                for c in range(n_slab):
                    u_scr[r * per:(r + 1) * per, c * LANES:(c + 1) * LANES] = (
                        slab[c, pl.ds(r, per, stride=dil), :].astype(BF16))
        else:
            slab2 = refs[3 * n_w + 2]
            d1, d2 = REGROUP_STRIDE, dil // REGROUP_STRIDE
            group = tm // d1
            for r1 in range(d1):
                for c in range(n_slab):
                    slab2[c, r1 * group:(r1 + 1) * group, :] = slab[c, pl.ds(r1, group, stride=d1), :]
            for r1 in range(d1):
                for r2 in range(d2):
                    r = r1 + d1 * r2
                    for c in range(n_slab):
                        u_scr[r * per:(r + 1) * per, c * LANES:(c + 1) * LANES] = (
                            slab2[c, pl.ds(r1 * group + r2, per, stride=d2), :].astype(BF16))
    ub = u_scr[...]
    for w_ref, b_ref, o_ref, sig in zip(w_refs, b_refs, o_refs, gate_bias):
        for c in range(w_ref.shape[1] // chunk):
            cols = slice(c * chunk, (c + 1) * chunk)
            acc = jnp.dot(ub, w_ref[:, cols], preferred_element_type=F32)
            if sig:
                acc = jax.nn.sigmoid(acc + b_ref[:, cols])
            for r in range(dil):
                o_ref[r, :, cols] = acc[r * per:(r + 1) * per].astype(o_ref.dtype)


def _inproj(x, gain, weights, dil, tm, chunk, name):
    B, S, D = x.shape
    ws = [w for w, _ in weights]
    bs = [jnp.zeros((1, w.shape[1]), F32) if b is None else b.reshape(1, -1) for w, b in weights]
    const = lambda a: pl.BlockSpec(a.shape, lambda b, i: (0, 0))
    scratch = [pltpu.VMEM((tm, D), BF16)]
    if dil > 1:
        scratch.append(pltpu.VMEM((D // LANES, tm, LANES), F32))
    if dil > REGROUP_STRIDE:
        assert dil % REGROUP_STRIDE == 0 and dil // REGROUP_STRIDE <= REGROUP_STRIDE
        scratch.append(pltpu.VMEM((D // LANES, tm, LANES), F32))
    return pl.pallas_call(
        functools.partial(_inproj_kernel, gate_bias=tuple(b is not None for _, b in weights),
                          dil=dil, chunk=chunk),
        grid=(B, S // tm),
        in_specs=[pl.BlockSpec((None, tm, D), lambda b, i: (b, i, 0)), const(gain.reshape(1, D))]
                 + [const(w) for w in ws] + [const(b) for b in bs],
        out_specs=[pl.BlockSpec((None, dil, tm // dil, w.shape[1]), lambda b, i: (b, 0, i, 0)) for w in ws],
        out_shape=[jax.ShapeDtypeStruct((B, dil, S // dil, w.shape[1]), BF16) for w in ws],
        scratch_shapes=scratch,
        compiler_params=pltpu.CompilerParams(
            dimension_semantics=("parallel", "parallel"), vmem_limit_bytes=VMEM_LIMIT),
        name=name,
    )(x, gain.reshape(1, D), *ws, *bs)


def _mem_kv_kernel(x_ref, g_ref, w_ref, o_ref):
    u = _rms(x_ref[...], g_ref[...]).astype(BF16)
    o_ref[...] = jnp.dot(u, w_ref[...], preferred_element_type=F32).astype(o_ref.dtype)


def _mem_kv(mem, gain, w):
    B, M, D = mem.shape
    N = w.shape[1]
    return pl.pallas_call(
        _mem_kv_kernel,
        grid=(B,),
        in_specs=[pl.BlockSpec((None, M, D), lambda b: (b, 0, 0)),
                  pl.BlockSpec((1, D), lambda b: (0, 0)),
                  pl.BlockSpec((D, N), lambda b: (0, 0))],
        out_specs=pl.BlockSpec((None, M, N), lambda b: (b, 0, 0)),
        out_shape=jax.ShapeDtypeStruct((B, M, N), BF16),
        compiler_params=pltpu.CompilerParams(
            dimension_semantics=("parallel",), vmem_limit_bytes=VMEM_LIMIT),
        name="mem_kv",
    )(mem, gain.reshape(1, D), w)


def _attn_block(q_ref, kp_ref, kc_ref, vp_ref, vc_ref, tab_ref, first, r, j):
    lane = lax.broadcasted_iota(I32, (Q_BLOCK, LANES), 1)
    low = lane < HEAD_DIM
    lse_tile = jnp.zeros((Q_BLOCK, LANES), F32)
    pairs = []
    if j > 0:
        first = 1
    for pair in range(ATTN_HEADS // 2):
        cols = slice(pair * LANES, (pair + 1) * LANES)
        q2 = q_ref[r, j * Q_BLOCK:(j + 1) * Q_BLOCK, cols] * (HEAD_DIM ** -0.5)
        if j == 0:
            k2 = jnp.concatenate([kp_ref[r, :, cols], kc_ref[r, :Q_BLOCK, cols]], axis=0)
            v2 = jnp.concatenate([vp_ref[r, :, cols], vc_ref[r, :Q_BLOCK, cols]], axis=0)
        else:
            k2 = kc_ref[r, (j - 1) * Q_BLOCK:(j + 1) * Q_BLOCK, cols]
            v2 = vc_ref[r, (j - 1) * Q_BLOCK:(j + 1) * Q_BLOCK, cols]
        outs = []
        for half in range(2):
            h = 2 * pair + half
            keep = low if half == 0 else jnp.logical_not(low)
            qm = jnp.where(keep, q2, jnp.zeros_like(q2))
            s = lax.dot_general(qm, k2, (((1,), (1,)), ((), ())), preferred_element_type=F32)
            s = s + tab_ref[first, h]
            m = jnp.max(s, axis=1, keepdims=True)
            p = jnp.exp(s - m)
            l = jnp.sum(p, axis=1, keepdims=True)
            o = jnp.dot(p.astype(BF16), v2, preferred_element_type=F32) * (1.0 / l)
            outs.append(o)
            lse_tile = jnp.where(lane == h, m + jnp.log(l), lse_tile)
        pairs.append(jnp.where(low, outs[0], outs[1]))
    return pairs, lse_tile


def _attn_kernel(q_ref, kp_ref, kc_ref, vp_ref, vc_ref, tab_ref, o_ref, lse_ref, *scratch, dil, qb):
    first = jnp.minimum(pl.program_id(1), 1)
    blocks = (q_ref, kp_ref, kc_ref, vp_ref, vc_ref, tab_ref)
    n_pair = ATTN_HEADS // 2
    if dil == 1:
        for j in range(qb):
            pairs, lse_tile = _attn_block(*blocks, first, 0, j)
            rows = slice(j * Q_BLOCK, (j + 1) * Q_BLOCK)
            for p in range(n_pair):
                o_ref[rows, p * LANES:(p + 1) * LANES] = pairs[p].astype(o_ref.dtype)
            lse_ref[rows, :] = lse_tile
        return

    o_scr, lse_scr = scratch

    unroll = min(dil, 8)

    def body(g, carry):
        for r in [g * unroll + t for t in range(unroll)]:
            for j in range(qb):
                pairs, lse_tile = _attn_block(*blocks, first, r, j)
                rows = pl.ds(j * Q_BLOCK * dil + r, Q_BLOCK, stride=dil)
                for p in range(n_pair):
                    o_scr[p, rows, :] = pairs[p]
                lse_scr[rows, :] = lse_tile
        return carry

    if dil == unroll:
        body(0, 0)
    else:
        lax.fori_loop(0, dil // unroll, body, 0)
    for p in range(n_pair):
        o_ref[:, p * LANES:(p + 1) * LANES] = o_scr[p].astype(o_ref.dtype)
    lse_ref[...] = lse_scr[...]


def _attn_bias_table(dil):
    slopes = np.power(2.0, -8.0 * np.arange(1, ATTN_HEADS + 1, dtype=np.float32) / ATTN_HEADS)
    iq = np.arange(Q_BLOCK)
    ik = np.arange(2 * Q_BLOCK)
    dist = iq[:, None] + Q_BLOCK - ik[None, :]
    band = (dist >= 0) & (dist <= Q_BLOCK)
    has_prev = np.stack([ik >= Q_BLOCK, np.ones_like(ik, dtype=bool)])
    mask = band[None] & has_prev[:, None, :]
    bias = -slopes[:, None, None] * (dist * dil).astype(np.float32)[None]
    return np.where(mask[:, None], bias[None], np.float32(NEG_INF)).astype(np.float32)


def _attention(qkv, dil, name):
    B, _, L, _ = qkv.shape
    W = ATTN_WIDTH
    qb = max(1, 4 // dil)
    nb = L // (Q_BLOCK * qb)
    span = Q_BLOCK * qb * dil
    blk = lambda part, prev: (
        pl.BlockSpec((None, dil, Q_BLOCK, W), lambda b, n: (b, 0, jnp.maximum(n * qb - 1, 0), part))
        if prev else pl.BlockSpec((None, dil, Q_BLOCK * qb, W), lambda b, n: (b, 0, n, part)))
    tab = _attn_bias_table(dil)
    scratch = [] if dil == 1 else [pltpu.VMEM((ATTN_HEADS // 2, span, LANES), F32),
                                   pltpu.VMEM((span, LANES), F32)]
    o, lse = pl.pallas_call(
        functools.partial(_attn_kernel, dil=dil, qb=qb),
        grid=(B, nb),
        in_specs=[blk(0, False), blk(1, True), blk(1, False), blk(2, True), blk(2, False),
                  pl.BlockSpec(tab.shape, lambda b, n: (0, 0, 0, 0))],
        out_specs=[pl.BlockSpec((None, span, W), lambda b, n: (b, n, 0)),
                   pl.BlockSpec((None, span, LANES), lambda b, n: (b, n, 0))],
        out_shape=[jax.ShapeDtypeStruct((B, L * dil, W), BF16),
                   jax.ShapeDtypeStruct((B, L * dil, LANES), F32)],
        scratch_shapes=scratch,
        compiler_params=pltpu.CompilerParams(
            dimension_semantics=("parallel", "parallel"), vmem_limit_bytes=VMEM_LIMIT),
        name=name,
    )(qkv, qkv, qkv, qkv, qkv, tab)
    return o.reshape(B * L * dil, W), lse.reshape(B * L * dil, LANES)


def _mix_kernel(o1_ref, o2_ref, o3_ref, l1_ref, l2_ref, l3_ref, gb_ref, gc_ref, xc_ref, gch_ref, xch_ref,
                ga_ref, gv_ref, x_ref, e_ref, wa_ref, wc_ref, wo_ref, cw_ref, h_ref, *, tm, seq):
    a1, a2, a3 = l1_ref[...], l2_ref[...], l3_ref[...]
    m = jnp.maximum(jnp.maximum(a1, a2), a3)
    e1, e2, e3 = jnp.exp(a1 - m), jnp.exp(a2 - m), jnp.exp(a3 - m)
    inv = 1.0 / (e1 + e2 + e3)
    y = jnp.zeros((tm, ATTN_WIDTH), F32)
    for e, o_ref in ((e1, o1_ref), (e2, o2_ref), (e3, o3_ref)):
        w = e * inv
        w_hi = w.astype(BF16)
        w_lo = (w - w_hi.astype(F32)).astype(BF16)
        w_heads = (jnp.dot(w_hi, e_ref[...], preferred_element_type=F32)
                   + jnp.dot(w_lo, e_ref[...], preferred_element_type=F32))
        y = y + w_heads * o_ref[...].astype(F32)
    branch_a = jnp.dot(y.astype(BF16), wa_ref[...], preferred_element_type=F32)

    at_start = (pl.program_id(0) * tm) % seq == 0
    halo = gch_ref[...].astype(F32) * xch_ref[...].astype(F32)
    halo = jnp.where(at_start, jnp.zeros_like(halo), halo)[CONV_HALO - 8:, :]
    z = gc_ref[...].astype(F32) * xc_ref[...].astype(F32)
    row = lax.broadcasted_iota(I32, (8, z.shape[1]), 0)
    taps = []
    for shift in (2, 1):
        zs = pltpu.roll(z, shift, axis=0)
        head = jnp.where(row < shift, pltpu.roll(halo, shift, axis=0), zs[0:8, :])
        taps.append(jnp.concatenate([head, zs[8:, :]], axis=0))
    conv = cw_ref[0:1, :] * taps[0] + cw_ref[1:2, :] * taps[1] + cw_ref[2:3, :] * z
    y_conv = gb_ref[...].astype(F32) * conv
    branch_c = jnp.dot(y_conv.astype(BF16), wc_ref[...], preferred_element_type=F32)

    mixed = ga_ref[...].astype(F32) * branch_a + gv_ref[...].astype(F32) * branch_c
    h_ref[...] = x_ref[...] + jnp.dot(mixed.astype(BF16), wo_ref[...], preferred_element_type=F32)


def _mix(o_list, lse_list, proj, gates, x2d, head_expand, wa, wc, wo, conv_w, seq, tm=512):
    T, D = x2d.shape
    W = ATTN_WIDTH
    row = lambda width, col: pl.BlockSpec((tm, width), lambda i: (i, col))
    halo = lambda col: pl.BlockSpec(
        (CONV_HALO, D), lambda i: (jnp.maximum(i * (tm // CONV_HALO) - 1, 0), col))
    full = lambda a: pl.BlockSpec(a.shape, lambda i: (0,) * a.ndim)
    return pl.pallas_call(
        functools.partial(_mix_kernel, tm=tm, seq=seq),
        grid=(T // tm,),
        in_specs=[row(W, 0)] * 3 + [row(LANES, 0)] * 3
                 + [row(D, 0), row(D, 1), row(D, 2), halo(1), halo(2)]
                 + [row(D, 0), row(D, 1), row(D, 0)]
                 + [full(head_expand), full(wa), full(wc), full(wo), full(conv_w)],
        out_specs=pl.BlockSpec((tm, D), lambda i: (i, 0)),
        out_shape=jax.ShapeDtypeStruct((T, D), F32),
        compiler_params=pltpu.CompilerParams(
            dimension_semantics=("parallel",), vmem_limit_bytes=VMEM_LIMIT),
        name="branch_mix",
    )(*o_list, *lse_list, proj, proj, proj, proj, proj, gates, gates, x2d,
      head_expand, wa, wc, wo, conv_w)


def _cross_kernel(h_ref, k_ref, v_ref, g2_ref, wq_ref, wo_ref, g3_ref, rw_ref, rb_ref, tri_ref,
                  h2_ref, u3_ref, meta_ref, wts_ref, cnt_ref, run_scr, *, tm):
    @pl.when(pl.program_id(0) == 0)
    def _():
        run_scr[...] = jnp.zeros_like(run_scr)

    h = h_ref[...]
    u = _rms(h, g2_ref[...]).astype(BF16)
    q = jnp.dot(u, wq_ref[...], preferred_element_type=F32) * (MEM_HEAD_DIM ** -0.5)
    heads = []
    for hd in range(MEM_HEADS):
        cols = slice(hd * MEM_HEAD_DIM, (hd + 1) * MEM_HEAD_DIM)
        s = lax.dot_general(q[:, cols].astype(BF16), k_ref[:, cols], (((1,), (1,)), ((), ())),
                            preferred_element_type=F32)
        m = jnp.max(s, axis=1, keepdims=True)
        p = jnp.exp(s - m)
        l = jnp.sum(p, axis=1, keepdims=True)
        heads.append(jnp.dot(p.astype(BF16), v_ref[:, cols], preferred_element_type=F32) * (1.0 / l))
    o = jnp.concatenate(heads, axis=1).astype(BF16)
    h2 = h + jnp.dot(o, wo_ref[...], preferred_element_type=F32)
    h2_ref[...] = h2
    u3 = _rms(h2, g3_ref[...])
    _pack_rows(u3, u3_ref, tm)

    logits = lax.dot_general(rw_ref[...], u3.astype(BF16), (((1,), (1,)), ((), ())),
                             preferred_element_type=F32) + rb_ref[:, 0:1]
    expert = lax.broadcasted_iota(I32, (N_EXPERTS, tm), 0).astype(F32)
    work = logits
    picked = jnp.zeros((N_EXPERTS, tm), F32)
    tops, idxs, sels = [], [], []
    for _ in range(TOP_K):
        mk = jnp.max(work, axis=0, keepdims=True)
        ik = jnp.min(jnp.where(work == mk, expert, float(N_EXPERTS)), axis=0, keepdims=True)
        sel = expert == ik
        work = jnp.where(sel, -jnp.inf, work)
        picked = picked + sel.astype(F32)
        tops.append(mk)
        idxs.append(ik)
        sels.append(sel)
    exps = [jnp.exp(t - tops[0]) for t in tops]
    inv = 1.0 / (exps[0] + exps[1] + exps[2] + exps[3])
    wts = [e * inv for e in exps]

    before = jnp.dot(picked.astype(BF16), tri_ref[...], preferred_element_type=F32) + run_scr[:, 0:1]
    ranks = [jnp.sum(jnp.where(sel, before, 0.0), axis=0, keepdims=True) for sel in sels]

    row = lax.broadcasted_iota(I32, (META_ROWS, tm), 0)
    meta = jnp.zeros((META_ROWS, tm), F32)
    for r, v in enumerate(idxs + ranks + wts):
        meta = jnp.where(row == r, v, meta)
    meta_ref[...] = meta
    row = lax.broadcasted_iota(I32, (LANES, tm), 0)
    wpad = jnp.zeros((LANES, tm), F32)
    for k in range(TOP_K):
        wpad = jnp.where(row == k, wts[k], wpad)
    wts_ref[...] = wpad.T
    run_scr[...] = run_scr[...] + jnp.sum(picked, axis=1, keepdims=True)
    cnt_ref[...] = run_scr[...]


def _cross_router(h1, kv, g2, wq, wo, g3, rw_t, rb, seq, tm=512):
    T, D = h1.shape
    n_mem = kv.shape[1]
    tri = (jnp.arange(tm)[:, None] < jnp.arange(tm)[None, :]).astype(BF16)
    full = lambda a: pl.BlockSpec(a.shape, lambda i: (0,) * a.ndim)
    kvspec = lambda col: pl.BlockSpec((None, n_mem, MEM_WIDTH), lambda i: ((i * tm) // seq, 0, col))
    tile = lambda width: pl.BlockSpec((tm, width), lambda i: (i, 0))
    return pl.pallas_call(
        functools.partial(_cross_kernel, tm=tm),
        grid=(T // tm,),
        in_specs=[tile(D), kvspec(0), kvspec(1), full(g2), full(wq), full(wo), full(g3),
                  full(rw_t), full(rb), full(tri)],
        out_specs=[tile(D), pl.BlockSpec((tm * PACK_ROWS, LANES), lambda i: (i, 0)),
                   pl.BlockSpec((META_ROWS, tm), lambda i: (0, i)), tile(LANES),
                   pl.BlockSpec((N_EXPERTS, LANES), lambda i: (0, 0))],
        out_shape=[jax.ShapeDtypeStruct((T, D), F32), jax.ShapeDtypeStruct((T * PACK_ROWS, LANES), F32),
                   jax.ShapeDtypeStruct((META_ROWS, T), F32), jax.ShapeDtypeStruct((T, LANES), F32),
                   jax.ShapeDtypeStruct((N_EXPERTS, LANES), F32)],
        scratch_shapes=[pltpu.VMEM((N_EXPERTS, LANES), F32)],
        compiler_params=pltpu.CompilerParams(
            dimension_semantics=("arbitrary",), vmem_limit_bytes=VMEM_LIMIT),
        name="cross_router",
    )(h1, kv, kv, g2, wq, wo, g3, rw_t, rb, tri)


def _plan_kernel(pstart_ref, pad_end_ref, fill_base_ref, meta_ref, pos_ref, fill_ref):
    ei = meta_ref[0:2 * TOP_K, :].astype(I32)
    start = jnp.zeros(ei.shape, I32)
    for j in range(N_EXPERTS):
        start = jnp.where(ei == j, pstart_ref[j], start)
    pos_ref[...] = start + pltpu.roll(ei, TOP_K, axis=0)

    rows, cols = fill_ref.shape
    f = lax.broadcasted_iota(I32, (rows, cols), 0) * cols + lax.broadcasted_iota(I32, (rows, cols), 1)
    base = jnp.full((rows, cols), fill_base_ref[0], I32)
    for j in range(1, N_EXPERTS + 1):
        base = jnp.where(f >= pad_end_ref[j - 1], fill_base_ref[j], base)
    fill_ref[...] = base + f


def _plan(meta, pstart, pad_end, fill_base, n_fill):
    T = meta.shape[1]
    rows = 2 * TOP_K
    assert n_fill % (rows * LANES) == 0
    grid_spec = pltpu.PrefetchScalarGridSpec(
        num_scalar_prefetch=3,
        grid=(1,),
        in_specs=[pl.BlockSpec(meta.shape, lambda i, a, b, c: (0, 0))],
        out_specs=[pl.BlockSpec((rows, T), lambda i, a, b, c: (0, 0)),
                   pl.BlockSpec((rows, n_fill // rows), lambda i, a, b, c: (0, 0))],
    )
    pos, fill = pl.pallas_call(
        _plan_kernel,
        grid_spec=grid_spec,
        out_shape=[jax.ShapeDtypeStruct((rows, T), I32), jax.ShapeDtypeStruct((rows, n_fill // rows), I32)],
        compiler_params=pltpu.CompilerParams(dimension_semantics=("arbitrary",)),
        name="row_plan",
    )(pstart, pad_end, fill_base, meta)
    return pos[:TOP_K], fill.reshape(-1)


def _wait_copies(src_hbm, dst, sem, rows):
    pltpu.make_async_copy(src_hbm.at[pl.ds(0, rows), :], dst.at[pl.ds(0, rows), :], sem).wait()


def _dispatch_kernel(pos_ref, u_hbm, x_hbm, stage, load_sem, row_sem, *, tm, n_token_steps):
    i = pl.program_id(0)
    n = pl.num_programs(0)
    batch = tm * TOP_K * PACK_ROWS
    tile_rows = tm * PACK_ROWS

    def tile_copy(step):
        first = pl.multiple_of(jnp.minimum(step, n_token_steps - 1) * tile_rows, tile_rows)
        return pltpu.make_async_copy(u_hbm.at[pl.ds(first, tile_rows), :], stage.at[step % 3],
                                     load_sem.at[step % 3])

    @pl.when(i == 0)
    def _():
        tile_copy(i).start()

    @pl.when(i + 1 < n)
    def _():
        tile_copy(i + 1).start()

    tile_copy(i).wait()
    src_tile = stage.at[i % 3]
    group = 16

    for k in range(TOP_K):
        def body(g, carry, k=k):
            rows = [pos_ref[0, k * tm + g * group + j] for j in range(group)]
            for j, p in enumerate(rows):
                src = src_tile.at[pl.ds(pl.multiple_of((g * group + j) * PACK_ROWS, PACK_ROWS), PACK_ROWS), :]
                dst = x_hbm.at[pl.ds(pl.multiple_of(p * PACK_ROWS, PACK_ROWS), PACK_ROWS), :]
                pltpu.make_async_copy(src, dst, row_sem.at[i % 2]).start(priority=j % 2)
            return carry

        lax.fori_loop(0, tm // group, body, 0)

    @pl.when(i > 0)
    def _():
        _wait_copies(u_hbm, x_hbm, row_sem.at[1 - i % 2], batch)

    @pl.when(i == n - 1)
    def _():
        _wait_copies(u_hbm, x_hbm, row_sem.at[i % 2], batch)


def _dispatch(u_packed, pos_kmajor, fill_rows, tm):
    T = pos_kmajor.shape[0] * tm
    per_step = TOP_K * tm
    dst_rows = jnp.concatenate([pos_kmajor.reshape(-1), fill_rows])
    n_rows = dst_rows.shape[0]
    assert n_rows % per_step == 0
    steps = n_rows // per_step
    return pl.pallas_call(
        functools.partial(_dispatch_kernel, tm=tm, n_token_steps=T // tm),
        grid=(steps,),
        in_specs=[pl.BlockSpec((None, 1, per_step), lambda i: (i, 0, 0), memory_space=pltpu.SMEM),
                  pl.BlockSpec(memory_space=pl.ANY)],
        out_specs=pl.BlockSpec(memory_space=pl.ANY),
        out_shape=jax.ShapeDtypeStruct((n_rows * PACK_ROWS, LANES), F32),
        scratch_shapes=[pltpu.VMEM((3, tm * PACK_ROWS, LANES), F32),
                        pltpu.SemaphoreType.DMA((3,)), pltpu.SemaphoreType.DMA((2,))],
        compiler_params=pltpu.CompilerParams(dimension_semantics=("arbitrary",)),
        name="dispatch_rows",
    )(dst_rows.reshape(steps, 1, per_step), u_packed)


def _expert_kernel(bexp_ref, slot_ref, next_ref, nused_ref, x_ref, wg_hbm, bg_ref, wu_hbm, bu_ref,
                   wd_hbm, bd_ref, y_ref, stage, w_bf, sem):
    weights = (wg_hbm, wu_hbm, wd_hbm)
    block_rows = MOE_ROWS * PACK_ROWS

    def fetch(expert, into):
        return [pltpu.make_async_copy(w.at[expert], stage.at[into, j], sem.at[into])
                for j, w in enumerate(weights)]

    for sub in range(MOE_STEP_BLOCKS):
        i = pl.program_id(0) * MOE_STEP_BLOCKS + sub
        used = i < nused_ref[0]
        e = bexp_ref[i]
        slot = slot_ref[i]
        changed = jnp.logical_or(i == 0, e != bexp_ref[jnp.maximum(i - 1, 0)])
        y_rows = y_ref.at[pl.ds(sub * block_rows, block_rows), :]

        @pl.when(jnp.logical_and(changed, used))
        def _():
            @pl.when(i == 0)
            def _():
                for c in fetch(e, slot):
                    c.start()

            for c in fetch(e, slot):
                c.wait()
            for s in range(2):
                @pl.when(slot == s)
                def _():
                    for j in range(len(weights)):
                        w_bf[j] = stage[s, j].astype(BF16)

            @pl.when(next_ref[i] >= 0)
            def _():
                for c in fetch(next_ref[i], 1 - slot):
                    c.start(priority=1)

        @pl.when(used)
        def _():
            x = _unpack_rows(x_ref, sub * block_rows, MOE_ROWS).astype(BF16)
            gate = jnp.minimum(jnp.dot(x, w_bf[0], preferred_element_type=F32) + bg_ref[e], SWIGLU_LIMIT)
            lin = jnp.clip(jnp.dot(x, w_bf[1], preferred_element_type=F32) + bu_ref[e],
                           -SWIGLU_LIMIT, SWIGLU_LIMIT)
            hdn = gate * jax.nn.sigmoid(SWIGLU_ALPHA * gate) * (lin + 1.0)
            y = jnp.dot(hdn.astype(BF16), w_bf[2], preferred_element_type=F32) + bd_ref[e]
            _pack_rows(y, y_rows, MOE_ROWS)

        @pl.when(jnp.logical_not(used))
        def _():
            y_rows[...] = jnp.zeros(y_rows.shape, y_rows.dtype)


def _experts(x_packed, block_expert, block_slot, block_next, n_used, w_eg, b_eg, w_eu, b_eu, w_ed, b_ed):
    nb = block_expert.shape[0]
    assert nb % MOE_STEP_BLOCKS == 0
    E, D, F = w_eg.shape
    assert D == F
    step_rows = MOE_STEP_BLOCKS * MOE_ROWS * PACK_ROWS
    rows = lambda i, be, sl, nx, nu: (i, 0)
    bspec = lambda n: pl.BlockSpec((E, 1, n), lambda i, be, sl, nx, nu: (0, 0, 0))
    anyspec = pl.BlockSpec(memory_space=pl.ANY)
    grid_spec = pltpu.PrefetchScalarGridSpec(
        num_scalar_prefetch=4,
        grid=(nb // MOE_STEP_BLOCKS,),
        in_specs=[pl.BlockSpec((step_rows, LANES), rows),
                  anyspec, bspec(F), anyspec, bspec(F), anyspec, bspec(D)],
        out_specs=pl.BlockSpec((step_rows, LANES), rows),
        scratch_shapes=[pltpu.VMEM((2, 3, D, F), F32), pltpu.VMEM((3, D, F), BF16),
                        pltpu.SemaphoreType.DMA((2,))],
    )
    return pl.pallas_call(
        _expert_kernel,
        grid_spec=grid_spec,
        out_shape=jax.ShapeDtypeStruct(x_packed.shape, F32),
        compiler_params=pltpu.CompilerParams(
            dimension_semantics=("arbitrary",), vmem_limit_bytes=VMEM_LIMIT),
        name="expert_ffn",
    )(block_expert, block_slot, block_next, n_used, x_packed, w_eg, b_eg.reshape(E, 1, F), w_eu,
      b_eu.reshape(E, 1, F), w_ed, b_ed.reshape(E, 1, D))


def _gather_packed(pos_ref, src_hbm, dst, sem, n):
    group = 16

    def body(g, carry):
        rows = [pos_ref[0, g * group + j] for j in range(group)]
        for j, p in enumerate(rows):
            r = g * group + j
            pltpu.make_async_copy(
                src_hbm.at[pl.ds(pl.multiple_of(p * PACK_ROWS, PACK_ROWS), PACK_ROWS), :],
                dst.at[pl.ds(pl.multiple_of(r * PACK_ROWS, PACK_ROWS), PACK_ROWS), :],
                sem).start(priority=j % 2)
        return carry
    lax.fori_loop(0, n // group, body, 0)


def _combine_kernel(pos0_ref, pos1_ref, y_hbm, h_ref, w_ref, g_ref, o_ref, ybuf, sem):
    i = pl.program_id(0)
    n = pl.num_programs(0)
    slot = i % 2
    rows = TOP_K * COMBINE_ROWS

    @pl.when(i == 0)
    def _():
        _gather_packed(pos0_ref, y_hbm, ybuf.at[0], sem.at[0], rows)

    @pl.when(i + 1 < n)
    def _():
        _gather_packed(pos1_ref, y_hbm, ybuf.at[1 - slot], sem.at[1 - slot], rows)

    _wait_copies(y_hbm, ybuf.at[slot], sem.at[slot], rows * PACK_ROWS)
    acc = h_ref[...]
    w = w_ref[...]
    for k in range(TOP_K):
        acc = acc + w[:, k:k + 1] * _unpack_rows(ybuf.at[slot], k * COMBINE_ROWS * PACK_ROWS, COMBINE_ROWS)
    o_ref[...] = _rms(acc, g_ref[...])


def _combine(y_rows, pos_kmajor, h2, wts, g_final):
    T, D = h2.shape
    tm = COMBINE_ROWS
    nt = T // tm
    posspec = lambda off: pl.BlockSpec(
        (None, 1, TOP_K * tm), lambda i: (jnp.minimum(i + off, nt - 1), 0, 0), memory_space=pltpu.SMEM)
    return pl.pallas_call(
        _combine_kernel,
        grid=(nt,),
        in_specs=[posspec(0), posspec(1), pl.BlockSpec(memory_space=pl.ANY),
                  pl.BlockSpec((tm, D), lambda i: (i, 0)), pl.BlockSpec((tm, LANES), lambda i: (i, 0)),
                  pl.BlockSpec((1, D), lambda i: (0, 0))],
        out_specs=pl.BlockSpec((tm, D), lambda i: (i, 0)),
        out_shape=jax.ShapeDtypeStruct((T, D), F32),
        scratch_shapes=[pltpu.VMEM((2, TOP_K * tm * PACK_ROWS, LANES), F32), pltpu.SemaphoreType.DMA((2,))],
        compiler_params=pltpu.CompilerParams(
            dimension_semantics=("arbitrary",), vmem_limit_bytes=VMEM_LIMIT),
        name="combine_norm",
    )(pos_kmajor, pos_kmajor, y_rows, h2, wts, g_final.reshape(1, D))


def _layer(h, mem, norm_mix, w_in, conv_w, w_branch_attn, w_branch_conv, w_gate, b_gate, w_out,
           norm_cross, norm_mem, w_cq, w_ckv, w_co, norm_moe, router_w, router_b,
           w_eg, b_eg, w_eu, b_eu, w_ed, b_ed, norm_final):
    B, S, D = h.shape
    T = B * S
    W3 = 3 * ATTN_WIDTH
    n_attn = len(ATTN_GROUPS) * W3

    assert ATTN_GROUPS[0][1] == 1
    w_in_b = w_in.astype(BF16)
    proj, qkv1, gates = _inproj(
        h, norm_mix, [(w_in_b[:, n_attn:], None), (w_in_b[:, :W3], None), (w_gate.astype(BF16), b_gate)],
        1, 512, 512, name="inproj_conv_g1_gates")
    proj2d = proj.reshape(T, -1)
    gates2d = gates.reshape(T, -1)

    o_list, lse_list = [], []
    for g, (window, dil) in enumerate(ATTN_GROUPS):
        assert window // dil == Q_BLOCK
        if g == 0:
            qkv = qkv1
        else:
            qkv, = _inproj(h, norm_mix, [(w_in_b[:, g * W3:(g + 1) * W3], None)], dil, 1024, 512,
                           name=f"inproj_g{g + 1}")
        o, lse = _attention(qkv, dil, name=f"dilated_attn_g{g + 1}")
        o_list.append(o)
        lse_list.append(lse)

    head_expand = (jnp.arange(LANES)[:, None] == jnp.arange(ATTN_WIDTH)[None, :] // HEAD_DIM).astype(BF16)
    h1 = _mix(o_list, lse_list, proj2d, gates2d, h.reshape(T, D), head_expand,
              w_branch_attn.astype(BF16), w_branch_conv.astype(BF16), w_out.astype(BF16), conv_w, S)

    kv = _mem_kv(mem, norm_mem, w_ckv.astype(BF16))
    rb = jnp.broadcast_to(router_b[:, None], (N_EXPERTS, LANES))
    h2, u3, meta, wts, counts = _cross_router(
        h1, kv, norm_cross.reshape(1, D), w_cq.astype(BF16), w_co.astype(BF16), norm_moe.reshape(1, D),
        router_w.T.astype(BF16), rb, S)

    nb = -(-(T * TOP_K) // MOE_ROWS) + N_EXPERTS
    cnt = counts[:, 0].astype(I32)
    padded = (cnt + MOE_ROWS - 1) // MOE_ROWS * MOE_ROWS
    pend = jnp.cumsum(padded)
    pstart = pend - padded
    experts = jnp.arange(N_EXPERTS, dtype=I32)
    n_pad = padded - cnt
    pad_end = jnp.cumsum(n_pad)
    fill_base = jnp.concatenate([pstart + cnt - (pad_end - n_pad), pend[-1:] - pad_end[-1:]])
    pos_kt, fill_rows = _plan(meta, pstart, pad_end, fill_base, nb * MOE_ROWS - T * TOP_K)

    tm = COMBINE_ROWS
    assert tm & (tm - 1) == 0 and T % tm == 0
    pos_kmajor = pos_kt.reshape(TOP_K, T // tm, tm).transpose(1, 0, 2).reshape(T // tm, 1, TOP_K * tm)
    block_start = jnp.arange(nb, dtype=I32) * MOE_ROWS
    block_expert = jnp.minimum(jnp.sum(block_start[:, None] >= pend[None, :], axis=1), N_EXPERTS - 1).astype(I32)
    active = cnt > 0
    slot_e = (jnp.cumsum(active.astype(I32)) - 1) % 2
    later = jnp.where(active[None, :] & (experts[None, :] > experts[:, None]), experts[None, :], N_EXPERTS)
    next_e = jnp.min(later, axis=1)
    next_e = jnp.where(next_e == N_EXPERTS, -1, next_e).astype(I32)
    x_rows = _dispatch(u3, pos_kmajor, fill_rows, tm)
    block_is = block_expert[:, None] == experts[None, :]
    per_block = lambda table: jnp.sum(jnp.where(block_is, table[None, :], 0), axis=1).astype(I32)
    y_rows = _experts(x_rows, block_expert, per_block(slot_e), per_block(next_e),
                      (pend[-1:] // MOE_ROWS).astype(I32), w_eg, b_eg, w_eu, b_eu, w_ed, b_ed)

    out = _combine(y_rows, pos_kmajor, h2, wts, norm_final)
    return out.reshape(B, S, D)


def kernel(x, mem, norm_mix, w_in, conv_w, w_branch_attn, w_branch_conv, w_gate, b_gate, w_out, norm_cross, norm_mem, w_cq, w_ckv, w_co, norm_moe, router_w, router_b, w_exp_gate, b_exp_gate, w_exp_up, b_exp_up, w_exp_down, b_exp_down, norm_final):
    depth = norm_mix.shape[0]
    assert depth == 1, "the final norm is fused into the last layer's combine step"
    return _layer(x, mem, norm_mix[0], w_in[0], conv_w[0], w_branch_attn[0], w_branch_conv[0], w_gate[0],
                  b_gate[0], w_out[0], norm_cross[0], norm_mem[0], w_cq[0], w_ckv[0], w_co[0], norm_moe[0],
                  router_w[0], router_b[0], w_exp_gate[0], b_exp_gate[0], w_exp_up[0], b_exp_up[0],
                  w_exp_down[0], b_exp_down[0], norm_final)
```

```python
import functools

import jax
import jax.numpy as jnp
import numpy as np
from jax import lax
from jax.experimental import pallas as pl
from jax.experimental.pallas import tpu as pltpu

F32 = jnp.float32
BF16 = jnp.bfloat16
I32 = jnp.int32

D_MODEL = 1024
ATTN_GROUPS = ((128, 1), (512, 4), (2048, 16))
ATTN_HEADS = 8
HEAD_DIM = 64
ATTN_WIDTH = ATTN_HEADS * HEAD_DIM
Q_BLOCK = 128
CONV_K = 3
MEM_HEADS = 4
MEM_HEAD_DIM = 128
MEM_WIDTH = MEM_HEADS * MEM_HEAD_DIM
N_EXPERTS = 32
TOP_K = 4
SWIGLU_LIMIT = 7.0
SWIGLU_ALPHA = 1.702
EPS = 1e-6
NEG_INF = -1e30

LANES = 128
VMEM_LIMIT = 56 * 1024 * 1024

MOE_ROWS = 256
MOE_STEP_BLOCKS = 4
COMBINE_ROWS = 1024
CONV_HALO = 16
META_ROWS = 16
REGROUP_STRIDE = 4


def _rms(x, g):
    ms = jnp.mean(x * x, axis=-1, keepdims=True)
    return x * lax.rsqrt(ms + EPS) * g


PACK_ROWS = D_MODEL // LANES


def _pack_rows(y, out_ref, n):
    for c in range(PACK_ROWS):
        out_ref[pl.ds(c, n, stride=PACK_ROWS), :] = y[:, c * LANES:(c + 1) * LANES]


def _unpack_rows(ref, start, n):
    return jnp.concatenate(
        [ref[pl.ds(start + c, n, stride=PACK_ROWS), :] for c in range(PACK_ROWS)], axis=1)


def _inproj_kernel(x_ref, g_ref, *refs, gate_bias, dil, chunk):
    n_w = len(gate_bias)
    w_refs, b_refs, o_refs = refs[:n_w], refs[n_w:2 * n_w], refs[2 * n_w:3 * n_w]
    u_scr = refs[3 * n_w]
    tm = x_ref.shape[0]
    per = tm // dil
    u = _rms(x_ref[...], g_ref[...])
    if dil == 1:
        u_scr[...] = u.astype(BF16)
    else:
        slab = refs[3 * n_w + 1]
        n_slab = u.shape[1] // LANES
        for c in range(n_slab):
            slab[c] = u[:, c * LANES:(c + 1) * LANES]
        if dil <= REGROUP_STRIDE:
            for r in range(dil):
                for c in range(n_slab):
                    u_scr[r * per:(r + 1) * per, c * LANES:(c + 1) * LANES] = (
                        slab[c, pl.ds(r, per, stride=dil), :].astype(BF16))
        else:
            slab2 = refs[3 * n_w + 2]
            d1, d2 = REGROUP_STRIDE, dil // REGROUP_STRIDE
            group = tm // d1
            for r1 in range(d1):
                for c in range(n_slab):
                    slab2[c, r1 * group:(r1 + 1) * group, :] = slab[c, pl.ds(r1, group, stride=d1), :]
            for r1 in range(d1):
                for r2 in range(d2):
                    r = r1 + d1 * r2
                    for c in range(n_slab):
                        u_scr[r * per:(r + 1) * per, c * LANES:(c + 1) * LANES] = (
                            slab2[c, pl.ds(r1 * group + r2, per, stride=d2), :].astype(BF16))
    ub = u_scr[...]
    for w_ref, b_ref, o_ref, sig in zip(w_refs, b_refs, o_refs, gate_bias):
        for c in range(w_ref.shape[1] // chunk):
            cols = slice(c * chunk, (c + 1) * chunk)
            acc = jnp.dot(ub, w_ref[:, cols], preferred_element_type=F32)
            if sig:
                acc = jax.nn.sigmoid(acc + b_ref[:, cols])
            for r in range(dil):
                o_ref[r, :, cols] = acc[r * per:(r + 1) * per].astype(o_ref.dtype)


def _inproj(x, gain, weights, dil, tm, chunk, name):
    B, S, D = x.shape
    ws = [w for w, _ in weights]
    bs = [jnp.zeros((1, w.shape[1]), F32) if b is None else b.reshape(1, -1) for w, b in weights]
    const = lambda a: pl.BlockSpec(a.shape, lambda b, i: (0, 0))
    scratch = [pltpu.VMEM((tm, D), BF16)]
    if dil > 1:
        scratch.append(pltpu.VMEM((D // LANES, tm, LANES), F32))
    if dil > REGROUP_STRIDE:
        assert dil % REGROUP_STRIDE == 0 and dil // REGROUP_STRIDE <= REGROUP_STRIDE
        scratch.append(pltpu.VMEM((D // LANES, tm, LANES), F32))
    return pl.pallas_call(
        functools.partial(_inproj_kernel, gate_bias=tuple(b is not None for _, b in weights),
                          dil=dil, chunk=chunk),
        grid=(B, S // tm),
        in_specs=[pl.BlockSpec((None, tm, D), lambda b, i: (b, i, 0)), const(gain.reshape(1, D))]
                 + [const(w) for w in ws] + [const(b) for b in bs],
        out_specs=[pl.BlockSpec((None, dil, tm // dil, w.shape[1]), lambda b, i: (b, 0, i, 0)) for w in ws],
        out_shape=[jax.ShapeDtypeStruct((B, dil, S // dil, w.shape[1]), BF16) for w in ws],
        scratch_shapes=scratch,
        compiler_params=pltpu.CompilerParams(
            dimension_semantics=("parallel", "parallel"), vmem_limit_bytes=VMEM_LIMIT),
        name=name,
    )(x, gain.reshape(1, D), *ws, *bs)


def _mem_kv_kernel(x_ref, g_ref, w_ref, o_ref):
    u = _rms(x_ref[...], g_ref[...]).astype(BF16)
    o_ref[...] = jnp.dot(u, w_ref[...], preferred_element_type=F32).astype(o_ref.dtype)


def _mem_kv(mem, gain, w):
    B, M, D = mem.shape
    N = w.shape[1]
    return pl.pallas_call(
        _mem_kv_kernel,
        grid=(B,),
        in_specs=[pl.BlockSpec((None, M, D), lambda b: (b, 0, 0)),
                  pl.BlockSpec((1, D), lambda b: (0, 0)),
                  pl.BlockSpec((D, N), lambda b: (0, 0))],
        out_specs=pl.BlockSpec((None, M, N), lambda b: (b, 0, 0)),
        out_shape=jax.ShapeDtypeStruct((B, M, N), BF16),
        compiler_params=pltpu.CompilerParams(
            dimension_semantics=("parallel",), vmem_limit_bytes=VMEM_LIMIT),
        name="mem_kv",
    )(mem, gain.reshape(1, D), w)


def _attn_block(q_ref, kp_ref, kc_ref, vp_ref, vc_ref, tab_ref, first, r, j):
    lane = lax.broadcasted_iota(I32, (Q_BLOCK, LANES), 1)
    low = lane < HEAD_DIM
    lse_tile = jnp.zeros((Q_BLOCK, LANES), F32)
    pairs = []
    if j > 0:
        first = 1
    for pair in range(ATTN_HEADS // 2):
        cols = slice(pair * LANES, (pair + 1) * LANES)
        q2 = q_ref[r, j * Q_BLOCK:(j + 1) * Q_BLOCK, cols] * (HEAD_DIM ** -0.5)
        if j == 0:
            k2 = jnp.concatenate([kp_ref[r, :, cols], kc_ref[r, :Q_BLOCK, cols]], axis=0)
            v2 = jnp.concatenate([vp_ref[r, :, cols], vc_ref[r, :Q_BLOCK, cols]], axis=0)
        else:
            k2 = kc_ref[r, (j - 1) * Q_BLOCK:(j + 1) * Q_BLOCK, cols]
            v2 = vc_ref[r, (j - 1) * Q_BLOCK:(j + 1) * Q_BLOCK, cols]
        outs = []
        for half in range(2):
            h = 2 * pair + half
            keep = low if half == 0 else jnp.logical_not(low)
            qm = jnp.where(keep, q2, jnp.zeros_like(q2))
            s = lax.dot_general(qm, k2, (((1,), (1,)), ((), ())), preferred_element_type=F32)
            s = s + tab_ref[first, h]
            m = jnp.max(s, axis=1, keepdims=True)
            p = jnp.exp(s - m)
            l = jnp.sum(p, axis=1, keepdims=True)
            o = jnp.dot(p.astype(BF16), v2, preferred_element_type=F32) * (1.0 / l)
            outs.append(o)
            lse_tile = jnp.where(lane == h, m + jnp.log(l), lse_tile)
        pairs.append(jnp.where(low, outs[0], outs[1]))
    return pairs, lse_tile


def _attn_kernel(q_ref, kp_ref, kc_ref, vp_ref, vc_ref, tab_ref, o_ref, lse_ref, *scratch, dil, qb):
    first = jnp.minimum(pl.program_id(1), 1)
    blocks = (q_ref, kp_ref, kc_ref, vp_ref, vc_ref, tab_ref)
    n_pair = ATTN_HEADS // 2
    if dil == 1:
        for j in range(qb):
            pairs, lse_tile = _attn_block(*blocks, first, 0, j)
            rows = slice(j * Q_BLOCK, (j + 1) * Q_BLOCK)
            for p in range(n_pair):
                o_ref[rows, p * LANES:(p + 1) * LANES] = pairs[p].astype(o_ref.dtype)
            lse_ref[rows, :] = lse_tile
        return

    o_scr, lse_scr = scratch

    unroll = min(dil, 8)

    def body(g, carry):
        for r in [g * unroll + t for t in range(unroll)]:
            for j in range(qb):
                pairs, lse_tile = _attn_block(*blocks, first, r, j)
                rows = pl.ds(j * Q_BLOCK * dil + r, Q_BLOCK, stride=dil)
                for p in range(n_pair):
                    o_scr[p, rows, :] = pairs[p]
                lse_scr[rows, :] = lse_tile
        return carry

    if dil == unroll:
        body(0, 0)
    else:
        lax.fori_loop(0, dil // unroll, body, 0)
    for p in range(n_pair):
        o_ref[:, p * LANES:(p + 1) * LANES] = o_scr[p].astype(o_ref.dtype)
    lse_ref[...] = lse_scr[...]


def _attn_bias_table(dil):
    slopes = np.power(2.0, -8.0 * np.arange(1, ATTN_HEADS + 1, dtype=np.float32) / ATTN_HEADS)
    iq = np.arange(Q_BLOCK)
    ik = np.arange(2 * Q_BLOCK)
    dist = iq[:, None] + Q_BLOCK - ik[None, :]
    band = (dist >= 0) & (dist <= Q_BLOCK)
    has_prev = np.stack([ik >= Q_BLOCK, np.ones_like(ik, dtype=bool)])
    mask = band[None] & has_prev[:, None, :]
    bias = -slopes[:, None, None] * (dist * dil).astype(np.float32)[None]
    return np.where(mask[:, None], bias[None], np.float32(NEG_INF)).astype(np.float32)


def _attention(qkv, dil, name):
    B, _, L, _ = qkv.shape
    W = ATTN_WIDTH
    qb = max(1, 8 // dil)
    nb = L // (Q_BLOCK * qb)
    span = Q_BLOCK * qb * dil
    blk = lambda part, prev: (
        pl.BlockSpec((None, dil, Q_BLOCK, W), lambda b, n: (b, 0, jnp.maximum(n * qb - 1, 0), part))
        if prev else pl.BlockSpec((None, dil, Q_BLOCK * qb, W), lambda b, n: (b, 0, n, part)))
    tab = _attn_bias_table(dil)
    scratch = [] if dil == 1 else [pltpu.VMEM((ATTN_HEADS // 2, span, LANES), F32),
                                   pltpu.VMEM((span, LANES), F32)]
    o, lse = pl.pallas_call(
        functools.partial(_attn_kernel, dil=dil, qb=qb),
        grid=(B, nb),
        in_specs=[blk(0, False), blk(1, True), blk(1, False), blk(2, True), blk(2, False),
                  pl.BlockSpec(tab.shape, lambda b, n: (0, 0, 0, 0))],
        out_specs=[pl.BlockSpec((None, span, W), lambda b, n: (b, n, 0)),
                   pl.BlockSpec((None, span, LANES), lambda b, n: (b, n, 0))],
        out_shape=[jax.ShapeDtypeStruct((B, L * dil, W), BF16),
                   jax.ShapeDtypeStruct((B, L * dil, LANES), F32)],
        scratch_shapes=scratch,
        compiler_params=pltpu.CompilerParams(
            dimension_semantics=("parallel", "parallel"), vmem_limit_bytes=VMEM_LIMIT),
        name=name,
    )(qkv, qkv, qkv, qkv, qkv, tab)
    return o.reshape(B * L * dil, W), lse.reshape(B * L * dil, LANES)


def _mix_kernel(o1_ref, o2_ref, o3_ref, l1_ref, l2_ref, l3_ref, gb_ref, gc_ref, xc_ref, gch_ref, xch_ref,
                ga_ref, gv_ref, x_ref, e_ref, wa_ref, wc_ref, wo_ref, cw_ref, h_ref, *, tm, seq):
    a1, a2, a3 = l1_ref[...], l2_ref[...], l3_ref[...]
    m = jnp.maximum(jnp.maximum(a1, a2), a3)
    e1, e2, e3 = jnp.exp(a1 - m), jnp.exp(a2 - m), jnp.exp(a3 - m)
    inv = 1.0 / (e1 + e2 + e3)
    y = jnp.zeros((tm, ATTN_WIDTH), F32)
    for e, o_ref in ((e1, o1_ref), (e2, o2_ref), (e3, o3_ref)):
        w = e * inv
        w_hi = w.astype(BF16)
        w_lo = (w - w_hi.astype(F32)).astype(BF16)
        w_heads = (jnp.dot(w_hi, e_ref[...], preferred_element_type=F32)
                   + jnp.dot(w_lo, e_ref[...], preferred_element_type=F32))
        y = y + w_heads * o_ref[...].astype(F32)
    branch_a = jnp.dot(y.astype(BF16), wa_ref[...], preferred_element_type=F32)

    at_start = (pl.program_id(0) * tm) % seq == 0
    halo = gch_ref[...].astype(F32) * xch_ref[...].astype(F32)
    halo = jnp.where(at_start, jnp.zeros_like(halo), halo)[CONV_HALO - 8:, :]
    z = gc_ref[...].astype(F32) * xc_ref[...].astype(F32)
    row = lax.broadcasted_iota(I32, (8, z.shape[1]), 0)
    taps = []
    for shift in (2, 1):
        zs = pltpu.roll(z, shift, axis=0)
        head = jnp.where(row < shift, pltpu.roll(halo, shift, axis=0), zs[0:8, :])
        taps.append(jnp.concatenate([head, zs[8:, :]], axis=0))
    conv = cw_ref[0:1, :] * taps[0] + cw_ref[1:2, :] * taps[1] + cw_ref[2:3, :] * z
    y_conv = gb_ref[...].astype(F32) * conv
    branch_c = jnp.dot(y_conv.astype(BF16), wc_ref[...], preferred_element_type=F32)

    mixed = ga_ref[...].astype(F32) * branch_a + gv_ref[...].astype(F32) * branch_c
    h_ref[...] = x_ref[...] + jnp.dot(mixed.astype(BF16), wo_ref[...], preferred_element_type=F32)


def _mix(o_list, lse_list, proj, gates, x2d, head_expand, wa, wc, wo, conv_w, seq, tm=512):
    T, D = x2d.shape
    W = ATTN_WIDTH
    row = lambda width, col: pl.BlockSpec((tm, width), lambda i: (i, col))
    halo = lambda col: pl.BlockSpec(
        (CONV_HALO, D), lambda i: (jnp.maximum(i * (tm // CONV_HALO) - 1, 0), col))
    full = lambda a: pl.BlockSpec(a.shape, lambda i: (0,) * a.ndim)
    return pl.pallas_call(
        functools.partial(_mix_kernel, tm=tm, seq=seq),
        grid=(T // tm,),
        in_specs=[row(W, 0)] * 3 + [row(LANES, 0)] * 3
                 + [row(D, 0), row(D, 1), row(D, 2), halo(1), halo(2)]
                 + [row(D, 0), row(D, 1), row(D, 0)]
                 + [full(head_expand), full(wa), full(wc), full(wo), full(conv_w)],
        out_specs=pl.BlockSpec((tm, D), lambda i: (i, 0)),
        out_shape=jax.ShapeDtypeStruct((T, D), F32),
        compiler_params=pltpu.CompilerParams(
            dimension_semantics=("parallel",), vmem_limit_bytes=VMEM_LIMIT),
        name="branch_mix",
    )(*o_list, *lse_list, proj, proj, proj, proj, proj, gates, gates, x2d,
      head_expand, wa, wc, wo, conv_w)


def _cross_kernel(h_ref, k_ref, v_ref, g2_ref, wq_ref, wo_ref, g3_ref, rw_ref, rb_ref, tri_ref,
                  h2_ref, u3_ref, meta_ref, wts_ref, cnt_ref, run_scr, *, tm):
    @pl.when(pl.program_id(0) == 0)
    def _():
        run_scr[...] = jnp.zeros_like(run_scr)

    h = h_ref[...]
    u = _rms(h, g2_ref[...]).astype(BF16)
    q = jnp.dot(u, wq_ref[...], preferred_element_type=F32) * (MEM_HEAD_DIM ** -0.5)
    heads = []
    for hd in range(MEM_HEADS):
        cols = slice(hd * MEM_HEAD_DIM, (hd + 1) * MEM_HEAD_DIM)
        s = lax.dot_general(q[:, cols].astype(BF16), k_ref[:, cols], (((1,), (1,)), ((), ())),
                            preferred_element_type=F32)
        m = jnp.max(s, axis=1, keepdims=True)
        p = jnp.exp(s - m)
        l = jnp.sum(p, axis=1, keepdims=True)
        heads.append(jnp.dot(p.astype(BF16), v_ref[:, cols], preferred_element_type=F32) * (1.0 / l))
    o = jnp.concatenate(heads, axis=1).astype(BF16)
    h2 = h + jnp.dot(o, wo_ref[...], preferred_element_type=F32)
    h2_ref[...] = h2
    u3 = _rms(h2, g3_ref[...])
    _pack_rows(u3, u3_ref, tm)

    logits = lax.dot_general(rw_ref[...], u3.astype(BF16), (((1,), (1,)), ((), ())),
                             preferred_element_type=F32) + rb_ref[:, 0:1]
    expert = lax.broadcasted_iota(I32, (N_EXPERTS, tm), 0).astype(F32)
    work = logits
    picked = jnp.zeros((N_EXPERTS, tm), F32)
    tops, idxs, sels = [], [], []
    for _ in range(TOP_K):
        mk = jnp.max(work, axis=0, keepdims=True)
        ik = jnp.min(jnp.where(work == mk, expert, float(N_EXPERTS)), axis=0, keepdims=True)
        sel = expert == ik
        work = jnp.where(sel, -jnp.inf, work)
        picked = picked + sel.astype(F32)
        tops.append(mk)
        idxs.append(ik)
        sels.append(sel)
    exps = [jnp.exp(t - tops[0]) for t in tops]
    inv = 1.0 / (exps[0] + exps[1] + exps[2] + exps[3])
    wts = [e * inv for e in exps]

    before = jnp.dot(picked.astype(BF16), tri_ref[...], preferred_element_type=F32) + run_scr[:, 0:1]
    ranks = [jnp.sum(jnp.where(sel, before, 0.0), axis=0, keepdims=True) for sel in sels]

    row = lax.broadcasted_iota(I32, (META_ROWS, tm), 0)
    meta = jnp.zeros((META_ROWS, tm), F32)
    for r, v in enumerate(idxs + ranks + wts):
        meta = jnp.where(row == r, v, meta)
    meta_ref[...] = meta
    row = lax.broadcasted_iota(I32, (LANES, tm), 0)
    wpad = jnp.zeros((LANES, tm), F32)
    for k in range(TOP_K):
        wpad = jnp.where(row == k, wts[k], wpad)
    wts_ref[...] = wpad.T
    run_scr[...] = run_scr[...] + jnp.sum(picked, axis=1, keepdims=True)
    cnt_ref[...] = run_scr[...]


def _cross_router(h1, kv, g2, wq, wo, g3, rw_t, rb, seq, tm=512):
    T, D = h1.shape
    n_mem = kv.shape[1]
    tri = (jnp.arange(tm)[:, None] < jnp.arange(tm)[None, :]).astype(BF16)
    full = lambda a: pl.BlockSpec(a.shape, lambda i: (0,) * a.ndim)
    kvspec = lambda col: pl.BlockSpec((None, n_mem, MEM_WIDTH), lambda i: ((i * tm) // seq, 0, col))
    tile = lambda width: pl.BlockSpec((tm, width), lambda i: (i, 0))
    return pl.pallas_call(
        functools.partial(_cross_kernel, tm=tm),
        grid=(T // tm,),
        in_specs=[tile(D), kvspec(0), kvspec(1), full(g2), full(wq), full(wo), full(g3),
                  full(rw_t), full(rb), full(tri)],
        out_specs=[tile(D), pl.BlockSpec((tm * PACK_ROWS, LANES), lambda i: (i, 0)),
                   pl.BlockSpec((META_ROWS, tm), lambda i: (0, i)), tile(LANES),
                   pl.BlockSpec((N_EXPERTS, LANES), lambda i: (0, 0))],
        out_shape=[jax.ShapeDtypeStruct((T, D), F32), jax.ShapeDtypeStruct((T * PACK_ROWS, LANES), F32),
                   jax.ShapeDtypeStruct((META_ROWS, T), F32), jax.ShapeDtypeStruct((T, LANES), F32),
                   jax.ShapeDtypeStruct((N_EXPERTS, LANES), F32)],
        scratch_shapes=[pltpu.VMEM((N_EXPERTS, LANES), F32)],
        compiler_params=pltpu.CompilerParams(
            dimension_semantics=("arbitrary",), vmem_limit_bytes=VMEM_LIMIT),
        name="cross_router",
    )(h1, kv, kv, g2, wq, wo, g3, rw_t, rb, tri)


def _plan_kernel(pstart_ref, pad_end_ref, fill_base_ref, meta_ref, pos_ref, fill_ref):
    ei = meta_ref[0:2 * TOP_K, :].astype(I32)
    start = jnp.zeros(ei.shape, I32)
    for j in range(N_EXPERTS):
        start = jnp.where(ei == j, pstart_ref[j], start)
    pos_ref[...] = start + pltpu.roll(ei, TOP_K, axis=0)

    rows, cols = fill_ref.shape
    f = lax.broadcasted_iota(I32, (rows, cols), 0) * cols + lax.broadcasted_iota(I32, (rows, cols), 1)
    base = jnp.full((rows, cols), fill_base_ref[0], I32)
    for j in range(1, N_EXPERTS + 1):
        base = jnp.where(f >= pad_end_ref[j - 1], fill_base_ref[j], base)
    fill_ref[...] = base + f


def _plan(meta, pstart, pad_end, fill_base, n_fill):
    T = meta.shape[1]
    rows = 2 * TOP_K
    assert n_fill % (rows * LANES) == 0
    grid_spec = pltpu.PrefetchScalarGridSpec(
        num_scalar_prefetch=3,
        grid=(1,),
        in_specs=[pl.BlockSpec(meta.shape, lambda i, a, b, c: (0, 0))],
        out_specs=[pl.BlockSpec((rows, T), lambda i, a, b, c: (0, 0)),
                   pl.BlockSpec((rows, n_fill // rows), lambda i, a, b, c: (0, 0))],
    )
    pos, fill = pl.pallas_call(
        _plan_kernel,
        grid_spec=grid_spec,
        out_shape=[jax.ShapeDtypeStruct((rows, T), I32), jax.ShapeDtypeStruct((rows, n_fill // rows), I32)],
        compiler_params=pltpu.CompilerParams(dimension_semantics=("arbitrary",)),
        name="row_plan",
    )(pstart, pad_end, fill_base, meta)
    return pos[:TOP_K], fill.reshape(-1)


def _wait_copies(src_hbm, dst, sem, rows):
    pltpu.make_async_copy(src_hbm.at[pl.ds(0, rows), :], dst.at[pl.ds(0, rows), :], sem).wait()


def _dispatch_kernel(pos_ref, u_hbm, x_hbm, stage, load_sem, row_sem, *, tm, n_token_steps):
    i = pl.program_id(0)
    n = pl.num_programs(0)
    batch = tm * TOP_K * PACK_ROWS
    tile_rows = tm * PACK_ROWS

    def tile_copy(step):
        first = pl.multiple_of(jnp.minimum(step, n_token_steps - 1) * tile_rows, tile_rows)
        return pltpu.make_async_copy(u_hbm.at[pl.ds(first, tile_rows), :], stage.at[step % 3],
                                     load_sem.at[step % 3])

    @pl.when(i == 0)
    def _():
        tile_copy(i).start()

    @pl.when(i + 1 < n)
    def _():
        tile_copy(i + 1).start()

    tile_copy(i).wait()
    src_tile = stage.at[i % 3]
    group = 16

    for k in range(TOP_K):
        def body(g, carry, k=k):
            rows = [pos_ref[0, k * tm + g * group + j] for j in range(group)]
            for j, p in enumerate(rows):
                src = src_tile.at[pl.ds(pl.multiple_of((g * group + j) * PACK_ROWS, PACK_ROWS), PACK_ROWS), :]
                dst = x_hbm.at[pl.ds(pl.multiple_of(p * PACK_ROWS, PACK_ROWS), PACK_ROWS), :]
                pltpu.make_async_copy(src, dst, row_sem.at[i % 2]).start(priority=j % 2)
            return carry

        lax.fori_loop(0, tm // group, body, 0)

    @pl.when(i > 0)
    def _():
        _wait_copies(u_hbm, x_hbm, row_sem.at[1 - i % 2], batch)

    @pl.when(i == n - 1)
    def _():
        _wait_copies(u_hbm, x_hbm, row_sem.at[i % 2], batch)


def _dispatch(u_packed, pos_kmajor, fill_rows, tm):
    T = pos_kmajor.shape[0] * tm
    per_step = TOP_K * tm
    dst_rows = jnp.concatenate([pos_kmajor.reshape(-1), fill_rows])
    n_rows = dst_rows.shape[0]
    assert n_rows % per_step == 0
    steps = n_rows // per_step
    return pl.pallas_call(
        functools.partial(_dispatch_kernel, tm=tm, n_token_steps=T // tm),
        grid=(steps,),
        in_specs=[pl.BlockSpec((None, 1, per_step), lambda i: (i, 0, 0), memory_space=pltpu.SMEM),
                  pl.BlockSpec(memory_space=pl.ANY)],
        out_specs=pl.BlockSpec(memory_space=pl.ANY),
        out_shape=jax.ShapeDtypeStruct((n_rows * PACK_ROWS, LANES), F32),
        scratch_shapes=[pltpu.VMEM((3, tm * PACK_ROWS, LANES), F32),
                        pltpu.SemaphoreType.DMA((3,)), pltpu.SemaphoreType.DMA((2,))],
        compiler_params=pltpu.CompilerParams(dimension_semantics=("arbitrary",)),
        name="dispatch_rows",
    )(dst_rows.reshape(steps, 1, per_step), u_packed)


def _expert_kernel(bexp_ref, slot_ref, next_ref, nused_ref, x_ref, wg_hbm, bg_ref, wu_hbm, bu_ref,
                   wd_hbm, bd_ref, y_ref, stage, w_bf, sem):
    weights = (wg_hbm, wu_hbm, wd_hbm)
    block_rows = MOE_ROWS * PACK_ROWS

    def fetch(expert, into):
        return [pltpu.make_async_copy(w.at[expert], stage.at[into, j], sem.at[into])
                for j, w in enumerate(weights)]

    for sub in range(MOE_STEP_BLOCKS):
        i = pl.program_id(0) * MOE_STEP_BLOCKS + sub
        used = i < nused_ref[0]
        e = bexp_ref[i]
        slot = slot_ref[i]
        changed = jnp.logical_or(i == 0, e != bexp_ref[jnp.maximum(i - 1, 0)])
        y_rows = y_ref.at[pl.ds(sub * block_rows, block_rows), :]

        @pl.when(jnp.logical_and(changed, used))
        def _():
            @pl.when(i == 0)
            def _():
                for c in fetch(e, slot):
                    c.start()

            for c in fetch(e, slot):
                c.wait()
            for s in range(2):
                @pl.when(slot == s)
                def _():
                    for j in range(len(weights)):
                        w_bf[j] = stage[s, j].astype(BF16)

            @pl.when(next_ref[i] >= 0)
            def _():
                for c in fetch(next_ref[i], 1 - slot):
                    c.start(priority=1)

        @pl.when(used)
        def _():
            x = _unpack_rows(x_ref, sub * block_rows, MOE_ROWS).astype(BF16)
            gate = jnp.minimum(jnp.dot(x, w_bf[0], preferred_element_type=F32) + bg_ref[e], SWIGLU_LIMIT)
            lin = jnp.clip(jnp.dot(x, w_bf[1], preferred_element_type=F32) + bu_ref[e],
                           -SWIGLU_LIMIT, SWIGLU_LIMIT)
            hdn = gate * jax.nn.sigmoid(SWIGLU_ALPHA * gate) * (lin + 1.0)
            y = jnp.dot(hdn.astype(BF16), w_bf[2], preferred_element_type=F32) + bd_ref[e]
            _pack_rows(y, y_rows, MOE_ROWS)

        @pl.when(jnp.logical_not(used))
        def _():
            y_rows[...] = jnp.zeros(y_rows.shape, y_rows.dtype)


def _experts(x_packed, block_expert, block_slot, block_next, n_used, w_eg, b_eg, w_eu, b_eu, w_ed, b_ed):
    nb = block_expert.shape[0]
    assert nb % MOE_STEP_BLOCKS == 0
    E, D, F = w_eg.shape
    assert D == F
    step_rows = MOE_STEP_BLOCKS * MOE_ROWS * PACK_ROWS
    rows = lambda i, be, sl, nx, nu: (i, 0)
    bspec = lambda n: pl.BlockSpec((E, 1, n), lambda i, be, sl, nx, nu: (0, 0, 0))
    anyspec = pl.BlockSpec(memory_space=pl.ANY)
    grid_spec = pltpu.PrefetchScalarGridSpec(
        num_scalar_prefetch=4,
        grid=(nb // MOE_STEP_BLOCKS,),
        in_specs=[pl.BlockSpec((step_rows, LANES), rows),
                  anyspec, bspec(F), anyspec, bspec(F), anyspec, bspec(D)],
        out_specs=pl.BlockSpec((step_rows, LANES), rows),
        scratch_shapes=[pltpu.VMEM((2, 3, D, F), F32), pltpu.VMEM((3, D, F), BF16),
                        pltpu.SemaphoreType.DMA((2,))],
    )
    return pl.pallas_call(
        _expert_kernel,
        grid_spec=grid_spec,
        out_shape=jax.ShapeDtypeStruct(x_packed.shape, F32),
        compiler_params=pltpu.CompilerParams(
            dimension_semantics=("arbitrary",), vmem_limit_bytes=VMEM_LIMIT),
        name="expert_ffn",
    )(block_expert, block_slot, block_next, n_used, x_packed, w_eg, b_eg.reshape(E, 1, F), w_eu,
      b_eu.reshape(E, 1, F), w_ed, b_ed.reshape(E, 1, D))


def _gather_packed(pos_ref, src_hbm, dst, sem, n):
    group = 16

    def body(g, carry):
        rows = [pos_ref[0, g * group + j] for j in range(group)]
        for j, p in enumerate(rows):
            r = g * group + j
            pltpu.make_async_copy(
                src_hbm.at[pl.ds(pl.multiple_of(p * PACK_ROWS, PACK_ROWS), PACK_ROWS), :],
                dst.at[pl.ds(pl.multiple_of(r * PACK_ROWS, PACK_ROWS), PACK_ROWS), :],
                sem).start(priority=j % 2)
        return carry
    lax.fori_loop(0, n // group, body, 0)


def _combine_kernel(pos0_ref, pos1_ref, y_hbm, h_ref, w_ref, g_ref, o_ref, ybuf, sem):
    i = pl.program_id(0)
    n = pl.num_programs(0)
    slot = i % 2
    rows = TOP_K * COMBINE_ROWS

    @pl.when(i == 0)
    def _():
        _gather_packed(pos0_ref, y_hbm, ybuf.at[0], sem.at[0], rows)

    @pl.when(i + 1 < n)
    def _():
        _gather_packed(pos1_ref, y_hbm, ybuf.at[1 - slot], sem.at[1 - slot], rows)

    _wait_copies(y_hbm, ybuf.at[slot], sem.at[slot], rows * PACK_ROWS)
    acc = h_ref[...]
    w = w_ref[...]
    for k in range(TOP_K):
        acc = acc + w[:, k:k + 1] * _unpack_rows(ybuf.at[slot], k * COMBINE_ROWS * PACK_ROWS, COMBINE_ROWS)
    o_ref[...] = _rms(acc, g_ref[...])


def _combine(y_rows, pos_kmajor, h2, wts, g_final):
    T, D = h2.shape
    tm = COMBINE_ROWS
    nt = T // tm
    posspec = lambda off: pl.BlockSpec(
        (None, 1, TOP_K * tm), lambda i: (jnp.minimum(i + off, nt - 1), 0, 0), memory_space=pltpu.SMEM)
    return pl.pallas_call(
        _combine_kernel,
        grid=(nt,),
        in_specs=[posspec(0), posspec(1), pl.BlockSpec(memory_space=pl.ANY),
                  pl.BlockSpec((tm, D), lambda i: (i, 0)), pl.BlockSpec((tm, LANES), lambda i: (i, 0)),
                  pl.BlockSpec((1, D), lambda i: (0, 0))],
        out_specs=pl.BlockSpec((tm, D), lambda i: (i, 0)),
        out_shape=jax.ShapeDtypeStruct((T, D), F32),
        scratch_shapes=[pltpu.VMEM((2, TOP_K * tm * PACK_ROWS, LANES), F32), pltpu.SemaphoreType.DMA((2,))],
        compiler_params=pltpu.CompilerParams(
            dimension_semantics=("arbitrary",), vmem_limit_bytes=VMEM_LIMIT),
        name="combine_norm",
    )(pos_kmajor, pos_kmajor, y_rows, h2, wts, g_final.reshape(1, D))


def _layer(h, mem, norm_mix, w_in, conv_w, w_branch_attn, w_branch_conv, w_gate, b_gate, w_out,
           norm_cross, norm_mem, w_cq, w_ckv, w_co, norm_moe, router_w, router_b,
           w_eg, b_eg, w_eu, b_eu, w_ed, b_ed, norm_final):
    B, S, D = h.shape
    T = B * S
    W3 = 3 * ATTN_WIDTH
    n_attn = len(ATTN_GROUPS) * W3

    assert ATTN_GROUPS[0][1] == 1
    w_in_b = w_in.astype(BF16)
    proj, qkv1, gates = _inproj(
        h, norm_mix, [(w_in_b[:, n_attn:], None), (w_in_b[:, :W3], None), (w_gate.astype(BF16), b_gate)],
        1, 512, 512, name="inproj_conv_g1_gates")
    proj2d = proj.reshape(T, -1)
    gates2d = gates.reshape(T, -1)

    o_list, lse_list = [], []
    for g, (window, dil) in enumerate(ATTN_GROUPS):
        assert window // dil == Q_BLOCK
        if g == 0:
            qkv = qkv1
        else:
            qkv, = _inproj(h, norm_mix, [(w_in_b[:, g * W3:(g + 1) * W3], None)], dil, 1024, 512,
                           name=f"inproj_g{g + 1}")
        o, lse = _attention(qkv, dil, name=f"dilated_attn_g{g + 1}")
        o_list.append(o)
        lse_list.append(lse)

    head_expand = (jnp.arange(LANES)[:, None] == jnp.arange(ATTN_WIDTH)[None, :] // HEAD_DIM).astype(BF16)
    h1 = _mix(o_list, lse_list, proj2d, gates2d, h.reshape(T, D), head_expand,
              w_branch_attn.astype(BF16), w_branch_conv.astype(BF16), w_out.astype(BF16), conv_w, S)

    kv = _mem_kv(mem, norm_mem, w_ckv.astype(BF16))
    rb = jnp.broadcast_to(router_b[:, None], (N_EXPERTS, LANES))
    h2, u3, meta, wts, counts = _cross_router(
        h1, kv, norm_cross.reshape(1, D), w_cq.astype(BF16), w_co.astype(BF16), norm_moe.reshape(1, D),
        router_w.T.astype(BF16), rb, S)

    nb = -(-(T * TOP_K) // MOE_ROWS) + N_EXPERTS
    cnt = counts[:, 0].astype(I32)
    padded = (cnt + MOE_ROWS - 1) // MOE_ROWS * MOE_ROWS
    pend = jnp.cumsum(padded)
    pstart = pend - padded
    experts = jnp.arange(N_EXPERTS, dtype=I32)
    n_pad = padded - cnt
    pad_end = jnp.cumsum(n_pad)
    fill_base = jnp.concatenate([pstart + cnt - (pad_end - n_pad), pend[-1:] - pad_end[-1:]])
    pos_kt, fill_rows = _plan(meta, pstart, pad_end, fill_base, nb * MOE_ROWS - T * TOP_K)

    tm = COMBINE_ROWS
    assert tm & (tm - 1) == 0 and T % tm == 0
    pos_kmajor = pos_kt.reshape(TOP_K, T // tm, tm).transpose(1, 0, 2).reshape(T // tm, 1, TOP_K * tm)
    block_start = jnp.arange(nb, dtype=I32) * MOE_ROWS
    block_expert = jnp.minimum(jnp.sum(block_start[:, None] >= pend[None, :], axis=1), N_EXPERTS - 1).astype(I32)
    active = cnt > 0
    slot_e = (jnp.cumsum(active.astype(I32)) - 1) % 2
    later = jnp.where(active[None, :] & (experts[None, :] > experts[:, None]), experts[None, :], N_EXPERTS)
    next_e = jnp.min(later, axis=1)
    next_e = jnp.where(next_e == N_EXPERTS, -1, next_e).astype(I32)
    x_rows = _dispatch(u3, pos_kmajor, fill_rows, tm)
    block_is = block_expert[:, None] == experts[None, :]
    per_block = lambda table: jnp.sum(jnp.where(block_is, table[None, :], 0), axis=1).astype(I32)
    y_rows = _experts(x_rows, block_expert, per_block(slot_e), per_block(next_e),
                      (pend[-1:] // MOE_ROWS).astype(I32), w_eg, b_eg, w_eu, b_eu, w_ed, b_ed)

    out = _combine(y_rows, pos_kmajor, h2, wts, norm_final)
    return out.reshape(B, S, D)


def kernel(x, mem, norm_mix, w_in, conv_w, w_branch_attn, w_branch_conv, w_gate, b_gate, w_out, norm_cross, norm_mem, w_cq, w_ckv, w_co, norm_moe, router_w, router_b, w_exp_gate, b_exp_gate, w_exp_up, b_exp_up, w_exp_down, b_exp_down, norm_final):
    depth = norm_mix.shape[0]
    assert depth == 1, "the final norm is fused into the last layer's combine step"
    return _layer(x, mem, norm_mix[0], w_in[0], conv_w[0], w_branch_attn[0], w_branch_conv[0], w_gate[0],
                  b_gate[0], w_out[0], norm_cross[0], norm_mem[0], w_cq[0], w_ckv[0], w_co[0], norm_moe[0],
                  router_w[0], router_b[0], w_exp_gate[0], b_exp_gate[0], w_exp_up[0], b_exp_up[0],
                  w_exp_down[0], b_exp_down[0], norm_final)
```

```python
import functools

import jax
import jax.numpy as jnp
import numpy as np
from jax import lax
from jax.experimental import pallas as pl
from jax.experimental.pallas import tpu as pltpu

F32 = jnp.float32
BF16 = jnp.bfloat16
I32 = jnp.int32

D_MODEL = 1024
ATTN_GROUPS = ((128, 1), (512, 4), (2048, 16))
ATTN_HEADS = 8
HEAD_DIM = 64
ATTN_WIDTH = ATTN_HEADS * HEAD_DIM
Q_BLOCK = 128
CONV_K = 3
MEM_HEADS = 4
MEM_HEAD_DIM = 128
MEM_WIDTH = MEM_HEADS * MEM_HEAD_DIM
N_EXPERTS = 32
TOP_K = 4
SWIGLU_LIMIT = 7.0
SWIGLU_ALPHA = 1.702
EPS = 1e-6
NEG_INF = -1e30

LANES = 128
VMEM_LIMIT = 56 * 1024 * 1024

MOE_ROWS = 256
MOE_STEP_BLOCKS = 4
COMBINE_ROWS = 512
CONV_HALO = 16
META_ROWS = 16
REGROUP_STRIDE = 4


def _rms(x, g):
    ms = jnp.mean(x * x, axis=-1, keepdims=True)
    return x * lax.rsqrt(ms + EPS) * g


PACK_ROWS = D_MODEL // LANES


def _pack_rows(y, out_ref, n):
    for c in range(PACK_ROWS):
        out_ref[pl.ds(c, n, stride=PACK_ROWS), :] = y[:, c * LANES:(c + 1) * LANES]


def _unpack_rows(ref, start, n):
    return jnp.concatenate(
        [ref[pl.ds(start + c, n, stride=PACK_ROWS), :] for c in range(PACK_ROWS)], axis=1)


def _inproj_kernel(x_ref, g_ref, *refs, gate_bias, dil, chunk):
    n_w = len(gate_bias)
    w_refs, b_refs, o_refs = refs[:n_w], refs[n_w:2 * n_w], refs[2 * n_w:3 * n_w]
    u_scr = refs[3 * n_w]
    tm = x_ref.shape[0]
    per = tm // dil
    u = _rms(x_ref[...], g_ref[...])
    if dil == 1:
        u_scr[...] = u.astype(BF16)
    else:
        slab = refs[3 * n_w + 1]
        n_slab = u.shape[1] // LANES
        for c in range(n_slab):
            slab[c] = u[:, c * LANES:(c + 1) * LANES]
        if dil <= REGROUP_STRIDE:
            for r in range(dil):
                for c in range(n_slab):
                    u_scr[r * per:(r + 1) * per, c * LANES:(c + 1) * LANES] = (
                        slab[c, pl.ds(r, per, stride=dil), :].astype(BF16))
        else:
            slab2 = refs[3 * n_w + 2]
            d1, d2 = REGROUP_STRIDE, dil // REGROUP_STRIDE
            group = tm // d1
            for r1 in range(d1):
                for c in range(n_slab):
                    slab2[c, r1 * group:(r1 + 1) * group, :] = slab[c, pl.ds(r1, group, stride=d1), :]
            for r1 in range(d1):
                for r2 in range(d2):
                    r = r1 + d1 * r2
                    for c in range(n_slab):
                        u_scr[r * per:(r + 1) * per, c * LANES:(c + 1) * LANES] = (
                            slab2[c, pl.ds(r1 * group + r2, per, stride=d2), :].astype(BF16))
    ub = u_scr[...]
    for w_ref, b_ref, o_ref, sig in zip(w_refs, b_refs, o_refs, gate_bias):
        for c in range(w_ref.shape[1] // chunk):
            cols = slice(c * chunk, (c + 1) * chunk)
            acc = jnp.dot(ub, w_ref[:, cols], preferred_element_type=F32)
            if sig:
                acc = jax.nn.sigmoid(acc + b_ref[:, cols])
            for r in range(dil):
                o_ref[r, :, cols] = acc[r * per:(r + 1) * per].astype(o_ref.dtype)


def _inproj(x, gain, weights, dil, tm, chunk, name):
    B, S, D = x.shape
    ws = [w for w, _ in weights]
    bs = [jnp.zeros((1, w.shape[1]), F32) if b is None else b.reshape(1, -1) for w, b in weights]
    const = lambda a: pl.BlockSpec(a.shape, lambda b, i: (0, 0))
    scratch = [pltpu.VMEM((tm, D), BF16)]
    if dil > 1:
        scratch.append(pltpu.VMEM((D // LANES, tm, LANES), F32))
    if dil > REGROUP_STRIDE:
        assert dil % REGROUP_STRIDE == 0 and dil // REGROUP_STRIDE <= REGROUP_STRIDE
        scratch.append(pltpu.VMEM((D // LANES, tm, LANES), F32))
    return pl.pallas_call(
        functools.partial(_inproj_kernel, gate_bias=tuple(b is not None for _, b in weights),
                          dil=dil, chunk=chunk),
        grid=(B, S // tm),
        in_specs=[pl.BlockSpec((None, tm, D), lambda b, i: (b, i, 0)), const(gain.reshape(1, D))]
                 + [const(w) for w in ws] + [const(b) for b in bs],
        out_specs=[pl.BlockSpec((None, dil, tm // dil, w.shape[1]), lambda b, i: (b, 0, i, 0)) for w in ws],
        out_shape=[jax.ShapeDtypeStruct((B, dil, S // dil, w.shape[1]), BF16) for w in ws],
        scratch_shapes=scratch,
        compiler_params=pltpu.CompilerParams(
            dimension_semantics=("parallel", "parallel"), vmem_limit_bytes=VMEM_LIMIT),
        name=name,
    )(x, gain.reshape(1, D), *ws, *bs)


def _mem_kv_kernel(x_ref, g_ref, w_ref, o_ref):
    u = _rms(x_ref[...], g_ref[...]).astype(BF16)
    o_ref[...] = jnp.dot(u, w_ref[...], preferred_element_type=F32).astype(o_ref.dtype)


def _mem_kv(mem, gain, w):
    B, M, D = mem.shape
    N = w.shape[1]
    return pl.pallas_call(
        _mem_kv_kernel,
        grid=(B,),
        in_specs=[pl.BlockSpec((None, M, D), lambda b: (b, 0, 0)),
                  pl.BlockSpec((1, D), lambda b: (0, 0)),
                  pl.BlockSpec((D, N), lambda b: (0, 0))],
        out_specs=pl.BlockSpec((None, M, N), lambda b: (b, 0, 0)),
        out_shape=jax.ShapeDtypeStruct((B, M, N), BF16),
        compiler_params=pltpu.CompilerParams(
            dimension_semantics=("parallel",), vmem_limit_bytes=VMEM_LIMIT),
        name="mem_kv",
    )(mem, gain.reshape(1, D), w)


def _attn_block(q_ref, kp_ref, kc_ref, vp_ref, vc_ref, tab_ref, first, r, j):
    lane = lax.broadcasted_iota(I32, (Q_BLOCK, LANES), 1)
    low = lane < HEAD_DIM
    lse_tile = jnp.zeros((Q_BLOCK, LANES), F32)
    pairs = []
    if j > 0:
        first = 1
    for pair in range(ATTN_HEADS // 2):
        cols = slice(pair * LANES, (pair + 1) * LANES)
        q2 = q_ref[r, j * Q_BLOCK:(j + 1) * Q_BLOCK, cols] * (HEAD_DIM ** -0.5)
        if j == 0:
            k2 = jnp.concatenate([kp_ref[r, :, cols], kc_ref[r, :Q_BLOCK, cols]], axis=0)
            v2 = jnp.concatenate([vp_ref[r, :, cols], vc_ref[r, :Q_BLOCK, cols]], axis=0)
        else:
            k2 = kc_ref[r, (j - 1) * Q_BLOCK:(j + 1) * Q_BLOCK, cols]
            v2 = vc_ref[r, (j - 1) * Q_BLOCK:(j + 1) * Q_BLOCK, cols]
        outs = []
        for half in range(2):
            h = 2 * pair + half
            keep = low if half == 0 else jnp.logical_not(low)
            qm = jnp.where(keep, q2, jnp.zeros_like(q2))
            s = lax.dot_general(qm, k2, (((1,), (1,)), ((), ())), preferred_element_type=F32)
            s = s + tab_ref[first, h]
            m = jnp.max(s, axis=1, keepdims=True)
            p = jnp.exp(s - m)
            l = jnp.sum(p, axis=1, keepdims=True)
            o = jnp.dot(p.astype(BF16), v2, preferred_element_type=F32) * (1.0 / l)
            outs.append(o)
            lse_tile = jnp.where(lane == h, m + jnp.log(l), lse_tile)
        pairs.append(jnp.where(low, outs[0], outs[1]))
    return pairs, lse_tile


def _attn_kernel(q_ref, kp_ref, kc_ref, vp_ref, vc_ref, tab_ref, o_ref, lse_ref, *scratch, dil, qb):
    first = jnp.minimum(pl.program_id(1), 1)
    blocks = (q_ref, kp_ref, kc_ref, vp_ref, vc_ref, tab_ref)
    n_pair = ATTN_HEADS // 2
    if dil == 1:
        for j in range(qb):
            pairs, lse_tile = _attn_block(*blocks, first, 0, j)
            rows = slice(j * Q_BLOCK, (j + 1) * Q_BLOCK)
            for p in range(n_pair):
                o_ref[rows, p * LANES:(p + 1) * LANES] = pairs[p].astype(o_ref.dtype)
            lse_ref[rows, :] = lse_tile
        return

    o_scr, lse_scr = scratch

    unroll = min(dil, 8)

    def body(g, carry):
        for r in [g * unroll + t for t in range(unroll)]:
            for j in range(qb):
                pairs, lse_tile = _attn_block(*blocks, first, r, j)
                rows = pl.ds(j * Q_BLOCK * dil + r, Q_BLOCK, stride=dil)
                for p in range(n_pair):
                    o_scr[p, rows, :] = pairs[p]
                lse_scr[rows, :] = lse_tile
        return carry

    if dil == unroll:
        body(0, 0)
    else:
        lax.fori_loop(0, dil // unroll, body, 0)
    for p in range(n_pair):
        o_ref[:, p * LANES:(p + 1) * LANES] = o_scr[p].astype(o_ref.dtype)
    lse_ref[...] = lse_scr[...]


def _attn_bias_table(dil):
    slopes = np.power(2.0, -8.0 * np.arange(1, ATTN_HEADS + 1, dtype=np.float32) / ATTN_HEADS)
    iq = np.arange(Q_BLOCK)
    ik = np.arange(2 * Q_BLOCK)
    dist = iq[:, None] + Q_BLOCK - ik[None, :]
    band = (dist >= 0) & (dist <= Q_BLOCK)
    has_prev = np.stack([ik >= Q_BLOCK, np.ones_like(ik, dtype=bool)])
    mask = band[None] & has_prev[:, None, :]
    bias = -slopes[:, None, None] * (dist * dil).astype(np.float32)[None]
    return np.where(mask[:, None], bias[None], np.float32(NEG_INF)).astype(np.float32)


def _attention(qkv, dil, name):
    B, _, L, _ = qkv.shape
    W = ATTN_WIDTH
    qb = max(1, 8 // dil)
    nb = L // (Q_BLOCK * qb)
    span = Q_BLOCK * qb * dil
    blk = lambda part, prev: (
        pl.BlockSpec((None, dil, Q_BLOCK, W), lambda b, n: (b, 0, jnp.maximum(n * qb - 1, 0), part))
        if prev else pl.BlockSpec((None, dil, Q_BLOCK * qb, W), lambda b, n: (b, 0, n, part)))
    tab = _attn_bias_table(dil)
    scratch = [] if dil == 1 else [pltpu.VMEM((ATTN_HEADS // 2, span, LANES), F32),
                                   pltpu.VMEM((span, LANES), F32)]
    o, lse = pl.pallas_call(
        functools.partial(_attn_kernel, dil=dil, qb=qb),
        grid=(B, nb),
        in_specs=[blk(0, False), blk(1, True), blk(1, False), blk(2, True), blk(2, False),
                  pl.BlockSpec(tab.shape, lambda b, n: (0, 0, 0, 0))],
        out_specs=[pl.BlockSpec((None, span, W), lambda b, n: (b, n, 0)),
                   pl.BlockSpec((None, span, LANES), lambda b, n: (b, n, 0))],
        out_shape=[jax.ShapeDtypeStruct((B, L * dil, W), BF16),
                   jax.ShapeDtypeStruct((B, L * dil, LANES), F32)],
        scratch_shapes=scratch,
        compiler_params=pltpu.CompilerParams(
            dimension_semantics=("parallel", "parallel"), vmem_limit_bytes=VMEM_LIMIT),
        name=name,
    )(qkv, qkv, qkv, qkv, qkv, tab)
    return o.reshape(B * L * dil, W), lse.reshape(B * L * dil, LANES)


def _mix_kernel(o1_ref, o2_ref, o3_ref, l1_ref, l2_ref, l3_ref, gb_ref, gc_ref, xc_ref, gch_ref, xch_ref,
                ga_ref, gv_ref, x_ref, e_ref, wa_ref, wc_ref, wo_ref, cw_ref, h_ref, *, tm, seq):
    a1, a2, a3 = l1_ref[...], l2_ref[...], l3_ref[...]
    m = jnp.maximum(jnp.maximum(a1, a2), a3)
    e1, e2, e3 = jnp.exp(a1 - m), jnp.exp(a2 - m), jnp.exp(a3 - m)
    inv = 1.0 / (e1 + e2 + e3)
    y = jnp.zeros((tm, ATTN_WIDTH), F32)
    for e, o_ref in ((e1, o1_ref), (e2, o2_ref), (e3, o3_ref)):
        w = e * inv
        w_hi = w.astype(BF16)
        w_lo = (w - w_hi.astype(F32)).astype(BF16)
        w_heads = (jnp.dot(w_hi, e_ref[...], preferred_element_type=F32)
                   + jnp.dot(w_lo, e_ref[...], preferred_element_type=F32))
        y = y + w_heads * o_ref[...].astype(F32)
    branch_a = jnp.dot(y.astype(BF16), wa_ref[...], preferred_element_type=F32)

    at_start = (pl.program_id(0) * tm) % seq == 0
    halo = gch_ref[...].astype(F32) * xch_ref[...].astype(F32)
    halo = jnp.where(at_start, jnp.zeros_like(halo), halo)[CONV_HALO - 8:, :]
    z = gc_ref[...].astype(F32) * xc_ref[...].astype(F32)
    row = lax.broadcasted_iota(I32, (8, z.shape[1]), 0)
    taps = []
    for shift in (2, 1):
        zs = pltpu.roll(z, shift, axis=0)
        head = jnp.where(row < shift, pltpu.roll(halo, shift, axis=0), zs[0:8, :])
        taps.append(jnp.concatenate([head, zs[8:, :]], axis=0))
    conv = cw_ref[0:1, :] * taps[0] + cw_ref[1:2, :] * taps[1] + cw_ref[2:3, :] * z
    y_conv = gb_ref[...].astype(F32) * conv
    branch_c = jnp.dot(y_conv.astype(BF16), wc_ref[...], preferred_element_type=F32)

    mixed = ga_ref[...].astype(F32) * branch_a + gv_ref[...].astype(F32) * branch_c
    h_ref[...] = x_ref[...] + jnp.dot(mixed.astype(BF16), wo_ref[...], preferred_element_type=F32)


def _mix(o_list, lse_list, proj, gates, x2d, head_expand, wa, wc, wo, conv_w, seq, tm=512):
    T, D = x2d.shape
    W = ATTN_WIDTH
    row = lambda width, col: pl.BlockSpec((tm, width), lambda i: (i, col))
    halo = lambda col: pl.BlockSpec(
        (CONV_HALO, D), lambda i: (jnp.maximum(i * (tm // CONV_HALO) - 1, 0), col))
    full = lambda a: pl.BlockSpec(a.shape, lambda i: (0,) * a.ndim)
    return pl.pallas_call(
        functools.partial(_mix_kernel, tm=tm, seq=seq),
        grid=(T // tm,),
        in_specs=[row(W, 0)] * 3 + [row(LANES, 0)] * 3
                 + [row(D, 0), row(D, 1), row(D, 2), halo(1), halo(2)]
                 + [row(D, 0), row(D, 1), row(D, 0)]
                 + [full(head_expand), full(wa), full(wc), full(wo), full(conv_w)],
        out_specs=pl.BlockSpec((tm, D), lambda i: (i, 0)),
        out_shape=jax.ShapeDtypeStruct((T, D), F32),
        compiler_params=pltpu.CompilerParams(
            dimension_semantics=("parallel",), vmem_limit_bytes=VMEM_LIMIT),
        name="branch_mix",
    )(*o_list, *lse_list, proj, proj, proj, proj, proj, gates, gates, x2d,
      head_expand, wa, wc, wo, conv_w)


def _cross_kernel(h_ref, k_ref, v_ref, g2_ref, wq_ref, wo_ref, g3_ref, rw_ref, rb_ref, tri_ref,
                  h2_ref, u3_ref, meta_ref, wts_ref, cnt_ref, run_scr, *, tm):
    @pl.when(pl.program_id(0) == 0)
    def _():
        run_scr[...] = jnp.zeros_like(run_scr)

    h = h_ref[...]
    u = _rms(h, g2_ref[...]).astype(BF16)
    q = jnp.dot(u, wq_ref[...], preferred_element_type=F32) * (MEM_HEAD_DIM ** -0.5)
    heads = []
    for hd in range(MEM_HEADS):
        cols = slice(hd * MEM_HEAD_DIM, (hd + 1) * MEM_HEAD_DIM)
        s = lax.dot_general(q[:, cols].astype(BF16), k_ref[:, cols], (((1,), (1,)), ((), ())),
                            preferred_element_type=F32)
        m = jnp.max(s, axis=1, keepdims=True)
        p = jnp.exp(s - m)
        l = jnp.sum(p, axis=1, keepdims=True)
        heads.append(jnp.dot(p.astype(BF16), v_ref[:, cols], preferred_element_type=F32) * (1.0 / l))
    o = jnp.concatenate(heads, axis=1).astype(BF16)
    h2 = h + jnp.dot(o, wo_ref[...], preferred_element_type=F32)
    h2_ref[...] = h2
    u3 = _rms(h2, g3_ref[...])
    _pack_rows(u3, u3_ref, tm)

    logits = lax.dot_general(rw_ref[...], u3.astype(BF16), (((1,), (1,)), ((), ())),
                             preferred_element_type=F32) + rb_ref[:, 0:1]
    expert = lax.broadcasted_iota(I32, (N_EXPERTS, tm), 0).astype(F32)
    work = logits
    picked = jnp.zeros((N_EXPERTS, tm), F32)
    tops, idxs, sels = [], [], []
    for _ in range(TOP_K):
        mk = jnp.max(work, axis=0, keepdims=True)
        ik = jnp.min(jnp.where(work == mk, expert, float(N_EXPERTS)), axis=0, keepdims=True)
        sel = expert == ik
        work = jnp.where(sel, -jnp.inf, work)
        picked = picked + sel.astype(F32)
        tops.append(mk)
        idxs.append(ik)
        sels.append(sel)
    exps = [jnp.exp(t - tops[0]) for t in tops]
    inv = 1.0 / (exps[0] + exps[1] + exps[2] + exps[3])
    wts = [e * inv for e in exps]

    before = jnp.dot(picked.astype(BF16), tri_ref[...], preferred_element_type=F32) + run_scr[:, 0:1]
    ranks = [jnp.sum(jnp.where(sel, before, 0.0), axis=0, keepdims=True) for sel in sels]

    row = lax.broadcasted_iota(I32, (META_ROWS, tm), 0)
    meta = jnp.zeros((META_ROWS, tm), F32)
    for r, v in enumerate(idxs + ranks + wts):
        meta = jnp.where(row == r, v, meta)
    meta_ref[...] = meta
    row = lax.broadcasted_iota(I32, (LANES, tm), 0)
    wpad = jnp.zeros((LANES, tm), F32)
    for k in range(TOP_K):
        wpad = jnp.where(row == k, wts[k], wpad)
    wts_ref[...] = wpad.T
    run_scr[...] = run_scr[...] + jnp.sum(picked, axis=1, keepdims=True)
    cnt_ref[...] = run_scr[...]


def _cross_router(h1, kv, g2, wq, wo, g3, rw_t, rb, seq, tm=512):
    T, D = h1.shape
    n_mem = kv.shape[1]
    tri = (jnp.arange(tm)[:, None] < jnp.arange(tm)[None, :]).astype(BF16)
    full = lambda a: pl.BlockSpec(a.shape, lambda i: (0,) * a.ndim)
    kvspec = lambda col: pl.BlockSpec((None, n_mem, MEM_WIDTH), lambda i: ((i * tm) // seq, 0, col))
    tile = lambda width: pl.BlockSpec((tm, width), lambda i: (i, 0))
    return pl.pallas_call(
        functools.partial(_cross_kernel, tm=tm),
        grid=(T // tm,),
        in_specs=[tile(D), kvspec(0), kvspec(1), full(g2), full(wq), full(wo), full(g3),
                  full(rw_t), full(rb), full(tri)],
        out_specs=[tile(D), pl.BlockSpec((tm * PACK_ROWS, LANES), lambda i: (i, 0)),
                   pl.BlockSpec((META_ROWS, tm), lambda i: (0, i)), tile(LANES),
                   pl.BlockSpec((N_EXPERTS, LANES), lambda i: (0, 0))],
        out_shape=[jax.ShapeDtypeStruct((T, D), F32), jax.ShapeDtypeStruct((T * PACK_ROWS, LANES), F32),
                   jax.ShapeDtypeStruct((META_ROWS, T), F32), jax.ShapeDtypeStruct((T, LANES), F32),
                   jax.ShapeDtypeStruct((N_EXPERTS, LANES), F32)],
        scratch_shapes=[pltpu.VMEM((N_EXPERTS, LANES), F32)],
        compiler_params=pltpu.CompilerParams(
            dimension_semantics=("arbitrary",), vmem_limit_bytes=VMEM_LIMIT),
        name="cross_router",
    )(h1, kv, kv, g2, wq, wo, g3, rw_t, rb, tri)


def _plan_kernel(pstart_ref, pad_end_ref, fill_base_ref, meta_ref, pos_ref, fill_ref):
    ei = meta_ref[0:2 * TOP_K, :].astype(I32)
    start = jnp.zeros(ei.shape, I32)
    for j in range(N_EXPERTS):
        start = jnp.where(ei == j, pstart_ref[j], start)
    pos_ref[...] = start + pltpu.roll(ei, TOP_K, axis=0)

    rows, cols = fill_ref.shape
    f = lax.broadcasted_iota(I32, (rows, cols), 0) * cols + lax.broadcasted_iota(I32, (rows, cols), 1)
    base = jnp.full((rows, cols), fill_base_ref[0], I32)
    for j in range(1, N_EXPERTS + 1):
        base = jnp.where(f >= pad_end_ref[j - 1], fill_base_ref[j], base)
    fill_ref[...] = base + f


def _plan(meta, pstart, pad_end, fill_base, n_fill):
    T = meta.shape[1]
    rows = 2 * TOP_K
    assert n_fill % (rows * LANES) == 0
    grid_spec = pltpu.PrefetchScalarGridSpec(
        num_scalar_prefetch=3,
        grid=(1,),
        in_specs=[pl.BlockSpec(meta.shape, lambda i, a, b, c: (0, 0))],
        out_specs=[pl.BlockSpec((rows, T), lambda i, a, b, c: (0, 0)),
                   pl.BlockSpec((rows, n_fill // rows), lambda i, a, b, c: (0, 0))],
    )
    pos, fill = pl.pallas_call(
        _plan_kernel,
        grid_spec=grid_spec,
        out_shape=[jax.ShapeDtypeStruct((rows, T), I32), jax.ShapeDtypeStruct((rows, n_fill // rows), I32)],
        compiler_params=pltpu.CompilerParams(dimension_semantics=("arbitrary",)),
        name="row_plan",
    )(pstart, pad_end, fill_base, meta)
    return pos[:TOP_K], fill.reshape(-1)


def _wait_copies(src_hbm, dst, sem, rows):
    pltpu.make_async_copy(src_hbm.at[pl.ds(0, rows), :], dst.at[pl.ds(0, rows), :], sem).wait()


def _dispatch_kernel(pos_ref, u_hbm, x_hbm, stage, load_sem, row_sem, *, tm, n_token_steps):
    i = pl.program_id(0)
    n = pl.num_programs(0)
    batch = tm * TOP_K * PACK_ROWS
    tile_rows = tm * PACK_ROWS

    def tile_copy(step):
        first = pl.multiple_of(jnp.minimum(step, n_token_steps - 1) * tile_rows, tile_rows)
        return pltpu.make_async_copy(u_hbm.at[pl.ds(first, tile_rows), :], stage.at[step % 3],
                                     load_sem.at[step % 3])

    @pl.when(i == 0)
    def _():
        tile_copy(i).start()

    @pl.when(i + 1 < n)
    def _():
        tile_copy(i + 1).start()

    tile_copy(i).wait()
    src_tile = stage.at[i % 3]
    group = 16

    for k in range(TOP_K):
        def body(g, carry, k=k):
            rows = [pos_ref[0, k * tm + g * group + j] for j in range(group)]
            for j, p in enumerate(rows):
                src = src_tile.at[pl.ds(pl.multiple_of((g * group + j) * PACK_ROWS, PACK_ROWS), PACK_ROWS), :]
                dst = x_hbm.at[pl.ds(pl.multiple_of(p * PACK_ROWS, PACK_ROWS), PACK_ROWS), :]
                pltpu.make_async_copy(src, dst, row_sem.at[i % 2]).start(priority=j % 2)
            return carry

        lax.fori_loop(0, tm // group, body, 0)

    @pl.when(i > 0)
    def _():
        _wait_copies(u_hbm, x_hbm, row_sem.at[1 - i % 2], batch)

    @pl.when(i == n - 1)
    def _():
        _wait_copies(u_hbm, x_hbm, row_sem.at[i % 2], batch)


def _dispatch(u_packed, pos_kmajor, fill_rows, tm):
    T = pos_kmajor.shape[0] * tm
    per_step = TOP_K * tm
    dst_rows = jnp.concatenate([pos_kmajor.reshape(-1), fill_rows])
    n_rows = dst_rows.shape[0]
    assert n_rows % per_step == 0
    steps = n_rows // per_step
    return pl.pallas_call(
        functools.partial(_dispatch_kernel, tm=tm, n_token_steps=T // tm),
        grid=(steps,),
        in_specs=[pl.BlockSpec((None, 1, per_step), lambda i: (i, 0, 0), memory_space=pltpu.SMEM),
                  pl.BlockSpec(memory_space=pl.ANY)],
        out_specs=pl.BlockSpec(memory_space=pl.ANY),
        out_shape=jax.ShapeDtypeStruct((n_rows * PACK_ROWS, LANES), F32),
        scratch_shapes=[pltpu.VMEM((3, tm * PACK_ROWS, LANES), F32),
                        pltpu.SemaphoreType.DMA((3,)), pltpu.SemaphoreType.DMA((2,))],
        compiler_params=pltpu.CompilerParams(dimension_semantics=("arbitrary",)),
        name="dispatch_rows",
    )(dst_rows.reshape(steps, 1, per_step), u_packed)


def _expert_kernel(bexp_ref, slot_ref, next_ref, nused_ref, x_ref, wg_hbm, bg_ref, wu_hbm, bu_ref,
                   wd_hbm, bd_ref, y_ref, stage, w_bf, sem):
    weights = (wg_hbm, wu_hbm, wd_hbm)
    block_rows = MOE_ROWS * PACK_ROWS

    def fetch(expert, into):
        return [pltpu.make_async_copy(w.at[expert], stage.at[into, j], sem.at[into])
                for j, w in enumerate(weights)]

    for sub in range(MOE_STEP_BLOCKS):
        i = pl.program_id(0) * MOE_STEP_BLOCKS + sub
        used = i < nused_ref[0]
        e = bexp_ref[i]
        slot = slot_ref[i]
        changed = jnp.logical_or(i == 0, e != bexp_ref[jnp.maximum(i - 1, 0)])
        y_rows = y_ref.at[pl.ds(sub * block_rows, block_rows), :]

        @pl.when(jnp.logical_and(changed, used))
        def _():
            @pl.when(i == 0)
            def _():
                for c in fetch(e, slot):
                    c.start()

            for c in fetch(e, slot):
                c.wait()
            for s in range(2):
                @pl.when(slot == s)
                def _():
                    for j in range(len(weights)):
                        w_bf[j] = stage[s, j].astype(BF16)

            @pl.when(next_ref[i] >= 0)
            def _():
                for c in fetch(next_ref[i], 1 - slot):
                    c.start(priority=1)

        @pl.when(used)
        def _():
            x = _unpack_rows(x_ref, sub * block_rows, MOE_ROWS).astype(BF16)
            gate = jnp.minimum(jnp.dot(x, w_bf[0], preferred_element_type=F32) + bg_ref[e], SWIGLU_LIMIT)
            lin = jnp.clip(jnp.dot(x, w_bf[1], preferred_element_type=F32) + bu_ref[e],
                           -SWIGLU_LIMIT, SWIGLU_LIMIT)
            hdn = gate * jax.nn.sigmoid(SWIGLU_ALPHA * gate) * (lin + 1.0)
            y = jnp.dot(hdn.astype(BF16), w_bf[2], preferred_element_type=F32) + bd_ref[e]
            _pack_rows(y, y_rows, MOE_ROWS)

        @pl.when(jnp.logical_not(used))
        def _():
            y_rows[...] = jnp.zeros(y_rows.shape, y_rows.dtype)


def _experts(x_packed, block_expert, block_slot, block_next, n_used, w_eg, b_eg, w_eu, b_eu, w_ed, b_ed):
    nb = block_expert.shape[0]
    assert nb % MOE_STEP_BLOCKS == 0
    E, D, F = w_eg.shape
    assert D == F
    step_rows = MOE_STEP_BLOCKS * MOE_ROWS * PACK_ROWS
    rows = lambda i, be, sl, nx, nu: (i, 0)
    bspec = lambda n: pl.BlockSpec((E, 1, n), lambda i, be, sl, nx, nu: (0, 0, 0))
    anyspec = pl.BlockSpec(memory_space=pl.ANY)
    grid_spec = pltpu.PrefetchScalarGridSpec(
        num_scalar_prefetch=4,
        grid=(nb // MOE_STEP_BLOCKS,),
        in_specs=[pl.BlockSpec((step_rows, LANES), rows),
                  anyspec, bspec(F), anyspec, bspec(F), anyspec, bspec(D)],
        out_specs=pl.BlockSpec((step_rows, LANES), rows),
        scratch_shapes=[pltpu.VMEM((2, 3, D, F), F32), pltpu.VMEM((3, D, F), BF16),
                        pltpu.SemaphoreType.DMA((2,))],
    )
    return pl.pallas_call(
        _expert_kernel,
        grid_spec=grid_spec,
        out_shape=jax.ShapeDtypeStruct(x_packed.shape, F32),
        compiler_params=pltpu.CompilerParams(
            dimension_semantics=("arbitrary",), vmem_limit_bytes=VMEM_LIMIT),
        name="expert_ffn",
    )(block_expert, block_slot, block_next, n_used, x_packed, w_eg, b_eg.reshape(E, 1, F), w_eu,
      b_eu.reshape(E, 1, F), w_ed, b_ed.reshape(E, 1, D))


def _gather_packed(pos_ref, src_hbm, dst, sem, n):
    group = 16

    def body(g, carry):
        rows = [pos_ref[0, g * group + j] for j in range(group)]
        for j, p in enumerate(rows):
            r = g * group + j
            pltpu.make_async_copy(
                src_hbm.at[pl.ds(pl.multiple_of(p * PACK_ROWS, PACK_ROWS), PACK_ROWS), :],
                dst.at[pl.ds(pl.multiple_of(r * PACK_ROWS, PACK_ROWS), PACK_ROWS), :],
                sem).start(priority=j % 2)
        return carry
    lax.fori_loop(0, n // group, body, 0)


def _combine_kernel(pos0_ref, pos1_ref, y_hbm, h_ref, w_ref, g_ref, o_ref, ybuf, sem):
    i = pl.program_id(0)
    n = pl.num_programs(0)
    slot = i % 2
    rows = TOP_K * COMBINE_ROWS

    @pl.when(i == 0)
    def _():
        _gather_packed(pos0_ref, y_hbm, ybuf.at[0], sem.at[0], rows)

    @pl.when(i + 1 < n)
    def _():
        _gather_packed(pos1_ref, y_hbm, ybuf.at[1 - slot], sem.at[1 - slot], rows)

    _wait_copies(y_hbm, ybuf.at[slot], sem.at[slot], rows * PACK_ROWS)
    acc = h_ref[...]
    w = w_ref[...]
    for k in range(TOP_K):
        acc = acc + w[:, k:k + 1] * _unpack_rows(ybuf.at[slot], k * COMBINE_ROWS * PACK_ROWS, COMBINE_ROWS)
    o_ref[...] = _rms(acc, g_ref[...])


def _combine(y_rows, pos_kmajor, h2, wts, g_final):
    T, D = h2.shape
    tm = COMBINE_ROWS
    nt = T // tm
    posspec = lambda off: pl.BlockSpec(
        (None, 1, TOP_K * tm), lambda i: (jnp.minimum(i + off, nt - 1), 0, 0), memory_space=pltpu.SMEM)
    return pl.pallas_call(
        _combine_kernel,
        grid=(nt,),
        in_specs=[posspec(0), posspec(1), pl.BlockSpec(memory_space=pl.ANY),
                  pl.BlockSpec((tm, D), lambda i: (i, 0)), pl.BlockSpec((tm, LANES), lambda i: (i, 0)),
                  pl.BlockSpec((1, D), lambda i: (0, 0))],
        out_specs=pl.BlockSpec((tm, D), lambda i: (i, 0)),
        out_shape=jax.ShapeDtypeStruct((T, D), F32),
        scratch_shapes=[pltpu.VMEM((2, TOP_K * tm * PACK_ROWS, LANES), F32), pltpu.SemaphoreType.DMA((2,))],
        compiler_params=pltpu.CompilerParams(
            dimension_semantics=("arbitrary",), vmem_limit_bytes=VMEM_LIMIT),
        name="combine_norm",
    )(pos_kmajor, pos_kmajor, y_rows, h2, wts, g_final.reshape(1, D))


def _layer(h, mem, norm_mix, w_in, conv_w, w_branch_attn, w_branch_conv, w_gate, b_gate, w_out,
           norm_cross, norm_mem, w_cq, w_ckv, w_co, norm_moe, router_w, router_b,
           w_eg, b_eg, w_eu, b_eu, w_ed, b_ed, norm_final):
    B, S, D = h.shape
    T = B * S
    W3 = 3 * ATTN_WIDTH
    n_attn = len(ATTN_GROUPS) * W3

    assert ATTN_GROUPS[0][1] == 1
    w_in_b = w_in.astype(BF16)
    proj, qkv1, gates = _inproj(
        h, norm_mix, [(w_in_b[:, n_attn:], None), (w_in_b[:, :W3], None), (w_gate.astype(BF16), b_gate)],
        1, 512, 512, name="inproj_conv_g1_gates")
    proj2d = proj.reshape(T, -1)
    gates2d = gates.reshape(T, -1)

    o_list, lse_list = [], []
    for g, (window, dil) in enumerate(ATTN_GROUPS):
        assert window // dil == Q_BLOCK
        if g == 0:
            qkv = qkv1
        else:
            qkv, = _inproj(h, norm_mix, [(w_in_b[:, g * W3:(g + 1) * W3], None)], dil, 1024, 512,
                           name=f"inproj_g{g + 1}")
        o, lse = _attention(qkv, dil, name=f"dilated_attn_g{g + 1}")
        o_list.append(o)
        lse_list.append(lse)

    head_expand = (jnp.arange(LANES)[:, None] == jnp.arange(ATTN_WIDTH)[None, :] // HEAD_DIM).astype(BF16)
    h1 = _mix(o_list, lse_list, proj2d, gates2d, h.reshape(T, D), head_expand,
              w_branch_attn.astype(BF16), w_branch_conv.astype(BF16), w_out.astype(BF16), conv_w, S)

    kv = _mem_kv(mem, norm_mem, w_ckv.astype(BF16))
    rb = jnp.broadcast_to(router_b[:, None], (N_EXPERTS, LANES))
    h2, u3, meta, wts, counts = _cross_router(
        h1, kv, norm_cross.reshape(1, D), w_cq.astype(BF16), w_co.astype(BF16), norm_moe.reshape(1, D),
        router_w.T.astype(BF16), rb, S)

    nb = -(-(T * TOP_K) // MOE_ROWS) + N_EXPERTS
    cnt = counts[:, 0].astype(I32)
    padded = (cnt + MOE_ROWS - 1) // MOE_ROWS * MOE_ROWS
    pend = jnp.cumsum(padded)
    pstart = pend - padded
    experts = jnp.arange(N_EXPERTS, dtype=I32)
    n_pad = padded - cnt
    pad_end = jnp.cumsum(n_pad)
    fill_base = jnp.concatenate([pstart + cnt - (pad_end - n_pad), pend[-1:] - pad_end[-1:]])
    pos_kt, fill_rows = _plan(meta, pstart, pad_end, fill_base, nb * MOE_ROWS - T * TOP_K)

    tm = COMBINE_ROWS
    assert tm & (tm - 1) == 0 and T % tm == 0
    pos_kmajor = pos_kt.reshape(TOP_K, T // tm, tm).transpose(1, 0, 2).reshape(T // tm, 1, TOP_K * tm)
    block_start = jnp.arange(nb, dtype=I32) * MOE_ROWS
    block_expert = jnp.minimum(jnp.sum(block_start[:, None] >= pend[None, :], axis=1), N_EXPERTS - 1).astype(I32)
    active = cnt > 0
    slot_e = (jnp.cumsum(active.astype(I32)) - 1) % 2
    later = jnp.where(active[None, :] & (experts[None, :] > experts[:, None]), experts[None, :], N_EXPERTS)
    next_e = jnp.min(later, axis=1)
    next_e = jnp.where(next_e == N_EXPERTS, -1, next_e).astype(I32)
    x_rows = _dispatch(u3, pos_kmajor, fill_rows, tm)
    block_is = block_expert[:, None] == experts[None, :]
    per_block = lambda table: jnp.sum(jnp.where(block_is, table[None, :], 0), axis=1).astype(I32)
    y_rows = _experts(x_rows, block_expert, per_block(slot_e), per_block(next_e),
                      (pend[-1:] // MOE_ROWS).astype(I32), w_eg, b_eg, w_eu, b_eu, w_ed, b_ed)

    out = _combine(y_rows, pos_kmajor, h2, wts, norm_final)
    return out.reshape(B, S, D)


def kernel(x, mem, norm_mix, w_in, conv_w, w_branch_attn, w_branch_conv, w_gate, b_gate, w_out, norm_cross, norm_mem, w_cq, w_ckv, w_co, norm_moe, router_w, router_b, w_exp_gate, b_exp_gate, w_exp_up, b_exp_up, w_exp_down, b_exp_down, norm_final):
    depth = norm_mix.shape[0]
    assert depth == 1, "the final norm is fused into the last layer's combine step"
    return _layer(x, mem, norm_mix[0], w_in[0], conv_w[0], w_branch_attn[0], w_branch_conv[0], w_gate[0],
                  b_gate[0], w_out[0], norm_cross[0], norm_mem[0], w_cq[0], w_ckv[0], w_co[0], norm_moe[0],
                  router_w[0], router_b[0], w_exp_gate[0], b_exp_gate[0], w_exp_up[0], b_exp_up[0],
                  w_exp_down[0], b_exp_down[0], norm_final)
```

```python
import functools

import jax
import jax.numpy as jnp
import numpy as np
from jax import lax
from jax.experimental import pallas as pl
from jax.experimental.pallas import tpu as pltpu

F32 = jnp.float32
BF16 = jnp.bfloat16
I32 = jnp.int32

D_MODEL = 1024
ATTN_GROUPS = ((128, 1), (512, 4), (2048, 16))
ATTN_HEADS = 8
HEAD_DIM = 64
ATTN_WIDTH = ATTN_HEADS * HEAD_DIM
Q_BLOCK = 128
CONV_K = 3
MEM_HEADS = 4
MEM_HEAD_DIM = 128
MEM_WIDTH = MEM_HEADS * MEM_HEAD_DIM
N_EXPERTS = 32
TOP_K = 4
SWIGLU_LIMIT = 7.0
SWIGLU_ALPHA = 1.702
EPS = 1e-6
NEG_INF = -1e30

LANES = 128
VMEM_LIMIT = 56 * 1024 * 1024

MOE_ROWS = 256
MOE_STEP_BLOCKS = 4
COMBINE_ROWS = 512
CONV_HALO = 16
META_ROWS = 16
REGROUP_STRIDE = 4


def _rms(x, g):
    ms = jnp.mean(x * x, axis=-1, keepdims=True)
    return x * lax.rsqrt(ms + EPS) * g


PACK_ROWS = D_MODEL // LANES


def _pack_rows(y, out_ref, n):
    for c in range(PACK_ROWS):
        out_ref[pl.ds(c, n, stride=PACK_ROWS), :] = y[:, c * LANES:(c + 1) * LANES]


def _unpack_rows(ref, start, n):
    return jnp.concatenate(
        [ref[pl.ds(start + c, n, stride=PACK_ROWS), :] for c in range(PACK_ROWS)], axis=1)


def _inproj_kernel(x_ref, g_ref, *refs, gate_bias, dil, chunk):
    n_w = len(gate_bias)
    w_refs, b_refs, o_refs = refs[:n_w], refs[n_w:2 * n_w], refs[2 * n_w:3 * n_w]
    u_scr = refs[3 * n_w]
    tm = x_ref.shape[0]
    per = tm // dil
    u = _rms(x_ref[...], g_ref[...])
    if dil == 1:
        u_scr[...] = u.astype(BF16)
    else:
        slab = refs[3 * n_w + 1]
        n_slab = u.shape[1] // LANES
        for c in range(n_slab):
            slab[c] = u[:, c * LANES:(c + 1) * LANES]
        if dil <= REGROUP_STRIDE:
            for r in range(dil):
                for c in range(n_slab):
                    u_scr[r * per:(r + 1) * per, c * LANES:(c + 1) * LANES] = (
                        slab[c, pl.ds(r, per, stride=dil), :].astype(BF16))
        else:
            slab2 = refs[3 * n_w + 2]
            d1, d2 = REGROUP_STRIDE, dil // REGROUP_STRIDE
            group = tm // d1
            for r1 in range(d1):
                for c in range(n_slab):
                    slab2[c, r1 * group:(r1 + 1) * group, :] = slab[c, pl.ds(r1, group, stride=d1), :]
            for r1 in range(d1):
                for r2 in range(d2):
                    r = r1 + d1 * r2
                    for c in range(n_slab):
                        u_scr[r * per:(r + 1) * per, c * LANES:(c + 1) * LANES] = (
                            slab2[c, pl.ds(r1 * group + r2, per, stride=d2), :].astype(BF16))
    ub = u_scr[...]
    for w_ref, b_ref, o_ref, sig in zip(w_refs, b_refs, o_refs, gate_bias):
        for c in range(w_ref.shape[1] // chunk):
            cols = slice(c * chunk, (c + 1) * chunk)
            acc = jnp.dot(ub, w_ref[:, cols], preferred_element_type=F32)
            if sig:
                acc = jax.nn.sigmoid(acc + b_ref[:, cols])
            for r in range(dil):
                o_ref[r, :, cols] = acc[r * per:(r + 1) * per].astype(o_ref.dtype)


def _inproj(x, gain, weights, dil, tm, chunk, name):
    B, S, D = x.shape
    ws = [w for w, _ in weights]
    bs = [jnp.zeros((1, w.shape[1]), F32) if b is None else b.reshape(1, -1) for w, b in weights]
    const = lambda a: pl.BlockSpec(a.shape, lambda b, i: (0, 0))
    scratch = [pltpu.VMEM((tm, D), BF16)]
    if dil > 1:
        scratch.append(pltpu.VMEM((D // LANES, tm, LANES), F32))
    if dil > REGROUP_STRIDE:
        assert dil % REGROUP_STRIDE == 0 and dil // REGROUP_STRIDE <= REGROUP_STRIDE
        scratch.append(pltpu.VMEM((D // LANES, tm, LANES), F32))
    return pl.pallas_call(
        functools.partial(_inproj_kernel, gate_bias=tuple(b is not None for _, b in weights),
                          dil=dil, chunk=chunk),
        grid=(B, S // tm),
        in_specs=[pl.BlockSpec((None, tm, D), lambda b, i: (b, i, 0)), const(gain.reshape(1, D))]
                 + [const(w) for w in ws] + [const(b) for b in bs],
        out_specs=[pl.BlockSpec((None, dil, tm // dil, w.shape[1]), lambda b, i: (b, 0, i, 0)) for w in ws],
        out_shape=[jax.ShapeDtypeStruct((B, dil, S // dil, w.shape[1]), BF16) for w in ws],
        scratch_shapes=scratch,
        compiler_params=pltpu.CompilerParams(
            dimension_semantics=("parallel", "parallel"), vmem_limit_bytes=VMEM_LIMIT),
        name=name,
    )(x, gain.reshape(1, D), *ws, *bs)


def _mem_kv_kernel(x_ref, g_ref, w_ref, o_ref):
    u = _rms(x_ref[...], g_ref[...]).astype(BF16)
    o_ref[...] = jnp.dot(u, w_ref[...], preferred_element_type=F32).astype(o_ref.dtype)


def _mem_kv(mem, gain, w):
    B, M, D = mem.shape
    N = w.shape[1]
    return pl.pallas_call(
        _mem_kv_kernel,
        grid=(B,),
        in_specs=[pl.BlockSpec((None, M, D), lambda b: (b, 0, 0)),
                  pl.BlockSpec((1, D), lambda b: (0, 0)),
                  pl.BlockSpec((D, N), lambda b: (0, 0))],
        out_specs=pl.BlockSpec((None, M, N), lambda b: (b, 0, 0)),
        out_shape=jax.ShapeDtypeStruct((B, M, N), BF16),
        compiler_params=pltpu.CompilerParams(
            dimension_semantics=("parallel",), vmem_limit_bytes=VMEM_LIMIT),
        name="mem_kv",
    )(mem, gain.reshape(1, D), w)


def _attn_block(q_ref, kp_ref, kc_ref, vp_ref, vc_ref, tab_ref, first, r, j):
    lane = lax.broadcasted_iota(I32, (Q_BLOCK, LANES), 1)
    low = lane < HEAD_DIM
    lse_tile = jnp.zeros((Q_BLOCK, LANES), F32)
    pairs = []
    if j > 0:
        first = 1
    for pair in range(ATTN_HEADS // 2):
        cols = slice(pair * LANES, (pair + 1) * LANES)
        q2 = q_ref[r, j * Q_BLOCK:(j + 1) * Q_BLOCK, cols] * (HEAD_DIM ** -0.5)
        if j == 0:
            k2 = jnp.concatenate([kp_ref[r, :, cols], kc_ref[r, :Q_BLOCK, cols]], axis=0)
            v2 = jnp.concatenate([vp_ref[r, :, cols], vc_ref[r, :Q_BLOCK, cols]], axis=0)
        else:
            k2 = kc_ref[r, (j - 1) * Q_BLOCK:(j + 1) * Q_BLOCK, cols]
            v2 = vc_ref[r, (j - 1) * Q_BLOCK:(j + 1) * Q_BLOCK, cols]
        outs = []
        for half in range(2):
            h = 2 * pair + half
            keep = low if half == 0 else jnp.logical_not(low)
            qm = jnp.where(keep, q2, jnp.zeros_like(q2))
            s = lax.dot_general(qm, k2, (((1,), (1,)), ((), ())), preferred_element_type=F32)
            s = s + tab_ref[first, h]
            m = jnp.max(s, axis=1, keepdims=True)
            p = jnp.exp(s - m)
            l = jnp.sum(p, axis=1, keepdims=True)
            o = jnp.dot(p.astype(BF16), v2, preferred_element_type=F32) * (1.0 / l)
            outs.append(o)
            lse_tile = jnp.where(lane == h, m + jnp.log(l), lse_tile)
        pairs.append(jnp.where(low, outs[0], outs[1]))
    return pairs, lse_tile


def _attn_kernel(q_ref, kp_ref, kc_ref, vp_ref, vc_ref, tab_ref, o_ref, lse_ref, *scratch, dil, qb):
    first = jnp.minimum(pl.program_id(1), 1)
    blocks = (q_ref, kp_ref, kc_ref, vp_ref, vc_ref, tab_ref)
    n_pair = ATTN_HEADS // 2
    if dil == 1:
        for j in range(qb):
            pairs, lse_tile = _attn_block(*blocks, first, 0, j)
            rows = slice(j * Q_BLOCK, (j + 1) * Q_BLOCK)
            for p in range(n_pair):
                o_ref[rows, p * LANES:(p + 1) * LANES] = pairs[p].astype(o_ref.dtype)
            lse_ref[rows, :] = lse_tile
        return

    o_scr, lse_scr = scratch

    unroll = min(dil, 8)

    def body(g, carry):
        for r in [g * unroll + t for t in range(unroll)]:
            for j in range(qb):
                pairs, lse_tile = _attn_block(*blocks, first, r, j)
                rows = pl.ds(j * Q_BLOCK * dil + r, Q_BLOCK, stride=dil)
                for p in range(n_pair):
                    o_scr[p, rows, :] = pairs[p]
                lse_scr[rows, :] = lse_tile
        return carry

    if dil == unroll:
        body(0, 0)
    else:
        lax.fori_loop(0, dil // unroll, body, 0)
    for p in range(n_pair):
        o_ref[:, p * LANES:(p + 1) * LANES] = o_scr[p].astype(o_ref.dtype)
    lse_ref[...] = lse_scr[...]


def _attn_bias_table(dil):
    slopes = np.power(2.0, -8.0 * np.arange(1, ATTN_HEADS + 1, dtype=np.float32) / ATTN_HEADS)
    iq = np.arange(Q_BLOCK)
    ik = np.arange(2 * Q_BLOCK)
    dist = iq[:, None] + Q_BLOCK - ik[None, :]
    band = (dist >= 0) & (dist <= Q_BLOCK)
    has_prev = np.stack([ik >= Q_BLOCK, np.ones_like(ik, dtype=bool)])
    mask = band[None] & has_prev[:, None, :]
    bias = -slopes[:, None, None] * (dist * dil).astype(np.float32)[None]
    return np.where(mask[:, None], bias[None], np.float32(NEG_INF)).astype(np.float32)


def _attention(qkv, dil, name):
    B, _, L, _ = qkv.shape
    W = ATTN_WIDTH
    qb = max(1, 8 // dil)
    nb = L // (Q_BLOCK * qb)
    span = Q_BLOCK * qb * dil
    blk = lambda part, prev: (
        pl.BlockSpec((None, dil, Q_BLOCK, W), lambda b, n: (b, 0, jnp.maximum(n * qb - 1, 0), part))
        if prev else pl.BlockSpec((None, dil, Q_BLOCK * qb, W), lambda b, n: (b, 0, n, part)))
    tab = _attn_bias_table(dil)
    scratch = [] if dil == 1 else [pltpu.VMEM((ATTN_HEADS // 2, span, LANES), F32),
                                   pltpu.VMEM((span, LANES), F32)]
    o, lse = pl.pallas_call(
        functools.partial(_attn_kernel, dil=dil, qb=qb),
        grid=(B, nb),
        in_specs=[blk(0, False), blk(1, True), blk(1, False), blk(2, True), blk(2, False),
                  pl.BlockSpec(tab.shape, lambda b, n: (0, 0, 0, 0))],
        out_specs=[pl.BlockSpec((None, span, W), lambda b, n: (b, n, 0)),
                   pl.BlockSpec((None, span, LANES), lambda b, n: (b, n, 0))],
        out_shape=[jax.ShapeDtypeStruct((B, L * dil, W), BF16),
                   jax.ShapeDtypeStruct((B, L * dil, LANES), F32)],
        scratch_shapes=scratch,
        compiler_params=pltpu.CompilerParams(
            dimension_semantics=("parallel", "parallel"), vmem_limit_bytes=VMEM_LIMIT),
        name=name,
    )(qkv, qkv, qkv, qkv, qkv, tab)
    return o.reshape(B * L * dil, W), lse.reshape(B * L * dil, LANES)


def _mix_kernel(o1_ref, o2_ref, o3_ref, l1_ref, l2_ref, l3_ref, gb_ref, gc_ref, xc_ref, gch_ref, xch_ref,
                ga_ref, gv_ref, x_ref, e_ref, wa_ref, wc_ref, wo_ref, cw_ref, h_ref, *, tm, seq):
    a1, a2, a3 = l1_ref[...], l2_ref[...], l3_ref[...]
    m = jnp.maximum(jnp.maximum(a1, a2), a3)
    e1, e2, e3 = jnp.exp(a1 - m), jnp.exp(a2 - m), jnp.exp(a3 - m)
    inv = 1.0 / (e1 + e2 + e3)
    y = jnp.zeros((tm, ATTN_WIDTH), F32)
    for e, o_ref in ((e1, o1_ref), (e2, o2_ref), (e3, o3_ref)):
        w = e * inv
        w_hi = w.astype(BF16)
        w_lo = (w - w_hi.astype(F32)).astype(BF16)
        w_heads = (jnp.dot(w_hi, e_ref[...], preferred_element_type=F32)
                   + jnp.dot(w_lo, e_ref[...], preferred_element_type=F32))
        y = y + w_heads * o_ref[...].astype(F32)
    branch_a = jnp.dot(y.astype(BF16), wa_ref[...], preferred_element_type=F32)

    at_start = (pl.program_id(0) * tm) % seq == 0
    halo = gch_ref[...].astype(F32) * xch_ref[...].astype(F32)
    halo = jnp.where(at_start, jnp.zeros_like(halo), halo)[CONV_HALO - 8:, :]
    z = gc_ref[...].astype(F32) * xc_ref[...].astype(F32)
    row = lax.broadcasted_iota(I32, (8, z.shape[1]), 0)
    taps = []
    for shift in (2, 1):
        zs = pltpu.roll(z, shift, axis=0)
        head = jnp.where(row < shift, pltpu.roll(halo, shift, axis=0), zs[0:8, :])
        taps.append(jnp.concatenate([head, zs[8:, :]], axis=0))
    conv = cw_ref[0:1, :] * taps[0] + cw_ref[1:2, :] * taps[1] + cw_ref[2:3, :] * z
    y_conv = gb_ref[...].astype(F32) * conv
    branch_c = jnp.dot(y_conv.astype(BF16), wc_ref[...], preferred_element_type=F32)

    mixed = ga_ref[...].astype(F32) * branch_a + gv_ref[...].astype(F32) * branch_c
    h_ref[...] = x_ref[...] + jnp.dot(mixed.astype(BF16), wo_ref[...], preferred_element_type=F32)


def _mix(o_list, lse_list, proj, gates, x2d, head_expand, wa, wc, wo, conv_w, seq, tm=512):
    T, D = x2d.shape
    W = ATTN_WIDTH
    row = lambda width, col: pl.BlockSpec((tm, width), lambda i: (i, col))
    halo = lambda col: pl.BlockSpec(
        (CONV_HALO, D), lambda i: (jnp.maximum(i * (tm // CONV_HALO) - 1, 0), col))
    full = lambda a: pl.BlockSpec(a.shape, lambda i: (0,) * a.ndim)
    return pl.pallas_call(
        functools.partial(_mix_kernel, tm=tm, seq=seq),
        grid=(T // tm,),
        in_specs=[row(W, 0)] * 3 + [row(LANES, 0)] * 3
                 + [row(D, 0), row(D, 1), row(D, 2), halo(1), halo(2)]
                 + [row(D, 0), row(D, 1), row(D, 0)]
                 + [full(head_expand), full(wa), full(wc), full(wo), full(conv_w)],
        out_specs=pl.BlockSpec((tm, D), lambda i: (i, 0)),
        out_shape=jax.ShapeDtypeStruct((T, D), F32),
        compiler_params=pltpu.CompilerParams(
            dimension_semantics=("parallel",), vmem_limit_bytes=VMEM_LIMIT),
        name="branch_mix",
    )(*o_list, *lse_list, proj, proj, proj, proj, proj, gates, gates, x2d,
      head_expand, wa, wc, wo, conv_w)


def _cross_kernel(h_ref, k_ref, v_ref, g2_ref, wq_ref, wo_ref, g3_ref, rw_ref, rb_ref, tri_ref,
                  h2_ref, u3_ref, meta_ref, wts_ref, cnt_ref, run_scr, *, tm):
    @pl.when(pl.program_id(0) == 0)
    def _():
        run_scr[...] = jnp.zeros_like(run_scr)

    h = h_ref[...]
    u = _rms(h, g2_ref[...]).astype(BF16)
    q = jnp.dot(u, wq_ref[...], preferred_element_type=F32) * (MEM_HEAD_DIM ** -0.5)
    heads = []
    for hd in range(MEM_HEADS):
        cols = slice(hd * MEM_HEAD_DIM, (hd + 1) * MEM_HEAD_DIM)
        s = lax.dot_general(q[:, cols].astype(BF16), k_ref[:, cols], (((1,), (1,)), ((), ())),
                            preferred_element_type=F32)
        m = jnp.max(s, axis=1, keepdims=True)
        p = jnp.exp(s - m)
        l = jnp.sum(p, axis=1, keepdims=True)
        heads.append(jnp.dot(p.astype(BF16), v_ref[:, cols], preferred_element_type=F32) * (1.0 / l))
    o = jnp.concatenate(heads, axis=1).astype(BF16)
    h2 = h + jnp.dot(o, wo_ref[...], preferred_element_type=F32)
    h2_ref[...] = h2
    u3 = _rms(h2, g3_ref[...])
    _pack_rows(u3, u3_ref, tm)

    logits = lax.dot_general(rw_ref[...], u3.astype(BF16), (((1,), (1,)), ((), ())),
                             preferred_element_type=F32) + rb_ref[:, 0:1]
    expert = lax.broadcasted_iota(I32, (N_EXPERTS, tm), 0).astype(F32)
    work = logits
    picked = jnp.zeros((N_EXPERTS, tm), F32)
    tops, idxs, sels = [], [], []
    for _ in range(TOP_K):
        mk = jnp.max(work, axis=0, keepdims=True)
        ik = jnp.min(jnp.where(work == mk, expert, float(N_EXPERTS)), axis=0, keepdims=True)
        sel = expert == ik
        work = jnp.where(sel, -jnp.inf, work)
        picked = picked + sel.astype(F32)
        tops.append(mk)
        idxs.append(ik)
        sels.append(sel)
    exps = [jnp.exp(t - tops[0]) for t in tops]
    inv = 1.0 / (exps[0] + exps[1] + exps[2] + exps[3])
    wts = [e * inv for e in exps]

    before = jnp.dot(picked.astype(BF16), tri_ref[...], preferred_element_type=F32) + run_scr[:, 0:1]
    ranks = [jnp.sum(jnp.where(sel, before, 0.0), axis=0, keepdims=True) for sel in sels]

    row = lax.broadcasted_iota(I32, (META_ROWS, tm), 0)
    meta = jnp.zeros((META_ROWS, tm), F32)
    for r, v in enumerate(idxs + ranks + wts):
        meta = jnp.where(row == r, v, meta)
    meta_ref[...] = meta
    row = lax.broadcasted_iota(I32, (LANES, tm), 0)
    wpad = jnp.zeros((LANES, tm), F32)
    for k in range(TOP_K):
        wpad = jnp.where(row == k, wts[k], wpad)
    wts_ref[...] = wpad.T
    run_scr[...] = run_scr[...] + jnp.sum(picked, axis=1, keepdims=True)
    cnt_ref[...] = run_scr[...]


def _cross_router(h1, kv, g2, wq, wo, g3, rw_t, rb, seq, tm=1024):
    T, D = h1.shape
    n_mem = kv.shape[1]
    tri = (jnp.arange(tm)[:, None] < jnp.arange(tm)[None, :]).astype(BF16)
    full = lambda a: pl.BlockSpec(a.shape, lambda i: (0,) * a.ndim)
    kvspec = lambda col: pl.BlockSpec((None, n_mem, MEM_WIDTH), lambda i: ((i * tm) // seq, 0, col))
    tile = lambda width: pl.BlockSpec((tm, width), lambda i: (i, 0))
    return pl.pallas_call(
        functools.partial(_cross_kernel, tm=tm),
        grid=(T // tm,),
        in_specs=[tile(D), kvspec(0), kvspec(1), full(g2), full(wq), full(wo), full(g3),
                  full(rw_t), full(rb), full(tri)],
        out_specs=[tile(D), pl.BlockSpec((tm * PACK_ROWS, LANES), lambda i: (i, 0)),
                   pl.BlockSpec((META_ROWS, tm), lambda i: (0, i)), tile(LANES),
                   pl.BlockSpec((N_EXPERTS, LANES), lambda i: (0, 0))],
        out_shape=[jax.ShapeDtypeStruct((T, D), F32), jax.ShapeDtypeStruct((T * PACK_ROWS, LANES), F32),
                   jax.ShapeDtypeStruct((META_ROWS, T), F32), jax.ShapeDtypeStruct((T, LANES), F32),
                   jax.ShapeDtypeStruct((N_EXPERTS, LANES), F32)],
        scratch_shapes=[pltpu.VMEM((N_EXPERTS, LANES), F32)],
        compiler_params=pltpu.CompilerParams(
            dimension_semantics=("arbitrary",), vmem_limit_bytes=VMEM_LIMIT),
        name="cross_router",
    )(h1, kv, kv, g2, wq, wo, g3, rw_t, rb, tri)


def _plan_kernel(pstart_ref, pad_end_ref, fill_base_ref, meta_ref, pos_ref, fill_ref):
    ei = meta_ref[0:2 * TOP_K, :].astype(I32)
    start = jnp.zeros(ei.shape, I32)
    for j in range(N_EXPERTS):
        start = jnp.where(ei == j, pstart_ref[j], start)
    pos_ref[...] = start + pltpu.roll(ei, TOP_K, axis=0)

    rows, cols = fill_ref.shape
    f = lax.broadcasted_iota(I32, (rows, cols), 0) * cols + lax.broadcasted_iota(I32, (rows, cols), 1)
    base = jnp.full((rows, cols), fill_base_ref[0], I32)
    for j in range(1, N_EXPERTS + 1):
        base = jnp.where(f >= pad_end_ref[j - 1], fill_base_ref[j], base)
    fill_ref[...] = base + f


def _plan(meta, pstart, pad_end, fill_base, n_fill):
    T = meta.shape[1]
    rows = 2 * TOP_K
    assert n_fill % (rows * LANES) == 0
    grid_spec = pltpu.PrefetchScalarGridSpec(
        num_scalar_prefetch=3,
        grid=(1,),
        in_specs=[pl.BlockSpec(meta.shape, lambda i, a, b, c: (0, 0))],
        out_specs=[pl.BlockSpec((rows, T), lambda i, a, b, c: (0, 0)),
                   pl.BlockSpec((rows, n_fill // rows), lambda i, a, b, c: (0, 0))],
    )
    pos, fill = pl.pallas_call(
        _plan_kernel,
        grid_spec=grid_spec,
        out_shape=[jax.ShapeDtypeStruct((rows, T), I32), jax.ShapeDtypeStruct((rows, n_fill // rows), I32)],
        compiler_params=pltpu.CompilerParams(dimension_semantics=("arbitrary",)),
        name="row_plan",
    )(pstart, pad_end, fill_base, meta)
    return pos[:TOP_K], fill.reshape(-1)


def _wait_copies(src_hbm, dst, sem, rows):
    pltpu.make_async_copy(src_hbm.at[pl.ds(0, rows), :], dst.at[pl.ds(0, rows), :], sem).wait()


def _dispatch_kernel(pos_ref, u_hbm, x_hbm, stage, load_sem, row_sem, *, tm, n_token_steps):
    i = pl.program_id(0)
    n = pl.num_programs(0)
    batch = tm * TOP_K * PACK_ROWS
    tile_rows = tm * PACK_ROWS

    def tile_copy(step):
        first = pl.multiple_of(jnp.minimum(step, n_token_steps - 1) * tile_rows, tile_rows)
        return pltpu.make_async_copy(u_hbm.at[pl.ds(first, tile_rows), :], stage.at[step % 3],
                                     load_sem.at[step % 3])

    @pl.when(i == 0)
    def _():
        tile_copy(i).start()

    @pl.when(i + 1 < n)
    def _():
        tile_copy(i + 1).start()

    tile_copy(i).wait()
    src_tile = stage.at[i % 3]
    group = 16

    for k in range(TOP_K):
        def body(g, carry, k=k):
            rows = [pos_ref[0, k * tm + g * group + j] for j in range(group)]
            for j, p in enumerate(rows):
                src = src_tile.at[pl.ds(pl.multiple_of((g * group + j) * PACK_ROWS, PACK_ROWS), PACK_ROWS), :]
                dst = x_hbm.at[pl.ds(pl.multiple_of(p * PACK_ROWS, PACK_ROWS), PACK_ROWS), :]
                pltpu.make_async_copy(src, dst, row_sem.at[i % 2]).start(priority=j % 2)
            return carry

        lax.fori_loop(0, tm // group, body, 0)

    @pl.when(i > 0)
    def _():
        _wait_copies(u_hbm, x_hbm, row_sem.at[1 - i % 2], batch)

    @pl.when(i == n - 1)
    def _():
        _wait_copies(u_hbm, x_hbm, row_sem.at[i % 2], batch)


def _dispatch(u_packed, pos_kmajor, fill_rows, tm):
    T = pos_kmajor.shape[0] * tm
    per_step = TOP_K * tm
    dst_rows = jnp.concatenate([pos_kmajor.reshape(-1), fill_rows])
    n_rows = dst_rows.shape[0]
    assert n_rows % per_step == 0
    steps = n_rows // per_step
    return pl.pallas_call(
        functools.partial(_dispatch_kernel, tm=tm, n_token_steps=T // tm),
        grid=(steps,),
        in_specs=[pl.BlockSpec((None, 1, per_step), lambda i: (i, 0, 0), memory_space=pltpu.SMEM),
                  pl.BlockSpec(memory_space=pl.ANY)],
        out_specs=pl.BlockSpec(memory_space=pl.ANY),
        out_shape=jax.ShapeDtypeStruct((n_rows * PACK_ROWS, LANES), F32),
        scratch_shapes=[pltpu.VMEM((3, tm * PACK_ROWS, LANES), F32),
                        pltpu.SemaphoreType.DMA((3,)), pltpu.SemaphoreType.DMA((2,))],
        compiler_params=pltpu.CompilerParams(dimension_semantics=("arbitrary",)),
        name="dispatch_rows",
    )(dst_rows.reshape(steps, 1, per_step), u_packed)


def _expert_kernel(bexp_ref, slot_ref, next_ref, nused_ref, x_ref, wg_hbm, bg_ref, wu_hbm, bu_ref,
                   wd_hbm, bd_ref, y_ref, stage, w_bf, sem):
    weights = (wg_hbm, wu_hbm, wd_hbm)
    block_rows = MOE_ROWS * PACK_ROWS

    def fetch(expert, into):
        return [pltpu.make_async_copy(w.at[expert], stage.at[into, j], sem.at[into])
                for j, w in enumerate(weights)]

    for sub in range(MOE_STEP_BLOCKS):
        i = pl.program_id(0) * MOE_STEP_BLOCKS + sub
        used = i < nused_ref[0]
        e = bexp_ref[i]
        slot = slot_ref[i]
        changed = jnp.logical_or(i == 0, e != bexp_ref[jnp.maximum(i - 1, 0)])
        y_rows = y_ref.at[pl.ds(sub * block_rows, block_rows), :]

        @pl.when(jnp.logical_and(changed, used))
        def _():
            @pl.when(i == 0)
            def _():
                for c in fetch(e, slot):
                    c.start()

            for c in fetch(e, slot):
                c.wait()
            for s in range(2):
                @pl.when(slot == s)
                def _():
                    for j in range(len(weights)):
                        w_bf[j] = stage[s, j].astype(BF16)

            @pl.when(next_ref[i] >= 0)
            def _():
                for c in fetch(next_ref[i], 1 - slot):
                    c.start(priority=1)

        @pl.when(used)
        def _():
            x = _unpack_rows(x_ref, sub * block_rows, MOE_ROWS).astype(BF16)
            gate = jnp.minimum(jnp.dot(x, w_bf[0], preferred_element_type=F32) + bg_ref[e], SWIGLU_LIMIT)
            lin = jnp.clip(jnp.dot(x, w_bf[1], preferred_element_type=F32) + bu_ref[e],
                           -SWIGLU_LIMIT, SWIGLU_LIMIT)
            hdn = gate * jax.nn.sigmoid(SWIGLU_ALPHA * gate) * (lin + 1.0)
            y = jnp.dot(hdn.astype(BF16), w_bf[2], preferred_element_type=F32) + bd_ref[e]
            _pack_rows(y, y_rows, MOE_ROWS)

        @pl.when(jnp.logical_not(used))
        def _():
            y_rows[...] = jnp.zeros(y_rows.shape, y_rows.dtype)


def _experts(x_packed, block_expert, block_slot, block_next, n_used, w_eg, b_eg, w_eu, b_eu, w_ed, b_ed):
    nb = block_expert.shape[0]
    assert nb % MOE_STEP_BLOCKS == 0
    E, D, F = w_eg.shape
    assert D == F
    step_rows = MOE_STEP_BLOCKS * MOE_ROWS * PACK_ROWS
    rows = lambda i, be, sl, nx, nu: (i, 0)
    bspec = lambda n: pl.BlockSpec((E, 1, n), lambda i, be, sl, nx, nu: (0, 0, 0))
    anyspec = pl.BlockSpec(memory_space=pl.ANY)
    grid_spec = pltpu.PrefetchScalarGridSpec(
        num_scalar_prefetch=4,
        grid=(nb // MOE_STEP_BLOCKS,),
        in_specs=[pl.BlockSpec((step_rows, LANES), rows),
                  anyspec, bspec(F), anyspec, bspec(F), anyspec, bspec(D)],
        out_specs=pl.BlockSpec((step_rows, LANES), rows),
        scratch_shapes=[pltpu.VMEM((2, 3, D, F), F32), pltpu.VMEM((3, D, F), BF16),
                        pltpu.SemaphoreType.DMA((2,))],
    )
    return pl.pallas_call(
        _expert_kernel,
        grid_spec=grid_spec,
        out_shape=jax.ShapeDtypeStruct(x_packed.shape, F32),
        compiler_params=pltpu.CompilerParams(
            dimension_semantics=("arbitrary",), vmem_limit_bytes=VMEM_LIMIT),
        name="expert_ffn",
    )(block_expert, block_slot, block_next, n_used, x_packed, w_eg, b_eg.reshape(E, 1, F), w_eu,
      b_eu.reshape(E, 1, F), w_ed, b_ed.reshape(E, 1, D))


def _gather_packed(pos_ref, src_hbm, dst, sem, n):
    group = 16

    def body(g, carry):
        rows = [pos_ref[0, g * group + j] for j in range(group)]
        for j, p in enumerate(rows):
            r = g * group + j
            pltpu.make_async_copy(
                src_hbm.at[pl.ds(pl.multiple_of(p * PACK_ROWS, PACK_ROWS), PACK_ROWS), :],
                dst.at[pl.ds(pl.multiple_of(r * PACK_ROWS, PACK_ROWS), PACK_ROWS), :],
                sem).start(priority=j % 2)
        return carry
    lax.fori_loop(0, n // group, body, 0)


def _combine_kernel(pos0_ref, pos1_ref, y_hbm, h_ref, w_ref, g_ref, o_ref, ybuf, sem):
    i = pl.program_id(0)
    n = pl.num_programs(0)
    slot = i % 2
    rows = TOP_K * COMBINE_ROWS

    @pl.when(i == 0)
    def _():
        _gather_packed(pos0_ref, y_hbm, ybuf.at[0], sem.at[0], rows)

    @pl.when(i + 1 < n)
    def _():
        _gather_packed(pos1_ref, y_hbm, ybuf.at[1 - slot], sem.at[1 - slot], rows)

    _wait_copies(y_hbm, ybuf.at[slot], sem.at[slot], rows * PACK_ROWS)
    acc = h_ref[...]
    w = w_ref[...]
    for k in range(TOP_K):
        acc = acc + w[:, k:k + 1] * _unpack_rows(ybuf.at[slot], k * COMBINE_ROWS * PACK_ROWS, COMBINE_ROWS)
    o_ref[...] = _rms(acc, g_ref[...])


def _combine(y_rows, pos_kmajor, h2, wts, g_final):
    T, D = h2.shape
    tm = COMBINE_ROWS
    nt = T // tm
    posspec = lambda off: pl.BlockSpec(
        (None, 1, TOP_K * tm), lambda i: (jnp.minimum(i + off, nt - 1), 0, 0), memory_space=pltpu.SMEM)
    return pl.pallas_call(
        _combine_kernel,
        grid=(nt,),
        in_specs=[posspec(0), posspec(1), pl.BlockSpec(memory_space=pl.ANY),
                  pl.BlockSpec((tm, D), lambda i: (i, 0)), pl.BlockSpec((tm, LANES), lambda i: (i, 0)),
                  pl.BlockSpec((1, D), lambda i: (0, 0))],
        out_specs=pl.BlockSpec((tm, D), lambda i: (i, 0)),
        out_shape=jax.ShapeDtypeStruct((T, D), F32),
        scratch_shapes=[pltpu.VMEM((2, TOP_K * tm * PACK_ROWS, LANES), F32), pltpu.SemaphoreType.DMA((2,))],
        compiler_params=pltpu.CompilerParams(
            dimension_semantics=("arbitrary",), vmem_limit_bytes=VMEM_LIMIT),
        name="combine_norm",
    )(pos_kmajor, pos_kmajor, y_rows, h2, wts, g_final.reshape(1, D))


def _layer(h, mem, norm_mix, w_in, conv_w, w_branch_attn, w_branch_conv, w_gate, b_gate, w_out,
           norm_cross, norm_mem, w_cq, w_ckv, w_co, norm_moe, router_w, router_b,
           w_eg, b_eg, w_eu, b_eu, w_ed, b_ed, norm_final):
    B, S, D = h.shape
    T = B * S
    W3 = 3 * ATTN_WIDTH
    n_attn = len(ATTN_GROUPS) * W3

    assert ATTN_GROUPS[0][1] == 1
    w_in_b = w_in.astype(BF16)
    proj, qkv1, gates = _inproj(
        h, norm_mix, [(w_in_b[:, n_attn:], None), (w_in_b[:, :W3], None), (w_gate.astype(BF16), b_gate)],
        1, 512, 512, name="inproj_conv_g1_gates")
    proj2d = proj.reshape(T, -1)
    gates2d = gates.reshape(T, -1)

    o_list, lse_list = [], []
    for g, (window, dil) in enumerate(ATTN_GROUPS):
        assert window // dil == Q_BLOCK
        if g == 0:
            qkv = qkv1
        else:
            qkv, = _inproj(h, norm_mix, [(w_in_b[:, g * W3:(g + 1) * W3], None)], dil, 1024, 512,
                           name=f"inproj_g{g + 1}")
        o, lse = _attention(qkv, dil, name=f"dilated_attn_g{g + 1}")
        o_list.append(o)
        lse_list.append(lse)

    head_expand = (jnp.arange(LANES)[:, None] == jnp.arange(ATTN_WIDTH)[None, :] // HEAD_DIM).astype(BF16)
    h1 = _mix(o_list, lse_list, proj2d, gates2d, h.reshape(T, D), head_expand,
              w_branch_attn.astype(BF16), w_branch_conv.astype(BF16), w_out.astype(BF16), conv_w, S)

    kv = _mem_kv(mem, norm_mem, w_ckv.astype(BF16))
    rb = jnp.broadcast_to(router_b[:, None], (N_EXPERTS, LANES))
    h2, u3, meta, wts, counts = _cross_router(
        h1, kv, norm_cross.reshape(1, D), w_cq.astype(BF16), w_co.astype(BF16), norm_moe.reshape(1, D),
        router_w.T.astype(BF16), rb, S)

    nb = -(-(T * TOP_K) // MOE_ROWS) + N_EXPERTS
    cnt = counts[:, 0].astype(I32)
    padded = (cnt + MOE_ROWS - 1) // MOE_ROWS * MOE_ROWS
    pend = jnp.cumsum(padded)
    pstart = pend - padded
    experts = jnp.arange(N_EXPERTS, dtype=I32)
    n_pad = padded - cnt
    pad_end = jnp.cumsum(n_pad)
    fill_base = jnp.concatenate([pstart + cnt - (pad_end - n_pad), pend[-1:] - pad_end[-1:]])
    pos_kt, fill_rows = _plan(meta, pstart, pad_end, fill_base, nb * MOE_ROWS - T * TOP_K)

    tm = COMBINE_ROWS
    assert tm & (tm - 1) == 0 and T % tm == 0
    pos_kmajor = pos_kt.reshape(TOP_K, T // tm, tm).transpose(1, 0, 2).reshape(T // tm, 1, TOP_K * tm)
    block_start = jnp.arange(nb, dtype=I32) * MOE_ROWS
    block_expert = jnp.minimum(jnp.sum(block_start[:, None] >= pend[None, :], axis=1), N_EXPERTS - 1).astype(I32)
    active = cnt > 0
    slot_e = (jnp.cumsum(active.astype(I32)) - 1) % 2
    later = jnp.where(active[None, :] & (experts[None, :] > experts[:, None]), experts[None, :], N_EXPERTS)
    next_e = jnp.min(later, axis=1)
    next_e = jnp.where(next_e == N_EXPERTS, -1, next_e).astype(I32)
    x_rows = _dispatch(u3, pos_kmajor, fill_rows, tm)
    block_is = block_expert[:, None] == experts[None, :]
    per_block = lambda table: jnp.sum(jnp.where(block_is, table[None, :], 0), axis=1).astype(I32)
    y_rows = _experts(x_rows, block_expert, per_block(slot_e), per_block(next_e),
                      (pend[-1:] // MOE_ROWS).astype(I32), w_eg, b_eg, w_eu, b_eu, w_ed, b_ed)

    out = _combine(y_rows, pos_kmajor, h2, wts, norm_final)
    return out.reshape(B, S, D)


def kernel(x, mem, norm_mix, w_in, conv_w, w_branch_attn, w_branch_conv, w_gate, b_gate, w_out, norm_cross, norm_mem, w_cq, w_ckv, w_co, norm_moe, router_w, router_b, w_exp_gate, b_exp_gate, w_exp_up, b_exp_up, w_exp_down, b_exp_down, norm_final):
    depth = norm_mix.shape[0]
    assert depth == 1, "the final norm is fused into the last layer's combine step"
    return _layer(x, mem, norm_mix[0], w_in[0], conv_w[0], w_branch_attn[0], w_branch_conv[0], w_gate[0],
                  b_gate[0], w_out[0], norm_cross[0], norm_mem[0], w_cq[0], w_ckv[0], w_co[0], norm_moe[0],
                  router_w[0], router_b[0], w_exp_gate[0], b_exp_gate[0], w_exp_up[0], b_exp_up[0],
                  w_exp_down[0], b_exp_down[0], norm_final)
```

```python
import functools

import jax
import jax.numpy as jnp
import numpy as np
from jax import lax
from jax.experimental import pallas as pl
from jax.experimental.pallas import tpu as pltpu

F32 = jnp.float32
BF16 = jnp.bfloat16
I32 = jnp.int32

D_MODEL = 1024
ATTN_GROUPS = ((128, 1), (512, 4), (2048, 16))
ATTN_HEADS = 8
HEAD_DIM = 64
ATTN_WIDTH = ATTN_HEADS * HEAD_DIM
Q_BLOCK = 128
CONV_K = 3
MEM_HEADS = 4
MEM_HEAD_DIM = 128
MEM_WIDTH = MEM_HEADS * MEM_HEAD_DIM
N_EXPERTS = 32
TOP_K = 4
SWIGLU_LIMIT = 7.0
SWIGLU_ALPHA = 1.702
EPS = 1e-6
NEG_INF = -1e30

LANES = 128
VMEM_LIMIT = 56 * 1024 * 1024

MOE_ROWS = 256
MOE_STEP_BLOCKS = 4
COMBINE_ROWS = 512
CONV_HALO = 16
META_ROWS = 16
REGROUP_STRIDE = 4


def _rms(x, g):
    ms = jnp.mean(x * x, axis=-1, keepdims=True)
    return x * lax.rsqrt(ms + EPS) * g


PACK_ROWS = D_MODEL // LANES


def _pack_rows(y, out_ref, n):
    for c in range(PACK_ROWS):
        out_ref[pl.ds(c, n, stride=PACK_ROWS), :] = y[:, c * LANES:(c + 1) * LANES]


def _unpack_rows(ref, start, n):
    return jnp.concatenate(
        [ref[pl.ds(start + c, n, stride=PACK_ROWS), :] for c in range(PACK_ROWS)], axis=1)


def _inproj_kernel(x_ref, g_ref, *refs, gate_bias, dil, chunk):
    n_w = len(gate_bias)
    w_refs, b_refs, o_refs = refs[:n_w], refs[n_w:2 * n_w], refs[2 * n_w:3 * n_w]
    u_scr = refs[3 * n_w]
    tm = x_ref.shape[0]
    per = tm // dil
    u = _rms(x_ref[...], g_ref[...])
    if dil == 1:
        u_scr[...] = u.astype(BF16)
    else:
        slab = refs[3 * n_w + 1]
        n_slab = u.shape[1] // LANES
        for c in range(n_slab):
            slab[c] = u[:, c * LANES:(c + 1) * LANES]
        if dil <= REGROUP_STRIDE:
            for r in range(dil):
                for c in range(n_slab):
                    u_scr[r * per:(r + 1) * per, c * LANES:(c + 1) * LANES] = (
                        slab[c, pl.ds(r, per, stride=dil), :].astype(BF16))
        else:
            slab2 = refs[3 * n_w + 2]
            d1, d2 = REGROUP_STRIDE, dil // REGROUP_STRIDE
            group = tm // d1
            for r1 in range(d1):
                for c in range(n_slab):
                    slab2[c, r1 * group:(r1 + 1) * group, :] = slab[c, pl.ds(r1, group, stride=d1), :]
            for r1 in range(d1):
                for r2 in range(d2):
                    r = r1 + d1 * r2
                    for c in range(n_slab):
                        u_scr[r * per:(r + 1) * per, c * LANES:(c + 1) * LANES] = (
                            slab2[c, pl.ds(r1 * group + r2, per, stride=d2), :].astype(BF16))
    ub = u_scr[...]
    for w_ref, b_ref, o_ref, sig in zip(w_refs, b_refs, o_refs, gate_bias):
        for c in range(w_ref.shape[1] // chunk):
            cols = slice(c * chunk, (c + 1) * chunk)
            acc = jnp.dot(ub, w_ref[:, cols], preferred_element_type=F32)
            if sig:
                acc = jax.nn.sigmoid(acc + b_ref[:, cols])
            for r in range(dil):
                o_ref[r, :, cols] = acc[r * per:(r + 1) * per].astype(o_ref.dtype)


def _inproj(x, gain, weights, dil, tm, chunk, name):
    B, S, D = x.shape
    ws = [w for w, _ in weights]
    bs = [jnp.zeros((1, w.shape[1]), F32) if b is None else b.reshape(1, -1) for w, b in weights]
    const = lambda a: pl.BlockSpec(a.shape, lambda b, i: (0, 0))
    scratch = [pltpu.VMEM((tm, D), BF16)]
    if dil > 1:
        scratch.append(pltpu.VMEM((D // LANES, tm, LANES), F32))
    if dil > REGROUP_STRIDE:
        assert dil % REGROUP_STRIDE == 0 and dil // REGROUP_STRIDE <= REGROUP_STRIDE
        scratch.append(pltpu.VMEM((D // LANES, tm, LANES), F32))
    return pl.pallas_call(
        functools.partial(_inproj_kernel, gate_bias=tuple(b is not None for _, b in weights),
                          dil=dil, chunk=chunk),
        grid=(B, S // tm),
        in_specs=[pl.BlockSpec((None, tm, D), lambda b, i: (b, i, 0)), const(gain.reshape(1, D))]
                 + [const(w) for w in ws] + [const(b) for b in bs],
        out_specs=[pl.BlockSpec((None, dil, tm // dil, w.shape[1]), lambda b, i: (b, 0, i, 0)) for w in ws],
        out_shape=[jax.ShapeDtypeStruct((B, dil, S // dil, w.shape[1]), BF16) for w in ws],
        scratch_shapes=scratch,
        compiler_params=pltpu.CompilerParams(
            dimension_semantics=("parallel", "parallel"), vmem_limit_bytes=VMEM_LIMIT),
        name=name,
    )(x, gain.reshape(1, D), *ws, *bs)


def _mem_kv_kernel(x_ref, g_ref, w_ref, o_ref):
    u = _rms(x_ref[...], g_ref[...]).astype(BF16)
    o_ref[...] = jnp.dot(u, w_ref[...], preferred_element_type=F32).astype(o_ref.dtype)


def _mem_kv(mem, gain, w):
    B, M, D = mem.shape
    N = w.shape[1]
    return pl.pallas_call(
        _mem_kv_kernel,
        grid=(B,),
        in_specs=[pl.BlockSpec((None, M, D), lambda b: (b, 0, 0)),
                  pl.BlockSpec((1, D), lambda b: (0, 0)),
                  pl.BlockSpec((D, N), lambda b: (0, 0))],
        out_specs=pl.BlockSpec((None, M, N), lambda b: (b, 0, 0)),
        out_shape=jax.ShapeDtypeStruct((B, M, N), BF16),
        compiler_params=pltpu.CompilerParams(
            dimension_semantics=("parallel",), vmem_limit_bytes=VMEM_LIMIT),
        name="mem_kv",
    )(mem, gain.reshape(1, D), w)


def _attn_block(q_ref, kp_ref, kc_ref, vp_ref, vc_ref, tab_ref, first, r, j):
    lane = lax.broadcasted_iota(I32, (Q_BLOCK, LANES), 1)
    low = lane < HEAD_DIM
    lse_tile = jnp.zeros((Q_BLOCK, LANES), F32)
    pairs = []
    if j > 0:
        first = 1
    for pair in range(ATTN_HEADS // 2):
        cols = slice(pair * LANES, (pair + 1) * LANES)
        q2 = q_ref[r, j * Q_BLOCK:(j + 1) * Q_BLOCK, cols] * (HEAD_DIM ** -0.5)
        if j == 0:
            k2 = jnp.concatenate([kp_ref[r, :, cols], kc_ref[r, :Q_BLOCK, cols]], axis=0)
            v2 = jnp.concatenate([vp_ref[r, :, cols], vc_ref[r, :Q_BLOCK, cols]], axis=0)
        else:
            k2 = kc_ref[r, (j - 1) * Q_BLOCK:(j + 1) * Q_BLOCK, cols]
            v2 = vc_ref[r, (j - 1) * Q_BLOCK:(j + 1) * Q_BLOCK, cols]
        outs = []
        for half in range(2):
            h = 2 * pair + half
            keep = low if half == 0 else jnp.logical_not(low)
            qm = jnp.where(keep, q2, jnp.zeros_like(q2))
            s = lax.dot_general(qm, k2, (((1,), (1,)), ((), ())), preferred_element_type=F32)
            s = s + tab_ref[first, h]
            m = jnp.max(s, axis=1, keepdims=True)
            p = jnp.exp(s - m)
            l = jnp.sum(p, axis=1, keepdims=True)
            o = jnp.dot(p.astype(BF16), v2, preferred_element_type=F32) * (1.0 / l)
            outs.append(o)
            lse_tile = jnp.where(lane == h, m + jnp.log(l), lse_tile)
        pairs.append(jnp.where(low, outs[0], outs[1]))
    return pairs, lse_tile


def _attn_kernel(q_ref, kp_ref, kc_ref, vp_ref, vc_ref, tab_ref, o_ref, lse_ref, *scratch, dil, qb):
    first = jnp.minimum(pl.program_id(1), 1)
    blocks = (q_ref, kp_ref, kc_ref, vp_ref, vc_ref, tab_ref)
    n_pair = ATTN_HEADS // 2
    if dil == 1:
        for j in range(qb):
            pairs, lse_tile = _attn_block(*blocks, first, 0, j)
            rows = slice(j * Q_BLOCK, (j + 1) * Q_BLOCK)
            for p in range(n_pair):
                o_ref[rows, p * LANES:(p + 1) * LANES] = pairs[p].astype(o_ref.dtype)
            lse_ref[rows, :] = lse_tile
        return

    o_scr, lse_scr = scratch

    unroll = min(dil, 8)

    def body(g, carry):
        for r in [g * unroll + t for t in range(unroll)]:
            for j in range(qb):
                pairs, lse_tile = _attn_block(*blocks, first, r, j)
                rows = pl.ds(j * Q_BLOCK * dil + r, Q_BLOCK, stride=dil)
                for p in range(n_pair):
                    o_scr[p, rows, :] = pairs[p]
                lse_scr[rows, :] = lse_tile
        return carry

    if dil == unroll:
        body(0, 0)
    else:
        lax.fori_loop(0, dil // unroll, body, 0)
    for p in range(n_pair):
        o_ref[:, p * LANES:(p + 1) * LANES] = o_scr[p].astype(o_ref.dtype)
    lse_ref[...] = lse_scr[...]


def _attn_bias_table(dil):
    slopes = np.power(2.0, -8.0 * np.arange(1, ATTN_HEADS + 1, dtype=np.float32) / ATTN_HEADS)
    iq = np.arange(Q_BLOCK)
    ik = np.arange(2 * Q_BLOCK)
    dist = iq[:, None] + Q_BLOCK - ik[None, :]
    band = (dist >= 0) & (dist <= Q_BLOCK)
    has_prev = np.stack([ik >= Q_BLOCK, np.ones_like(ik, dtype=bool)])
    mask = band[None] & has_prev[:, None, :]
    bias = -slopes[:, None, None] * (dist * dil).astype(np.float32)[None]
    return np.where(mask[:, None], bias[None], np.float32(NEG_INF)).astype(np.float32)


def _attention(qkv, dil, name):
    B, _, L, _ = qkv.shape
    W = ATTN_WIDTH
    qb = max(1, 8 // dil)
    nb = L // (Q_BLOCK * qb)
    span = Q_BLOCK * qb * dil
    blk = lambda part, prev: (
        pl.BlockSpec((None, dil, Q_BLOCK, W), lambda b, n: (b, 0, jnp.maximum(n * qb - 1, 0), part))
        if prev else pl.BlockSpec((None, dil, Q_BLOCK * qb, W), lambda b, n: (b, 0, n, part)))
    tab = _attn_bias_table(dil)
    scratch = [] if dil == 1 else [pltpu.VMEM((ATTN_HEADS // 2, span, LANES), F32),
                                   pltpu.VMEM((span, LANES), F32)]
    o, lse = pl.pallas_call(
        functools.partial(_attn_kernel, dil=dil, qb=qb),
        grid=(B, nb),
        in_specs=[blk(0, False), blk(1, True), blk(1, False), blk(2, True), blk(2, False),
                  pl.BlockSpec(tab.shape, lambda b, n: (0, 0, 0, 0))],
        out_specs=[pl.BlockSpec((None, span, W), lambda b, n: (b, n, 0)),
                   pl.BlockSpec((None, span, LANES), lambda b, n: (b, n, 0))],
        out_shape=[jax.ShapeDtypeStruct((B, L * dil, W), BF16),
                   jax.ShapeDtypeStruct((B, L * dil, LANES), F32)],
        scratch_shapes=scratch,
        compiler_params=pltpu.CompilerParams(
            dimension_semantics=("parallel", "parallel"), vmem_limit_bytes=VMEM_LIMIT),
        name=name,
    )(qkv, qkv, qkv, qkv, qkv, tab)
    return o.reshape(B * L * dil, W), lse.reshape(B * L * dil, LANES)


def _mix_kernel(o1_ref, o2_ref, o3_ref, l1_ref, l2_ref, l3_ref, gb_ref, gc_ref, xc_ref, gch_ref, xch_ref,
                ga_ref, gv_ref, x_ref, e_ref, wa_ref, wc_ref, wo_ref, cw_ref, h_ref, *, tm, seq):
    a1, a2, a3 = l1_ref[...], l2_ref[...], l3_ref[...]
    m = jnp.maximum(jnp.maximum(a1, a2), a3)
    e1, e2, e3 = jnp.exp(a1 - m), jnp.exp(a2 - m), jnp.exp(a3 - m)
    inv = 1.0 / (e1 + e2 + e3)
    y = jnp.zeros((tm, ATTN_WIDTH), F32)
    for e, o_ref in ((e1, o1_ref), (e2, o2_ref), (e3, o3_ref)):
        w = e * inv
        w_hi = w.astype(BF16)
        w_lo = (w - w_hi.astype(F32)).astype(BF16)
        w_heads = (jnp.dot(w_hi, e_ref[...], preferred_element_type=F32)
                   + jnp.dot(w_lo, e_ref[...], preferred_element_type=F32))
        y = y + w_heads * o_ref[...].astype(F32)
    branch_a = jnp.dot(y.astype(BF16), wa_ref[...], preferred_element_type=F32)

    at_start = (pl.program_id(0) * tm) % seq == 0
    halo = gch_ref[...].astype(F32) * xch_ref[...].astype(F32)
    halo = jnp.where(at_start, jnp.zeros_like(halo), halo)[CONV_HALO - 8:, :]
    z = gc_ref[...].astype(F32) * xc_ref[...].astype(F32)
    row = lax.broadcasted_iota(I32, (8, z.shape[1]), 0)
    taps = []
    for shift in (2, 1):
        zs = pltpu.roll(z, shift, axis=0)
        head = jnp.where(row < shift, pltpu.roll(halo, shift, axis=0), zs[0:8, :])
        taps.append(jnp.concatenate([head, zs[8:, :]], axis=0))
    conv = cw_ref[0:1, :] * taps[0] + cw_ref[1:2, :] * taps[1] + cw_ref[2:3, :] * z
    y_conv = gb_ref[...].astype(F32) * conv
    branch_c = jnp.dot(y_conv.astype(BF16), wc_ref[...], preferred_element_type=F32)

    mixed = ga_ref[...].astype(F32) * branch_a + gv_ref[...].astype(F32) * branch_c
    h_ref[...] = x_ref[...] + jnp.dot(mixed.astype(BF16), wo_ref[...], preferred_element_type=F32)


def _mix(o_list, lse_list, proj, gates, x2d, head_expand, wa, wc, wo, conv_w, seq, tm=512):
    T, D = x2d.shape
    W = ATTN_WIDTH
    row = lambda width, col: pl.BlockSpec((tm, width), lambda i: (i, col))
    halo = lambda col: pl.BlockSpec(
        (CONV_HALO, D), lambda i: (jnp.maximum(i * (tm // CONV_HALO) - 1, 0), col))
    full = lambda a: pl.BlockSpec(a.shape, lambda i: (0,) * a.ndim)
    return pl.pallas_call(
        functools.partial(_mix_kernel, tm=tm, seq=seq),
        grid=(T // tm,),
        in_specs=[row(W, 0)] * 3 + [row(LANES, 0)] * 3
                 + [row(D, 0), row(D, 1), row(D, 2), halo(1), halo(2)]
                 + [row(D, 0), row(D, 1), row(D, 0)]
                 + [full(head_expand), full(wa), full(wc), full(wo), full(conv_w)],
        out_specs=pl.BlockSpec((tm, D), lambda i: (i, 0)),
        out_shape=jax.ShapeDtypeStruct((T, D), F32),
        compiler_params=pltpu.CompilerParams(
            dimension_semantics=("parallel",), vmem_limit_bytes=VMEM_LIMIT),
        name="branch_mix",
    )(*o_list, *lse_list, proj, proj, proj, proj, proj, gates, gates, x2d,
      head_expand, wa, wc, wo, conv_w)


def _cross_kernel(h_ref, k_ref, v_ref, g2_ref, wq_ref, wo_ref, g3_ref, rw_ref, rb_ref, tri_ref,
                  h2_ref, u3_ref, meta_ref, wts_ref, cnt_ref, run_scr, *, tm):
    @pl.when(pl.program_id(0) == 0)
    def _():
        run_scr[...] = jnp.zeros_like(run_scr)

    h = h_ref[...]
    u = _rms(h, g2_ref[...]).astype(BF16)
    q = jnp.dot(u, wq_ref[...], preferred_element_type=F32) * (MEM_HEAD_DIM ** -0.5)
    heads = []
    for hd in range(MEM_HEADS):
        cols = slice(hd * MEM_HEAD_DIM, (hd + 1) * MEM_HEAD_DIM)
        s = lax.dot_general(q[:, cols].astype(BF16), k_ref[:, cols], (((1,), (1,)), ((), ())),
                            preferred_element_type=F32)
        m = jnp.max(s, axis=1, keepdims=True)
        p = jnp.exp(s - m)
        l = jnp.sum(p, axis=1, keepdims=True)
        heads.append(jnp.dot(p.astype(BF16), v_ref[:, cols], preferred_element_type=F32) * (1.0 / l))
    o = jnp.concatenate(heads, axis=1).astype(BF16)
    h2 = h + jnp.dot(o, wo_ref[...], preferred_element_type=F32)
    h2_ref[...] = h2
    u3 = _rms(h2, g3_ref[...])
    _pack_rows(u3, u3_ref, tm)

    logits = lax.dot_general(rw_ref[...], u3.astype(BF16), (((1,), (1,)), ((), ())),
                             preferred_element_type=F32) + rb_ref[:, 0:1]
    expert = lax.broadcasted_iota(I32, (N_EXPERTS, tm), 0).astype(F32)
    work = logits
    picked = jnp.zeros((N_EXPERTS, tm), F32)
    tops, idxs, sels = [], [], []
    for _ in range(TOP_K):
        mk = jnp.max(work, axis=0, keepdims=True)
        ik = jnp.min(jnp.where(work == mk, expert, float(N_EXPERTS)), axis=0, keepdims=True)
        sel = expert == ik
        work = jnp.where(sel, -jnp.inf, work)
        picked = picked + sel.astype(F32)
        tops.append(mk)
        idxs.append(ik)
        sels.append(sel)
    exps = [jnp.exp(t - tops[0]) for t in tops]
    inv = 1.0 / (exps[0] + exps[1] + exps[2] + exps[3])
    wts = [e * inv for e in exps]

    before = jnp.dot(picked.astype(BF16), tri_ref[...], preferred_element_type=F32) + run_scr[:, 0:1]
    ranks = [jnp.sum(jnp.where(sel, before, 0.0), axis=0, keepdims=True) for sel in sels]

    row = lax.broadcasted_iota(I32, (META_ROWS, tm), 0)
    meta = jnp.zeros((META_ROWS, tm), F32)
    for r, v in enumerate(idxs + ranks + wts):
        meta = jnp.where(row == r, v, meta)
    meta_ref[...] = meta
    row = lax.broadcasted_iota(I32, (LANES, tm), 0)
    wpad = jnp.zeros((LANES, tm), F32)
    for k in range(TOP_K):
        wpad = jnp.where(row == k, wts[k], wpad)
    wts_ref[...] = wpad.T
    run_scr[...] = run_scr[...] + jnp.sum(picked, axis=1, keepdims=True)
    cnt_ref[...] = run_scr[...]


def _cross_router(h1, kv, g2, wq, wo, g3, rw_t, rb, seq, tm=1024):
    T, D = h1.shape
    n_mem = kv.shape[1]
    tri = (jnp.arange(tm)[:, None] < jnp.arange(tm)[None, :]).astype(BF16)
    full = lambda a: pl.BlockSpec(a.shape, lambda i: (0,) * a.ndim)
    kvspec = lambda col: pl.BlockSpec((None, n_mem, MEM_WIDTH), lambda i: ((i * tm) // seq, 0, col))
    tile = lambda width: pl.BlockSpec((tm, width), lambda i: (i, 0))
    return pl.pallas_call(
        functools.partial(_cross_kernel, tm=tm),
        grid=(T // tm,),
        in_specs=[tile(D), kvspec(0), kvspec(1), full(g2), full(wq), full(wo), full(g3),
                  full(rw_t), full(rb), full(tri)],
        out_specs=[tile(D), pl.BlockSpec((tm * PACK_ROWS, LANES), lambda i: (i, 0)),
                   pl.BlockSpec((META_ROWS, tm), lambda i: (0, i)), tile(LANES),
                   pl.BlockSpec((N_EXPERTS, LANES), lambda i: (0, 0))],
        out_shape=[jax.ShapeDtypeStruct((T, D), F32), jax.ShapeDtypeStruct((T * PACK_ROWS, LANES), F32),
                   jax.ShapeDtypeStruct((META_ROWS, T), F32), jax.ShapeDtypeStruct((T, LANES), F32),
                   jax.ShapeDtypeStruct((N_EXPERTS, LANES), F32)],
        scratch_shapes=[pltpu.VMEM((N_EXPERTS, LANES), F32)],
        compiler_params=pltpu.CompilerParams(
            dimension_semantics=("arbitrary",), vmem_limit_bytes=VMEM_LIMIT),
        name="cross_router",
    )(h1, kv, kv, g2, wq, wo, g3, rw_t, rb, tri)


def _plan_kernel(pstart_ref, pad_end_ref, fill_base_ref, meta_ref, pos_ref, fill_ref):
    ei = meta_ref[0:2 * TOP_K, :].astype(I32)
    start = jnp.zeros(ei.shape, I32)
    for j in range(N_EXPERTS):
        start = jnp.where(ei == j, pstart_ref[j], start)
    pos_ref[...] = start + pltpu.roll(ei, TOP_K, axis=0)

    rows, cols = fill_ref.shape
    f = lax.broadcasted_iota(I32, (rows, cols), 0) * cols + lax.broadcasted_iota(I32, (rows, cols), 1)
    base = jnp.full((rows, cols), fill_base_ref[0], I32)
    for j in range(1, N_EXPERTS + 1):
        base = jnp.where(f >= pad_end_ref[j - 1], fill_base_ref[j], base)
    fill_ref[...] = base + f


def _plan(meta, pstart, pad_end, fill_base, n_fill):
    T = meta.shape[1]
    rows = 2 * TOP_K
    assert n_fill % (rows * LANES) == 0
    grid_spec = pltpu.PrefetchScalarGridSpec(
        num_scalar_prefetch=3,
        grid=(1,),
        in_specs=[pl.BlockSpec(meta.shape, lambda i, a, b, c: (0, 0))],
        out_specs=[pl.BlockSpec((rows, T), lambda i, a, b, c: (0, 0)),
                   pl.BlockSpec((rows, n_fill // rows), lambda i, a, b, c: (0, 0))],
    )
    pos, fill = pl.pallas_call(
        _plan_kernel,
        grid_spec=grid_spec,
        out_shape=[jax.ShapeDtypeStruct((rows, T), I32), jax.ShapeDtypeStruct((rows, n_fill // rows), I32)],
        compiler_params=pltpu.CompilerParams(dimension_semantics=("arbitrary",)),
        name="row_plan",
    )(pstart, pad_end, fill_base, meta)
    return pos[:TOP_K], fill.reshape(-1)


def _wait_copies(src_hbm, dst, sem, rows):
    pltpu.make_async_copy(src_hbm.at[pl.ds(0, rows), :], dst.at[pl.ds(0, rows), :], sem).wait()


def _dispatch_kernel(pos_ref, u_hbm, x_hbm, stage, load_sem, row_sem, *, tm, n_token_steps):
    i = pl.program_id(0)
    n = pl.num_programs(0)
    batch = tm * TOP_K * PACK_ROWS
    tile_rows = tm * PACK_ROWS

    def tile_copy(step):
        first = pl.multiple_of(jnp.minimum(step, n_token_steps - 1) * tile_rows, tile_rows)
        return pltpu.make_async_copy(u_hbm.at[pl.ds(first, tile_rows), :], stage.at[step % 3],
                                     load_sem.at[step % 3])

    @pl.when(i == 0)
    def _():
        tile_copy(i).start()

    @pl.when(i + 1 < n)
    def _():
        tile_copy(i + 1).start()

    tile_copy(i).wait()
    src_tile = stage.at[i % 3]
    group = 16

    for k in range(TOP_K):
        def body(g, carry, k=k):
            rows = [pos_ref[0, k * tm + g * group + j] for j in range(group)]
            for j, p in enumerate(rows):
                src = src_tile.at[pl.ds(pl.multiple_of((g * group + j) * PACK_ROWS, PACK_ROWS), PACK_ROWS), :]
                dst = x_hbm.at[pl.ds(pl.multiple_of(p * PACK_ROWS, PACK_ROWS), PACK_ROWS), :]
                pltpu.make_async_copy(src, dst, row_sem.at[i % 2]).start(priority=j % 2)
            return carry

        lax.fori_loop(0, tm // group, body, 0)

    @pl.when(i > 0)
    def _():
        _wait_copies(u_hbm, x_hbm, row_sem.at[1 - i % 2], batch)

    @pl.when(i == n - 1)
    def _():
        _wait_copies(u_hbm, x_hbm, row_sem.at[i % 2], batch)


def _dispatch(u_packed, pos_kmajor, fill_rows, tm):
    T = pos_kmajor.shape[0] * tm
    per_step = TOP_K * tm
    dst_rows = jnp.concatenate([pos_kmajor.reshape(-1), fill_rows])
    n_rows = dst_rows.shape[0]
    assert n_rows % per_step == 0
    steps = n_rows // per_step
    return pl.pallas_call(
        functools.partial(_dispatch_kernel, tm=tm, n_token_steps=T // tm),
        grid=(steps,),
        in_specs=[pl.BlockSpec((None, 1, per_step), lambda i: (i, 0, 0), memory_space=pltpu.SMEM),
                  pl.BlockSpec(memory_space=pl.ANY)],
        out_specs=pl.BlockSpec(memory_space=pl.ANY),
        out_shape=jax.ShapeDtypeStruct((n_rows * PACK_ROWS, LANES), F32),
        scratch_shapes=[pltpu.VMEM((3, tm * PACK_ROWS, LANES), F32),
                        pltpu.SemaphoreType.DMA((3,)), pltpu.SemaphoreType.DMA((2,))],
        compiler_params=pltpu.CompilerParams(dimension_semantics=("arbitrary",)),
        name="dispatch_rows",
    )(dst_rows.reshape(steps, 1, per_step), u_packed)


def _expert_kernel(bexp_ref, slot_ref, next_ref, nused_ref, x_ref, wg_hbm, bg_ref, wu_hbm, bu_ref,
                   wd_hbm, bd_ref, y_ref, stage, w_bf, sem):
    weights = (wg_hbm, wu_hbm, wd_hbm)
    block_rows = MOE_ROWS * PACK_ROWS

    def fetch(expert, into):
        return [pltpu.make_async_copy(w.at[expert], stage.at[into, j], sem.at[into])
                for j, w in enumerate(weights)]

    for sub in range(MOE_STEP_BLOCKS):
        i = pl.program_id(0) * MOE_STEP_BLOCKS + sub
        used = i < nused_ref[0]
        e = bexp_ref[i]
        slot = slot_ref[i]
        changed = jnp.logical_or(i == 0, e != bexp_ref[jnp.maximum(i - 1, 0)])
        y_rows = y_ref.at[pl.ds(sub * block_rows, block_rows), :]

        @pl.when(jnp.logical_and(changed, used))
        def _():
            @pl.when(i == 0)
            def _():
                for c in fetch(e, slot):
                    c.start()

            for c in fetch(e, slot):
                c.wait()
            for s in range(2):
                @pl.when(slot == s)
                def _():
                    for j in range(len(weights)):
                        w_bf[j] = stage[s, j].astype(BF16)

            @pl.when(next_ref[i] >= 0)
            def _():
                for c in fetch(next_ref[i], 1 - slot):
                    c.start(priority=1)

        @pl.when(used)
        def _():
            x = _unpack_rows(x_ref, sub * block_rows, MOE_ROWS).astype(BF16)
            gate = jnp.minimum(jnp.dot(x, w_bf[0], preferred_element_type=F32) + bg_ref[e], SWIGLU_LIMIT)
            lin = jnp.clip(jnp.dot(x, w_bf[1], preferred_element_type=F32) + bu_ref[e],
                           -SWIGLU_LIMIT, SWIGLU_LIMIT)
            hdn = gate * jax.nn.sigmoid(SWIGLU_ALPHA * gate) * (lin + 1.0)
            y = jnp.dot(hdn.astype(BF16), w_bf[2], preferred_element_type=F32) + bd_ref[e]
            _pack_rows(y, y_rows, MOE_ROWS)

        @pl.when(jnp.logical_not(used))
        def _():
            y_rows[...] = jnp.zeros(y_rows.shape, y_rows.dtype)


def _experts(x_packed, block_expert, block_slot, block_next, n_used, w_eg, b_eg, w_eu, b_eu, w_ed, b_ed):
    nb = block_expert.shape[0]
    assert nb % MOE_STEP_BLOCKS == 0
    E, D, F = w_eg.shape
    assert D == F
    step_rows = MOE_STEP_BLOCKS * MOE_ROWS * PACK_ROWS
    rows = lambda i, be, sl, nx, nu: (i, 0)
    bspec = lambda n: pl.BlockSpec((E, 1, n), lambda i, be, sl, nx, nu: (0, 0, 0))
    anyspec = pl.BlockSpec(memory_space=pl.ANY)
    grid_spec = pltpu.PrefetchScalarGridSpec(
        num_scalar_prefetch=4,
        grid=(nb // MOE_STEP_BLOCKS,),
        in_specs=[pl.BlockSpec((step_rows, LANES), rows),
                  anyspec, bspec(F), anyspec, bspec(F), anyspec, bspec(D)],
        out_specs=pl.BlockSpec((step_rows, LANES), rows),
        scratch_shapes=[pltpu.VMEM((2, 3, D, F), F32), pltpu.VMEM((3, D, F), BF16),
                        pltpu.SemaphoreType.DMA((2,))],
    )
    return pl.pallas_call(
        _expert_kernel,
        grid_spec=grid_spec,
        out_shape=jax.ShapeDtypeStruct(x_packed.shape, F32),
        compiler_params=pltpu.CompilerParams(
            dimension_semantics=("arbitrary",), vmem_limit_bytes=VMEM_LIMIT),
        name="expert_ffn",
    )(block_expert, block_slot, block_next, n_used, x_packed, w_eg, b_eg.reshape(E, 1, F), w_eu,
      b_eu.reshape(E, 1, F), w_ed, b_ed.reshape(E, 1, D))


def _gather_packed(pos_ref, src_hbm, dst, sem, n):
    group = 16

    def body(g, carry):
        rows = [pos_ref[0, g * group + j] for j in range(group)]
        for j, p in enumerate(rows):
            r = g * group + j
            pltpu.make_async_copy(
                src_hbm.at[pl.ds(pl.multiple_of(p * PACK_ROWS, PACK_ROWS), PACK_ROWS), :],
                dst.at[pl.ds(pl.multiple_of(r * PACK_ROWS, PACK_ROWS), PACK_ROWS), :],
                sem).start(priority=j % 2)
        return carry
    lax.fori_loop(0, n // group, body, 0)


def _combine_kernel(pos0_ref, pos1_ref, y_hbm, h_ref, w_ref, g_ref, o_ref, ybuf, sem):
    i = pl.program_id(0)
    n = pl.num_programs(0)
    slot = i % 2
    rows = TOP_K * COMBINE_ROWS

    @pl.when(i == 0)
    def _():
        _gather_packed(pos0_ref, y_hbm, ybuf.at[0], sem.at[0], rows)

    @pl.when(i + 1 < n)
    def _():
        _gather_packed(pos1_ref, y_hbm, ybuf.at[1 - slot], sem.at[1 - slot], rows)

    _wait_copies(y_hbm, ybuf.at[slot], sem.at[slot], rows * PACK_ROWS)
    acc = h_ref[...]
    w = w_ref[...]
    for k in range(TOP_K):
        acc = acc + w[:, k:k + 1] * _unpack_rows(ybuf.at[slot], k * COMBINE_ROWS * PACK_ROWS, COMBINE_ROWS)
    o_ref[...] = _rms(acc, g_ref[...])


def _combine(y_rows, pos_kmajor, h2, wts, g_final):
    T, D = h2.shape
    tm = COMBINE_ROWS
    nt = T // tm
    posspec = lambda off: pl.BlockSpec(
        (None, 1, TOP_K * tm), lambda i: (jnp.minimum(i + off, nt - 1), 0, 0), memory_space=pltpu.SMEM)
    return pl.pallas_call(
        _combine_kernel,
        grid=(nt,),
        in_specs=[posspec(0), posspec(1), pl.BlockSpec(memory_space=pl.ANY),
                  pl.BlockSpec((tm, D), lambda i: (i, 0)), pl.BlockSpec((tm, LANES), lambda i: (i, 0)),
                  pl.BlockSpec((1, D), lambda i: (0, 0))],
        out_specs=pl.BlockSpec((tm, D), lambda i: (i, 0)),
        out_shape=jax.ShapeDtypeStruct((T, D), F32),
        scratch_shapes=[pltpu.VMEM((2, TOP_K * tm * PACK_ROWS, LANES), F32), pltpu.SemaphoreType.DMA((2,))],
        compiler_params=pltpu.CompilerParams(
            dimension_semantics=("arbitrary",), vmem_limit_bytes=VMEM_LIMIT),
        name="combine_norm",
    )(pos_kmajor, pos_kmajor, y_rows, h2, wts, g_final.reshape(1, D))


def _layer(h, mem, norm_mix, w_in, conv_w, w_branch_attn, w_branch_conv, w_gate, b_gate, w_out,
           norm_cross, norm_mem, w_cq, w_ckv, w_co, norm_moe, router_w, router_b,
           w_eg, b_eg, w_eu, b_eu, w_ed, b_ed, norm_final):
    B, S, D = h.shape
    T = B * S
    W3 = 3 * ATTN_WIDTH
    n_attn = len(ATTN_GROUPS) * W3

    assert ATTN_GROUPS[0][1] == 1
    w_in_b = w_in.astype(BF16)
    proj, qkv1, gates = _inproj(
        h, norm_mix, [(w_in_b[:, n_attn:], None), (w_in_b[:, :W3], None), (w_gate.astype(BF16), b_gate)],
        1, 1024, 512, name="inproj_conv_g1_gates")
    proj2d = proj.reshape(T, -1)
    gates2d = gates.reshape(T, -1)

    o_list, lse_list = [], []
    for g, (window, dil) in enumerate(ATTN_GROUPS):
        assert window // dil == Q_BLOCK
        if g == 0:
            qkv = qkv1
        else:
            qkv, = _inproj(h, norm_mix, [(w_in_b[:, g * W3:(g + 1) * W3], None)], dil, 1024, 512,
                           name=f"inproj_g{g + 1}")
        o, lse = _attention(qkv, dil, name=f"dilated_attn_g{g + 1}")
        o_list.append(o)
        lse_list.append(lse)

    head_expand = (jnp.arange(LANES)[:, None] == jnp.arange(ATTN_WIDTH)[None, :] // HEAD_DIM).astype(BF16)
    h1 = _mix(o_list, lse_list, proj2d, gates2d, h.reshape(T, D), head_expand,
              w_branch_attn.astype(BF16), w_branch_conv.astype(BF16), w_out.astype(BF16), conv_w, S)

    kv = _mem_kv(mem, norm_mem, w_ckv.astype(BF16))
    rb = jnp.broadcast_to(router_b[:, None], (N_EXPERTS, LANES))
    h2, u3, meta, wts, counts = _cross_router(
        h1, kv, norm_cross.reshape(1, D), w_cq.astype(BF16), w_co.astype(BF16), norm_moe.reshape(1, D),
        router_w.T.astype(BF16), rb, S)

    nb = -(-(T * TOP_K) // MOE_ROWS) + N_EXPERTS
    cnt = counts[:, 0].astype(I32)
    padded = (cnt + MOE_ROWS - 1) // MOE_ROWS * MOE_ROWS
    pend = jnp.cumsum(padded)
    pstart = pend - padded
    experts = jnp.arange(N_EXPERTS, dtype=I32)
    n_pad = padded - cnt
    pad_end = jnp.cumsum(n_pad)
    fill_base = jnp.concatenate([pstart + cnt - (pad_end - n_pad), pend[-1:] - pad_end[-1:]])
    pos_kt, fill_rows = _plan(meta, pstart, pad_end, fill_base, nb * MOE_ROWS - T * TOP_K)

    tm = COMBINE_ROWS
    assert tm & (tm - 1) == 0 and T % tm == 0
    pos_kmajor = pos_kt.reshape(TOP_K, T // tm, tm).transpose(1, 0, 2).reshape(T // tm, 1, TOP_K * tm)
    block_start = jnp.arange(nb, dtype=I32) * MOE_ROWS
    block_expert = jnp.minimum(jnp.sum(block_start[:, None] >= pend[None, :], axis=1), N_EXPERTS - 1).astype(I32)
    active = cnt > 0
    slot_e = (jnp.cumsum(active.astype(I32)) - 1) % 2
    later = jnp.where(active[None, :] & (experts[None, :] > experts[:, None]), experts[None, :], N_EXPERTS)
    next_e = jnp.min(later, axis=1)
    next_e = jnp.where(next_e == N_EXPERTS, -1, next_e).astype(I32)
    x_rows = _dispatch(u3, pos_kmajor, fill_rows, tm)
    block_is = block_expert[:, None] == experts[None, :]
    per_block = lambda table: jnp.sum(jnp.where(block_is, table[None, :], 0), axis=1).astype(I32)
    y_rows = _experts(x_rows, block_expert, per_block(slot_e), per_block(next_e),
                      (pend[-1:] // MOE_ROWS).astype(I32), w_eg, b_eg, w_eu, b_eu, w_ed, b_ed)

    out = _combine(y_rows, pos_kmajor, h2, wts, norm_final)
    return out.reshape(B, S, D)


def kernel(x, mem, norm_mix, w_in, conv_w, w_branch_attn, w_branch_conv, w_gate, b_gate, w_out, norm_cross, norm_mem, w_cq, w_ckv, w_co, norm_moe, router_w, router_b, w_exp_gate, b_exp_gate, w_exp_up, b_exp_up, w_exp_down, b_exp_down, norm_final):
    depth = norm_mix.shape[0]
    assert depth == 1, "the final norm is fused into the last layer's combine step"
    return _layer(x, mem, norm_mix[0], w_in[0], conv_w[0], w_branch_attn[0], w_branch_conv[0], w_gate[0],
                  b_gate[0], w_out[0], norm_cross[0], norm_mem[0], w_cq[0], w_ckv[0], w_co[0], norm_moe[0],
                  router_w[0], router_b[0], w_exp_gate[0], b_exp_gate[0], w_exp_up[0], b_exp_up[0],
                  w_exp_down[0], b_exp_down[0], norm_final)
```
